```python
import math
import jax, jax.numpy as jnp
from jax import lax
import numpy as np

D_MODEL = 1024
BATCH = 16
SEQ = 2048
DEPTH = 2

N_MIXERS = 2
N_META = 16
RWKV_HEAD = 64
RWKV_HEADS = D_MODEL // RWKV_HEAD
DECAY_LORA = max(32, int(round(1.8 * D_MODEL ** 0.5 / 32)) * 32)
AAA_LORA = max(32, int(round(1.8 * D_MODEL ** 0.5 / 32)) * 32)
GATE_LORA = max(32, int(round(0.6 * D_MODEL ** 0.8 / 32)) * 32)
GN_EPS = 64e-5
ATT_HEADS = 16
ATT_KV_HEADS = 4
ATT_HEAD_DIM = D_MODEL // ATT_HEADS
ATT_GROUP = ATT_HEADS // ATT_KV_HEADS
WINDOW = 128
ATT_BLOCK = 128
QKV_WIDTH = (ATT_HEADS + 2 * ATT_KV_HEADS) * ATT_HEAD_DIM
N_BUCKETS = 32
MAX_DISTANCE = 128
N_EXPERTS = 64
TOP_K = 8
N_GROUPS = 8
TOPK_GROUPS = 4
EXPERT_FF = 256
SHARED_FF = 256
ROUTED_SCALE = 2.5
MOE_BLOCK = 256
DEEPNORM_ALPHA = (2 * DEPTH) ** 0.25
DEEPNORM_BETA = (8 * DEPTH) ** -0.25
LN_EPS = 1e-5
N_RWKV_LAYERS = (DEPTH + 1) // 2
N_ATT_LAYERS = DEPTH // 2

kernel_name = 'hybrid_rwkv7_swa_sink_moe_deepnorm'

F32 = jnp.float32


def layer_norm(x, g, b):
    xf = x.astype(F32)
    mu = jnp.mean(xf, -1, keepdims=True)
    var = jnp.mean(jnp.square(xf - mu), -1, keepdims=True)
    return ((xf - mu) * lax.rsqrt(var + LN_EPS) * g.astype(F32) + b.astype(F32)).astype(x.dtype)


def _wkv7_step(S, inp):
    r_t, d_t, k_t, v_t, kk_t, a_t = inp
    sa = jnp.einsum('bhij,bhj->bhi', S, -kk_t)
    S = (S * d_t[:, :, None, :] + sa[..., None] * (kk_t * a_t)[:, :, None, :]
         + v_t[..., None] * k_t[:, :, None, :])
    y = jnp.einsum('bhij,bhj->bhi', S, r_t)
    return S, y


def rwkv7_time_mix(x, mu, w0, w1, w2, a0, a1, a2, g1, g2, k_k, k_a, r_k,
                   w_r, w_k, w_v, w_o, lnx_g, lnx_b):
    B, L, D = x.shape
    H, N = RWKV_HEADS, RWKV_HEAD
    xx = jnp.pad(x, ((0, 0), (1, 0), (0, 0)))[:, :L] - x
    r = (x + xx * mu[0]) @ w_r
    logw = -jax.nn.softplus(-(w0 + jnp.tanh((x + xx * mu[1]) @ w1) @ w2)) - 0.5
    k = (x + xx * mu[2]) @ w_k
    v = (x + xx * mu[3]) @ w_v
    a = jax.nn.sigmoid(a0 + ((x + xx * mu[4]) @ a1) @ a2)
    g = jax.nn.sigmoid((x + xx * mu[5]) @ g1) @ g2
    kk = (k * k_k).reshape(B, L, H, N).astype(F32)
    kk = kk / jnp.maximum(jnp.linalg.norm(kk, axis=-1, keepdims=True), 1e-12)
    k = k * (1 + (a - 1) * k_a)
    decay = jnp.exp(-jnp.exp(logw.astype(F32)))
    to_t = lambda t: jnp.swapaxes(t.reshape(B, L, H, N).astype(F32), 0, 1)
    xs = (to_t(r), to_t(decay), to_t(k), to_t(v), jnp.swapaxes(kk, 0, 1), to_t(a))
    _, y = lax.scan(_wkv7_step, jnp.zeros((B, H, N, N), F32), xs)
    y = jnp.swapaxes(y, 0, 1)
    m = jnp.mean(y, -1, keepdims=True)
    var = jnp.mean(jnp.square(y - m), -1, keepdims=True)
    y = ((y - m) * lax.rsqrt(var + GN_EPS)).reshape(B, L, D) * lnx_g.astype(F32) + lnx_b.astype(F32)
    rh, kh, vh = [t.reshape(B, L, H, N).astype(F32) for t in (r, k, v)]
    bonus = (jnp.sum(rh * kh * r_k.astype(F32), -1, keepdims=True) * vh).reshape(B, L, D)
    out = ((y + bonus) * g.astype(F32)).astype(x.dtype)
    return out @ w_o


def t5_bucket(dist):
    exact = N_BUCKETS // 2
    d = jnp.maximum(dist, 0)
    ratio = jnp.log(jnp.maximum(d, 1).astype(F32) / exact) / math.log(MAX_DISTANCE / exact)
    large = jnp.minimum(exact + (ratio * (N_BUCKETS - exact)).astype(jnp.int32), N_BUCKETS - 1)
    return jnp.where(d < exact, d, large)


def rel_bias_for(dist, rel_bias):
    b = rel_bias[t5_bucket(dist)]
    return jnp.moveaxis(b, -1, 0).reshape(ATT_KV_HEADS, ATT_GROUP, *dist.shape).astype(F32)


def sink_softmax_attend(logits, vals, sinks):
    s = sinks.astype(F32)[None, :, :, None, None]
    m = jnp.maximum(jnp.max(logits, -1, keepdims=True), s)
    p = jnp.exp(logits - m)
    probs = p / (jnp.sum(p, -1, keepdims=True) + jnp.exp(s - m))
    return jnp.einsum('bkgqs,bskd->bqkgd', probs.astype(vals.dtype), vals)


def swa_sink_attention(x, w_qkv, b_qkv, sinks, w_o, b_o, rel_bias):
    B, L, _ = x.shape
    S = L - N_META
    nb = S // ATT_BLOCK
    H, KV, G, hd = ATT_HEADS, ATT_KV_HEADS, ATT_GROUP, ATT_HEAD_DIM
    qkv = x @ w_qkv + b_qkv
    q = (qkv[..., :H * hd] * hd ** -0.5).reshape(B, L, KV, G, hd)
    k = qkv[..., H * hd:(H + KV) * hd].reshape(B, L, KV, hd)
    v = qkv[..., (H + KV) * hd:].reshape(B, L, KV, hd)
    sinks = sinks.reshape(KV, G)
    qm, km, vm = q[:, :N_META], k[:, :N_META], v[:, :N_META]
    pm = jnp.arange(N_META)
    d_mm = pm[:, None] - pm[None, :]
    lg = jnp.einsum('bqkgd,bskd->bkgqs', qm, km).astype(F32) + rel_bias_for(d_mm, rel_bias)
    lg = jnp.where(d_mm >= 0, lg, -jnp.inf)
    out_meta = sink_softmax_attend(lg, vm, sinks)
    blocks = lambda t: jnp.moveaxis(t[:, N_META:].reshape(B, nb, ATT_BLOCK, *t.shape[2:]), 1, 0)
    qb, kb, vb = blocks(q), blocks(k), blocks(v)
    band = lambda t: jnp.concatenate([jnp.concatenate([jnp.zeros_like(t[:1]), t[:-1]], 0), t], axis=2)
    kband, vband = band(kb), band(vb)
    qi = jnp.arange(ATT_BLOCK)[:, None]
    sj = jnp.arange(2 * ATT_BLOCK)[None, :]
    d_band = qi + ATT_BLOCK - sj
    in_window = (d_band >= 0) & (d_band < WINDOW)
    bias_band = rel_bias_for(d_band, rel_bias)
    meta_cols = jnp.arange(N_META)[None, :]

    def block_attend(args):
        j, q_j, k_j, v_j = args
        d_meta = N_META + j * ATT_BLOCK + qi - meta_cols
        lg_meta = jnp.einsum('bqkgd,bmkd->bkgqm', q_j, km).astype(F32) + rel_bias_for(d_meta, rel_bias)
        visible = in_window & ((j > 0) | (sj >= ATT_BLOCK))
        lg_band = jnp.einsum('bqkgd,bskd->bkgqs', q_j, k_j).astype(F32) + bias_band
        lg_band = jnp.where(visible, lg_band, -jnp.inf)
        return sink_softmax_attend(jnp.concatenate([lg_meta, lg_band], -1),
                                   jnp.concatenate([vm, v_j], 1), sinks)

    out_real = lax.map(block_attend, (jnp.arange(nb), qb, kband, vband))
    out_real = jnp.moveaxis(out_real, 0, 1).reshape(B, S, H * hd)
    out = jnp.concatenate([out_meta.reshape(B, N_META, H * hd), out_real], 1)
    return out @ w_o + b_o


def swiglu(x, wg, wu, wd):
    return (jax.nn.silu(x @ wg) * (x @ wu)) @ wd


def moe_ffn(x2d, router, bias, w_gate, w_up, w_down, sh_gate, sh_up, sh_down):
    T, D = x2d.shape
    scores = jax.nn.sigmoid(x2d.astype(F32) @ router.astype(F32))
    choice = scores + bias.astype(F32)
    grp = choice.reshape(T, N_GROUPS, N_EXPERTS // N_GROUPS)
    grp_score = jnp.sum(lax.top_k(grp, 2)[0], -1)
    _, top_g = lax.top_k(grp_score, TOPK_GROUPS)
    keep_g = jnp.any(top_g[:, :, None] == jnp.arange(N_GROUPS)[None, None, :], axis=1)
    keep = jnp.repeat(keep_g, N_EXPERTS // N_GROUPS, axis=1)
    _, idx = lax.top_k(jnp.where(keep, choice, -jnp.inf), TOP_K)
    gate = jnp.take_along_axis(scores, idx, axis=1)
    gate = gate / jnp.sum(gate, -1, keepdims=True) * ROUTED_SCALE
    A = T * TOP_K
    e_flat = idx.reshape(A)
    tok_flat = jnp.repeat(jnp.arange(T, dtype=jnp.int32), TOP_K)
    order = jnp.argsort(e_flat)
    e_sorted = e_flat[order]
    counts = jnp.zeros((N_EXPERTS,), jnp.int32).at[e_flat].add(1)
    padded = (counts + MOE_BLOCK - 1) // MOE_BLOCK * MOE_BLOCK
    start = jnp.cumsum(counts) - counts
    pend = jnp.cumsum(padded)
    pstart = pend - padded
    dest = pstart[e_sorted] + jnp.arange(A, dtype=jnp.int32) - start[e_sorted]
    nblk = -(-A // MOE_BLOCK) + N_EXPERTS
    P = nblk * MOE_BLOCK
    slot_tok = jnp.zeros((P,), jnp.int32).at[dest].set(tok_flat[order])
    slot_gate = jnp.zeros((P,), F32).at[dest].set(gate.reshape(A)[order])
    block_e = jnp.minimum(jnp.searchsorted(pend, jnp.arange(nblk) * MOE_BLOCK, side='right'),
                          N_EXPERTS - 1)

    def expert_block(args):
        tok, e, gt = args
        y = swiglu(x2d[tok], w_gate[e], w_up[e], w_down[e])
        return y * gt[:, None].astype(y.dtype)

    y = lax.map(expert_block, (slot_tok.reshape(nblk, MOE_BLOCK), block_e,
                               slot_gate.reshape(nblk, MOE_BLOCK)))
    routed = jax.ops.segment_sum(y.reshape(P, D), slot_tok, num_segments=T)
    return swiglu(x2d, sh_gate, sh_up, sh_down) + routed


def setup_inputs(seed: int = 0) -> dict:
    key = jax.random.key(seed)
    ks = iter(jax.random.split(key, 48))
    D, NA, NB = D_MODEL, N_RWKV_LAYERS, N_ATT_LAYERS
    H, N = RWKV_HEADS, RWKV_HEAD
    att_w = ATT_HEADS * ATT_HEAD_DIM

    def nrm(shape, scale):
        return jax.random.normal(next(ks), shape, F32) * scale

    def unif(shape, lo, hi):
        return jax.random.uniform(next(ks), shape, F32, lo, hi)

    return {
        'x': nrm((BATCH, SEQ, D), 1.0),
        'meta': nrm((N_META, D), 1.0),
        'rel_bias': nrm((N_BUCKETS, ATT_HEADS), 0.5),
        'rwkv_mu': unif((NA, 6, D), 0.0, 1.0),
        'rwkv_w0': unif((NA, D), -6.5, -1.5),
        'rwkv_w1': nrm((NA, D, DECAY_LORA), D ** -0.5),
        'rwkv_w2': nrm((NA, DECAY_LORA, D), 0.5 * DECAY_LORA ** -0.5),
        'rwkv_a0': nrm((NA, D), 0.1),
        'rwkv_a1': nrm((NA, D, AAA_LORA), D ** -0.5),
        'rwkv_a2': nrm((NA, AAA_LORA, D), AAA_LORA ** -0.5),
        'rwkv_g1': nrm((NA, D, GATE_LORA), D ** -0.5),
        'rwkv_g2': nrm((NA, GATE_LORA, D), GATE_LORA ** -0.5),
        'rwkv_k_k': 0.85 + nrm((NA, D), 0.05),
        'rwkv_k_a': 1.0 + nrm((NA, D), 0.05),
        'rwkv_r_k': nrm((NA, H, N), 0.1),
        'rwkv_w_r': nrm((NA, D, D), D ** -0.5),
        'rwkv_w_k': nrm((NA, D, D), D ** -0.5),
        'rwkv_w_v': nrm((NA, D, D), D ** -0.5),
        'rwkv_w_o': nrm((NA, D, D), DEEPNORM_BETA * D ** -0.5),
        'rwkv_lnx_g': 1.0 + nrm((NA, D), 0.02),
        'rwkv_lnx_b': nrm((NA, D), 0.02),
        'attn_w_qkv': nrm((NB, D, QKV_WIDTH), D ** -0.5),
        'attn_b_qkv': nrm((NB, QKV_WIDTH), 0.02),
        'attn_sinks': nrm((NB, ATT_HEADS), 0.5),
        'attn_w_o': nrm((NB, att_w, D), DEEPNORM_BETA * att_w ** -0.5),
        'attn_b_o': nrm((NB, D), 0.02),
        'ln_mix_g': 1.0 + nrm((DEPTH, D), 0.02),
        'ln_mix_b': nrm((DEPTH, D), 0.02),
        'ln_ffn_g': 1.0 + nrm((DEPTH, D), 0.02),
        'ln_ffn_b': nrm((DEPTH, D), 0.02),
        'moe_router': nrm((DEPTH, D, N_EXPERTS), D ** -0.5),
        'moe_bias': nrm((DEPTH, N_EXPERTS), 0.01),
        'moe_w_gate': nrm((DEPTH, N_EXPERTS, D, EXPERT_FF), D ** -0.5),
        'moe_w_up': nrm((DEPTH, N_EXPERTS, D, EXPERT_FF), D ** -0.5),
        'moe_w_down': nrm((DEPTH, N_EXPERTS, EXPERT_FF, D), DEEPNORM_BETA * EXPERT_FF ** -0.5),
        'shared_w_gate': nrm((DEPTH, D, SHARED_FF), D ** -0.5),
        'shared_w_up': nrm((DEPTH, D, SHARED_FF), D ** -0.5),
        'shared_w_down': nrm((DEPTH, SHARED_FF, D), DEEPNORM_BETA * SHARED_FF ** -0.5),
    }


def reference(x, meta, rel_bias,
              rwkv_mu, rwkv_w0, rwkv_w1, rwkv_w2, rwkv_a0, rwkv_a1, rwkv_a2, rwkv_g1, rwkv_g2,
              rwkv_k_k, rwkv_k_a, rwkv_r_k, rwkv_w_r, rwkv_w_k, rwkv_w_v, rwkv_w_o,
              rwkv_lnx_g, rwkv_lnx_b,
              attn_w_qkv, attn_b_qkv, attn_sinks, attn_w_o, attn_b_o,
              ln_mix_g, ln_mix_b, ln_ffn_g, ln_ffn_b,
              moe_router, moe_bias, moe_w_gate, moe_w_up, moe_w_down,
              shared_w_gate, shared_w_up, shared_w_down):
    B = x.shape[0]
    h = jnp.concatenate([jnp.broadcast_to(meta[None].astype(x.dtype), (B, N_META, D_MODEL)), x], axis=1)
    L = h.shape[1]
    for i in range(DEPTH):
        j = i // N_MIXERS
        if i % N_MIXERS == 0:
            mix = rwkv7_time_mix(h, rwkv_mu[j], rwkv_w0[j], rwkv_w1[j], rwkv_w2[j], rwkv_a0[j],
                                 rwkv_a1[j], rwkv_a2[j], rwkv_g1[j], rwkv_g2[j], rwkv_k_k[j],
                                 rwkv_k_a[j], rwkv_r_k[j], rwkv_w_r[j], rwkv_w_k[j], rwkv_w_v[j],
                                 rwkv_w_o[j], rwkv_lnx_g[j], rwkv_lnx_b[j])
        else:
            mix = swa_sink_attention(h, attn_w_qkv[j], attn_b_qkv[j], attn_sinks[j],
                                     attn_w_o[j], attn_b_o[j], rel_bias)
        h = layer_norm(DEEPNORM_ALPHA * h + mix, ln_mix_g[i], ln_mix_b[i])
        f = moe_ffn(h.reshape(B * L, D_MODEL), moe_router[i], moe_bias[i], moe_w_gate[i],
                    moe_w_up[i], moe_w_down[i], shared_w_gate[i], shared_w_up[i],
                    shared_w_down[i]).reshape(B, L, D_MODEL)
        h = layer_norm(DEEPNORM_ALPHA * h + f, ln_ffn_g[i], ln_ffn_b[i])
    return h[:, N_META:]
```

```python
import functools
import math

import jax
import jax.numpy as jnp
from jax import lax
from jax.experimental import pallas as pl
from jax.experimental.pallas import tpu as pltpu

F32 = jnp.float32
BF16 = jnp.bfloat16

D_MODEL = 1024
DEPTH = 2
N_META = 16
RWKV_HEAD = 64
RWKV_HEADS = D_MODEL // RWKV_HEAD
GN_EPS = 64e-5
ATT_HEADS = 16
ATT_KV_HEADS = 4
ATT_HEAD_DIM = D_MODEL // ATT_HEADS
ATT_GROUP = ATT_HEADS // ATT_KV_HEADS
WINDOW = 128
ATT_BLOCK = 128
N_BUCKETS = 32
MAX_DISTANCE = 128
N_EXPERTS = 64
TOP_K = 8
N_GROUPS = 8
TOPK_GROUPS = 4
EXPERT_FF = 256
ROUTED_SCALE = 2.5
DEEPNORM_ALPHA = (2 * DEPTH) ** 0.25
LN_EPS = 1e-5

LANES = 128
SUBLANES = 8
BF16_ROWS = 16
VMEM_LIMIT = 56 * 1024 * 1024

FRONT = 48
TM = 256
WKV_CHUNK = 64
WKV_PAIRS = 4
MOE_CH = BF16_ROWS
MOE_SLOTS = TM * TOP_K + N_EXPERTS * MOE_CH
MOE_CPT = MOE_SLOTS // MOE_CH
MOE_CB = 32
NEG = -1e30

_NN = (((1,), (0,)), ((), ()))
_NT = (((1,), (1,)), ((), ()))


def _dot(a, b, dn=_NN):
    return lax.dot_general(a.astype(BF16), b.astype(BF16), dn, preferred_element_type=F32)


def _split2(x):
    hi = x.astype(BF16)
    lo = (x - hi.astype(F32)).astype(BF16)
    return hi, lo


def _split3(x):
    h1 = x.astype(BF16)
    r1 = x - h1.astype(F32)
    h2 = r1.astype(BF16)
    h3 = (r1 - h2.astype(F32)).astype(BF16)
    return h1, h2, h3


def _dot_exact_lhs(a01, b, dn=_NN):
    a = a01.astype(BF16)
    return sum(lax.dot_general(a, p, dn, preferred_element_type=F32) for p in _split3(b))


def _dot_exact_rhs(a, b01, dn=_NN):
    b = b01.astype(BF16)
    return sum(lax.dot_general(p, b, dn, preferred_element_type=F32) for p in _split3(a))


def _dot3(a, b, dn=_NN):
    ah, al = _split2(a)
    bh, bl = _split2(b)
    d = lambda x, y: lax.dot_general(x, y, dn, preferred_element_type=F32)
    return d(ah, bh) + (d(ah, bl) + d(al, bh))


def _sigmoid(x):
    return 1.0 / (1.0 + jnp.exp(-x))


def _silu(x):
    return x * _sigmoid(x)


def _layer_norm(x, g, b):
    mu = jnp.mean(x, axis=-1, keepdims=True)
    xc = x - mu
    var = jnp.mean(xc * xc, axis=-1, keepdims=True)
    return xc * lax.rsqrt(var + LN_EPS) * g + b


def _full(shape):
    nd = len(shape)
    return pl.BlockSpec(shape, lambda *_: (0,) * nd)


def _params(*sem):
    return pltpu.CompilerParams(dimension_semantics=sem, vmem_limit_bytes=VMEM_LIMIT)


def _rwkv_proj_kernel(x_ref, xp_ref, mu_ref, vec_ref, wr_ref, wk_ref, wv_ref, w1_ref, w2_ref,
                      a1_ref, a2_ref, g1_ref, g2_ref, gsum_ref, gexp_ref,
                      r_out, w_out, k_out, v_out, kk_out, b_out, g_out, *, lp):
    i = pl.program_id(0)
    x = x_ref[...]
    row = lax.broadcasted_iota(jnp.int32, (TM, 1), 0)
    pos = lax.rem(i * TM, lp) + row
    pos = jnp.where(pos >= lp, pos - lp, pos)
    prev = jnp.where(row == 0, xp_ref[SUBLANES - 1:SUBLANES, :], pltpu.roll(x, 1, axis=0))
    prev = jnp.where(pos == FRONT, 0.0, prev)
    xx = prev - x
    valid = pos >= FRONT

    def mix(j):
        return (x + xx * mu_ref[j:j + 1, :]).astype(BF16)

    w0, a0, k_k, k_a = (vec_ref[j:j + 1, :] for j in range(4))
    r = _dot(mix(0), wr_ref[...])
    z = w0 + _dot(jnp.tanh(_dot(mix(1), w1_ref[...])), w2_ref[...])
    w = -math.exp(-0.5) * _sigmoid(z)
    k = _dot(mix(2), wk_ref[...])
    v = _dot(mix(3), wv_ref[...])
    a = _sigmoid(a0 + _dot(_dot(mix(4), a1_ref[...]), a2_ref[...]))
    g = _dot(_sigmoid(_dot(mix(5), g1_ref[...])), g2_ref[...])
    kk = k * k_k
    ssq = _dot_exact_rhs(kk * kk, gsum_ref[...])
    nrm = jnp.sqrt(_dot_exact_rhs(ssq, gexp_ref[...]))
    kk = kk / jnp.maximum(nrm, 1e-12)
    k = k * (1.0 + (a - 1.0) * k_a)
    r_out[...] = r
    w_out[...] = jnp.where(valid, w, 0.0)
    k_out[...] = jnp.where(valid, k, 0.0)
    v_out[...] = jnp.where(valid, v, 0.0)
    kk_out[...] = jnp.where(valid, kk, 0.0)
    b_out[...] = jnp.where(valid, kk * a, 0.0)
    g_out[...] = g


def _rwkv_proj(h2d, lp, mu, vec, wr, wk, wv, w1, w2, a1, a2, g1, g2, gsum, gexp):
    T, D = h2d.shape
    row_spec = pl.BlockSpec((TM, D), lambda i: (i, 0))
    prev_spec = pl.BlockSpec((SUBLANES, D), lambda i: (jnp.maximum(i * (TM // SUBLANES) - 1, 0), 0))
    ws = [mu, vec, wr, wk, wv, w1, w2, a1, a2, g1, g2, gsum, gexp]
    return pl.pallas_call(
        functools.partial(_rwkv_proj_kernel, lp=lp),
        grid=(T // TM,),
        in_specs=[row_spec, prev_spec] + [_full(w.shape) for w in ws],
        out_specs=[row_spec] * 7,
        out_shape=[jax.ShapeDtypeStruct((T, D), F32)] * 7,
        compiler_params=_params("parallel"),
        name="rwkv_proj",
    )(h2d, h2d, *ws)


def _wkv_pair_chunk(r, w, k, v, kk, b, S, c):
    C = WKV_CHUNK
    cum = _dot_exact_lhs(c["tril"], w)
    tot = cum[C - 1:C, :]
    cin = jnp.exp(cum)
    cex = jnp.exp(cum - w)
    inv = jnp.exp(-cum)
    dend = jnp.exp(tot - cum)
    ctot = jnp.exp(tot)

    def stack(x):
        return jnp.concatenate([jnp.where(c["m0"], x, 0.0), jnp.where(c["m0"], 0.0, x)], axis=0)

    kap_s = stack(kk * cex)
    rt_s = stack(r * cin)
    bt_s = stack(b * inv)
    kt_s = stack(k * inv)
    bh_s = stack(b * dend)
    kh_s = stack(k * dend)
    v_s = stack(v)

    lab = jnp.where(c["strict"], _dot(kap_s, bt_s, _NT), 0.0)
    lak = jnp.where(c["strict"], _dot(kap_s, kt_s, _NT), 0.0)
    arb = jnp.where(c["lowinc"], _dot(rt_s, bt_s, _NT), 0.0)
    ark = jnp.where(c["lowinc"], _dot(rt_s, kt_s, _NT), 0.0)

    eye = c["eye"]
    ld = jnp.where(c["blk"], lab, 0.0)
    nn = lab - ld
    ld2 = _dot3(ld, ld)
    ld4 = _dot3(ld2, ld2)
    ld8 = _dot3(ld4, ld4)
    dinv = _dot3(_dot3(_dot3(eye - ld, eye + ld2), eye + ld4), eye + ld8)
    zz = _dot3(dinv, nn)
    tt = _dot3(_dot3(eye - zz, eye + _dot3(zz, zz)), dinv)

    vs_t = v_s.T
    rhs_t = -(_dot(S, kap_s, _NT) + _dot(vs_t, lak, _NT))
    u_t = _dot(rhs_t, tt, _NT)
    ys_t = _dot(S, rt_s, _NT) + _dot(u_t, arb, _NT) + _dot(vs_t, ark, _NT)
    ys = ys_t.T
    y = ys[:C] + ys[C:]
    s_new = S * ctot + _dot(u_t, bh_s) + _dot(vs_t, kh_s)
    return y, s_new


def _wkv_kernel(r_ref, w_ref, k_ref, v_ref, kk_ref, b_ref, g_ref, prm_ref, o_ref, s_ref, *, lb):
    C = WKV_CHUNK

    @pl.when(pl.program_id(2) == 0)
    def _():
        s_ref[...] = jnp.zeros_like(s_ref)

    ri = lax.broadcasted_iota(jnp.int32, (2 * C, 2 * C), 0)
    ci = lax.broadcasted_iota(jnp.int32, (2 * C, 2 * C), 1)
    lane = lax.broadcasted_iota(jnp.int32, (1, LANES), 1)
    ti = lax.broadcasted_iota(jnp.int32, (C, C), 0)
    tj = lax.broadcasted_iota(jnp.int32, (C, C), 1)
    consts = dict(
        tril=(ti >= tj).astype(F32),
        m0=lane < RWKV_HEAD,
        strict=ri > ci,
        lowinc=ri >= ci,
        blk=(ri // 16) == (ci // 16),
        eye=(ri == ci).astype(F32),
    )
    m0 = consts["m0"]

    def head_sum(x):
        s0 = jnp.sum(jnp.where(m0, x, 0.0), axis=-1, keepdims=True)
        s1 = jnp.sum(jnp.where(m0, 0.0, x), axis=-1, keepdims=True)
        return jnp.where(m0, s0, s1)

    def chunk(ci_, carry):
        rows = pl.ds(pl.multiple_of(ci_ * C, C), C)
        for p in range(WKV_PAIRS):
            ln = slice(p * LANES, (p + 1) * LANES)
            r, w, k, v = r_ref[rows, ln], w_ref[rows, ln], k_ref[rows, ln], v_ref[rows, ln]
            kk, b = kk_ref[rows, ln], b_ref[rows, ln]
            y, s_new = _wkv_pair_chunk(r, w, k, v, kk, b, s_ref[p], consts)
            s_ref[p] = s_new
            r_k, lg, lb_ = prm_ref[0:1, ln], prm_ref[1:2, ln], prm_ref[2:3, ln]
            mean = head_sum(y) * (1.0 / RWKV_HEAD)
            yc = y - mean
            var = head_sum(yc * yc) * (1.0 / RWKV_HEAD)
            yn = yc * lax.rsqrt(var + GN_EPS) * lg + lb_
            bonus = head_sum(r * k * r_k) * v
            o_ref[rows, ln] = ((yn + bonus) * g_ref[rows, ln]).astype(o_ref.dtype)
        return carry

    lax.fori_loop(0, lb // C, chunk, 0)


def _wkv_row_block(lp):
    nch = lp // WKV_CHUNK
    for d in (11, 8, 6, 5, 4, 3, 2, 1):
        if nch % d == 0:
            return d * WKV_CHUNK
    return WKV_CHUNK


def _wkv(r, w, k, v, kk, b, g, prm):
    B, lp, D = r.shape
    lb = _wkv_row_block(lp)
    wl = WKV_PAIRS * LANES
    spec = pl.BlockSpec((None, lb, wl), lambda bi, pi, li: (bi, li, pi))
    return pl.pallas_call(
        functools.partial(_wkv_kernel, lb=lb),
        grid=(B, D // wl, lp // lb),
        in_specs=[spec] * 7 + [pl.BlockSpec((SUBLANES, wl), lambda bi, pi, li: (0, pi))],
        out_specs=spec,
        out_shape=jax.ShapeDtypeStruct((B, lp, D), BF16),
        scratch_shapes=[pltpu.VMEM((WKV_PAIRS, LANES, LANES), F32)],
        compiler_params=_params("parallel", "parallel", "arbitrary"),
        name="wkv7",
    )(r, w, k, v, kk, b, g, prm)


def _proj_ln_kernel(a_ref, w_ref, vec_ref, h_ref, o_ref):
    mix = _dot(a_ref[...], w_ref[...]) + vec_ref[0:1, :]
    o_ref[...] = _layer_norm(DEEPNORM_ALPHA * h_ref[...] + mix, vec_ref[1:2, :], vec_ref[2:3, :])


def _proj_ln(a2d, w, vec, h2d):
    T, D = h2d.shape
    K = a2d.shape[1]
    return pl.pallas_call(
        _proj_ln_kernel,
        grid=(T // TM,),
        in_specs=[pl.BlockSpec((TM, K), lambda i: (i, 0)), _full(w.shape), _full(vec.shape),
                  pl.BlockSpec((TM, D), lambda i: (i, 0))],
        out_specs=pl.BlockSpec((TM, D), lambda i: (i, 0)),
        out_shape=jax.ShapeDtypeStruct((T, D), F32),
        compiler_params=_params("parallel"),
        name="proj_ln",
    )(a2d, w, vec, h2d)


def _qkv_kernel(x_ref, wq_ref, wk_ref, wv_ref, bq_ref, bk_ref, bv_ref, q_out, k_out, v_out):
    x = x_ref[...].astype(BF16)
    q_out[...] = ((_dot(x, wq_ref[...]) + bq_ref[...]) * ATT_HEAD_DIM ** -0.5).astype(BF16)
    k_out[...] = (_dot(x, wk_ref[...]) + bk_ref[...]).astype(BF16)
    v_out[...] = (_dot(x, wv_ref[...]) + bv_ref[...]).astype(BF16)


def _qkv(h2d, wq, wkd, wvd, bq, bkd, bvd):
    T, D = h2d.shape
    nq, nk = wq.shape[1], wkd.shape[1]
    ws = [wq, wkd, wvd, bq, bkd, bvd]
    return pl.pallas_call(
        _qkv_kernel,
        grid=(T // TM,),
        in_specs=[pl.BlockSpec((TM, D), lambda i: (i, 0))] + [_full(w.shape) for w in ws],
        out_specs=[pl.BlockSpec((TM, nq), lambda i: (i, 0)), pl.BlockSpec((TM, nk), lambda i: (i, 0)),
                   pl.BlockSpec((TM, nk), lambda i: (i, 0))],
        out_shape=[jax.ShapeDtypeStruct((T, nq), BF16), jax.ShapeDtypeStruct((T, nk), BF16),
                   jax.ShapeDtypeStruct((T, nk), BF16)],
        compiler_params=_params("parallel"),
        name="qkv_proj",
    )(h2d, *ws)


def _attn_kernel(sink_ref, q_ref, k_ref, v_ref, bband_ref, bmeta_ref, bmm_ref, o_ref, *, nb):
    c = pl.program_id(1)
    BLK = ATT_BLOCK
    M0 = FRONT
    lane = lax.broadcasted_iota(jnp.int32, (1, LANES), 1)
    m0 = lane < ATT_HEAD_DIM
    qi = lax.broadcasted_iota(jnp.int32, (2 * BLK, BLK), 0) % BLK
    sj = lax.broadcasted_iota(jnp.int32, (2 * BLK, BLK), 1)
    cur_vis = sj <= qi
    prev_vis = sj > qi
    half = lax.broadcasted_iota(jnp.int32, (2 * BLK, 1), 0) < BLK
    eye2 = (lax.broadcasted_iota(jnp.int32, (2 * BLK, 2 * BLK), 0)
            == lax.broadcasted_iota(jnp.int32, (2 * BLK, 2 * BLK), 1)).astype(BF16)

    def stack(x):
        z = jnp.zeros_like(x)
        return jnp.concatenate([jnp.where(m0, x, z), jnp.where(m0, z, x)], axis=0)

    def unstack(x, n):
        return jnp.where(m0, x[:n], x[n:])

    def attend(segs, sink):
        m = sink
        for lg, _ in segs:
            m = jnp.maximum(m, jnp.max(lg, axis=-1, keepdims=True))
        den = jnp.exp(sink - m)
        acc = None
        for lg, vals in segs:
            p = jnp.exp(lg - m)
            den = den + jnp.sum(p, axis=-1, keepdims=True)
            t = _dot(p, vals)
            acc = t if acc is None else acc + t
        return acc / den

    o_ref[0:M0, :] = jnp.zeros((M0, o_ref.shape[1]), o_ref.dtype)

    for kvh in range(2):
        kl = slice(kvh * LANES, (kvh + 1) * LANES)
        k_meta = k_ref[M0:M0 + N_META, kl]
        v_meta = v_ref[M0:M0 + N_META, kl]
        for pr in range(2):
            hp = kvh * 2 + pr
            ql = slice(hp * LANES, (hp + 1) * LANES)
            s0 = sink_ref[c * 8 + hp * 2]
            s1 = sink_ref[c * 8 + hp * 2 + 1]

            qm = stack(q_ref[M0:M0 + N_META, ql])
            sink_m = jnp.where(lax.broadcasted_iota(jnp.int32, (2 * N_META, 1), 0) < N_META, s0, s1)
            mi = lax.broadcasted_iota(jnp.int32, (2 * N_META, N_META), 0) % N_META
            mj = lax.broadcasted_iota(jnp.int32, (2 * N_META, N_META), 1)
            lg = jnp.where(mj <= mi, _dot(qm, k_meta, _NT) + bmm_ref[hp], NEG)
            om = attend([(lg, v_meta)], sink_m)
            o_ref[M0:M0 + N_META, ql] = unstack(om, N_META).astype(o_ref.dtype)

            sink_q = jnp.where(half, s0, s1)

            def block(j, carry):
                start = pl.multiple_of(M0 + N_META + j * BLK, 64)
                pstart = pl.multiple_of(jnp.maximum(start - BLK, 0), 64)
                qs = stack(q_ref[pl.ds(start, BLK), ql])
                lg_meta = _dot(qs, k_meta, _NT) + _dot_exact_lhs(eye2, bmeta_ref[hp, j], _NT)
                lg_prev = _dot(qs, k_ref[pl.ds(pstart, BLK), kl], _NT) + bband_ref[hp, :, 0:BLK]
                lg_prev = jnp.where(prev_vis & (j > 0), lg_prev, NEG)
                lg_cur = _dot(qs, k_ref[pl.ds(start, BLK), kl], _NT) + bband_ref[hp, :, BLK:2 * BLK]
                lg_cur = jnp.where(cur_vis, lg_cur, NEG)
                out = attend([(lg_meta, v_meta), (lg_prev, v_ref[pl.ds(pstart, BLK), kl]),
                              (lg_cur, v_ref[pl.ds(start, BLK), kl])], sink_q)
                o_ref[pl.ds(start, BLK), ql] = unstack(out, BLK).astype(o_ref.dtype)
                return carry

            lax.fori_loop(0, nb, block, 0)


def _attention(q, kd, vd, sinks, bband, bmeta, bmm):
    B, lp, _ = q.shape
    nb = (lp - FRONT - N_META) // ATT_BLOCK
    qw, kw = 4 * LANES, 2 * LANES
    return pl.pallas_call(
        functools.partial(_attn_kernel, nb=nb),
        grid=(B, 2),
        in_specs=[pl.BlockSpec(memory_space=pltpu.SMEM),
                  pl.BlockSpec((None, lp, qw), lambda b, c: (b, 0, c)),
                  pl.BlockSpec((None, lp, kw), lambda b, c: (b, 0, c)),
                  pl.BlockSpec((None, lp, kw), lambda b, c: (b, 0, c)),
                  pl.BlockSpec((4,) + bband.shape[1:], lambda b, c: (c, 0, 0)),
                  pl.BlockSpec((4,) + bmeta.shape[1:], lambda b, c: (c, 0, 0, 0)),
                  pl.BlockSpec((4,) + bmm.shape[1:], lambda b, c: (c, 0, 0))],
        out_specs=pl.BlockSpec((None, lp, qw), lambda b, c: (b, 0, c)),
        out_shape=jax.ShapeDtypeStruct((B, lp, D_MODEL), BF16),
        compiler_params=_params("parallel", "parallel"),
        name="swa_attention",
    )(sinks, q, kd, vd, bband, bmeta, bmm)


def _t5_bucket(dist):
    exact = N_BUCKETS // 2
    d = jnp.maximum(dist, 0)
    ratio = jnp.log(jnp.maximum(d, 1).astype(F32) / exact) / math.log(MAX_DISTANCE / exact)
    large = jnp.minimum(exact + (ratio * (N_BUCKETS - exact)).astype(jnp.int32), N_BUCKETS - 1)
    return jnp.where(d < exact, d, large)


def _bias_tables(rel_bias, nb):
    H, BLK = ATT_HEADS, ATT_BLOCK
    qi = jnp.arange(BLK)[:, None]
    band = rel_bias[_t5_bucket(qi + BLK - jnp.arange(2 * BLK)[None, :])]
    band = jnp.moveaxis(band, -1, 0).reshape(H // 2, 2 * BLK, 2 * BLK)
    pos = jnp.arange(nb * BLK)[:, None]
    meta = rel_bias[_t5_bucket(N_META + pos - jnp.arange(N_META)[None, :])]
    meta = meta.reshape(nb, BLK, N_META, H // 2, 2).transpose(3, 0, 2, 4, 1)
    meta = meta.reshape(H // 2, nb, N_META, 2 * BLK)
    pm = jnp.arange(N_META)
    mm = rel_bias[_t5_bucket(pm[:, None] - pm[None, :])]
    mm = jnp.moveaxis(mm, -1, 0).reshape(H // 2, 2 * N_META, N_META)
    return band.astype(F32), meta.astype(F32), mm.astype(F32)


def _dispatch_geometry(a1):
    E = N_EXPERTS
    routed = (a1 > 0.0).astype(BF16)
    n_col = jnp.sum(routed.astype(F32), axis=1, keepdims=True)
    ch_col = jnp.floor((n_col + (MOE_CH - 1)) * (1.0 / MOE_CH))
    ei = lax.broadcasted_iota(jnp.int32, (E, E), 0)
    ej = lax.broadcasted_iota(jnp.int32, (E, E), 1)
    off_col = _dot((ei > ej).astype(BF16), jnp.broadcast_to(ch_col, (E, LANES)))[:, 0:1]
    n_row = _dot(jnp.ones((SUBLANES, TM), BF16), routed, _NT)
    ch_row = jnp.floor((n_row + (MOE_CH - 1)) * (1.0 / MOE_CH))
    off_row = _dot(ch_row, (ei < ej).astype(BF16))
    return ch_col, off_col, ch_row, off_row


def _router_kernel(x_ref, rt_ref, bias_ref, xs_out, gs_out, a1_out, a1t_out, cnt_out):
    E, G, EPG = N_EXPERTS, N_GROUPS, N_EXPERTS // N_GROUPS
    x = x_ref[...]
    logits = _dot3(rt_ref[...], x, _NT)
    scores = _sigmoid(logits)
    choice = scores + bias_ref[:, 0:1]

    grp = choice.reshape(G, EPG, TM)
    sub = lax.broadcasted_iota(jnp.int32, (G, EPG, TM), 1)
    top1 = jnp.max(grp, axis=1, keepdims=True)
    first = jnp.min(jnp.where(grp == top1, sub, EPG), axis=1, keepdims=True)
    top2 = jnp.max(jnp.where(sub == first, -jnp.inf, grp), axis=1, keepdims=True)
    gscore = jnp.broadcast_to(top1 + top2, (G, EPG, TM))

    gi = lax.broadcasted_iota(jnp.int32, (G, EPG, TM), 0)
    keep = jnp.zeros((G, EPG, TM), jnp.bool_)
    for _ in range(TOPK_GROUPS):
        m = jnp.max(gscore, axis=0, keepdims=True)
        sel = gi == jnp.min(jnp.where(gscore == m, gi, G), axis=0, keepdims=True)
        keep = keep | sel
        gscore = jnp.where(sel, -jnp.inf, gscore)
    keep_e = keep.reshape(E, TM)

    cand = jnp.where(keep_e, choice, -jnp.inf)
    ei = lax.broadcasted_iota(jnp.int32, (E, TM), 0)
    routed = jnp.zeros((E, TM), jnp.bool_)
    for _ in range(TOP_K):
        m = jnp.max(cand, axis=0, keepdims=True)
        sel = ei == jnp.min(jnp.where(cand == m, ei, E), axis=0, keepdims=True)
        routed = routed | sel
        cand = jnp.where(sel, -jnp.inf, cand)
    gate = jnp.where(routed, scores, 0.0)
    gate = gate / jnp.sum(gate, axis=0, keepdims=True) * ROUTED_SCALE

    ti = lax.broadcasted_iota(jnp.int32, (TM, TM), 0)
    tj = lax.broadcasted_iota(jnp.int32, (TM, TM), 1)
    routed_b = routed.astype(BF16)
    rank = _dot(routed_b, (ti < tj).astype(BF16))
    a1 = jnp.where(routed, rank + 1.0, 0.0)
    rank_t = _dot((tj < ti).astype(BF16), routed_b, _NT)
    a1t_out[...] = _dot((ti == tj).astype(BF16), routed_b, _NT) * (rank_t + 1.0)
    _, off_col, ch_row, off_row = _dispatch_geometry(a1)

    si = lax.broadcasted_iota(jnp.int32, (MOE_SLOTS, E), 0).astype(F32)
    lo = off_row[0:1, :] * MOE_CH
    hi = lo + ch_row[0:1, :] * MOE_CH
    esel = ((si >= lo) & (si < hi)).astype(BF16)
    seg0 = _dot(esel, jnp.broadcast_to(off_col, (E, TM))) * MOE_CH
    rs1 = lax.broadcasted_iota(jnp.int32, (MOE_SLOTS, TM), 0).astype(F32) - seg0 + 1.0
    perm = _dot(esel, a1) == rs1
    xs_out[...] = _dot(perm.astype(BF16), x).astype(BF16)
    gsel = _dot_exact_lhs(esel, gate)
    gs_out[...] = jnp.sum(jnp.where(perm, gsel, 0.0), axis=1, keepdims=True)
    a1_out[...] = a1
    cnt_out[...] = ch_row


def _router(h2d, router_t, bias_col):
    T, D = h2d.shape
    nt = T // TM
    return pl.pallas_call(
        _router_kernel,
        grid=(nt,),
        in_specs=[pl.BlockSpec((TM, D), lambda i: (i, 0)), _full(router_t.shape), _full(bias_col.shape)],
        out_specs=[pl.BlockSpec((None, MOE_SLOTS, D), lambda i: (i, 0, 0)),
                   pl.BlockSpec((None, MOE_SLOTS, 1), lambda i: (i, 0, 0)),
                   pl.BlockSpec((None, N_EXPERTS, TM), lambda i: (i, 0, 0)),
                   pl.BlockSpec((None, TM, N_EXPERTS), lambda i: (i, 0, 0)),
                   pl.BlockSpec((None, SUBLANES, N_EXPERTS), lambda i: (i, 0, 0))],
        out_shape=[jax.ShapeDtypeStruct((nt, MOE_SLOTS, D), BF16),
                   jax.ShapeDtypeStruct((nt, MOE_SLOTS, 1), F32),
                   jax.ShapeDtypeStruct((nt, N_EXPERTS, TM), F32),
                   jax.ShapeDtypeStruct((nt, TM, N_EXPERTS), F32),
                   jax.ShapeDtypeStruct((nt, SUBLANES, N_EXPERTS), F32)],
        compiler_params=_params("parallel"),
        name="moe_router",
    )(h2d, router_t, bias_col)


def _expert_kernel(be_ref, ps_ref, nb_ref, src_ref, xs_hbm, wgu_ref, wd_ref, ys_hbm,
                   xbuf, ybuf, in_sem, out_sem, *, nblk):
    b = pl.program_id(0)
    slot = lax.rem(b, 2)

    def in_copy(src_chunk, sl, c):
        return pltpu.make_async_copy(xs_hbm.at[src_chunk], xbuf.at[sl, c], in_sem.at[sl])

    def out_copy(dst_chunk, sl, c):
        return pltpu.make_async_copy(ybuf.at[sl, c], ys_hbm.at[dst_chunk], out_sem.at[sl])

    def for_chunks(blk, fn):
        p0 = ps_ref[blk]

        def body(c, carry):
            fn(src_ref[p0 + c], c)
            return carry

        lax.fori_loop(0, nb_ref[blk], body, 0)

    @pl.when(b == 0)
    def _():
        xbuf[...] = jnp.zeros_like(xbuf)
        for_chunks(0, lambda s, c: in_copy(s, 0, c).start())

    @pl.when(b + 1 < nblk)
    def _():
        for_chunks(b + 1, lambda s, c: in_copy(s, 1 - slot, c).start())

    for_chunks(b, lambda s, c: in_copy(s, slot, c).wait())

    @pl.when(b >= 2)
    def _():
        for_chunks(b - 2, lambda s, c: out_copy(s, slot, c).wait())

    @pl.when(nb_ref[b] > 0)
    def _():
        x = xbuf[slot].reshape(MOE_CB * MOE_CH, D_MODEL)
        hid = _dot(x, wgu_ref[...])
        act = _silu(hid[:, :EXPERT_FF]) * hid[:, EXPERT_FF:]
        y = _dot(act, wd_ref[...])
        ybuf[slot] = y.astype(BF16).reshape(MOE_CB, MOE_CH, D_MODEL)

    for_chunks(b, lambda s, c: out_copy(s, slot, c).start())

    @pl.when(b == nblk - 1)
    def _():
        for_chunks(b, lambda s, c: out_copy(s, slot, c).wait())

        @pl.when(b >= 1)
        def _():
            for_chunks(b - 1, lambda s, c: out_copy(s, 1 - slot, c).wait())


def _experts(xs, wgu, wd, be, pstart, nbc, src):
    nt = xs.shape[0]
    nblk = be.shape[0]
    xs_c = xs.reshape(nt * MOE_CPT, MOE_CH, D_MODEL)
    grid_spec = pltpu.PrefetchScalarGridSpec(
        num_scalar_prefetch=4,
        grid=(nblk,),
        in_specs=[pl.BlockSpec(memory_space=pl.ANY),
                  pl.BlockSpec((None, D_MODEL, 2 * EXPERT_FF), lambda b, be, ps, nb, src: (be[b], 0, 0)),
                  pl.BlockSpec((None, EXPERT_FF, D_MODEL), lambda b, be, ps, nb, src: (be[b], 0, 0))],
        out_specs=pl.BlockSpec(memory_space=pl.ANY),
        scratch_shapes=[pltpu.VMEM((2, MOE_CB, MOE_CH, D_MODEL), BF16),
                        pltpu.VMEM((2, MOE_CB, MOE_CH, D_MODEL), BF16),
                        pltpu.SemaphoreType.DMA((2,)), pltpu.SemaphoreType.DMA((2,))],
    )
    ys = pl.pallas_call(
        functools.partial(_expert_kernel, nblk=nblk),
        grid_spec=grid_spec,
        out_shape=jax.ShapeDtypeStruct(xs_c.shape, BF16),
        input_output_aliases={4: 0},
        compiler_params=_params("arbitrary"),
        name="moe_experts",
    )(be, pstart, nbc, src, xs_c, wgu, wd)
    return ys.reshape(nt, MOE_SLOTS, D_MODEL)


def _combine_kernel(h_ref, ys_ref, gs_ref, a1_ref, a1t_ref, wgu_ref, wd_ref, vec_ref, o_ref):
    E = N_EXPERTS
    h = h_ref[...]
    ch_col, off_col, _, off_row = _dispatch_geometry(a1_ref[...])
    si = lax.broadcasted_iota(jnp.int32, (E, MOE_SLOTS), 1).astype(F32)
    lo = off_col * MOE_CH
    esel_t = ((si >= lo) & (si < lo + ch_col * MOE_CH)).astype(BF16)
    seg0 = _dot(jnp.broadcast_to(off_row[0:1, :], (TM, E)), esel_t) * MOE_CH
    rs1 = lax.broadcasted_iota(jnp.int32, (TM, MOE_SLOTS), 1).astype(F32) - seg0 + 1.0
    perm_t = _dot(a1t_ref[...], esel_t) == rs1
    routed = _dot(perm_t.astype(BF16), ys_ref[...].astype(F32) * gs_ref[...])
    xb = h.astype(BF16)
    hid = _dot(xb, wgu_ref[...])
    shared = _dot(_silu(hid[:, :EXPERT_FF]) * hid[:, EXPERT_FF:], wd_ref[...])
    o_ref[...] = _layer_norm(DEEPNORM_ALPHA * h + (shared + routed), vec_ref[0:1, :], vec_ref[1:2, :])


def _combine(h2d, ys, gs, a1, a1t, wgu_s, wd_s, vec):
    T, D = h2d.shape
    nt = T // TM
    return pl.pallas_call(
        _combine_kernel,
        grid=(nt,),
        in_specs=[pl.BlockSpec((TM, D), lambda i: (i, 0)),
                  pl.BlockSpec((None, MOE_SLOTS, D), lambda i: (i, 0, 0)),
                  pl.BlockSpec((None, MOE_SLOTS, 1), lambda i: (i, 0, 0)),
                  pl.BlockSpec((None, N_EXPERTS, TM), lambda i: (i, 0, 0)),
                  pl.BlockSpec((None, TM, N_EXPERTS), lambda i: (i, 0, 0)),
                  _full(wgu_s.shape), _full(wd_s.shape), _full(vec.shape)],
        out_specs=pl.BlockSpec((TM, D), lambda i: (i, 0)),
        out_shape=jax.ShapeDtypeStruct((T, D), F32),
        compiler_params=_params("parallel"),
        name="moe_combine",
    )(h2d, ys, gs, a1, a1t, wgu_s, wd_s, vec)


def _dispatch_tables(chunks):
    nt, E = chunks.shape
    maxc = nt * MOE_CPT
    nblk = -(-maxc // MOE_CB) + E
    off = jnp.cumsum(chunks, axis=1) - chunks
    cnt_e = jnp.sum(chunks, axis=0)
    base_e = jnp.cumsum(cnt_e) - cnt_e
    within = jnp.cumsum(chunks, axis=0) - chunks
    seg_pos = (base_e[None, :] + within).T.reshape(-1)
    seg_src = (jnp.arange(nt, dtype=jnp.int32)[:, None] * MOE_CPT + off).T.reshape(-1)
    delta = jnp.repeat(seg_src - seg_pos, chunks.T.reshape(-1), total_repeat_length=maxc)
    src = jnp.clip(delta + jnp.arange(maxc, dtype=jnp.int32), 0, maxc - 1).astype(jnp.int32)
    nblk_e = (cnt_e + MOE_CB - 1) // MOE_CB
    bend = jnp.cumsum(nblk_e)
    bidx = jnp.arange(nblk, dtype=jnp.int32)
    be = jnp.minimum(jnp.searchsorted(bend, bidx, side="right"), E - 1).astype(jnp.int32)
    q0 = (bidx - (bend - nblk_e)[be]) * MOE_CB
    nbc = jnp.clip(cnt_e[be] - q0, 0, MOE_CB).astype(jnp.int32)
    pstart = jnp.clip(base_e[be] + q0, 0, maxc - MOE_CB).astype(jnp.int32)
    return be, pstart, nbc, src


def _moe_layer(h2d, router, bias, w_gate, w_up, w_down, sh_gate, sh_up, sh_down, ln_g, ln_b):
    router_t = router.T.astype(F32)
    bias_col = jnp.broadcast_to(bias.astype(F32)[:, None], (N_EXPERTS, LANES))
    xs, gs, a1, a1t, cnt = _router(h2d, router_t, bias_col)
    be, pstart, nbc, src = _dispatch_tables(cnt[:, 0, :].astype(jnp.int32))
    wgu = jnp.concatenate([w_gate, w_up], axis=-1).astype(BF16)
    ys = _experts(xs, wgu, w_down.astype(BF16), be, pstart, nbc, src)
    wgu_s = jnp.concatenate([sh_gate, sh_up], axis=-1).astype(BF16)
    vec = jnp.zeros((SUBLANES, D_MODEL), F32).at[0].set(ln_g).at[1].set(ln_b)
    return _combine(h2d, ys, gs, a1, a1t, wgu_s, sh_down.astype(BF16), vec)


def _rows(*vs):
    out = jnp.zeros((SUBLANES, vs[0].shape[-1]), F32)
    for i, v in enumerate(vs):
        out = out.at[i].set(v.astype(F32))
    return out


def kernel(x, meta, rel_bias, rwkv_mu, rwkv_w0, rwkv_w1, rwkv_w2, rwkv_a0, rwkv_a1, rwkv_a2, rwkv_g1, rwkv_g2, rwkv_k_k, rwkv_k_a, rwkv_r_k, rwkv_w_r, rwkv_w_k, rwkv_w_v, rwkv_w_o, rwkv_lnx_g, rwkv_lnx_b, attn_w_qkv, attn_b_qkv, attn_sinks, attn_w_o, attn_b_o, ln_mix_g, ln_mix_b, ln_ffn_g, ln_ffn_b, moe_router, moe_bias, moe_w_gate, moe_w_up, moe_w_down, shared_w_gate, shared_w_up, shared_w_down):
    B, S, D = x.shape
    assert D == D_MODEL and S % ATT_BLOCK == 0 and (FRONT + N_META) % WKV_CHUNK == 0
    lp = FRONT + N_META + S
    T = B * lp
    assert T % TM == 0 and lp >= TM
    h = jnp.concatenate([jnp.zeros((B, FRONT, D), x.dtype),
                         jnp.broadcast_to(meta[None].astype(x.dtype), (B, N_META, D)), x], axis=1)
    h = h.reshape(T, D)
    bf = lambda w: w.astype(BF16)

    H, N = RWKV_HEADS, RWKV_HEAD
    head_of = jnp.arange(D) // N
    gsum = (head_of[:, None] == jnp.arange(LANES)[None, :]).astype(BF16)
    gexp = gsum.T
    r, w, k, v, kk, b, g = _rwkv_proj(
        h, lp, _rows(*rwkv_mu[0]), _rows(rwkv_w0[0], rwkv_a0[0], rwkv_k_k[0], rwkv_k_a[0]),
        bf(rwkv_w_r[0]), bf(rwkv_w_k[0]), bf(rwkv_w_v[0]), bf(rwkv_w1[0]), bf(rwkv_w2[0]),
        bf(rwkv_a1[0]), bf(rwkv_a2[0]), bf(rwkv_g1[0]), bf(rwkv_g2[0]), gsum, gexp)
    to3 = lambda t: t.reshape(B, lp, D)
    prm = _rows(rwkv_r_k[0].reshape(D), rwkv_lnx_g[0], rwkv_lnx_b[0])
    o = _wkv(to3(r), to3(w), to3(k), to3(v), to3(kk), to3(b), to3(g), prm)
    h = _proj_ln(o.reshape(T, D), bf(rwkv_w_o[0]), _rows(jnp.zeros((D,), F32), ln_mix_g[0], ln_mix_b[0]), h)
    h = _moe_layer(h, moe_router[0], moe_bias[0], moe_w_gate[0], moe_w_up[0], moe_w_down[0],
                   shared_w_gate[0], shared_w_up[0], shared_w_down[0], ln_ffn_g[0], ln_ffn_b[0])

    HD, KV = ATT_HEAD_DIM, ATT_KV_HEADS
    qw = ATT_HEADS * HD
    wqkv, bqkv = attn_w_qkv[0], attn_b_qkv[0]
    dup = lambda t: jnp.concatenate([t.reshape(-1, KV, 1, HD)] * 2, axis=2).reshape(t.shape[0], 2 * KV * HD)
    wq, wkd, wvd = wqkv[:, :qw], dup(wqkv[:, qw:qw + KV * HD]), dup(wqkv[:, qw + KV * HD:])
    bq, bkd, bvd = bqkv[None, :qw], dup(bqkv[None, qw:qw + KV * HD]), dup(bqkv[None, qw + KV * HD:])
    q, kd, vd = _qkv(h, bf(wq), bf(wkd), bf(wvd), bq.astype(F32), bkd.astype(F32), bvd.astype(F32))
    bband, bmeta, bmm = _bias_tables(rel_bias, S // ATT_BLOCK)
    o = _attention(q.reshape(B, lp, qw), kd.reshape(B, lp, -1), vd.reshape(B, lp, -1),
                   attn_sinks[0].astype(F32), bband, bmeta, bmm)
    h = _proj_ln(o.reshape(T, D), bf(attn_w_o[0]), _rows(attn_b_o[0], ln_mix_g[1], ln_mix_b[1]), h)
    h = _moe_layer(h, moe_router[1], moe_bias[1], moe_w_gate[1], moe_w_up[1], moe_w_down[1],
                   shared_w_gate[1], shared_w_up[1], shared_w_down[1], ln_ffn_g[1], ln_ffn_b[1])
    return h.reshape(B, lp, D)[:, FRONT + N_META:]
```

```python
import functools
import math

import jax
import jax.numpy as jnp
from jax import lax
from jax.experimental import pallas as pl
from jax.experimental.pallas import tpu as pltpu

F32 = jnp.float32
BF16 = jnp.bfloat16

D_MODEL = 1024
DEPTH = 2
N_META = 16
RWKV_HEAD = 64
RWKV_HEADS = D_MODEL // RWKV_HEAD
GN_EPS = 64e-5
ATT_HEADS = 16
ATT_KV_HEADS = 4
ATT_HEAD_DIM = D_MODEL // ATT_HEADS
ATT_GROUP = ATT_HEADS // ATT_KV_HEADS
WINDOW = 128
ATT_BLOCK = 128
N_BUCKETS = 32
MAX_DISTANCE = 128
N_EXPERTS = 64
TOP_K = 8
N_GROUPS = 8
TOPK_GROUPS = 4
EXPERT_FF = 256
ROUTED_SCALE = 2.5
DEEPNORM_ALPHA = (2 * DEPTH) ** 0.25
LN_EPS = 1e-5

LANES = 128
SUBLANES = 8
BF16_ROWS = 16
VMEM_LIMIT = 56 * 1024 * 1024

FRONT = 48
TM = 256
WKV_CHUNK = 64
WKV_PAIRS = 8
MOE_CH = BF16_ROWS
MOE_SLOTS = TM * TOP_K + N_EXPERTS * MOE_CH
MOE_CPT = MOE_SLOTS // MOE_CH
MOE_CB = 32
NEG = -1e30

_NN = (((1,), (0,)), ((), ()))
_NT = (((1,), (1,)), ((), ()))


def _dot(a, b, dn=_NN):
    return lax.dot_general(a.astype(BF16), b.astype(BF16), dn, preferred_element_type=F32)


def _split2(x):
    hi = x.astype(BF16)
    lo = (x - hi.astype(F32)).astype(BF16)
    return hi, lo


def _split3(x):
    h1 = x.astype(BF16)
    r1 = x - h1.astype(F32)
    h2 = r1.astype(BF16)
    h3 = (r1 - h2.astype(F32)).astype(BF16)
    return h1, h2, h3


def _dot_exact_lhs(a01, b, dn=_NN):
    a = a01.astype(BF16)
    return sum(lax.dot_general(a, p, dn, preferred_element_type=F32) for p in _split3(b))


def _dot_exact_rhs(a, b01, dn=_NN):
    b = b01.astype(BF16)
    return sum(lax.dot_general(p, b, dn, preferred_element_type=F32) for p in _split3(a))


def _dot3(a, b, dn=_NN):
    ah, al = _split2(a)
    bh, bl = _split2(b)
    d = lambda x, y: lax.dot_general(x, y, dn, preferred_element_type=F32)
    return d(ah, bh) + (d(ah, bl) + d(al, bh))


def _sigmoid(x):
    return 1.0 / (1.0 + jnp.exp(-x))


def _silu(x):
    return x * _sigmoid(x)


def _layer_norm(x, g, b):
    mu = jnp.mean(x, axis=-1, keepdims=True)
    xc = x - mu
    var = jnp.mean(xc * xc, axis=-1, keepdims=True)
    return xc * lax.rsqrt(var + LN_EPS) * g + b


def _full(shape):
    nd = len(shape)
    return pl.BlockSpec(shape, lambda *_: (0,) * nd)


def _params(*sem):
    return pltpu.CompilerParams(dimension_semantics=sem, vmem_limit_bytes=VMEM_LIMIT)


def _rwkv_proj_kernel(x_ref, xp_ref, mu_ref, vec_ref, wr_ref, wk_ref, wv_ref, w1_ref, w2_ref,
                      a1_ref, a2_ref, g1_ref, g2_ref, gsum_ref, gexp_ref,
                      r_out, w_out, k_out, v_out, kk_out, b_out, g_out, *, lp):
    i = pl.program_id(0)
    x = x_ref[...]
    row = lax.broadcasted_iota(jnp.int32, (TM, 1), 0)
    pos = lax.rem(i * TM, lp) + row
    pos = jnp.where(pos >= lp, pos - lp, pos)
    prev = jnp.where(row == 0, xp_ref[SUBLANES - 1:SUBLANES, :], pltpu.roll(x, 1, axis=0))
    prev = jnp.where(pos == FRONT, 0.0, prev)
    xx = prev - x
    valid = pos >= FRONT

    def mix(j):
        return (x + xx * mu_ref[j:j + 1, :]).astype(BF16)

    w0, a0, k_k, k_a = (vec_ref[j:j + 1, :] for j in range(4))
    r = _dot(mix(0), wr_ref[...])
    z = w0 + _dot(jnp.tanh(_dot(mix(1), w1_ref[...])), w2_ref[...])
    w = -math.exp(-0.5) * _sigmoid(z)
    k = _dot(mix(2), wk_ref[...])
    v = _dot(mix(3), wv_ref[...])
    a = _sigmoid(a0 + _dot(_dot(mix(4), a1_ref[...]), a2_ref[...]))
    g = _dot(_sigmoid(_dot(mix(5), g1_ref[...])), g2_ref[...])
    kk = k * k_k
    ssq = _dot_exact_rhs(kk * kk, gsum_ref[...])
    nrm = jnp.sqrt(_dot_exact_rhs(ssq, gexp_ref[...]))
    kk = kk / jnp.maximum(nrm, 1e-12)
    k = k * (1.0 + (a - 1.0) * k_a)
    r_out[...] = r
    w_out[...] = jnp.where(valid, w, 0.0)
    k_out[...] = jnp.where(valid, k, 0.0)
    v_out[...] = jnp.where(valid, v, 0.0)
    kk_out[...] = jnp.where(valid, kk, 0.0)
    b_out[...] = jnp.where(valid, kk * a, 0.0)
    g_out[...] = g


def _rwkv_proj(h2d, lp, mu, vec, wr, wk, wv, w1, w2, a1, a2, g1, g2, gsum, gexp):
    T, D = h2d.shape
    row_spec = pl.BlockSpec((TM, D), lambda i: (i, 0))
    prev_spec = pl.BlockSpec((SUBLANES, D), lambda i: (jnp.maximum(i * (TM // SUBLANES) - 1, 0), 0))
    ws = [mu, vec, wr, wk, wv, w1, w2, a1, a2, g1, g2, gsum, gexp]
    return pl.pallas_call(
        functools.partial(_rwkv_proj_kernel, lp=lp),
        grid=(T // TM,),
        in_specs=[row_spec, prev_spec] + [_full(w.shape) for w in ws],
        out_specs=[row_spec] * 7,
        out_shape=[jax.ShapeDtypeStruct((T, D), F32)] * 7,
        compiler_params=_params("parallel"),
        name="rwkv_proj",
    )(h2d, h2d, *ws)


def _wkv_chunk(r, w, k, v, kk, b, S, c):
    C = WKV_CHUNK
    P = range(len(r))
    bf = lambda xs: [x.astype(BF16) for x in xs]
    each = lambda f, *ls: [f(*a) for a in zip(*ls)]

    def stack(x):
        return jnp.concatenate([jnp.where(c["m0"], x, 0.0), jnp.where(c["m0"], 0.0, x)], axis=0)

    cum = [_dot_exact_lhs(c["tril"], w[p]) for p in P]
    tot = [cum[p][C - 1:C, :] for p in P]
    kap_s = bf([stack(kk[p] * jnp.exp(cum[p] - w[p])) for p in P])
    rt_s = bf([stack(r[p] * jnp.exp(cum[p])) for p in P])
    inv = [jnp.exp(-cum[p]) for p in P]
    bt_s = bf([stack(b[p] * inv[p]) for p in P])
    kt_s = bf([stack(k[p] * inv[p]) for p in P])
    dend = [jnp.exp(tot[p] - cum[p]) for p in P]
    bh_s = bf([stack(b[p] * dend[p]) for p in P])
    kh_s = bf([stack(k[p] * dend[p]) for p in P])
    v_s = [stack(v[p]) for p in P]
    vs_t = bf([x.T for x in v_s])

    nt = lambda x, y: _dot(x, y, _NT)
    lab = [jnp.where(c["strict"], x, 0.0) for x in each(nt, kap_s, bt_s)]
    lak = bf([jnp.where(c["strict"], x, 0.0) for x in each(nt, kap_s, kt_s)])
    arb = bf([jnp.where(c["lowinc"], x, 0.0) for x in each(nt, rt_s, bt_s)])
    ark = bf([jnp.where(c["lowinc"], x, 0.0) for x in each(nt, rt_s, kt_s)])

    eye = c["eye"]
    ld = [jnp.where(c["blk"], x, 0.0) for x in lab]
    nn = bf([x - y for x, y in zip(lab, ld)])
    ldb = bf(ld)
    ld2 = each(_dot, ldb, ldb)
    ld2b = bf(ld2)
    ld4 = each(_dot, ld2b, ld2b)
    ld4b = bf(ld4)
    ld8 = each(_dot, ld4b, ld4b)
    d1 = each(_dot, bf([eye - x for x in ld]), bf([eye + x for x in ld2]))
    d2 = each(_dot, bf(d1), bf([eye + x for x in ld4]))
    dinv = bf(each(_dot, bf(d2), bf([eye + x for x in ld8])))
    zz = each(_dot, dinv, nn)
    zzb = bf(zz)
    z2 = each(_dot, zzb, zzb)
    q = each(_dot, bf([eye - x for x in zz]), bf([eye + x for x in z2]))
    tt = bf(each(_dot, bf(q), dinv))

    sb = bf(S)
    vl = each(nt, vs_t, lak)
    yv = each(nt, vs_t, ark)
    sv = each(_dot, vs_t, kh_s)
    rhs_t = [-(x + y) for x, y in zip(each(nt, sb, kap_s), vl)]
    u_t = bf(each(nt, bf(rhs_t), tt))
    s_new = [S[p] * jnp.exp(tot[p]) + x + sv[p] for p, x in enumerate(each(_dot, u_t, bh_s))]
    ys_t = [x + y + z for x, y, z in zip(each(nt, sb, rt_s), each(nt, u_t, arb), yv)]
    y = []
    for x in ys_t:
        ys = x.T
        y.append(ys[:C] + ys[C:])
    return y, s_new


def _wkv_kernel(r_ref, w_ref, k_ref, v_ref, kk_ref, b_ref, g_ref, prm_ref, o_ref, *s_refs, lb):
    C = WKV_CHUNK

    @pl.when(pl.program_id(2) == 0)
    def _():
        for s_ref in s_refs:
            s_ref[...] = jnp.zeros_like(s_ref)

    ri = lax.broadcasted_iota(jnp.int32, (2 * C, 2 * C), 0)
    ci = lax.broadcasted_iota(jnp.int32, (2 * C, 2 * C), 1)
    lane = lax.broadcasted_iota(jnp.int32, (1, LANES), 1)
    ti = lax.broadcasted_iota(jnp.int32, (C, C), 0)
    tj = lax.broadcasted_iota(jnp.int32, (C, C), 1)
    consts = dict(
        tril=(ti >= tj).astype(F32),
        m0=lane < RWKV_HEAD,
        strict=ri > ci,
        lowinc=ri >= ci,
        blk=(ri // 16) == (ci // 16),
        eye=(ri == ci).astype(F32),
    )
    m0 = consts["m0"]

    def head_sum(x):
        s0 = jnp.sum(jnp.where(m0, x, 0.0), axis=-1, keepdims=True)
        s1 = jnp.sum(jnp.where(m0, 0.0, x), axis=-1, keepdims=True)
        return jnp.where(m0, s0, s1)

    def chunk(ci_, carry):
        rows = pl.ds(pl.multiple_of(ci_ * C, C), C)
        lns = [slice(p * LANES, (p + 1) * LANES) for p in range(WKV_PAIRS)]
        ld = lambda ref: [ref[rows, ln] for ln in lns]
        r, w, k, v, kk, b = ld(r_ref), ld(w_ref), ld(k_ref), ld(v_ref), ld(kk_ref), ld(b_ref)
        ys, s_new = _wkv_chunk(r, w, k, v, kk, b, [s[...] for s in s_refs], consts)
        for p, ln in enumerate(lns):
            s_refs[p][...] = s_new[p]
            y = ys[p]
            r_k, lg, lb_ = prm_ref[0:1, ln], prm_ref[1:2, ln], prm_ref[2:3, ln]
            mean = head_sum(y) * (1.0 / RWKV_HEAD)
            yc = y - mean
            var = head_sum(yc * yc) * (1.0 / RWKV_HEAD)
            yn = yc * lax.rsqrt(var + GN_EPS) * lg + lb_
            bonus = head_sum(r[p] * k[p] * r_k) * v[p]
            o_ref[rows, ln] = ((yn + bonus) * g_ref[rows, ln]).astype(o_ref.dtype)
        return carry

    lax.fori_loop(0, lb // C, chunk, 0)


def _wkv_row_block(lp):
    nch = lp // WKV_CHUNK
    for d in (3, 4, 2, 5, 1):
        if nch % d == 0:
            return d * WKV_CHUNK
    return WKV_CHUNK


def _wkv(r, w, k, v, kk, b, g, prm):
    B, lp, D = r.shape
    lb = _wkv_row_block(lp)
    wl = WKV_PAIRS * LANES
    spec = pl.BlockSpec((None, lb, wl), lambda bi, pi, li: (bi, li, pi))
    return pl.pallas_call(
        functools.partial(_wkv_kernel, lb=lb),
        grid=(B, D // wl, lp // lb),
        in_specs=[spec] * 7 + [pl.BlockSpec((SUBLANES, wl), lambda bi, pi, li: (0, pi))],
        out_specs=spec,
        out_shape=jax.ShapeDtypeStruct((B, lp, D), BF16),
        scratch_shapes=[pltpu.VMEM((LANES, LANES), F32)] * WKV_PAIRS,
        compiler_params=_params("parallel", "parallel", "arbitrary"),
        name="wkv7",
    )(r, w, k, v, kk, b, g, prm)


def _proj_ln_kernel(a_ref, w_ref, vec_ref, h_ref, o_ref):
    mix = _dot(a_ref[...], w_ref[...]) + vec_ref[0:1, :]
    o_ref[...] = _layer_norm(DEEPNORM_ALPHA * h_ref[...] + mix, vec_ref[1:2, :], vec_ref[2:3, :])


def _proj_ln(a2d, w, vec, h2d):
    T, D = h2d.shape
    K = a2d.shape[1]
    return pl.pallas_call(
        _proj_ln_kernel,
        grid=(T // TM,),
        in_specs=[pl.BlockSpec((TM, K), lambda i: (i, 0)), _full(w.shape), _full(vec.shape),
                  pl.BlockSpec((TM, D), lambda i: (i, 0))],
        out_specs=pl.BlockSpec((TM, D), lambda i: (i, 0)),
        out_shape=jax.ShapeDtypeStruct((T, D), F32),
        compiler_params=_params("parallel"),
        name="proj_ln",
    )(a2d, w, vec, h2d)


def _qkv_kernel(x_ref, wq_ref, wk_ref, wv_ref, bq_ref, bk_ref, bv_ref, q_out, k_out, v_out):
    x = x_ref[...].astype(BF16)
    q_out[...] = ((_dot(x, wq_ref[...]) + bq_ref[...]) * ATT_HEAD_DIM ** -0.5).astype(BF16)
    k_out[...] = (_dot(x, wk_ref[...]) + bk_ref[...]).astype(BF16)
    v_out[...] = (_dot(x, wv_ref[...]) + bv_ref[...]).astype(BF16)


def _qkv(h2d, wq, wkd, wvd, bq, bkd, bvd):
    T, D = h2d.shape
    nq, nk = wq.shape[1], wkd.shape[1]
    ws = [wq, wkd, wvd, bq, bkd, bvd]
    return pl.pallas_call(
        _qkv_kernel,
        grid=(T // TM,),
        in_specs=[pl.BlockSpec((TM, D), lambda i: (i, 0))] + [_full(w.shape) for w in ws],
        out_specs=[pl.BlockSpec((TM, nq), lambda i: (i, 0)), pl.BlockSpec((TM, nk), lambda i: (i, 0)),
                   pl.BlockSpec((TM, nk), lambda i: (i, 0))],
        out_shape=[jax.ShapeDtypeStruct((T, nq), BF16), jax.ShapeDtypeStruct((T, nk), BF16),
                   jax.ShapeDtypeStruct((T, nk), BF16)],
        compiler_params=_params("parallel"),
        name="qkv_proj",
    )(h2d, *ws)


def _attn_kernel(sink_ref, q_ref, k_ref, v_ref, bband_ref, bmeta_ref, bmm_ref, o_ref, *, nb):
    c = pl.program_id(1)
    BLK = ATT_BLOCK
    M0 = FRONT
    lane = lax.broadcasted_iota(jnp.int32, (1, LANES), 1)
    m0 = lane < ATT_HEAD_DIM
    qi = lax.broadcasted_iota(jnp.int32, (2 * BLK, BLK), 0) % BLK
    sj = lax.broadcasted_iota(jnp.int32, (2 * BLK, BLK), 1)
    cur_vis = sj <= qi
    prev_vis = sj > qi
    half = lax.broadcasted_iota(jnp.int32, (2 * BLK, 1), 0) < BLK
    eye2 = (lax.broadcasted_iota(jnp.int32, (2 * BLK, 2 * BLK), 0)
            == lax.broadcasted_iota(jnp.int32, (2 * BLK, 2 * BLK), 1)).astype(BF16)

    def stack(x):
        z = jnp.zeros_like(x)
        return jnp.concatenate([jnp.where(m0, x, z), jnp.where(m0, z, x)], axis=0)

    def unstack(x, n):
        return jnp.where(m0, x[:n], x[n:])

    HP = range(4)
    kl = [slice((hp // 2) * LANES, (hp // 2 + 1) * LANES) for hp in HP]
    ql = [slice(hp * LANES, (hp + 1) * LANES) for hp in HP]

    def attend(segs, sinks):
        mx = [functools.reduce(jnp.maximum, [jnp.max(lg, axis=-1, keepdims=True) for lg, _ in segs[hp]],
                               sinks[hp]) for hp in HP]
        ps = [[jnp.exp(lg - mx[hp]) for lg, _ in segs[hp]] for hp in HP]
        den = [sum(jnp.sum(p, axis=-1, keepdims=True) for p in ps[hp]) + jnp.exp(sinks[hp] - mx[hp])
               for hp in HP]
        acc = [sum(_dot(p, vals) for p, (_, vals) in zip(ps[hp], segs[hp])) for hp in HP]
        return [a / d for a, d in zip(acc, den)]

    o_ref[0:M0, :] = jnp.zeros((M0, o_ref.shape[1]), o_ref.dtype)
    s0 = [sink_ref[c * 8 + hp * 2] for hp in HP]
    s1 = [sink_ref[c * 8 + hp * 2 + 1] for hp in HP]
    k_meta = [k_ref[M0:M0 + N_META, kl[hp]] for hp in HP]
    v_meta = [v_ref[M0:M0 + N_META, kl[hp]] for hp in HP]

    first = lax.broadcasted_iota(jnp.int32, (2 * N_META, 1), 0) < N_META
    mi = lax.broadcasted_iota(jnp.int32, (2 * N_META, N_META), 0) % N_META
    mj = lax.broadcasted_iota(jnp.int32, (2 * N_META, N_META), 1)
    qm = [stack(q_ref[M0:M0 + N_META, ql[hp]]) for hp in HP]
    lg = [jnp.where(mj <= mi, _dot(qm[hp], k_meta[hp], _NT) + bmm_ref[hp], NEG) for hp in HP]
    om = attend([[(lg[hp], v_meta[hp])] for hp in HP], [jnp.where(first, s0[hp], s1[hp]) for hp in HP])
    for hp in HP:
        o_ref[M0:M0 + N_META, ql[hp]] = unstack(om[hp], N_META).astype(o_ref.dtype)

    sink_q = [jnp.where(half, s0[hp], s1[hp]) for hp in HP]

    def block(j, carry):
        start = pl.multiple_of(M0 + N_META + j * BLK, 64)
        pstart = pl.multiple_of(jnp.maximum(start - BLK, 0), 64)
        qs = [stack(q_ref[pl.ds(start, BLK), ql[hp]]) for hp in HP]
        k_prev = [k_ref[pl.ds(pstart, BLK), kl[hp]] for hp in HP]
        k_cur = [k_ref[pl.ds(start, BLK), kl[hp]] for hp in HP]
        lg_meta = [_dot(qs[hp], k_meta[hp], _NT) + _dot_exact_lhs(eye2, bmeta_ref[hp, j], _NT) for hp in HP]
        lg_prev = [jnp.where(prev_vis & (j > 0), _dot(qs[hp], k_prev[hp], _NT) + bband_ref[hp, :, 0:BLK], NEG)
                   for hp in HP]
        lg_cur = [jnp.where(cur_vis, _dot(qs[hp], k_cur[hp], _NT) + bband_ref[hp, :, BLK:2 * BLK], NEG)
                  for hp in HP]
        out = attend([[(lg_meta[hp], v_meta[hp]), (lg_prev[hp], v_ref[pl.ds(pstart, BLK), kl[hp]]),
                       (lg_cur[hp], v_ref[pl.ds(start, BLK), kl[hp]])] for hp in HP], sink_q)
        for hp in HP:
            o_ref[pl.ds(start, BLK), ql[hp]] = unstack(out[hp], BLK).astype(o_ref.dtype)
        return carry

    lax.fori_loop(0, nb, block, 0)


def _attention(q, kd, vd, sinks, bband, bmeta, bmm):
    B, lp, _ = q.shape
    nb = (lp - FRONT - N_META) // ATT_BLOCK
    qw, kw = 4 * LANES, 2 * LANES
    return pl.pallas_call(
        functools.partial(_attn_kernel, nb=nb),
        grid=(B, 2),
        in_specs=[pl.BlockSpec(memory_space=pltpu.SMEM),
                  pl.BlockSpec((None, lp, qw), lambda b, c: (b, 0, c)),
                  pl.BlockSpec((None, lp, kw), lambda b, c: (b, 0, c)),
                  pl.BlockSpec((None, lp, kw), lambda b, c: (b, 0, c)),
                  pl.BlockSpec((4,) + bband.shape[1:], lambda b, c: (c, 0, 0)),
                  pl.BlockSpec((4,) + bmeta.shape[1:], lambda b, c: (c, 0, 0, 0)),
                  pl.BlockSpec((4,) + bmm.shape[1:], lambda b, c: (c, 0, 0))],
        out_specs=pl.BlockSpec((None, lp, qw), lambda b, c: (b, 0, c)),
        out_shape=jax.ShapeDtypeStruct((B, lp, D_MODEL), BF16),
        compiler_params=_params("parallel", "parallel"),
        name="swa_attention",
    )(sinks, q, kd, vd, bband, bmeta, bmm)


def _t5_bucket(dist):
    exact = N_BUCKETS // 2
    d = jnp.maximum(dist, 0)
    ratio = jnp.log(jnp.maximum(d, 1).astype(F32) / exact) / math.log(MAX_DISTANCE / exact)
    large = jnp.minimum(exact + (ratio * (N_BUCKETS - exact)).astype(jnp.int32), N_BUCKETS - 1)
    return jnp.where(d < exact, d, large)


def _bias_tables(rel_bias, nb):
    H, BLK = ATT_HEADS, ATT_BLOCK
    qi = jnp.arange(BLK)[:, None]
    band = rel_bias[_t5_bucket(qi + BLK - jnp.arange(2 * BLK)[None, :])]
    band = jnp.moveaxis(band, -1, 0).reshape(H // 2, 2 * BLK, 2 * BLK)
    pos = jnp.arange(nb * BLK)[:, None]
    meta = rel_bias[_t5_bucket(N_META + pos - jnp.arange(N_META)[None, :])]
    meta = meta.reshape(nb, BLK, N_META, H // 2, 2).transpose(3, 0, 2, 4, 1)
    meta = meta.reshape(H // 2, nb, N_META, 2 * BLK)
    pm = jnp.arange(N_META)
    mm = rel_bias[_t5_bucket(pm[:, None] - pm[None, :])]
    mm = jnp.moveaxis(mm, -1, 0).reshape(H // 2, 2 * N_META, N_META)
    return band.astype(F32), meta.astype(F32), mm.astype(F32)


def _dispatch_geometry(a1):
    E = N_EXPERTS
    routed = (a1 > 0.0).astype(BF16)
    n_col = jnp.sum(routed.astype(F32), axis=1, keepdims=True)
    ch_col = jnp.floor((n_col + (MOE_CH - 1)) * (1.0 / MOE_CH))
    ei = lax.broadcasted_iota(jnp.int32, (E, E), 0)
    ej = lax.broadcasted_iota(jnp.int32, (E, E), 1)
    off_col = _dot((ei > ej).astype(BF16), jnp.broadcast_to(ch_col, (E, LANES)))[:, 0:1]
    n_row = _dot(jnp.ones((SUBLANES, TM), BF16), routed, _NT)
    ch_row = jnp.floor((n_row + (MOE_CH - 1)) * (1.0 / MOE_CH))
    off_row = _dot(ch_row, (ei < ej).astype(BF16))
    return ch_col, off_col, ch_row, off_row


def _router_kernel(x_ref, rt_ref, bias_ref, xs_out, gs_out, a1_out, a1t_out, cnt_out):
    E, G, EPG = N_EXPERTS, N_GROUPS, N_EXPERTS // N_GROUPS
    x = x_ref[...]
    logits = _dot3(rt_ref[...], x, _NT)
    scores = _sigmoid(logits)
    choice = scores + bias_ref[:, 0:1]

    grp = choice.reshape(G, EPG, TM)
    sub = lax.broadcasted_iota(jnp.int32, (G, EPG, TM), 1)
    top1 = jnp.max(grp, axis=1, keepdims=True)
    first = jnp.min(jnp.where(grp == top1, sub, EPG), axis=1, keepdims=True)
    top2 = jnp.max(jnp.where(sub == first, -jnp.inf, grp), axis=1, keepdims=True)
    gscore = jnp.broadcast_to(top1 + top2, (G, EPG, TM))

    gi = lax.broadcasted_iota(jnp.int32, (G, EPG, TM), 0)
    keep = jnp.zeros((G, EPG, TM), jnp.bool_)
    for _ in range(TOPK_GROUPS):
        m = jnp.max(gscore, axis=0, keepdims=True)
        sel = gi == jnp.min(jnp.where(gscore == m, gi, G), axis=0, keepdims=True)
        keep = keep | sel
        gscore = jnp.where(sel, -jnp.inf, gscore)
    keep_e = keep.reshape(E, TM)

    cand = jnp.where(keep_e, choice, -jnp.inf)
    ei = lax.broadcasted_iota(jnp.int32, (E, TM), 0)
    routed = jnp.zeros((E, TM), jnp.bool_)
    for _ in range(TOP_K):
        m = jnp.max(cand, axis=0, keepdims=True)
        sel = ei == jnp.min(jnp.where(cand == m, ei, E), axis=0, keepdims=True)
        routed = routed | sel
        cand = jnp.where(sel, -jnp.inf, cand)
    gate = jnp.where(routed, scores, 0.0)
    gate = gate / jnp.sum(gate, axis=0, keepdims=True) * ROUTED_SCALE

    ti = lax.broadcasted_iota(jnp.int32, (TM, TM), 0)
    tj = lax.broadcasted_iota(jnp.int32, (TM, TM), 1)
    routed_b = routed.astype(BF16)
    rank = _dot(routed_b, (ti < tj).astype(BF16))
    a1 = jnp.where(routed, rank + 1.0, 0.0)
    rank_t = _dot((tj < ti).astype(BF16), routed_b, _NT)
    a1t_out[...] = _dot((ti == tj).astype(BF16), routed_b, _NT) * (rank_t + 1.0)
    _, off_col, ch_row, off_row = _dispatch_geometry(a1)

    si = lax.broadcasted_iota(jnp.int32, (MOE_SLOTS, E), 0).astype(F32)
    lo = off_row[0:1, :] * MOE_CH
    hi = lo + ch_row[0:1, :] * MOE_CH
    esel = ((si >= lo) & (si < hi)).astype(BF16)
    seg0 = _dot(esel, jnp.broadcast_to(off_col, (E, TM))) * MOE_CH
    rs1 = lax.broadcasted_iota(jnp.int32, (MOE_SLOTS, TM), 0).astype(F32) - seg0 + 1.0
    perm = _dot(esel, a1) == rs1
    xs_out[...] = _dot(perm.astype(BF16), x).astype(BF16)
    gsel = _dot_exact_lhs(esel, gate)
    gs_out[...] = jnp.sum(jnp.where(perm, gsel, 0.0), axis=1, keepdims=True)
    a1_out[...] = a1
    cnt_out[...] = ch_row


def _router(h2d, router_t, bias_col):
    T, D = h2d.shape
    nt = T // TM
    return pl.pallas_call(
        _router_kernel,
        grid=(nt,),
        in_specs=[pl.BlockSpec((TM, D), lambda i: (i, 0)), _full(router_t.shape), _full(bias_col.shape)],
        out_specs=[pl.BlockSpec((None, MOE_SLOTS, D), lambda i: (i, 0, 0)),
                   pl.BlockSpec((None, MOE_SLOTS, 1), lambda i: (i, 0, 0)),
                   pl.BlockSpec((None, N_EXPERTS, TM), lambda i: (i, 0, 0)),
                   pl.BlockSpec((None, TM, N_EXPERTS), lambda i: (i, 0, 0)),
                   pl.BlockSpec((None, SUBLANES, N_EXPERTS), lambda i: (i, 0, 0))],
        out_shape=[jax.ShapeDtypeStruct((nt, MOE_SLOTS, D), BF16),
                   jax.ShapeDtypeStruct((nt, MOE_SLOTS, 1), F32),
                   jax.ShapeDtypeStruct((nt, N_EXPERTS, TM), F32),
                   jax.ShapeDtypeStruct((nt, TM, N_EXPERTS), F32),
                   jax.ShapeDtypeStruct((nt, SUBLANES, N_EXPERTS), F32)],
        compiler_params=_params("parallel"),
        name="moe_router",
    )(h2d, router_t, bias_col)


def _expert_kernel(be_ref, ps_ref, nb_ref, src_ref, xs_hbm, wg_ref, wu_ref, wd_ref, ys_hbm,
                   xbuf, ybuf, wgu_bf, wd_bf, in_sem, out_sem, *, nblk):
    b = pl.program_id(0)
    slot = lax.rem(b, 2)

    def in_copy(src_chunk, sl, c):
        return pltpu.make_async_copy(xs_hbm.at[src_chunk], xbuf.at[sl, c], in_sem.at[sl])

    def out_copy(dst_chunk, sl, c):
        return pltpu.make_async_copy(ybuf.at[sl, c], ys_hbm.at[dst_chunk], out_sem.at[sl])

    def for_chunks(blk, fn):
        p0 = ps_ref[blk]

        def body(c, carry):
            fn(src_ref[p0 + c], c)
            return carry

        lax.fori_loop(0, nb_ref[blk], body, 0)

    @pl.when(b == 0)
    def _():
        xbuf[...] = jnp.zeros_like(xbuf)
        for_chunks(0, lambda s, c: in_copy(s, 0, c).start())

    @pl.when(b + 1 < nblk)
    def _():
        for_chunks(b + 1, lambda s, c: in_copy(s, 1 - slot, c).start())

    for_chunks(b, lambda s, c: in_copy(s, slot, c).wait())

    @pl.when(b >= 2)
    def _():
        for_chunks(b - 2, lambda s, c: out_copy(s, slot, c).wait())

    @pl.when((b == 0) | (be_ref[b] != be_ref[jnp.maximum(b - 1, 0)]))
    def _():
        wgu_bf[:, :EXPERT_FF] = wg_ref[...].astype(BF16)
        wgu_bf[:, EXPERT_FF:] = wu_ref[...].astype(BF16)
        wd_bf[...] = wd_ref[...].astype(BF16)

    @pl.when(nb_ref[b] > 0)
    def _():
        x = xbuf[slot].reshape(MOE_CB * MOE_CH, D_MODEL)
        hid = _dot(x, wgu_bf[...])
        act = _silu(hid[:, :EXPERT_FF]) * hid[:, EXPERT_FF:]
        y = _dot(act, wd_bf[...])
        ybuf[slot] = y.astype(BF16).reshape(MOE_CB, MOE_CH, D_MODEL)

    for_chunks(b, lambda s, c: out_copy(s, slot, c).start())

    @pl.when(b == nblk - 1)
    def _():
        for_chunks(b, lambda s, c: out_copy(s, slot, c).wait())

        @pl.when(b >= 1)
        def _():
            for_chunks(b - 1, lambda s, c: out_copy(s, 1 - slot, c).wait())


def _experts(xs, wg, wu, wd, be, pstart, nbc, src):
    nt = xs.shape[0]
    nblk = be.shape[0]
    xs_c = xs.reshape(nt * MOE_CPT, MOE_CH, D_MODEL)
    by_expert = lambda b, be, ps, nb, src: (be[b], 0, 0)
    grid_spec = pltpu.PrefetchScalarGridSpec(
        num_scalar_prefetch=4,
        grid=(nblk,),
        in_specs=[pl.BlockSpec(memory_space=pl.ANY),
                  pl.BlockSpec((None, D_MODEL, EXPERT_FF), by_expert),
                  pl.BlockSpec((None, D_MODEL, EXPERT_FF), by_expert),
                  pl.BlockSpec((None, EXPERT_FF, D_MODEL), by_expert)],
        out_specs=pl.BlockSpec(memory_space=pl.ANY),
        scratch_shapes=[pltpu.VMEM((2, MOE_CB, MOE_CH, D_MODEL), BF16),
                        pltpu.VMEM((2, MOE_CB, MOE_CH, D_MODEL), BF16),
                        pltpu.VMEM((D_MODEL, 2 * EXPERT_FF), BF16),
                        pltpu.VMEM((EXPERT_FF, D_MODEL), BF16),
                        pltpu.SemaphoreType.DMA((2,)), pltpu.SemaphoreType.DMA((2,))],
    )
    ys = pl.pallas_call(
        functools.partial(_expert_kernel, nblk=nblk),
        grid_spec=grid_spec,
        out_shape=jax.ShapeDtypeStruct(xs_c.shape, BF16),
        input_output_aliases={4: 0},
        compiler_params=_params("arbitrary"),
        name="moe_experts",
    )(be, pstart, nbc, src, xs_c, wg, wu, wd)
    return ys.reshape(nt, MOE_SLOTS, D_MODEL)


def _combine_kernel(h_ref, ys_ref, gs_ref, a1_ref, a1t_ref, wgu_ref, wd_ref, vec_ref, o_ref):
    E = N_EXPERTS
    h = h_ref[...]
    ch_col, off_col, _, off_row = _dispatch_geometry(a1_ref[...])
    si = lax.broadcasted_iota(jnp.int32, (E, MOE_SLOTS), 1).astype(F32)
    lo = off_col * MOE_CH
    esel_t = ((si >= lo) & (si < lo + ch_col * MOE_CH)).astype(BF16)
    seg0 = _dot(jnp.broadcast_to(off_row[0:1, :], (TM, E)), esel_t) * MOE_CH
    rs1 = lax.broadcasted_iota(jnp.int32, (TM, MOE_SLOTS), 1).astype(F32) - seg0 + 1.0
    perm_t = _dot(a1t_ref[...], esel_t) == rs1
    routed = _dot(perm_t.astype(BF16), ys_ref[...].astype(F32) * gs_ref[...])
    xb = h.astype(BF16)
    hid = _dot(xb, wgu_ref[...])
    shared = _dot(_silu(hid[:, :EXPERT_FF]) * hid[:, EXPERT_FF:], wd_ref[...])
    o_ref[...] = _layer_norm(DEEPNORM_ALPHA * h + (shared + routed), vec_ref[0:1, :], vec_ref[1:2, :])


def _combine(h2d, ys, gs, a1, a1t, wgu_s, wd_s, vec):
    T, D = h2d.shape
    nt = T // TM
    return pl.pallas_call(
        _combine_kernel,
        grid=(nt,),
        in_specs=[pl.BlockSpec((TM, D), lambda i: (i, 0)),
                  pl.BlockSpec((None, MOE_SLOTS, D), lambda i: (i, 0, 0)),
                  pl.BlockSpec((None, MOE_SLOTS, 1), lambda i: (i, 0, 0)),
                  pl.BlockSpec((None, N_EXPERTS, TM), lambda i: (i, 0, 0)),
                  pl.BlockSpec((None, TM, N_EXPERTS), lambda i: (i, 0, 0)),
                  _full(wgu_s.shape), _full(wd_s.shape), _full(vec.shape)],
        out_specs=pl.BlockSpec((TM, D), lambda i: (i, 0)),
        out_shape=jax.ShapeDtypeStruct((T, D), F32),
        compiler_params=_params("parallel"),
        name="moe_combine",
    )(h2d, ys, gs, a1, a1t, wgu_s, wd_s, vec)


def _dispatch_tables(chunks):
    nt, E = chunks.shape
    maxc = nt * MOE_CPT
    nblk = -(-maxc // MOE_CB) + E
    hp = lax.Precision.HIGHEST
    ch = chunks.astype(F32)
    off = jnp.cumsum(ch, axis=1) - ch
    wend = jnp.cumsum(ch, axis=0)
    cnt_e = wend[-1]
    cend = jnp.cumsum(cnt_e)
    base_e = cend - cnt_e
    srcbase = jnp.arange(nt, dtype=F32)[:, None] * MOE_CPT + off - (wend - ch)
    p = jnp.arange(maxc, dtype=F32)
    e_p = jnp.minimum(jnp.sum((cend[None, :] <= p[:, None]).astype(jnp.int32), axis=1), E - 1)
    oh_e = (e_p[:, None] == jnp.arange(E, dtype=jnp.int32)[None, :]).astype(F32)
    look = jnp.dot(oh_e, jnp.concatenate([wend.T, srcbase.T, base_e[:, None]], axis=1), precision=hp)
    q = p - look[:, 2 * nt]
    i_p = jnp.sum((look[:, :nt] <= q[:, None]).astype(jnp.int32), axis=1)
    oh_i = i_p[:, None] == jnp.arange(nt, dtype=jnp.int32)[None, :]
    src = jnp.sum(jnp.where(oh_i, look[:, nt:2 * nt], 0.0), axis=1) + q
    src = jnp.clip(src, 0, maxc - 1).astype(jnp.int32)
    nblk_e = jnp.floor((cnt_e + (MOE_CB - 1)) * (1.0 / MOE_CB))
    bend = jnp.cumsum(nblk_e)
    bidx = jnp.arange(nblk, dtype=F32)
    be = jnp.minimum(jnp.sum((bend[None, :] <= bidx[:, None]).astype(jnp.int32), axis=1), E - 1)
    oh_b = (be[:, None] == jnp.arange(E, dtype=jnp.int32)[None, :]).astype(F32)
    lookb = jnp.dot(oh_b, jnp.stack([bend - nblk_e, cnt_e, base_e], axis=1), precision=hp)
    q0 = (bidx - lookb[:, 0]) * MOE_CB
    nbc = jnp.clip(lookb[:, 1] - q0, 0, MOE_CB).astype(jnp.int32)
    pstart = jnp.clip(lookb[:, 2] + q0, 0, maxc - MOE_CB).astype(jnp.int32)
    return be.astype(jnp.int32), pstart, nbc, src


def _moe_layer(h2d, router, bias, w_gate, w_up, w_down, sh_gate, sh_up, sh_down, ln_g, ln_b):
    router_t = router.T.astype(F32)
    bias_col = jnp.broadcast_to(bias.astype(F32)[:, None], (N_EXPERTS, LANES))
    xs, gs, a1, a1t, cnt = _router(h2d, router_t, bias_col)
    be, pstart, nbc, src = _dispatch_tables(cnt[:, 0, :].astype(jnp.int32))
    ys = _experts(xs, w_gate, w_up, w_down, be, pstart, nbc, src)
    wgu_s = jnp.concatenate([sh_gate, sh_up], axis=-1).astype(BF16)
    vec = jnp.zeros((SUBLANES, D_MODEL), F32).at[0].set(ln_g).at[1].set(ln_b)
    return _combine(h2d, ys, gs, a1, a1t, wgu_s, sh_down.astype(BF16), vec)


def _rows(*vs):
    out = jnp.zeros((SUBLANES, vs[0].shape[-1]), F32)
    for i, v in enumerate(vs):
        out = out.at[i].set(v.astype(F32))
    return out


def kernel(x, meta, rel_bias, rwkv_mu, rwkv_w0, rwkv_w1, rwkv_w2, rwkv_a0, rwkv_a1, rwkv_a2, rwkv_g1, rwkv_g2, rwkv_k_k, rwkv_k_a, rwkv_r_k, rwkv_w_r, rwkv_w_k, rwkv_w_v, rwkv_w_o, rwkv_lnx_g, rwkv_lnx_b, attn_w_qkv, attn_b_qkv, attn_sinks, attn_w_o, attn_b_o, ln_mix_g, ln_mix_b, ln_ffn_g, ln_ffn_b, moe_router, moe_bias, moe_w_gate, moe_w_up, moe_w_down, shared_w_gate, shared_w_up, shared_w_down):
    B, S, D = x.shape
    assert D == D_MODEL and S % ATT_BLOCK == 0 and (FRONT + N_META) % WKV_CHUNK == 0
    lp = FRONT + N_META + S
    T = B * lp
    assert T % TM == 0 and lp >= TM
    h = jnp.concatenate([jnp.zeros((B, FRONT, D), x.dtype),
                         jnp.broadcast_to(meta[None].astype(x.dtype), (B, N_META, D)), x], axis=1)
    h = h.reshape(T, D)
    bf = lambda w: w.astype(BF16)

    H, N = RWKV_HEADS, RWKV_HEAD
    head_of = jnp.arange(D) // N
    gsum = (head_of[:, None] == jnp.arange(LANES)[None, :]).astype(BF16)
    gexp = gsum.T
    r, w, k, v, kk, b, g = _rwkv_proj(
        h, lp, _rows(*rwkv_mu[0]), _rows(rwkv_w0[0], rwkv_a0[0], rwkv_k_k[0], rwkv_k_a[0]),
        bf(rwkv_w_r[0]), bf(rwkv_w_k[0]), bf(rwkv_w_v[0]), bf(rwkv_w1[0]), bf(rwkv_w2[0]),
        bf(rwkv_a1[0]), bf(rwkv_a2[0]), bf(rwkv_g1[0]), bf(rwkv_g2[0]), gsum, gexp)
    to3 = lambda t: t.reshape(B, lp, D)
    prm = _rows(rwkv_r_k[0].reshape(D), rwkv_lnx_g[0], rwkv_lnx_b[0])
    o = _wkv(to3(r), to3(w), to3(k), to3(v), to3(kk), to3(b), to3(g), prm)
    h = _proj_ln(o.reshape(T, D), bf(rwkv_w_o[0]), _rows(jnp.zeros((D,), F32), ln_mix_g[0], ln_mix_b[0]), h)
    h = _moe_layer(h, moe_router[0], moe_bias[0], moe_w_gate[0], moe_w_up[0], moe_w_down[0],
                   shared_w_gate[0], shared_w_up[0], shared_w_down[0], ln_ffn_g[0], ln_ffn_b[0])

    HD, KV = ATT_HEAD_DIM, ATT_KV_HEADS
    qw = ATT_HEADS * HD
    wqkv, bqkv = attn_w_qkv[0], attn_b_qkv[0]
    dup = lambda t: jnp.concatenate([t.reshape(-1, KV, 1, HD)] * 2, axis=2).reshape(t.shape[0], 2 * KV * HD)
    wq, wkd, wvd = wqkv[:, :qw], dup(wqkv[:, qw:qw + KV * HD]), dup(wqkv[:, qw + KV * HD:])
    bq, bkd, bvd = bqkv[None, :qw], dup(bqkv[None, qw:qw + KV * HD]), dup(bqkv[None, qw + KV * HD:])
    q, kd, vd = _qkv(h, bf(wq), bf(wkd), bf(wvd), bq.astype(F32), bkd.astype(F32), bvd.astype(F32))
    bband, bmeta, bmm = _bias_tables(rel_bias, S // ATT_BLOCK)
    o = _attention(q.reshape(B, lp, qw), kd.reshape(B, lp, -1), vd.reshape(B, lp, -1),
                   attn_sinks[0].astype(F32), bband, bmeta, bmm)
    h = _proj_ln(o.reshape(T, D), bf(attn_w_o[0]), _rows(attn_b_o[0], ln_mix_g[1], ln_mix_b[1]), h)
    h = _moe_layer(h, moe_router[1], moe_bias[1], moe_w_gate[1], moe_w_up[1], moe_w_down[1],
                   shared_w_gate[1], shared_w_up[1], shared_w_down[1], ln_ffn_g[1], ln_ffn_b[1])
    return h.reshape(B, lp, D)[:, FRONT + N_META:]
```

```python
import functools
import math

import jax
import jax.numpy as jnp
from jax import lax
from jax.experimental import pallas as pl
from jax.experimental.pallas import tpu as pltpu

F32 = jnp.float32
BF16 = jnp.bfloat16

D_MODEL = 1024
DEPTH = 2
N_META = 16
RWKV_HEAD = 64
RWKV_HEADS = D_MODEL // RWKV_HEAD
GN_EPS = 64e-5
ATT_HEADS = 16
ATT_KV_HEADS = 4
ATT_HEAD_DIM = D_MODEL // ATT_HEADS
ATT_GROUP = ATT_HEADS // ATT_KV_HEADS
WINDOW = 128
ATT_BLOCK = 128
N_BUCKETS = 32
MAX_DISTANCE = 128
N_EXPERTS = 64
TOP_K = 8
N_GROUPS = 8
TOPK_GROUPS = 4
EXPERT_FF = 256
ROUTED_SCALE = 2.5
DEEPNORM_ALPHA = (2 * DEPTH) ** 0.25
LN_EPS = 1e-5

LANES = 128
SUBLANES = 8
BF16_ROWS = 16
VMEM_LIMIT = 56 * 1024 * 1024

FRONT = 48
TM = 256
WKV_CHUNK = 64
WKV_PAIRS = 8
MOE_CH = BF16_ROWS
MOE_SLOTS = TM * TOP_K + N_EXPERTS * MOE_CH
MOE_CPT = MOE_SLOTS // MOE_CH
MOE_CB = 32
NEG = -1e30

_NN = (((1,), (0,)), ((), ()))
_NT = (((1,), (1,)), ((), ()))


def _dot(a, b, dn=_NN):
    return lax.dot_general(a.astype(BF16), b.astype(BF16), dn, preferred_element_type=F32)


def _split2(x):
    hi = x.astype(BF16)
    lo = (x - hi.astype(F32)).astype(BF16)
    return hi, lo


def _split3(x):
    h1 = x.astype(BF16)
    r1 = x - h1.astype(F32)
    h2 = r1.astype(BF16)
    h3 = (r1 - h2.astype(F32)).astype(BF16)
    return h1, h2, h3


def _dot_exact_lhs(a01, b, dn=_NN):
    a = a01.astype(BF16)
    return sum(lax.dot_general(a, p, dn, preferred_element_type=F32) for p in _split3(b))


def _dot_exact_rhs(a, b01, dn=_NN):
    b = b01.astype(BF16)
    return sum(lax.dot_general(p, b, dn, preferred_element_type=F32) for p in _split3(a))


def _dot3(a, b, dn=_NN):
    ah, al = _split2(a)
    bh, bl = _split2(b)
    d = lambda x, y: lax.dot_general(x, y, dn, preferred_element_type=F32)
    return d(ah, bh) + (d(ah, bl) + d(al, bh))


def _sigmoid(x):
    return 1.0 / (1.0 + jnp.exp(-x))


def _silu(x):
    return x * _sigmoid(x)


def _layer_norm(x, g, b):
    mu = jnp.mean(x, axis=-1, keepdims=True)
    xc = x - mu
    var = jnp.mean(xc * xc, axis=-1, keepdims=True)
    return xc * lax.rsqrt(var + LN_EPS) * g + b


def _full(shape):
    nd = len(shape)
    return pl.BlockSpec(shape, lambda *_: (0,) * nd)


def _params(*sem):
    return pltpu.CompilerParams(dimension_semantics=sem, vmem_limit_bytes=VMEM_LIMIT)


def _rwkv_proj_kernel(x_ref, xp_ref, mu_ref, vec_ref, wr_ref, wk_ref, wv_ref, w1_ref, w2_ref,
                      a1_ref, a2_ref, g1_ref, g2_ref, gsum_ref, gexp_ref,
                      r_out, w_out, k_out, v_out, kk_out, b_out, g_out, *, lp):
    i = pl.program_id(0)
    x = x_ref[...]
    row = lax.broadcasted_iota(jnp.int32, (TM, 1), 0)
    pos = lax.rem(i * TM, lp) + row
    pos = jnp.where(pos >= lp, pos - lp, pos)
    prev = jnp.where(row == 0, xp_ref[SUBLANES - 1:SUBLANES, :], pltpu.roll(x, 1, axis=0))
    prev = jnp.where(pos == FRONT, 0.0, prev)
    xx = prev - x
    valid = pos >= FRONT

    def mix(j):
        return (x + xx * mu_ref[j:j + 1, :]).astype(BF16)

    w0, a0, k_k, k_a = (vec_ref[j:j + 1, :] for j in range(4))
    r = _dot(mix(0), wr_ref[...])
    z = w0 + _dot(jnp.tanh(_dot(mix(1), w1_ref[...])), w2_ref[...])
    w = -math.exp(-0.5) * _sigmoid(z)
    k = _dot(mix(2), wk_ref[...])
    v = _dot(mix(3), wv_ref[...])
    a = _sigmoid(a0 + _dot(_dot(mix(4), a1_ref[...]), a2_ref[...]))
    g = _dot(_sigmoid(_dot(mix(5), g1_ref[...])), g2_ref[...])
    kk = k * k_k
    ssq = _dot_exact_rhs(kk * kk, gsum_ref[...])
    nrm = jnp.sqrt(_dot_exact_rhs(ssq, gexp_ref[...]))
    kk = kk / jnp.maximum(nrm, 1e-12)
    k = k * (1.0 + (a - 1.0) * k_a)
    r_out[...] = r
    w_out[...] = jnp.where(valid, w, 0.0)
    k_out[...] = jnp.where(valid, k, 0.0)
    v_out[...] = jnp.where(valid, v, 0.0)
    kk_out[...] = jnp.where(valid, kk, 0.0)
    b_out[...] = jnp.where(valid, kk * a, 0.0)
    g_out[...] = g


def _rwkv_proj(h2d, lp, mu, vec, wr, wk, wv, w1, w2, a1, a2, g1, g2, gsum, gexp):
    T, D = h2d.shape
    row_spec = pl.BlockSpec((TM, D), lambda i: (i, 0))
    prev_spec = pl.BlockSpec((SUBLANES, D), lambda i: (jnp.maximum(i * (TM // SUBLANES) - 1, 0), 0))
    ws = [mu, vec, wr, wk, wv, w1, w2, a1, a2, g1, g2, gsum, gexp]
    return pl.pallas_call(
        functools.partial(_rwkv_proj_kernel, lp=lp),
        grid=(T // TM,),
        in_specs=[row_spec, prev_spec] + [_full(w.shape) for w in ws],
        out_specs=[row_spec] * 7,
        out_shape=[jax.ShapeDtypeStruct((T, D), F32)] * 7,
        compiler_params=_params("parallel"),
        name="rwkv_proj",
    )(h2d, h2d, *ws)


def _wkv_chunk(r, w, cum, k, v, kk, b, S, c):
    C = WKV_CHUNK
    P = range(len(r))
    bf = lambda xs: [x.astype(BF16) for x in xs]
    each = lambda f, *ls: [f(*a) for a in zip(*ls)]

    def stack(x):
        return jnp.concatenate([jnp.where(c["m0"], x, 0.0), jnp.where(c["m0"], 0.0, x)], axis=0)

    f32 = lambda xs: [x.astype(F32) for x in xs]
    cat0 = lambda x, y: jnp.concatenate([x, y], axis=0)
    cat1 = lambda x, y: jnp.concatenate([x, y], axis=1)
    nt = lambda x, y: _dot(x, y, _NT)
    C2 = 2 * C

    tot = [cum[p][C - 1:C, :] for p in P]
    inv = [jnp.exp(-cum[p]) for p in P]
    dend = [jnp.exp(tot[p] - cum[p]) for p in P]
    kr_s = bf([cat0(stack(kk[p] * jnp.exp(cum[p] - w[p])), stack(r[p] * jnp.exp(cum[p]))) for p in P])
    bk_s = bf([cat0(stack(b[p] * inv[p]), stack(k[p] * inv[p])) for p in P])
    bh_s = bf([stack(b[p] * dend[p]) for p in P])
    kh_s = bf([stack(k[p] * dend[p]) for p in P])
    vs_t = bf([stack(v[p]).T for p in P])

    sc = each(nt, bk_s, kr_s)
    lab_t = [jnp.where(c["upper"], x[:C2, :C2], 0.0) for x in sc]
    arb_t = bf([jnp.where(c["upinc"], x[:C2, C2:], 0.0) for x in sc])
    lak_t = bf([jnp.where(c["upper"], x[C2:, :C2], 0.0) for x in sc])
    ark_t = bf([jnp.where(c["upinc"], x[C2:, C2:], 0.0) for x in sc])

    eye = c["eye"]
    ud = [jnp.where(c["blk"], x, 0.0) for x in lab_t]
    nu = bf([x - y for x, y in zip(lab_t, ud)])
    udb = bf(ud)
    u2b = bf(each(_dot, udb, udb))
    u2 = f32(u2b)
    t1 = each(_dot, bf([eye + x for x in u2]), [cat1(x, (eye - y).astype(BF16)) for x, y in zip(u2b, ud)])
    u4b = bf([x[:, :C2] - y for x, y in zip(t1, u2)])
    u4 = f32(u4b)
    t2 = each(_dot, bf([eye + x for x in u4]), [cat1(x, y[:, C2:].astype(BF16)) for x, y in zip(u4b, t1)])
    u8 = [x[:, :C2] - y for x, y in zip(t2, u4)]
    dinv = each(_dot, bf([eye + x for x in u8]), bf([x[:, C2:] for x in t2]))
    dinvb = bf(dinv)
    wzb = bf(each(_dot, dinvb, nu))
    t3 = each(_dot, wzb, [cat1(x, y) for x, y in zip(wzb, dinvb)])
    tt_t = bf(each(_dot, bf([eye + x[:, :C2] for x in t3]), bf([y - x[:, C2:] for x, y in zip(t3, dinv)])))

    sb = bf(S)
    p2 = each(_dot, vs_t, [cat1(x, y) for x, y in zip(lak_t, ark_t)])
    p3 = each(_dot, vs_t, kh_s)
    p1 = each(nt, sb, kr_s)
    u_t = bf(each(_dot, bf([-(x[:, :C2] + y[:, :C2]) for x, y in zip(p1, p2)]), tt_t))
    p4 = each(_dot, u_t, [cat1(x, y) for x, y in zip(arb_t, bh_s)])
    s_new = [S[p] * jnp.exp(tot[p]) + p4[p][:, C2:] + p3[p] for p in P]
    yn = []
    for p in P:
        yt = p1[p][:, C2:] + p4[p][:, :C2] + p2[p][:, C2:]
        mean = jnp.sum(yt, axis=0, keepdims=True) * (1.0 / RWKV_HEAD)
        yc = jnp.where(c["hblk"], yt - mean, 0.0)
        var = jnp.sum(yc * yc, axis=0, keepdims=True) * (1.0 / RWKV_HEAD)
        ys = (yc * lax.rsqrt(var + GN_EPS)).T
        yn.append(ys[:C] + ys[C:])
    return yn, s_new


def _wkv_kernel(r_ref, w_ref, k_ref, v_ref, kk_ref, b_ref, g_ref, prm_ref, o_ref, *s_refs, lb):
    C = WKV_CHUNK

    @pl.when(pl.program_id(2) == 0)
    def _():
        for s_ref in s_refs:
            s_ref[...] = jnp.zeros_like(s_ref)

    ri = lax.broadcasted_iota(jnp.int32, (2 * C, 2 * C), 0)
    ci = lax.broadcasted_iota(jnp.int32, (2 * C, 2 * C), 1)
    lane = lax.broadcasted_iota(jnp.int32, (1, LANES), 1)
    ti = lax.broadcasted_iota(jnp.int32, (C, C), 0)
    tj = lax.broadcasted_iota(jnp.int32, (C, C), 1)
    consts = dict(
        m0=lane < RWKV_HEAD,
        upper=ri < ci,
        upinc=ri <= ci,
        blk=(ri // 16) == (ci // 16),
        hblk=(ri // RWKV_HEAD) == (ci // C),
        eye=(ri == ci).astype(F32),
    )
    m0 = consts["m0"]
    tril = (ti >= tj).astype(BF16)

    def head_sum(x):
        s0 = jnp.sum(jnp.where(m0, x, 0.0), axis=-1, keepdims=True)
        s1 = jnp.sum(jnp.where(m0, 0.0, x), axis=-1, keepdims=True)
        return jnp.where(m0, s0, s1)

    def chunk(ci_, carry):
        rows = pl.ds(pl.multiple_of(ci_ * C, C), C)
        lns = [slice(p * LANES, (p + 1) * LANES) for p in range(WKV_PAIRS)]
        ld = lambda ref: [ref[rows, ln] for ln in lns]
        r, w, k, v, kk, b = ld(r_ref), ld(w_ref), ld(k_ref), ld(v_ref), ld(kk_ref), ld(b_ref)
        cum_all = _dot_exact_lhs(tril, w_ref[rows, :])
        cum = [cum_all[:, ln] for ln in lns]
        ys, s_new = _wkv_chunk(r, w, cum, k, v, kk, b, [s[...] for s in s_refs], consts)
        for p, ln in enumerate(lns):
            s_refs[p][...] = s_new[p]
            r_k, lg, lb_ = prm_ref[0:1, ln], prm_ref[1:2, ln], prm_ref[2:3, ln]
            bonus = head_sum(r[p] * k[p] * r_k) * v[p]
            o_ref[rows, ln] = ((ys[p] * lg + lb_ + bonus) * g_ref[rows, ln]).astype(o_ref.dtype)
        return carry

    lax.fori_loop(0, lb // C, chunk, 0)


def _wkv_row_block(lp):
    nch = lp // WKV_CHUNK
    for d in (3, 4, 2, 5, 1):
        if nch % d == 0:
            return d * WKV_CHUNK
    return WKV_CHUNK


def _wkv(r, w, k, v, kk, b, g, prm):
    B, lp, D = r.shape
    lb = _wkv_row_block(lp)
    wl = WKV_PAIRS * LANES
    spec = pl.BlockSpec((None, lb, wl), lambda bi, pi, li: (bi, li, pi))
    return pl.pallas_call(
        functools.partial(_wkv_kernel, lb=lb),
        grid=(B, D // wl, lp // lb),
        in_specs=[spec] * 7 + [pl.BlockSpec((SUBLANES, wl), lambda bi, pi, li: (0, pi))],
        out_specs=spec,
        out_shape=jax.ShapeDtypeStruct((B, lp, D), BF16),
        scratch_shapes=[pltpu.VMEM((LANES, LANES), F32)] * WKV_PAIRS,
        compiler_params=_params("parallel", "parallel", "arbitrary"),
        name="wkv7",
    )(r, w, k, v, kk, b, g, prm)


def _proj_ln_kernel(a_ref, w_ref, vec_ref, h_ref, o_ref):
    mix = _dot(a_ref[...], w_ref[...]) + vec_ref[0:1, :]
    o_ref[...] = _layer_norm(DEEPNORM_ALPHA * h_ref[...] + mix, vec_ref[1:2, :], vec_ref[2:3, :])


def _proj_ln(a2d, w, vec, h2d):
    T, D = h2d.shape
    K = a2d.shape[1]
    return pl.pallas_call(
        _proj_ln_kernel,
        grid=(T // TM,),
        in_specs=[pl.BlockSpec((TM, K), lambda i: (i, 0)), _full(w.shape), _full(vec.shape),
                  pl.BlockSpec((TM, D), lambda i: (i, 0))],
        out_specs=pl.BlockSpec((TM, D), lambda i: (i, 0)),
        out_shape=jax.ShapeDtypeStruct((T, D), F32),
        compiler_params=_params("parallel"),
        name="proj_ln",
    )(a2d, w, vec, h2d)


def _qkv_kernel(x_ref, wq_ref, wk_ref, wv_ref, bq_ref, bk_ref, bv_ref, q_out, k_out, v_out):
    x = x_ref[...].astype(BF16)
    q_out[...] = ((_dot(x, wq_ref[...]) + bq_ref[...]) * ATT_HEAD_DIM ** -0.5).astype(BF16)
    k_out[...] = (_dot(x, wk_ref[...]) + bk_ref[...]).astype(BF16)
    v_out[...] = (_dot(x, wv_ref[...]) + bv_ref[...]).astype(BF16)


def _qkv(h2d, wq, wkd, wvd, bq, bkd, bvd):
    T, D = h2d.shape
    nq, nk = wq.shape[1], wkd.shape[1]
    ws = [wq, wkd, wvd, bq, bkd, bvd]
    return pl.pallas_call(
        _qkv_kernel,
        grid=(T // TM,),
        in_specs=[pl.BlockSpec((TM, D), lambda i: (i, 0))] + [_full(w.shape) for w in ws],
        out_specs=[pl.BlockSpec((TM, nq), lambda i: (i, 0)), pl.BlockSpec((TM, nk), lambda i: (i, 0)),
                   pl.BlockSpec((TM, nk), lambda i: (i, 0))],
        out_shape=[jax.ShapeDtypeStruct((T, nq), BF16), jax.ShapeDtypeStruct((T, nk), BF16),
                   jax.ShapeDtypeStruct((T, nk), BF16)],
        compiler_params=_params("parallel"),
        name="qkv_proj",
    )(h2d, *ws)


def _attn_kernel(sink_ref, q_ref, k_ref, v_ref, bband_ref, bmeta_ref, bmm_ref, o_ref, *, nb):
    c = pl.program_id(1)
    BLK = ATT_BLOCK
    M0 = FRONT
    lane = lax.broadcasted_iota(jnp.int32, (1, LANES), 1)
    m0 = lane < ATT_HEAD_DIM
    qi = lax.broadcasted_iota(jnp.int32, (2 * BLK, BLK), 0) % BLK
    sj = lax.broadcasted_iota(jnp.int32, (2 * BLK, BLK), 1)
    cur_vis = sj <= qi
    prev_vis = sj > qi
    half = lax.broadcasted_iota(jnp.int32, (2 * BLK, 1), 0) < BLK
    eye2 = (lax.broadcasted_iota(jnp.int32, (2 * BLK, 2 * BLK), 0)
            == lax.broadcasted_iota(jnp.int32, (2 * BLK, 2 * BLK), 1)).astype(BF16)

    def stack(x):
        z = jnp.zeros_like(x)
        return jnp.concatenate([jnp.where(m0, x, z), jnp.where(m0, z, x)], axis=0)

    def unstack(x, n):
        return jnp.where(m0, x[:n], x[n:])

    HP = range(4)
    kl = [slice((hp // 2) * LANES, (hp // 2 + 1) * LANES) for hp in HP]
    ql = [slice(hp * LANES, (hp + 1) * LANES) for hp in HP]

    def attend(segs, sinks):
        mx = [functools.reduce(jnp.maximum, [jnp.max(lg, axis=-1, keepdims=True) for lg, _ in segs[hp]],
                               sinks[hp]) for hp in HP]
        ps = [[jnp.exp(lg - mx[hp]) for lg, _ in segs[hp]] for hp in HP]
        den = [sum(jnp.sum(p, axis=-1, keepdims=True) for p in ps[hp]) + jnp.exp(sinks[hp] - mx[hp])
               for hp in HP]
        acc = [sum(_dot(p, vals) for p, (_, vals) in zip(ps[hp], segs[hp])) for hp in HP]
        return [a / d for a, d in zip(acc, den)]

    o_ref[0:M0, :] = jnp.zeros((M0, o_ref.shape[1]), o_ref.dtype)
    s0 = [sink_ref[c * 8 + hp * 2] for hp in HP]
    s1 = [sink_ref[c * 8 + hp * 2 + 1] for hp in HP]
    k_meta = [k_ref[M0:M0 + N_META, kl[hp]] for hp in HP]
    v_meta = [v_ref[M0:M0 + N_META, kl[hp]] for hp in HP]

    first = lax.broadcasted_iota(jnp.int32, (2 * N_META, 1), 0) < N_META
    mi = lax.broadcasted_iota(jnp.int32, (2 * N_META, N_META), 0) % N_META
    mj = lax.broadcasted_iota(jnp.int32, (2 * N_META, N_META), 1)
    qm = [stack(q_ref[M0:M0 + N_META, ql[hp]]) for hp in HP]
    lg = [jnp.where(mj <= mi, _dot(qm[hp], k_meta[hp], _NT) + bmm_ref[hp], NEG) for hp in HP]
    om = attend([[(lg[hp], v_meta[hp])] for hp in HP], [jnp.where(first, s0[hp], s1[hp]) for hp in HP])
    for hp in HP:
        o_ref[M0:M0 + N_META, ql[hp]] = unstack(om[hp], N_META).astype(o_ref.dtype)

    sink_q = [jnp.where(half, s0[hp], s1[hp]) for hp in HP]

    def block(j, carry):
        start = pl.multiple_of(M0 + N_META + j * BLK, 64)
        pstart = pl.multiple_of(jnp.maximum(start - BLK, 0), 64)
        qs = [stack(q_ref[pl.ds(start, BLK), ql[hp]]) for hp in HP]
        k_prev = [k_ref[pl.ds(pstart, BLK), kl[hp]] for hp in HP]
        k_cur = [k_ref[pl.ds(start, BLK), kl[hp]] for hp in HP]
        lg_meta = [_dot(qs[hp], k_meta[hp], _NT) + _dot_exact_lhs(eye2, bmeta_ref[hp, j], _NT) for hp in HP]
        lg_prev = [jnp.where(prev_vis & (j > 0), _dot(qs[hp], k_prev[hp], _NT) + bband_ref[hp, :, 0:BLK], NEG)
                   for hp in HP]
        lg_cur = [jnp.where(cur_vis, _dot(qs[hp], k_cur[hp], _NT) + bband_ref[hp, :, BLK:2 * BLK], NEG)
                  for hp in HP]
        out = attend([[(lg_meta[hp], v_meta[hp]), (lg_prev[hp], v_ref[pl.ds(pstart, BLK), kl[hp]]),
                       (lg_cur[hp], v_ref[pl.ds(start, BLK), kl[hp]])] for hp in HP], sink_q)
        for hp in HP:
            o_ref[pl.ds(start, BLK), ql[hp]] = unstack(out[hp], BLK).astype(o_ref.dtype)
        return carry

    lax.fori_loop(0, nb, block, 0)


def _attention(q, kd, vd, sinks, bband, bmeta, bmm):
    B, lp, _ = q.shape
    nb = (lp - FRONT - N_META) // ATT_BLOCK
    qw, kw = 4 * LANES, 2 * LANES
    return pl.pallas_call(
        functools.partial(_attn_kernel, nb=nb),
        grid=(B, 2),
        in_specs=[pl.BlockSpec(memory_space=pltpu.SMEM),
                  pl.BlockSpec((None, lp, qw), lambda b, c: (b, 0, c)),
                  pl.BlockSpec((None, lp, kw), lambda b, c: (b, 0, c)),
                  pl.BlockSpec((None, lp, kw), lambda b, c: (b, 0, c)),
                  pl.BlockSpec((4,) + bband.shape[1:], lambda b, c: (c, 0, 0)),
                  pl.BlockSpec((4,) + bmeta.shape[1:], lambda b, c: (c, 0, 0, 0)),
                  pl.BlockSpec((4,) + bmm.shape[1:], lambda b, c: (c, 0, 0))],
        out_specs=pl.BlockSpec((None, lp, qw), lambda b, c: (b, 0, c)),
        out_shape=jax.ShapeDtypeStruct((B, lp, D_MODEL), BF16),
        compiler_params=_params("parallel", "parallel"),
        name="swa_attention",
    )(sinks, q, kd, vd, bband, bmeta, bmm)


def _t5_bucket(dist):
    exact = N_BUCKETS // 2
    d = jnp.maximum(dist, 0)
    ratio = jnp.log(jnp.maximum(d, 1).astype(F32) / exact) / math.log(MAX_DISTANCE / exact)
    large = jnp.minimum(exact + (ratio * (N_BUCKETS - exact)).astype(jnp.int32), N_BUCKETS - 1)
    return jnp.where(d < exact, d, large)


def _bias_tables(rel_bias, nb):
    H, BLK = ATT_HEADS, ATT_BLOCK

    def lookup(dist):
        onehot = (_t5_bucket(dist)[..., None] == jnp.arange(N_BUCKETS)).astype(F32)
        return jnp.dot(onehot, rel_bias.astype(F32), precision=lax.Precision.HIGHEST)

    qi = jnp.arange(BLK)[:, None]
    band = lookup(qi + BLK - jnp.arange(2 * BLK)[None, :])
    band = jnp.moveaxis(band, -1, 0).reshape(H // 2, 2 * BLK, 2 * BLK)
    pos = jnp.arange(nb * BLK)[:, None]
    meta = lookup(N_META + pos - jnp.arange(N_META)[None, :])
    meta = meta.reshape(nb, BLK, N_META, H // 2, 2).transpose(3, 0, 2, 4, 1)
    meta = meta.reshape(H // 2, nb, N_META, 2 * BLK)
    pm = jnp.arange(N_META)
    mm = lookup(pm[:, None] - pm[None, :])
    mm = jnp.moveaxis(mm, -1, 0).reshape(H // 2, 2 * N_META, N_META)
    return band.astype(F32), meta.astype(F32), mm.astype(F32)


def _dispatch_geometry(a1):
    E = N_EXPERTS
    routed = (a1 > 0.0).astype(BF16)
    n_col = jnp.sum(routed.astype(F32), axis=1, keepdims=True)
    ch_col = jnp.floor((n_col + (MOE_CH - 1)) * (1.0 / MOE_CH))
    ei = lax.broadcasted_iota(jnp.int32, (E, E), 0)
    ej = lax.broadcasted_iota(jnp.int32, (E, E), 1)
    off_col = _dot((ei > ej).astype(BF16), jnp.broadcast_to(ch_col, (E, LANES)))[:, 0:1]
    n_row = _dot(jnp.ones((SUBLANES, TM), BF16), routed, _NT)
    ch_row = jnp.floor((n_row + (MOE_CH - 1)) * (1.0 / MOE_CH))
    off_row = _dot(ch_row, (ei < ej).astype(BF16))
    return ch_col, off_col, ch_row, off_row


def _router_kernel(x_ref, rt_ref, bias_ref, xs_out, gs_out, a1_out, a1t_out, cnt_out):
    E, G, EPG = N_EXPERTS, N_GROUPS, N_EXPERTS // N_GROUPS
    x = x_ref[...]
    logits = _dot3(rt_ref[...], x, _NT)
    scores = _sigmoid(logits)
    choice = scores + bias_ref[:, 0:1]

    grp = choice.reshape(G, EPG, TM)
    sub = lax.broadcasted_iota(jnp.int32, (G, EPG, TM), 1)
    top1 = jnp.max(grp, axis=1, keepdims=True)
    first = jnp.min(jnp.where(grp == top1, sub, EPG), axis=1, keepdims=True)
    top2 = jnp.max(jnp.where(sub == first, -jnp.inf, grp), axis=1, keepdims=True)
    gscore = jnp.broadcast_to(top1 + top2, (G, EPG, TM))

    gi = lax.broadcasted_iota(jnp.int32, (G, EPG, TM), 0)
    keep = jnp.zeros((G, EPG, TM), jnp.bool_)
    for _ in range(TOPK_GROUPS):
        m = jnp.max(gscore, axis=0, keepdims=True)
        sel = gi == jnp.min(jnp.where(gscore == m, gi, G), axis=0, keepdims=True)
        keep = keep | sel
        gscore = jnp.where(sel, -jnp.inf, gscore)
    keep_e = keep.reshape(E, TM)

    cand = jnp.where(keep_e, choice, -jnp.inf)
    ei = lax.broadcasted_iota(jnp.int32, (E, TM), 0)
    routed = jnp.zeros((E, TM), jnp.bool_)
    for _ in range(TOP_K):
        m = jnp.max(cand, axis=0, keepdims=True)
        sel = ei == jnp.min(jnp.where(cand == m, ei, E), axis=0, keepdims=True)
        routed = routed | sel
        cand = jnp.where(sel, -jnp.inf, cand)
    gate = jnp.where(routed, scores, 0.0)
    gate = gate / jnp.sum(gate, axis=0, keepdims=True) * ROUTED_SCALE

    ti = lax.broadcasted_iota(jnp.int32, (TM, TM), 0)
    tj = lax.broadcasted_iota(jnp.int32, (TM, TM), 1)
    routed_b = routed.astype(BF16)
    rank = _dot(routed_b, (ti < tj).astype(BF16))
    a1 = jnp.where(routed, rank + 1.0, 0.0)
    rank_t = _dot((tj < ti).astype(BF16), routed_b, _NT)
    a1t_out[...] = _dot((ti == tj).astype(BF16), routed_b, _NT) * (rank_t + 1.0)
    _, _, ch_row, off_row = _dispatch_geometry(a1)

    lo = jnp.concatenate([off_row[0:1, :]] * 2, axis=1) * MOE_CH
    hi = lo + jnp.concatenate([ch_row[0:1, :]] * 2, axis=1) * MOE_CH
    first = lax.broadcasted_iota(jnp.int32, (1, 2 * E), 1) < E
    lo1 = jnp.where(first, lo, 0.0)
    gate_t = _dot_exact_lhs((ti == tj).astype(BF16), gate, _NT)
    g_hi, g_lo = _split2(gate_t)
    xg = jnp.concatenate([x.astype(BF16), g_hi, g_lo], axis=1)
    a1b = a1.astype(BF16)
    SB = MOE_SLOTS // 3
    for s0 in range(0, MOE_SLOTS, SB):
        si = (lax.broadcasted_iota(jnp.int32, (SB, 2 * E), 0) + s0).astype(F32)
        member = (si >= lo) & (si < hi)
        seg0 = jnp.sum(jnp.where(member, lo1, 0.0), axis=1, keepdims=True)
        rs1 = (lax.broadcasted_iota(jnp.int32, (SB, 1), 0) + (s0 + 1)).astype(F32) - seg0
        perm = _dot(member[:, :E], a1b) == rs1
        disp = _dot(perm, xg)
        xs_out[s0:s0 + SB, :] = disp[:, :D_MODEL].astype(BF16)
        gs_out[s0:s0 + SB, :] = jnp.sum(jnp.where(member, disp[:, D_MODEL:], 0.0), axis=1, keepdims=True)
    a1_out[...] = a1
    cnt_out[...] = ch_row


def _router(h2d, router_t, bias_col):
    T, D = h2d.shape
    nt = T // TM
    return pl.pallas_call(
        _router_kernel,
        grid=(nt,),
        in_specs=[pl.BlockSpec((TM, D), lambda i: (i, 0)), _full(router_t.shape), _full(bias_col.shape)],
        out_specs=[pl.BlockSpec((None, MOE_SLOTS, D), lambda i: (i, 0, 0)),
                   pl.BlockSpec((None, MOE_SLOTS, 1), lambda i: (i, 0, 0)),
                   pl.BlockSpec((None, N_EXPERTS, TM), lambda i: (i, 0, 0)),
                   pl.BlockSpec((None, TM, N_EXPERTS), lambda i: (i, 0, 0)),
                   pl.BlockSpec((None, SUBLANES, N_EXPERTS), lambda i: (i, 0, 0))],
        out_shape=[jax.ShapeDtypeStruct((nt, MOE_SLOTS, D), BF16),
                   jax.ShapeDtypeStruct((nt, MOE_SLOTS, 1), F32),
                   jax.ShapeDtypeStruct((nt, N_EXPERTS, TM), F32),
                   jax.ShapeDtypeStruct((nt, TM, N_EXPERTS), F32),
                   jax.ShapeDtypeStruct((nt, SUBLANES, N_EXPERTS), F32)],
        compiler_params=_params("parallel"),
        name="moe_router",
    )(h2d, router_t, bias_col)


def _expert_kernel(be_ref, ps_ref, nb_ref, src_ref, xs_hbm, wg_ref, wu_ref, wd_ref, ys_hbm,
                   xbuf, ybuf, wgu_bf, wd_bf, in_sem, out_sem, *, nblk):
    b = pl.program_id(0)
    slot = lax.rem(b, 2)

    def in_copy(src_chunk, sl, c):
        return pltpu.make_async_copy(xs_hbm.at[src_chunk], xbuf.at[sl, c], in_sem.at[sl])

    def out_copy(dst_chunk, sl, c):
        return pltpu.make_async_copy(ybuf.at[sl, c], ys_hbm.at[dst_chunk], out_sem.at[sl])

    def for_chunks(blk, fn):
        p0 = ps_ref[blk]
        n = nb_ref[blk]

        @pl.when(n == MOE_CB)
        def _():
            for c in range(MOE_CB):
                fn(src_ref[p0 + c], c)

        @pl.when(n != MOE_CB)
        def _():
            def body(c, carry):
                fn(src_ref[p0 + c], c)
                return carry

            lax.fori_loop(0, n, body, 0)

    @pl.when(b == 0)
    def _():
        xbuf[...] = jnp.zeros_like(xbuf)
        for_chunks(0, lambda s, c: in_copy(s, 0, c).start())

    @pl.when(b + 1 < nblk)
    def _():
        for_chunks(b + 1, lambda s, c: in_copy(s, 1 - slot, c).start())

    for_chunks(b, lambda s, c: in_copy(s, slot, c).wait())

    @pl.when(b >= 2)
    def _():
        for_chunks(b - 2, lambda s, c: out_copy(s, slot, c).wait())

    @pl.when((b == 0) | (be_ref[b] != be_ref[jnp.maximum(b - 1, 0)]))
    def _():
        wgu_bf[:, :EXPERT_FF] = wg_ref[...].astype(BF16)
        wgu_bf[:, EXPERT_FF:] = wu_ref[...].astype(BF16)
        wd_bf[...] = wd_ref[...].astype(BF16)

    @pl.when(nb_ref[b] > 0)
    def _():
        x = xbuf[slot].reshape(MOE_CB * MOE_CH, D_MODEL)
        hid = _dot(x, wgu_bf[...])
        act = _silu(hid[:, :EXPERT_FF]) * hid[:, EXPERT_FF:]
        y = _dot(act, wd_bf[...])
        ybuf[slot] = y.astype(BF16).reshape(MOE_CB, MOE_CH, D_MODEL)

    for_chunks(b, lambda s, c: out_copy(s, slot, c).start())

    @pl.when(b == nblk - 1)
    def _():
        for_chunks(b, lambda s, c: out_copy(s, slot, c).wait())

        @pl.when(b >= 1)
        def _():
            for_chunks(b - 1, lambda s, c: out_copy(s, 1 - slot, c).wait())


def _experts(xs, wg, wu, wd, layer, be, pstart, nbc, src):
    nt = xs.shape[0]
    nblk = be.shape[0]
    xs_c = xs.reshape(nt * MOE_CPT, MOE_CH, D_MODEL)
    by_expert = lambda b, be, ps, nb, src: (layer, be[b], 0, 0)
    grid_spec = pltpu.PrefetchScalarGridSpec(
        num_scalar_prefetch=4,
        grid=(nblk,),
        in_specs=[pl.BlockSpec(memory_space=pl.ANY),
                  pl.BlockSpec((None, None, D_MODEL, EXPERT_FF), by_expert),
                  pl.BlockSpec((None, None, D_MODEL, EXPERT_FF), by_expert),
                  pl.BlockSpec((None, None, EXPERT_FF, D_MODEL), by_expert)],
        out_specs=pl.BlockSpec(memory_space=pl.ANY),
        scratch_shapes=[pltpu.VMEM((2, MOE_CB, MOE_CH, D_MODEL), BF16),
                        pltpu.VMEM((2, MOE_CB, MOE_CH, D_MODEL), BF16),
                        pltpu.VMEM((D_MODEL, 2 * EXPERT_FF), BF16),
                        pltpu.VMEM((EXPERT_FF, D_MODEL), BF16),
                        pltpu.SemaphoreType.DMA((2,)), pltpu.SemaphoreType.DMA((2,))],
    )
    ys = pl.pallas_call(
        functools.partial(_expert_kernel, nblk=nblk),
        grid_spec=grid_spec,
        out_shape=jax.ShapeDtypeStruct(xs_c.shape, BF16),
        input_output_aliases={4: 0},
        compiler_params=_params("arbitrary"),
        name="moe_experts",
    )(be, pstart, nbc, src, xs_c, wg, wu, wd)
    return ys.reshape(nt, MOE_SLOTS, D_MODEL)


def _combine_kernel(h_ref, ys_ref, gs_ref, a1_ref, a1t_ref, wgu_ref, wd_ref, vec_ref, o_ref):
    E = N_EXPERTS
    h = h_ref[...]
    ch_col, off_col, _, _ = _dispatch_geometry(a1_ref[...])
    si = lax.broadcasted_iota(jnp.int32, (E, MOE_SLOTS), 1).astype(F32)
    lo = off_col * MOE_CH
    member_t = (si >= lo) & (si < lo + ch_col * MOE_CH)
    seg0 = jnp.sum(jnp.where(member_t, lo, 0.0), axis=0, keepdims=True)
    rs1 = lax.broadcasted_iota(jnp.int32, (1, MOE_SLOTS), 1).astype(F32) - seg0 + 1.0
    perm_t = _dot(a1t_ref[...], member_t) == rs1
    routed = _dot(perm_t.astype(BF16), ys_ref[...].astype(F32) * gs_ref[...])
    xb = h.astype(BF16)
    hid = _dot(xb, wgu_ref[...])
    shared = _dot(_silu(hid[:, :EXPERT_FF]) * hid[:, EXPERT_FF:], wd_ref[...])
    o_ref[...] = _layer_norm(DEEPNORM_ALPHA * h + (shared + routed), vec_ref[0:1, :], vec_ref[1:2, :])


def _combine(h2d, ys, gs, a1, a1t, wgu_s, wd_s, vec):
    T, D = h2d.shape
    nt = T // TM
    return pl.pallas_call(
        _combine_kernel,
        grid=(nt,),
        in_specs=[pl.BlockSpec((TM, D), lambda i: (i, 0)),
                  pl.BlockSpec((None, MOE_SLOTS, D), lambda i: (i, 0, 0)),
                  pl.BlockSpec((None, MOE_SLOTS, 1), lambda i: (i, 0, 0)),
                  pl.BlockSpec((None, N_EXPERTS, TM), lambda i: (i, 0, 0)),
                  pl.BlockSpec((None, TM, N_EXPERTS), lambda i: (i, 0, 0)),
                  _full(wgu_s.shape), _full(wd_s.shape), _full(vec.shape)],
        out_specs=pl.BlockSpec((TM, D), lambda i: (i, 0)),
        out_shape=jax.ShapeDtypeStruct((T, D), F32),
        compiler_params=_params("parallel"),
        name="moe_combine",
    )(h2d, ys, gs, a1, a1t, wgu_s, wd_s, vec)


def _dispatch_tables(chunks):
    nt, E = chunks.shape
    maxc = nt * MOE_CPT
    nblk = -(-maxc // MOE_CB) + E
    hp = lax.Precision.HIGHEST
    ch = chunks.astype(F32)
    off = jnp.cumsum(ch, axis=1) - ch
    wend = jnp.cumsum(ch, axis=0)
    cnt_e = wend[-1]
    cend = jnp.cumsum(cnt_e)
    base_e = cend - cnt_e
    srcbase = jnp.arange(nt, dtype=F32)[:, None] * MOE_CPT + off - (wend - ch)
    p = jnp.arange(maxc, dtype=F32)
    e_p = jnp.minimum(jnp.sum((cend[None, :] <= p[:, None]).astype(jnp.int32), axis=1), E - 1)
    oh_e = (e_p[:, None] == jnp.arange(E, dtype=jnp.int32)[None, :]).astype(F32)
    look = jnp.dot(oh_e, jnp.concatenate([wend.T, srcbase.T, base_e[:, None]], axis=1), precision=hp)
    q = p - look[:, 2 * nt]
    i_p = jnp.sum((look[:, :nt] <= q[:, None]).astype(jnp.int32), axis=1)
    oh_i = i_p[:, None] == jnp.arange(nt, dtype=jnp.int32)[None, :]
    src = jnp.sum(jnp.where(oh_i, look[:, nt:2 * nt], 0.0), axis=1) + q
    src = jnp.clip(src, 0, maxc - 1).astype(jnp.int32)
    nblk_e = jnp.floor((cnt_e + (MOE_CB - 1)) * (1.0 / MOE_CB))
    bend = jnp.cumsum(nblk_e)
    bidx = jnp.arange(nblk, dtype=F32)
    be = jnp.minimum(jnp.sum((bend[None, :] <= bidx[:, None]).astype(jnp.int32), axis=1), E - 1)
    oh_b = (be[:, None] == jnp.arange(E, dtype=jnp.int32)[None, :]).astype(F32)
    lookb = jnp.dot(oh_b, jnp.stack([bend - nblk_e, cnt_e, base_e], axis=1), precision=hp)
    q0 = (bidx - lookb[:, 0]) * MOE_CB
    nbc = jnp.clip(lookb[:, 1] - q0, 0, MOE_CB).astype(jnp.int32)
    pstart = jnp.clip(lookb[:, 2] + q0, 0, maxc - MOE_CB).astype(jnp.int32)
    return be.astype(jnp.int32), pstart, nbc, src


def _moe_layer(h2d, layer, router, bias, w_gate, w_up, w_down, sh_gate, sh_up, sh_down, ln_g, ln_b):
    router_t = router.T.astype(F32)
    bias_col = jnp.broadcast_to(bias.astype(F32)[:, None], (N_EXPERTS, LANES))
    xs, gs, a1, a1t, cnt = _router(h2d, router_t, bias_col)
    be, pstart, nbc, src = _dispatch_tables(cnt[:, 0, :].astype(jnp.int32))
    ys = _experts(xs, w_gate, w_up, w_down, layer, be, pstart, nbc, src)
    wgu_s = jnp.concatenate([sh_gate, sh_up], axis=-1).astype(BF16)
    vec = jnp.zeros((SUBLANES, D_MODEL), F32).at[0].set(ln_g).at[1].set(ln_b)
    return _combine(h2d, ys, gs, a1, a1t, wgu_s, sh_down.astype(BF16), vec)


def _rows(*vs):
    out = jnp.zeros((SUBLANES, vs[0].shape[-1]), F32)
    for i, v in enumerate(vs):
        out = out.at[i].set(v.astype(F32))
    return out


def kernel(x, meta, rel_bias, rwkv_mu, rwkv_w0, rwkv_w1, rwkv_w2, rwkv_a0, rwkv_a1, rwkv_a2, rwkv_g1, rwkv_g2, rwkv_k_k, rwkv_k_a, rwkv_r_k, rwkv_w_r, rwkv_w_k, rwkv_w_v, rwkv_w_o, rwkv_lnx_g, rwkv_lnx_b, attn_w_qkv, attn_b_qkv, attn_sinks, attn_w_o, attn_b_o, ln_mix_g, ln_mix_b, ln_ffn_g, ln_ffn_b, moe_router, moe_bias, moe_w_gate, moe_w_up, moe_w_down, shared_w_gate, shared_w_up, shared_w_down):
    B, S, D = x.shape
    assert D == D_MODEL and S % ATT_BLOCK == 0 and (FRONT + N_META) % WKV_CHUNK == 0
    lp = FRONT + N_META + S
    T = B * lp
    assert T % TM == 0 and lp >= TM
    h = jnp.concatenate([jnp.zeros((B, FRONT, D), x.dtype),
                         jnp.broadcast_to(meta[None].astype(x.dtype), (B, N_META, D)), x], axis=1)
    h = h.reshape(T, D)
    bf = lambda w: w.astype(BF16)

    H, N = RWKV_HEADS, RWKV_HEAD
    head_of = jnp.arange(D) // N
    gsum = (head_of[:, None] == jnp.arange(LANES)[None, :]).astype(BF16)
    gexp = gsum.T
    r, w, k, v, kk, b, g = _rwkv_proj(
        h, lp, _rows(*rwkv_mu[0]), _rows(rwkv_w0[0], rwkv_a0[0], rwkv_k_k[0], rwkv_k_a[0]),
        bf(rwkv_w_r[0]), bf(rwkv_w_k[0]), bf(rwkv_w_v[0]), bf(rwkv_w1[0]), bf(rwkv_w2[0]),
        bf(rwkv_a1[0]), bf(rwkv_a2[0]), bf(rwkv_g1[0]), bf(rwkv_g2[0]), gsum, gexp)
    to3 = lambda t: t.reshape(B, lp, D)
    prm = _rows(rwkv_r_k[0].reshape(D), rwkv_lnx_g[0], rwkv_lnx_b[0])
    o = _wkv(to3(r), to3(w), to3(k), to3(v), to3(kk), to3(b), to3(g), prm)
    h = _proj_ln(o.reshape(T, D), bf(rwkv_w_o[0]), _rows(jnp.zeros((D,), F32), ln_mix_g[0], ln_mix_b[0]), h)
    h = _moe_layer(h, 0, moe_router[0], moe_bias[0], moe_w_gate, moe_w_up, moe_w_down,
                   shared_w_gate[0], shared_w_up[0], shared_w_down[0], ln_ffn_g[0], ln_ffn_b[0])

    HD, KV = ATT_HEAD_DIM, ATT_KV_HEADS
    qw = ATT_HEADS * HD
    wqkv, bqkv = attn_w_qkv[0], attn_b_qkv[0]
    dup = lambda t: jnp.concatenate([t.reshape(-1, KV, 1, HD)] * 2, axis=2).reshape(t.shape[0], 2 * KV * HD)
    wq, wkd, wvd = wqkv[:, :qw], dup(wqkv[:, qw:qw + KV * HD]), dup(wqkv[:, qw + KV * HD:])
    bq, bkd, bvd = bqkv[None, :qw], dup(bqkv[None, qw:qw + KV * HD]), dup(bqkv[None, qw + KV * HD:])
    q, kd, vd = _qkv(h, bf(wq), bf(wkd), bf(wvd), bq.astype(F32), bkd.astype(F32), bvd.astype(F32))
    bband, bmeta, bmm = _bias_tables(rel_bias, S // ATT_BLOCK)
    o = _attention(q.reshape(B, lp, qw), kd.reshape(B, lp, -1), vd.reshape(B, lp, -1),
                   attn_sinks[0].astype(F32), bband, bmeta, bmm)
    h = _proj_ln(o.reshape(T, D), bf(attn_w_o[0]), _rows(attn_b_o[0], ln_mix_g[1], ln_mix_b[1]), h)
    h = _moe_layer(h, 1, moe_router[1], moe_bias[1], moe_w_gate, moe_w_up, moe_w_down,
                   shared_w_gate[1], shared_w_up[1], shared_w_down[1], ln_ffn_g[1], ln_ffn_b[1])
    return h.reshape(B, lp, D)[:, FRONT + N_META:]
```

```python
import functools
import math

import jax
import jax.numpy as jnp
from jax import lax
from jax.experimental import pallas as pl
from jax.experimental.pallas import tpu as pltpu

F32 = jnp.float32
BF16 = jnp.bfloat16

D_MODEL = 1024
DEPTH = 2
N_META = 16
RWKV_HEAD = 64
RWKV_HEADS = D_MODEL // RWKV_HEAD
GN_EPS = 64e-5
ATT_HEADS = 16
ATT_KV_HEADS = 4
ATT_HEAD_DIM = D_MODEL // ATT_HEADS
ATT_GROUP = ATT_HEADS // ATT_KV_HEADS
WINDOW = 128
ATT_BLOCK = 128
N_BUCKETS = 32
MAX_DISTANCE = 128
N_EXPERTS = 64
TOP_K = 8
N_GROUPS = 8
TOPK_GROUPS = 4
EXPERT_FF = 256
ROUTED_SCALE = 2.5
DEEPNORM_ALPHA = (2 * DEPTH) ** 0.25
LN_EPS = 1e-5

LANES = 128
SUBLANES = 8
BF16_ROWS = 16
VMEM_LIMIT = 56 * 1024 * 1024

FRONT = 48
TM = 256
WKV_CHUNK = 64
WKV_PAIRS = 8
MOE_CH = BF16_ROWS
MOE_SLOTS = TM * TOP_K + N_EXPERTS * MOE_CH
MOE_CPT = MOE_SLOTS // MOE_CH
MOE_CB = 64
MOE_SEG = 32
NEG = -1e30

_NN = (((1,), (0,)), ((), ()))
_NT = (((1,), (1,)), ((), ()))


def _dot(a, b, dn=_NN):
    return lax.dot_general(a.astype(BF16), b.astype(BF16), dn, preferred_element_type=F32)


def _split2(x):
    hi = x.astype(BF16)
    lo = (x - hi.astype(F32)).astype(BF16)
    return hi, lo


def _split3(x):
    h1 = x.astype(BF16)
    r1 = x - h1.astype(F32)
    h2 = r1.astype(BF16)
    h3 = (r1 - h2.astype(F32)).astype(BF16)
    return h1, h2, h3


def _dot_exact_lhs(a01, b, dn=_NN):
    a = a01.astype(BF16)
    return sum(lax.dot_general(a, p, dn, preferred_element_type=F32) for p in _split3(b))


def _dot_hilo_rhs(a, b01, dn=_NN):
    b = b01.astype(BF16)
    return sum(lax.dot_general(p, b, dn, preferred_element_type=F32) for p in _split2(a))


def _dot3(a, b, dn=_NN):
    ah, al = _split2(a)
    bh, bl = _split2(b)
    d = lambda x, y: lax.dot_general(x, y, dn, preferred_element_type=F32)
    return d(ah, bh) + (d(ah, bl) + d(al, bh))


def _sigmoid(x):
    return 1.0 / (1.0 + jnp.exp(-x))


def _silu(x):
    return x * _sigmoid(x)


def _layer_norm(x, g, b):
    mu = jnp.mean(x, axis=-1, keepdims=True)
    xc = x - mu
    var = jnp.mean(xc * xc, axis=-1, keepdims=True)
    return xc * lax.rsqrt(var + LN_EPS) * g + b


def _full(shape):
    nd = len(shape)
    return pl.BlockSpec(shape, lambda *_: (0,) * nd)


def _params(*sem):
    return pltpu.CompilerParams(dimension_semantics=sem, vmem_limit_bytes=VMEM_LIMIT)


def _rwkv_proj_kernel(x_ref, xp_ref, mu_ref, vec_ref, wr_ref, wk_ref, wv_ref, w1_ref, w2_ref,
                      a1_ref, a2_ref, g1_ref, g2_ref, gsum_ref, gexp_ref,
                      r_out, w_out, k_out, v_out, kk_out, b_out, g_out, *, lp):
    i = pl.program_id(0)
    x = x_ref[...]
    row = lax.broadcasted_iota(jnp.int32, (TM, 1), 0)
    pos = lax.rem(i * TM, lp) + row
    pos = jnp.where(pos >= lp, pos - lp, pos)
    prev = jnp.where(row == 0, xp_ref[SUBLANES - 1:SUBLANES, :], pltpu.roll(x, 1, axis=0))
    prev = jnp.where(pos == FRONT, 0.0, prev)
    xx = prev - x
    valid = pos >= FRONT

    def mix(j):
        return (x + xx * mu_ref[j:j + 1, :]).astype(BF16)

    w0, a0, k_k, k_a = (vec_ref[j:j + 1, :] for j in range(4))
    r = _dot(mix(0), wr_ref[...])
    z = w0 + _dot(jnp.tanh(_dot(mix(1), w1_ref[...])), w2_ref[...])
    w = -math.exp(-0.5) * _sigmoid(z)
    k = _dot(mix(2), wk_ref[...])
    v = _dot(mix(3), wv_ref[...])
    a = _sigmoid(a0 + _dot(_dot(mix(4), a1_ref[...]), a2_ref[...]))
    g = _dot(_sigmoid(_dot(mix(5), g1_ref[...])), g2_ref[...])
    kk = k * k_k
    ssq = _dot_hilo_rhs(kk * kk, gsum_ref[...])
    nrm = jnp.sqrt(_dot_hilo_rhs(ssq, gexp_ref[...]))
    kk = kk / jnp.maximum(nrm, 1e-12)
    k = k * (1.0 + (a - 1.0) * k_a)
    r_out[...] = r
    w_out[...] = jnp.where(valid, w, 0.0)
    k_out[...] = jnp.where(valid, k, 0.0)
    v_out[...] = jnp.where(valid, v, 0.0)
    kk_out[...] = jnp.where(valid, kk, 0.0)
    b_out[...] = jnp.where(valid, kk * a, 0.0)
    g_out[...] = g


def _rwkv_proj(h2d, lp, mu, vec, wr, wk, wv, w1, w2, a1, a2, g1, g2, gsum, gexp):
    T, D = h2d.shape
    row_spec = pl.BlockSpec((TM, D), lambda i: (i, 0))
    prev_spec = pl.BlockSpec((SUBLANES, D), lambda i: (jnp.maximum(i * (TM // SUBLANES) - 1, 0), 0))
    ws = [mu, vec, wr, wk, wv, w1, w2, a1, a2, g1, g2, gsum, gexp]
    return pl.pallas_call(
        functools.partial(_rwkv_proj_kernel, lp=lp),
        grid=(T // TM,),
        in_specs=[row_spec, prev_spec] + [_full(w.shape) for w in ws],
        out_specs=[row_spec] * 7,
        out_shape=[jax.ShapeDtypeStruct((T, D), F32)] * 7,
        compiler_params=_params("parallel"),
        name="rwkv_proj",
    )(h2d, h2d, *ws)


def _wkv_chunk(r, w, cum, k, v, kk, b, S, c):
    C = WKV_CHUNK
    P = range(len(r))
    bf = lambda xs: [x.astype(BF16) for x in xs]
    each = lambda f, *ls: [f(*a) for a in zip(*ls)]

    def stack(x):
        return jnp.concatenate([jnp.where(c["m0"], x, 0.0), jnp.where(c["m0"], 0.0, x)], axis=0)

    f32 = lambda xs: [x.astype(F32) for x in xs]
    cat0 = lambda x, y: jnp.concatenate([x, y], axis=0)
    cat1 = lambda x, y: jnp.concatenate([x, y], axis=1)
    nt = lambda x, y: _dot(x, y, _NT)
    C2 = 2 * C

    tot = [cum[p][C - 1:C, :] for p in P]
    inv = [jnp.exp(-cum[p]) for p in P]
    dend = [jnp.exp(tot[p] - cum[p]) for p in P]
    kr_s = bf([cat0(stack(kk[p] * jnp.exp(cum[p] - w[p])), stack(r[p] * jnp.exp(cum[p]))) for p in P])
    bk_s = bf([cat0(stack(b[p] * inv[p]), stack(k[p] * inv[p])) for p in P])
    bh_s = bf([stack(b[p] * dend[p]) for p in P])
    kh_s = bf([stack(k[p] * dend[p]) for p in P])
    vs_t = bf([stack(v[p]).T for p in P])

    sc = each(nt, bk_s, kr_s)
    lab_t = [jnp.where(c["upper"], x[:C2, :C2], 0.0) for x in sc]
    arb_t = bf([jnp.where(c["upinc"], x[:C2, C2:], 0.0) for x in sc])
    lak_t = bf([jnp.where(c["upper"], x[C2:, :C2], 0.0) for x in sc])
    ark_t = bf([jnp.where(c["upinc"], x[C2:, C2:], 0.0) for x in sc])

    eye = c["eye"]
    ud = [jnp.where(c["blk"], x, 0.0) for x in lab_t]
    nu = bf([x - y for x, y in zip(lab_t, ud)])
    udb = bf(ud)
    u2b = bf(each(_dot, udb, udb))
    u2 = f32(u2b)
    t1 = each(_dot, bf([eye + x for x in u2]), [cat1(x, (eye - y).astype(BF16)) for x, y in zip(u2b, ud)])
    u4b = bf([x[:, :C2] - y for x, y in zip(t1, u2)])
    u4 = f32(u4b)
    t2 = each(_dot, bf([eye + x for x in u4]), [cat1(x, y[:, C2:].astype(BF16)) for x, y in zip(u4b, t1)])
    u8 = [x[:, :C2] - y for x, y in zip(t2, u4)]
    dinv = each(_dot, bf([eye + x for x in u8]), bf([x[:, C2:] for x in t2]))
    dinvb = bf(dinv)
    wzb = bf(each(_dot, dinvb, nu))
    t3 = each(_dot, wzb, [cat1(x, y) for x, y in zip(wzb, dinvb)])
    tt_t = bf(each(_dot, bf([eye + x[:, :C2] for x in t3]), bf([y - x[:, C2:] for x, y in zip(t3, dinv)])))

    sb = bf(S)
    p2 = each(_dot, vs_t, [cat1(x, y) for x, y in zip(lak_t, ark_t)])
    p3 = each(_dot, vs_t, kh_s)
    p1 = each(nt, sb, kr_s)
    u_t = bf(each(_dot, bf([-(x[:, :C2] + y[:, :C2]) for x, y in zip(p1, p2)]), tt_t))
    p4 = each(_dot, u_t, [cat1(x, y) for x, y in zip(arb_t, bh_s)])
    s_new = [S[p] * jnp.exp(tot[p]) + p4[p][:, C2:] + p3[p] for p in P]
    y_t = [p1[p][:, C2:] + p4[p][:, :C2] + p2[p][:, C2:] for p in P]
    return y_t, s_new


def _wkv_group_norm(y_t, hblk):
    C = WKV_CHUNK
    mean = jnp.sum(y_t, axis=0, keepdims=True) * (1.0 / RWKV_HEAD)
    yc = jnp.where(hblk, y_t - mean, 0.0)
    var = jnp.sum(yc * yc, axis=0, keepdims=True) * (1.0 / RWKV_HEAD)
    ys = (yc * lax.rsqrt(var + GN_EPS)).T
    return ys[:C] + ys[C:]


def _wkv_kernel(r_ref, w_ref, k_ref, v_ref, kk_ref, b_ref, g_ref, prm_ref, o_ref, yt_ref, *s_refs, lb):
    C = WKV_CHUNK

    @pl.when(pl.program_id(2) == 0)
    def _():
        for s_ref in s_refs:
            s_ref[...] = jnp.zeros_like(s_ref)

    ri = lax.broadcasted_iota(jnp.int32, (2 * C, 2 * C), 0)
    ci = lax.broadcasted_iota(jnp.int32, (2 * C, 2 * C), 1)
    lane = lax.broadcasted_iota(jnp.int32, (1, LANES), 1)
    ti = lax.broadcasted_iota(jnp.int32, (C, C), 0)
    tj = lax.broadcasted_iota(jnp.int32, (C, C), 1)
    consts = dict(
        m0=lane < RWKV_HEAD,
        upper=ri < ci,
        upinc=ri <= ci,
        blk=(ri // 16) == (ci // 16),
        hblk=(ri // RWKV_HEAD) == (ci // C),
        eye=(ri == ci).astype(F32),
    )
    m0 = consts["m0"]
    tril = (ti >= tj).astype(BF16)

    def head_sum(x):
        s0 = jnp.sum(jnp.where(m0, x, 0.0), axis=-1, keepdims=True)
        s1 = jnp.sum(jnp.where(m0, 0.0, x), axis=-1, keepdims=True)
        return jnp.where(m0, s0, s1)

    lns = [slice(p * LANES, (p + 1) * LANES) for p in range(WKV_PAIRS)]

    def recurrence(ci_):
        rows = pl.ds(pl.multiple_of(ci_ * C, C), C)
        ld = lambda ref: [ref[rows, ln] for ln in lns]
        r, w, k, v, kk, b = ld(r_ref), ld(w_ref), ld(k_ref), ld(v_ref), ld(kk_ref), ld(b_ref)
        cum_all = _dot_exact_lhs(tril, w_ref[rows, :])
        cum = [cum_all[:, ln] for ln in lns]
        ys, s_new = _wkv_chunk(r, w, cum, k, v, kk, b, [s[...] for s in s_refs], consts)
        for p, ln in enumerate(lns):
            s_refs[p][...] = s_new[p]
            yt_ref[:, ln] = ys[p]

    def finish(ci_):
        rows = pl.ds(pl.multiple_of(ci_ * C, C), C)
        for ln in lns:
            r_k, lg, lb_ = prm_ref[0:1, ln], prm_ref[1:2, ln], prm_ref[2:3, ln]
            yn = _wkv_group_norm(yt_ref[:, ln], consts["hblk"])
            bonus = head_sum(r_ref[rows, ln] * k_ref[rows, ln] * r_k) * v_ref[rows, ln]
            o_ref[rows, ln] = ((yn * lg + lb_ + bonus) * g_ref[rows, ln]).astype(o_ref.dtype)

    def chunk(ci_, carry):
        finish(ci_ - 1)
        recurrence(ci_)
        return carry

    recurrence(0)
    lax.fori_loop(1, lb // C, chunk, 0)
    finish(lb // C - 1)


def _wkv_row_block(lp):
    nch = lp // WKV_CHUNK
    for d in (11, 8, 6, 4, 3, 2, 1):
        if nch % d == 0:
            return d * WKV_CHUNK
    return WKV_CHUNK


def _wkv(r, w, k, v, kk, b, g, prm):
    B, lp, D = r.shape
    lb = _wkv_row_block(lp)
    wl = WKV_PAIRS * LANES
    spec = pl.BlockSpec((None, lb, wl), lambda bi, pi, li: (bi, li, pi))
    return pl.pallas_call(
        functools.partial(_wkv_kernel, lb=lb),
        grid=(B, D // wl, lp // lb),
        in_specs=[spec] * 7 + [pl.BlockSpec((SUBLANES, wl), lambda bi, pi, li: (0, pi))],
        out_specs=spec,
        out_shape=jax.ShapeDtypeStruct((B, lp, D), BF16),
        scratch_shapes=[pltpu.VMEM((LANES, wl), F32)] + [pltpu.VMEM((LANES, LANES), F32)] * WKV_PAIRS,
        compiler_params=_params("parallel", "parallel", "arbitrary"),
        name="wkv7",
    )(r, w, k, v, kk, b, g, prm)


def _proj_ln_kernel(a_ref, w_ref, vec_ref, h_ref, o_ref):
    mix = _dot(a_ref[...], w_ref[...]) + vec_ref[0:1, :]
    o_ref[...] = _layer_norm(DEEPNORM_ALPHA * h_ref[...] + mix, vec_ref[1:2, :], vec_ref[2:3, :])


def _proj_ln(a2d, w, vec, h2d):
    T, D = h2d.shape
    K = a2d.shape[1]
    return pl.pallas_call(
        _proj_ln_kernel,
        grid=(T // TM,),
        in_specs=[pl.BlockSpec((TM, K), lambda i: (i, 0)), _full(w.shape), _full(vec.shape),
                  pl.BlockSpec((TM, D), lambda i: (i, 0))],
        out_specs=pl.BlockSpec((TM, D), lambda i: (i, 0)),
        out_shape=jax.ShapeDtypeStruct((T, D), F32),
        compiler_params=_params("parallel"),
        name="proj_ln",
    )(a2d, w, vec, h2d)


def _qkv_kernel(x_ref, wq_ref, wk_ref, wv_ref, bq_ref, bk_ref, bv_ref, q_out, k_out, v_out):
    x = x_ref[...].astype(BF16)
    q_out[...] = ((_dot(x, wq_ref[...]) + bq_ref[...]) * ATT_HEAD_DIM ** -0.5).astype(BF16)
    k_out[...] = (_dot(x, wk_ref[...]) + bk_ref[...]).astype(BF16)
    v_out[...] = (_dot(x, wv_ref[...]) + bv_ref[...]).astype(BF16)


def _qkv(h2d, wq, wkd, wvd, bq, bkd, bvd):
    T, D = h2d.shape
    nq, nk = wq.shape[1], wkd.shape[1]
    ws = [wq, wkd, wvd, bq, bkd, bvd]
    return pl.pallas_call(
        _qkv_kernel,
        grid=(T // TM,),
        in_specs=[pl.BlockSpec((TM, D), lambda i: (i, 0))] + [_full(w.shape) for w in ws],
        out_specs=[pl.BlockSpec((TM, nq), lambda i: (i, 0)), pl.BlockSpec((TM, nk), lambda i: (i, 0)),
                   pl.BlockSpec((TM, nk), lambda i: (i, 0))],
        out_shape=[jax.ShapeDtypeStruct((T, nq), BF16), jax.ShapeDtypeStruct((T, nk), BF16),
                   jax.ShapeDtypeStruct((T, nk), BF16)],
        compiler_params=_params("parallel"),
        name="qkv_proj",
    )(h2d, *ws)


def _attn_kernel(sink_ref, q_ref, k_ref, v_ref, bband_ref, bmeta_ref, bmm_ref, o_ref, *, nb):
    c = pl.program_id(0)
    BLK = ATT_BLOCK
    M0 = FRONT
    lane = lax.broadcasted_iota(jnp.int32, (1, LANES), 1)
    m0 = lane < ATT_HEAD_DIM
    qi = lax.broadcasted_iota(jnp.int32, (2 * BLK, BLK), 0) % BLK
    sj = lax.broadcasted_iota(jnp.int32, (2 * BLK, BLK), 1)
    cur_vis = sj <= qi
    prev_vis = sj > qi
    half = lax.broadcasted_iota(jnp.int32, (2 * BLK, 1), 0) < BLK

    def stack(x):
        z = jnp.zeros_like(x)
        return jnp.concatenate([jnp.where(m0, x, z), jnp.where(m0, z, x)], axis=0)

    HP = range(4)
    kl = [slice((hp // 2) * LANES, (hp // 2 + 1) * LANES) for hp in HP]
    ql = [slice(hp * LANES, (hp + 1) * LANES) for hp in HP]

    def rowmax(lgs, floor):
        by_width = {}
        for lg in lgs:
            by_width.setdefault(lg.shape[1], []).append(lg)
        tops = [jnp.max(functools.reduce(jnp.maximum, g), axis=-1, keepdims=True) for g in by_width.values()]
        return functools.reduce(jnp.maximum, tops, floor)

    def attend(segs, sinks, n):
        mx = [rowmax([lg for lg, _ in segs[hp]], sinks[hp]) for hp in HP]
        ps = [[jnp.exp(lg - mx[hp]).astype(BF16) for lg, _ in segs[hp]] for hp in HP]
        one = jnp.ones((1, LANES), BF16)
        va = [[jnp.where(m0, vals, one) for _, vals in segs[hp]] for hp in HP]
        vb = [[jnp.where(m0, one, vals) for _, vals in segs[hp]] for hp in HP]
        oa = [sum(_dot(p[:n], v) for p, v in zip(ps[hp], va[hp])) for hp in HP]
        ob = [sum(_dot(p[n:], v) for p, v in zip(ps[hp], vb[hp])) for hp in HP]
        out = []
        for hp in HP:
            st = jnp.exp(sinks[hp] - mx[hp])
            den = pltpu.roll(jnp.where(m0, ob[hp], oa[hp]), ATT_HEAD_DIM, axis=1)
            out.append(jnp.where(m0, oa[hp], ob[hp]) / (den + jnp.where(m0, st[:n], st[n:])))
        return out

    o_ref[0:M0, :] = jnp.zeros((M0, o_ref.shape[1]), o_ref.dtype)
    s0 = [sink_ref[c * 8 + hp * 2] for hp in HP]
    s1 = [sink_ref[c * 8 + hp * 2 + 1] for hp in HP]
    k_meta = [k_ref[M0:M0 + N_META, kl[hp]] for hp in HP]
    v_meta = [v_ref[M0:M0 + N_META, kl[hp]] for hp in HP]

    first = lax.broadcasted_iota(jnp.int32, (2 * N_META, 1), 0) < N_META
    mi = lax.broadcasted_iota(jnp.int32, (2 * N_META, N_META), 0) % N_META
    mj = lax.broadcasted_iota(jnp.int32, (2 * N_META, N_META), 1)
    qm = [stack(q_ref[M0:M0 + N_META, ql[hp]]) for hp in HP]
    lg = [jnp.where(mj <= mi, _dot(qm[hp], k_meta[hp], _NT) + bmm_ref[hp], NEG) for hp in HP]
    om = attend([[(lg[hp], v_meta[hp])] for hp in HP], [jnp.where(first, s0[hp], s1[hp]) for hp in HP], N_META)
    for hp in HP:
        o_ref[M0:M0 + N_META, ql[hp]] = om[hp].astype(o_ref.dtype)

    sink_q = [jnp.where(half, s0[hp], s1[hp]) for hp in HP]

    def block(j, carry):
        start = pl.multiple_of(M0 + N_META + j * BLK, 64)
        pstart = pl.multiple_of(jnp.maximum(start - BLK, 0), 64)
        qs = [stack(q_ref[pl.ds(start, BLK), ql[hp]]) for hp in HP]
        k_prev = [k_ref[pl.ds(pstart, BLK), kl[hp]] for hp in HP]
        k_cur = [k_ref[pl.ds(start, BLK), kl[hp]] for hp in HP]
        lg_meta = [_dot(qs[hp], k_meta[hp], _NT) + bmeta_ref[hp, j] for hp in HP]
        lg_prev = [jnp.where(prev_vis & (j > 0), _dot(qs[hp], k_prev[hp], _NT) + bband_ref[hp, :, 0:BLK], NEG)
                   for hp in HP]
        lg_cur = [jnp.where(cur_vis, _dot(qs[hp], k_cur[hp], _NT) + bband_ref[hp, :, BLK:2 * BLK], NEG)
                  for hp in HP]
        out = attend([[(lg_meta[hp], v_meta[hp]), (lg_prev[hp], v_ref[pl.ds(pstart, BLK), kl[hp]]),
                       (lg_cur[hp], v_ref[pl.ds(start, BLK), kl[hp]])] for hp in HP], sink_q, BLK)
        for hp in HP:
            o_ref[pl.ds(start, BLK), ql[hp]] = out[hp].astype(o_ref.dtype)
        return carry

    lax.fori_loop(0, nb, block, 0)


def _attention(q, kd, vd, sinks, bband, bmeta, bmm):
    B, lp, _ = q.shape
    nb = (lp - FRONT - N_META) // ATT_BLOCK
    qw, kw = 4 * LANES, 2 * LANES
    return pl.pallas_call(
        functools.partial(_attn_kernel, nb=nb),
        grid=(2, B),
        in_specs=[pl.BlockSpec(memory_space=pltpu.SMEM),
                  pl.BlockSpec((None, lp, qw), lambda c, b: (b, 0, c)),
                  pl.BlockSpec((None, lp, kw), lambda c, b: (b, 0, c)),
                  pl.BlockSpec((None, lp, kw), lambda c, b: (b, 0, c)),
                  pl.BlockSpec((4,) + bband.shape[1:], lambda c, b: (c, 0, 0)),
                  pl.BlockSpec((4,) + bmeta.shape[1:], lambda c, b: (c, 0, 0, 0)),
                  pl.BlockSpec((4,) + bmm.shape[1:], lambda c, b: (c, 0, 0))],
        out_specs=pl.BlockSpec((None, lp, qw), lambda c, b: (b, 0, c)),
        out_shape=jax.ShapeDtypeStruct((B, lp, D_MODEL), BF16),
        compiler_params=_params("parallel", "parallel"),
        name="swa_attention",
    )(sinks, q, kd, vd, bband, bmeta, bmm)


def _t5_bucket(dist):
    exact = N_BUCKETS // 2
    d = jnp.maximum(dist, 0)
    ratio = jnp.log(jnp.maximum(d, 1).astype(F32) / exact) / math.log(MAX_DISTANCE / exact)
    large = jnp.minimum(exact + (ratio * (N_BUCKETS - exact)).astype(jnp.int32), N_BUCKETS - 1)
    return jnp.where(d < exact, d, large)


def _bias_tables(rel_bias, nb):
    H, BLK = ATT_HEADS, ATT_BLOCK

    def lookup(dist):
        onehot = (_t5_bucket(dist)[..., None] == jnp.arange(N_BUCKETS)).astype(F32)
        return jnp.dot(onehot, rel_bias.astype(F32), precision=lax.Precision.HIGHEST)

    qi = jnp.arange(BLK)[:, None]
    band = lookup(qi + BLK - jnp.arange(2 * BLK)[None, :])
    band = jnp.moveaxis(band, -1, 0).reshape(H // 2, 2 * BLK, 2 * BLK)
    pos = jnp.arange(nb * BLK)[:, None]
    meta = lookup(N_META + pos - jnp.arange(N_META)[None, :])
    meta = meta.reshape(nb, BLK, N_META, H // 2, 2).transpose(3, 0, 4, 1, 2)
    meta = meta.reshape(H // 2, nb, 2 * BLK, N_META)
    pm = jnp.arange(N_META)
    mm = lookup(pm[:, None] - pm[None, :])
    mm = jnp.moveaxis(mm, -1, 0).reshape(H // 2, 2 * N_META, N_META)
    return band.astype(F32), meta.astype(F32), mm.astype(F32)


def _dispatch_geometry(a1):
    E = N_EXPERTS
    routed = (a1 > 0.0).astype(BF16)
    n_col = jnp.sum(routed.astype(F32), axis=1, keepdims=True)
    ch_col = jnp.floor((n_col + (MOE_CH - 1)) * (1.0 / MOE_CH))
    ei = lax.broadcasted_iota(jnp.int32, (E, E), 0)
    ej = lax.broadcasted_iota(jnp.int32, (E, E), 1)
    off_col = _dot((ei > ej).astype(BF16), jnp.broadcast_to(ch_col, (E, LANES)))[:, 0:1]
    n_row = _dot(jnp.ones((SUBLANES, TM), BF16), routed, _NT)
    ch_row = jnp.floor((n_row + (MOE_CH - 1)) * (1.0 / MOE_CH))
    off_row = _dot(ch_row, (ei < ej).astype(BF16))
    return ch_col, off_col, ch_row, off_row


def _router_kernel(x_ref, rt_ref, bias_ref, xs_out, gs_out, a1_out, a1t_out, cnt_out):
    E, G, EPG = N_EXPERTS, N_GROUPS, N_EXPERTS // N_GROUPS
    x = x_ref[...]
    logits = _dot3(rt_ref[...], x, _NT)
    scores = _sigmoid(logits)
    choice = scores + bias_ref[:, 0:1]

    grp = choice.reshape(G, EPG, TM)
    sub = lax.broadcasted_iota(jnp.int32, (G, EPG, TM), 1)
    top1 = jnp.max(grp, axis=1, keepdims=True)
    first = jnp.min(jnp.where(grp == top1, sub, EPG), axis=1, keepdims=True)
    top2 = jnp.max(jnp.where(sub == first, -jnp.inf, grp), axis=1, keepdims=True)
    gscore = jnp.broadcast_to(top1 + top2, (G, EPG, TM))

    gi = lax.broadcasted_iota(jnp.int32, (G, EPG, TM), 0)
    keep = jnp.zeros((G, EPG, TM), jnp.bool_)
    for _ in range(TOPK_GROUPS):
        m = jnp.max(gscore, axis=0, keepdims=True)
        sel = gi == jnp.min(jnp.where(gscore == m, gi, G), axis=0, keepdims=True)
        keep = keep | sel
        gscore = jnp.where(sel, -jnp.inf, gscore)
    keep_e = keep.reshape(E, TM)

    cand = jnp.where(keep_e, choice, -jnp.inf)
    ei = lax.broadcasted_iota(jnp.int32, (E, TM), 0)
    routed = jnp.zeros((E, TM), jnp.bool_)
    for _ in range(TOP_K):
        m = jnp.max(cand, axis=0, keepdims=True)
        sel = ei == jnp.min(jnp.where(cand == m, ei, E), axis=0, keepdims=True)
        routed = routed | sel
        cand = jnp.where(sel, -jnp.inf, cand)
    gate = jnp.where(routed, scores, 0.0)
    gate = gate / jnp.sum(gate, axis=0, keepdims=True) * ROUTED_SCALE

    ti = lax.broadcasted_iota(jnp.int32, (TM, TM), 0)
    tj = lax.broadcasted_iota(jnp.int32, (TM, TM), 1)
    routed_b = routed.astype(BF16)
    rank = _dot(routed_b, (ti < tj).astype(BF16))
    a1 = jnp.where(routed, rank + 1.0, 0.0)
    rank_t = _dot((tj < ti).astype(BF16), routed_b, _NT)
    a1t_out[...] = _dot((ti == tj).astype(BF16), routed_b, _NT) * (rank_t + 1.0)
    _, _, ch_row, off_row = _dispatch_geometry(a1)

    lo = jnp.concatenate([off_row[0:1, :]] * 2, axis=1) * MOE_CH
    hi = lo + jnp.concatenate([ch_row[0:1, :]] * 2, axis=1) * MOE_CH
    first = lax.broadcasted_iota(jnp.int32, (1, 2 * E), 1) < E
    lo1 = jnp.where(first, lo, 0.0)
    gate_t = _dot_exact_lhs((ti == tj).astype(BF16), gate, _NT)
    g_hi, g_lo = _split2(gate_t)
    xg = jnp.concatenate([x.astype(BF16), g_hi, g_lo], axis=1)
    a1b = a1.astype(BF16)
    SB = MOE_SLOTS // 3
    for s0 in range(0, MOE_SLOTS, SB):
        si = (lax.broadcasted_iota(jnp.int32, (SB, 2 * E), 0) + s0).astype(F32)
        member = (si >= lo) & (si < hi)
        seg0 = jnp.sum(jnp.where(member, lo1, 0.0), axis=1, keepdims=True)
        rs1 = (lax.broadcasted_iota(jnp.int32, (SB, 1), 0) + (s0 + 1)).astype(F32) - seg0
        perm = _dot(member[:, :E], a1b) == rs1
        disp = _dot(perm, xg)
        xs_out[s0:s0 + SB, :] = disp[:, :D_MODEL].astype(BF16)
        gs_out[s0:s0 + SB, :] = jnp.sum(jnp.where(member, disp[:, D_MODEL:], 0.0), axis=1, keepdims=True)
    a1_out[...] = a1
    cnt_out[...] = ch_row


def _router(h2d, router_t, bias_col):
    T, D = h2d.shape
    nt = T // TM
    return pl.pallas_call(
        _router_kernel,
        grid=(nt,),
        in_specs=[pl.BlockSpec((TM, D), lambda i: (i, 0)), _full(router_t.shape), _full(bias_col.shape)],
        out_specs=[pl.BlockSpec((None, MOE_SLOTS, D), lambda i: (i, 0, 0)),
                   pl.BlockSpec((None, MOE_SLOTS, 1), lambda i: (i, 0, 0)),
                   pl.BlockSpec((None, N_EXPERTS, TM), lambda i: (i, 0, 0)),
                   pl.BlockSpec((None, TM, N_EXPERTS), lambda i: (i, 0, 0)),
                   pl.BlockSpec((None, SUBLANES, N_EXPERTS), lambda i: (i, 0, 0))],
        out_shape=[jax.ShapeDtypeStruct((nt, MOE_SLOTS, D), BF16),
                   jax.ShapeDtypeStruct((nt, MOE_SLOTS, 1), F32),
                   jax.ShapeDtypeStruct((nt, N_EXPERTS, TM), F32),
                   jax.ShapeDtypeStruct((nt, TM, N_EXPERTS), F32),
                   jax.ShapeDtypeStruct((nt, SUBLANES, N_EXPERTS), F32)],
        compiler_params=_params("parallel"),
        name="moe_router",
    )(h2d, router_t, bias_col)


def _expert_kernel(be_ref, ps_ref, nb_ref, src_ref, xs_hbm, wg_ref, wu_ref, wd_ref, ys_hbm,
                   xbuf, ybuf, wgu_bf, wd_bf, in_sem, out_sem, *, nblk):
    b = pl.program_id(0)
    slot = lax.rem(b, 2)

    def in_copy(src_chunk, sl, c):
        return pltpu.make_async_copy(xs_hbm.at[src_chunk], xbuf.at[sl, c], in_sem.at[sl])

    def out_copy(dst_chunk, sl, c):
        return pltpu.make_async_copy(ybuf.at[sl, c], ys_hbm.at[dst_chunk], out_sem.at[sl])

    def for_chunks(blk, fn):
        p0 = ps_ref[blk]
        n = nb_ref[blk]
        for c0 in range(0, MOE_CB, MOE_SEG):

            @pl.when(n >= c0 + MOE_SEG)
            def _():
                for c in range(c0, c0 + MOE_SEG):
                    fn(src_ref[p0 + c], c)

            @pl.when((n > c0) & (n < c0 + MOE_SEG))
            def _():
                def body(c, carry):
                    fn(src_ref[p0 + c], c)
                    return carry

                lax.fori_loop(c0, n, body, 0)

    @pl.when(b == 0)
    def _():
        xbuf[...] = jnp.zeros_like(xbuf)
        for_chunks(0, lambda s, c: in_copy(s, 0, c).start())

    @pl.when(b + 1 < nblk)
    def _():
        for_chunks(b + 1, lambda s, c: in_copy(s, 1 - slot, c).start())

    for_chunks(b, lambda s, c: in_copy(s, slot, c).wait())

    @pl.when(b >= 2)
    def _():
        for_chunks(b - 2, lambda s, c: out_copy(s, slot, c).wait())

    @pl.when((b == 0) | (be_ref[b] != be_ref[jnp.maximum(b - 1, 0)]))
    def _():
        wgu_bf[:, :EXPERT_FF] = wg_ref[...].astype(BF16)
        wgu_bf[:, EXPERT_FF:] = wu_ref[...].astype(BF16)
        wd_bf[...] = wd_ref[...].astype(BF16)

    for c0 in range(0, MOE_CB, MOE_SEG):

        @pl.when(nb_ref[b] > c0)
        def _():
            x = xbuf[slot, c0:c0 + MOE_SEG].reshape(MOE_SEG * MOE_CH, D_MODEL)
            hid = _dot(x, wgu_bf[...])
            act = _silu(hid[:, :EXPERT_FF]) * hid[:, EXPERT_FF:]
            y = _dot(act, wd_bf[...])
            ybuf[slot, c0:c0 + MOE_SEG] = y.astype(BF16).reshape(MOE_SEG, MOE_CH, D_MODEL)

    for_chunks(b, lambda s, c: out_copy(s, slot, c).start())

    @pl.when(b == nblk - 1)
    def _():
        for_chunks(b, lambda s, c: out_copy(s, slot, c).wait())

        @pl.when(b >= 1)
        def _():
            for_chunks(b - 1, lambda s, c: out_copy(s, 1 - slot, c).wait())


def _experts(xs, wg, wu, wd, layer, be, pstart, nbc, src):
    nt = xs.shape[0]
    nblk = be.shape[0]
    xs_c = xs.reshape(nt * MOE_CPT, MOE_CH, D_MODEL)
    by_expert = lambda b, be, ps, nb, src: (layer, be[b], 0, 0)
    grid_spec = pltpu.PrefetchScalarGridSpec(
        num_scalar_prefetch=4,
        grid=(nblk,),
        in_specs=[pl.BlockSpec(memory_space=pl.ANY),
                  pl.BlockSpec((None, None, D_MODEL, EXPERT_FF), by_expert),
                  pl.BlockSpec((None, None, D_MODEL, EXPERT_FF), by_expert),
                  pl.BlockSpec((None, None, EXPERT_FF, D_MODEL), by_expert)],
        out_specs=pl.BlockSpec(memory_space=pl.ANY),
        scratch_shapes=[pltpu.VMEM((2, MOE_CB, MOE_CH, D_MODEL), BF16),
                        pltpu.VMEM((2, MOE_CB, MOE_CH, D_MODEL), BF16),
                        pltpu.VMEM((D_MODEL, 2 * EXPERT_FF), BF16),
                        pltpu.VMEM((EXPERT_FF, D_MODEL), BF16),
                        pltpu.SemaphoreType.DMA((2,)), pltpu.SemaphoreType.DMA((2,))],
    )
    ys = pl.pallas_call(
        functools.partial(_expert_kernel, nblk=nblk),
        grid_spec=grid_spec,
        out_shape=jax.ShapeDtypeStruct(xs_c.shape, BF16),
        input_output_aliases={4: 0},
        compiler_params=_params("arbitrary"),
        name="moe_experts",
    )(be, pstart, nbc, src, xs_c, wg, wu, wd)
    return ys.reshape(nt, MOE_SLOTS, D_MODEL)


def _combine_kernel(h_ref, ys_ref, gs_ref, a1_ref, a1t_ref, wgu_ref, wd_ref, vec_ref, o_ref):
    E = N_EXPERTS
    h = h_ref[...]
    ch_col, off_col, _, _ = _dispatch_geometry(a1_ref[...])
    si = lax.broadcasted_iota(jnp.int32, (E, MOE_SLOTS), 1).astype(F32)
    lo = off_col * MOE_CH
    member_t = (si >= lo) & (si < lo + ch_col * MOE_CH)
    seg0 = jnp.sum(jnp.where(member_t, lo, 0.0), axis=0, keepdims=True)
    rs1 = lax.broadcasted_iota(jnp.int32, (1, MOE_SLOTS), 1).astype(F32) - seg0 + 1.0
    perm_t = _dot(a1t_ref[...], member_t) == rs1
    routed = _dot(perm_t.astype(BF16), ys_ref[...].astype(F32) * gs_ref[...])
    xb = h.astype(BF16)
    hid = _dot(xb, wgu_ref[...])
    shared = _dot(_silu(hid[:, :EXPERT_FF]) * hid[:, EXPERT_FF:], wd_ref[...])
    o_ref[...] = _layer_norm(DEEPNORM_ALPHA * h + (shared + routed), vec_ref[0:1, :], vec_ref[1:2, :])


def _combine(h2d, ys, gs, a1, a1t, wgu_s, wd_s, vec):
    T, D = h2d.shape
    nt = T // TM
    return pl.pallas_call(
        _combine_kernel,
        grid=(nt,),
        in_specs=[pl.BlockSpec((TM, D), lambda i: (i, 0)),
                  pl.BlockSpec((None, MOE_SLOTS, D), lambda i: (i, 0, 0)),
                  pl.BlockSpec((None, MOE_SLOTS, 1), lambda i: (i, 0, 0)),
                  pl.BlockSpec((None, N_EXPERTS, TM), lambda i: (i, 0, 0)),
                  pl.BlockSpec((None, TM, N_EXPERTS), lambda i: (i, 0, 0)),
                  _full(wgu_s.shape), _full(wd_s.shape), _full(vec.shape)],
        out_specs=pl.BlockSpec((TM, D), lambda i: (i, 0)),
        out_shape=jax.ShapeDtypeStruct((T, D), F32),
        compiler_params=_params("parallel"),
        name="moe_combine",
    )(h2d, ys, gs, a1, a1t, wgu_s, wd_s, vec)


def _dispatch_tables(chunks):
    nt, E = chunks.shape
    maxc = nt * MOE_CPT
    nblk = -(-maxc // MOE_CB) + E
    hp = lax.Precision.HIGHEST
    ch = chunks.astype(F32)
    off = jnp.cumsum(ch, axis=1) - ch
    wend = jnp.cumsum(ch, axis=0)
    cnt_e = wend[-1]
    cend = jnp.cumsum(cnt_e)
    base_e = cend - cnt_e
    srcbase = jnp.arange(nt, dtype=F32)[:, None] * MOE_CPT + off - (wend - ch)
    p = jnp.arange(maxc, dtype=F32)
    e_p = jnp.minimum(jnp.sum((cend[None, :] <= p[:, None]).astype(jnp.int32), axis=1), E - 1)
    oh_e = (e_p[:, None] == jnp.arange(E, dtype=jnp.int32)[None, :]).astype(F32)
    look = jnp.dot(oh_e, jnp.concatenate([wend.T, srcbase.T, base_e[:, None]], axis=1), precision=hp)
    q = p - look[:, 2 * nt]
    i_p = jnp.sum((look[:, :nt] <= q[:, None]).astype(jnp.int32), axis=1)
    oh_i = i_p[:, None] == jnp.arange(nt, dtype=jnp.int32)[None, :]
    src = jnp.sum(jnp.where(oh_i, look[:, nt:2 * nt], 0.0), axis=1) + q
    src = jnp.clip(src, 0, maxc - 1).astype(jnp.int32)
    nblk_e = jnp.floor((cnt_e + (MOE_CB - 1)) * (1.0 / MOE_CB))
    bend = jnp.cumsum(nblk_e)
    bidx = jnp.arange(nblk, dtype=F32)
    be = jnp.minimum(jnp.sum((bend[None, :] <= bidx[:, None]).astype(jnp.int32), axis=1), E - 1)
    oh_b = (be[:, None] == jnp.arange(E, dtype=jnp.int32)[None, :]).astype(F32)
    lookb = jnp.dot(oh_b, jnp.stack([bend - nblk_e, cnt_e, base_e], axis=1), precision=hp)
    q0 = (bidx - lookb[:, 0]) * MOE_CB
    nbc = jnp.clip(lookb[:, 1] - q0, 0, MOE_CB).astype(jnp.int32)
    pstart = jnp.clip(lookb[:, 2] + q0, 0, maxc - MOE_CB).astype(jnp.int32)
    return be.astype(jnp.int32), pstart, nbc, src


def _moe_layer(h2d, layer, router, bias, w_gate, w_up, w_down, sh_gate, sh_up, sh_down, ln_g, ln_b):
    router_t = router.T.astype(F32)
    bias_col = jnp.broadcast_to(bias.astype(F32)[:, None], (N_EXPERTS, LANES))
    xs, gs, a1, a1t, cnt = _router(h2d, router_t, bias_col)
    be, pstart, nbc, src = _dispatch_tables(cnt[:, 0, :].astype(jnp.int32))
    ys = _experts(xs, w_gate, w_up, w_down, layer, be, pstart, nbc, src)
    wgu_s = jnp.concatenate([sh_gate, sh_up], axis=-1).astype(BF16)
    vec = jnp.zeros((SUBLANES, D_MODEL), F32).at[0].set(ln_g).at[1].set(ln_b)
    return _combine(h2d, ys, gs, a1, a1t, wgu_s, sh_down.astype(BF16), vec)


def _rows(*vs):
    out = jnp.zeros((SUBLANES, vs[0].shape[-1]), F32)
    for i, v in enumerate(vs):
        out = out.at[i].set(v.astype(F32))
    return out


def kernel(x, meta, rel_bias, rwkv_mu, rwkv_w0, rwkv_w1, rwkv_w2, rwkv_a0, rwkv_a1, rwkv_a2, rwkv_g1, rwkv_g2, rwkv_k_k, rwkv_k_a, rwkv_r_k, rwkv_w_r, rwkv_w_k, rwkv_w_v, rwkv_w_o, rwkv_lnx_g, rwkv_lnx_b, attn_w_qkv, attn_b_qkv, attn_sinks, attn_w_o, attn_b_o, ln_mix_g, ln_mix_b, ln_ffn_g, ln_ffn_b, moe_router, moe_bias, moe_w_gate, moe_w_up, moe_w_down, shared_w_gate, shared_w_up, shared_w_down):
    B, S, D = x.shape
    assert D == D_MODEL and S % ATT_BLOCK == 0 and (FRONT + N_META) % WKV_CHUNK == 0
    lp = FRONT + N_META + S
    T = B * lp
    assert T % TM == 0 and lp >= TM
    h = jnp.concatenate([jnp.zeros((B, FRONT, D), x.dtype),
                         jnp.broadcast_to(meta[None].astype(x.dtype), (B, N_META, D)), x], axis=1)
    h = h.reshape(T, D)
    bf = lambda w: w.astype(BF16)

    H, N = RWKV_HEADS, RWKV_HEAD
    head_of = jnp.arange(D) // N
    gsum = (head_of[:, None] == jnp.arange(LANES)[None, :]).astype(BF16)
    gexp = gsum.T
    r, w, k, v, kk, b, g = _rwkv_proj(
        h, lp, _rows(*rwkv_mu[0]), _rows(rwkv_w0[0], rwkv_a0[0], rwkv_k_k[0], rwkv_k_a[0]),
        bf(rwkv_w_r[0]), bf(rwkv_w_k[0]), bf(rwkv_w_v[0]), bf(rwkv_w1[0]), bf(rwkv_w2[0]),
        bf(rwkv_a1[0]), bf(rwkv_a2[0]), bf(rwkv_g1[0]), bf(rwkv_g2[0]), gsum, gexp)
    to3 = lambda t: t.reshape(B, lp, D)
    prm = _rows(rwkv_r_k[0].reshape(D), rwkv_lnx_g[0], rwkv_lnx_b[0])
    o = _wkv(to3(r), to3(w), to3(k), to3(v), to3(kk), to3(b), to3(g), prm)
    h = _proj_ln(o.reshape(T, D), bf(rwkv_w_o[0]), _rows(jnp.zeros((D,), F32), ln_mix_g[0], ln_mix_b[0]), h)
    h = _moe_layer(h, 0, moe_router[0], moe_bias[0], moe_w_gate, moe_w_up, moe_w_down,
                   shared_w_gate[0], shared_w_up[0], shared_w_down[0], ln_ffn_g[0], ln_ffn_b[0])

    HD, KV = ATT_HEAD_DIM, ATT_KV_HEADS
    qw = ATT_HEADS * HD
    wqkv, bqkv = attn_w_qkv[0], attn_b_qkv[0]
    dup = lambda t: jnp.concatenate([t.reshape(-1, KV, 1, HD)] * 2, axis=2).reshape(t.shape[0], 2 * KV * HD)
    wq, wkd, wvd = wqkv[:, :qw], dup(wqkv[:, qw:qw + KV * HD]), dup(wqkv[:, qw + KV * HD:])
    bq, bkd, bvd = bqkv[None, :qw], dup(bqkv[None, qw:qw + KV * HD]), dup(bqkv[None, qw + KV * HD:])
    q, kd, vd = _qkv(h, bf(wq), bf(wkd), bf(wvd), bq.astype(F32), bkd.astype(F32), bvd.astype(F32))
    bband, bmeta, bmm = _bias_tables(rel_bias, S // ATT_BLOCK)
    o = _attention(q.reshape(B, lp, qw), kd.reshape(B, lp, -1), vd.reshape(B, lp, -1),
                   attn_sinks[0].astype(F32), bband, bmeta, bmm)
    h = _proj_ln(o.reshape(T, D), bf(attn_w_o[0]), _rows(attn_b_o[0], ln_mix_g[1], ln_mix_b[1]), h)
    h = _moe_layer(h, 1, moe_router[1], moe_bias[1], moe_w_gate, moe_w_up, moe_w_down,
                   shared_w_gate[1], shared_w_up[1], shared_w_down[1], ln_ffn_g[1], ln_ffn_b[1])
    return h.reshape(B, lp, D)[:, FRONT + N_META:]
```

```python
import functools
import math

import jax
import jax.numpy as jnp
from jax import lax
from jax.experimental import pallas as pl
from jax.experimental.pallas import tpu as pltpu

F32 = jnp.float32
BF16 = jnp.bfloat16

D_MODEL = 1024
DEPTH = 2
N_META = 16
RWKV_HEAD = 64
RWKV_HEADS = D_MODEL // RWKV_HEAD
GN_EPS = 64e-5
ATT_HEADS = 16
ATT_KV_HEADS = 4
ATT_HEAD_DIM = D_MODEL // ATT_HEADS
ATT_GROUP = ATT_HEADS // ATT_KV_HEADS
WINDOW = 128
ATT_BLOCK = 128
N_BUCKETS = 32
MAX_DISTANCE = 128
N_EXPERTS = 64
TOP_K = 8
N_GROUPS = 8
TOPK_GROUPS = 4
EXPERT_FF = 256
ROUTED_SCALE = 2.5
DEEPNORM_ALPHA = (2 * DEPTH) ** 0.25
LN_EPS = 1e-5

LANES = 128
SUBLANES = 8
BF16_ROWS = 16
VMEM_LIMIT = 56 * 1024 * 1024

FRONT = 48
TM = 256
WKV_CHUNK = 64
WKV_PAIRS = 8
MOE_CH = BF16_ROWS
MOE_SLOTS = TM * TOP_K + N_EXPERTS * MOE_CH
MOE_CPT = MOE_SLOTS // MOE_CH
MOE_CB = 64
MOE_SEG = 32
NEG = -1e30

_NN = (((1,), (0,)), ((), ()))
_NT = (((1,), (1,)), ((), ()))


def _dot(a, b, dn=_NN):
    return lax.dot_general(a.astype(BF16), b.astype(BF16), dn, preferred_element_type=F32)


def _split2(x):
    hi = x.astype(BF16)
    lo = (x - hi.astype(F32)).astype(BF16)
    return hi, lo


def _split3(x):
    h1 = x.astype(BF16)
    r1 = x - h1.astype(F32)
    h2 = r1.astype(BF16)
    h3 = (r1 - h2.astype(F32)).astype(BF16)
    return h1, h2, h3


def _dot_exact_lhs(a01, b, dn=_NN):
    a = a01.astype(BF16)
    return sum(lax.dot_general(a, p, dn, preferred_element_type=F32) for p in _split3(b))


def _dot_hilo_lhs(a01, b, dn=_NN):
    a = a01.astype(BF16)
    return sum(lax.dot_general(a, p, dn, preferred_element_type=F32) for p in _split2(b))


def _dot_hilo_rhs(a, b01, dn=_NN):
    b = b01.astype(BF16)
    return sum(lax.dot_general(p, b, dn, preferred_element_type=F32) for p in _split2(a))


def _dot3(a, b, dn=_NN):
    ah, al = _split2(a)
    bh, bl = _split2(b)
    d = lambda x, y: lax.dot_general(x, y, dn, preferred_element_type=F32)
    return d(ah, bh) + (d(ah, bl) + d(al, bh))


def _sigmoid(x):
    return 1.0 / (1.0 + jnp.exp(-x))


def _silu(x):
    return x * _sigmoid(x)


def _layer_norm(x, g, b):
    mu = jnp.mean(x, axis=-1, keepdims=True)
    xc = x - mu
    var = jnp.mean(xc * xc, axis=-1, keepdims=True)
    return xc * lax.rsqrt(var + LN_EPS) * g + b


def _full(shape):
    nd = len(shape)
    return pl.BlockSpec(shape, lambda *_: (0,) * nd)


def _params(*sem):
    return pltpu.CompilerParams(dimension_semantics=sem, vmem_limit_bytes=VMEM_LIMIT)


def _rwkv_proj_kernel(x_ref, xp_ref, mu_ref, vec_ref, wr_ref, wk_ref, wv_ref, w1_ref, w2_ref,
                      a1_ref, a2_ref, g1_ref, g2_ref, gsum_ref, gexp_ref,
                      r_out, w_out, k_out, v_out, kk_out, b_out, g_out, *, lp):
    i = pl.program_id(0)
    x = x_ref[...]
    row = lax.broadcasted_iota(jnp.int32, (TM, 1), 0)
    pos = lax.rem(i * TM, lp) + row
    pos = jnp.where(pos >= lp, pos - lp, pos)
    prev = jnp.where(row == 0, xp_ref[SUBLANES - 1:SUBLANES, :], pltpu.roll(x, 1, axis=0))
    prev = jnp.where(pos == FRONT, 0.0, prev)
    xx = prev - x
    valid = pos >= FRONT

    def mix(j):
        return (x + xx * mu_ref[j:j + 1, :]).astype(BF16)

    w0, a0, k_k, k_a = (vec_ref[j:j + 1, :] for j in range(4))
    r = _dot(mix(0), wr_ref[...])
    z = w0 + _dot(jnp.tanh(_dot(mix(1), w1_ref[...])), w2_ref[...])
    w = -math.exp(-0.5) * _sigmoid(z)
    k = _dot(mix(2), wk_ref[...])
    v = _dot(mix(3), wv_ref[...])
    a = _sigmoid(a0 + _dot(_dot(mix(4), a1_ref[...]), a2_ref[...]))
    g = _dot(_sigmoid(_dot(mix(5), g1_ref[...])), g2_ref[...])
    kk = k * k_k
    ssq = _dot_hilo_rhs(kk * kk, gsum_ref[...])
    nrm = jnp.sqrt(_dot_hilo_rhs(ssq, gexp_ref[...]))
    kk = kk / jnp.maximum(nrm, 1e-12)
    k = k * (1.0 + (a - 1.0) * k_a)
    r_out[...] = r
    w_out[...] = jnp.where(valid, w, 0.0)
    k_out[...] = jnp.where(valid, k, 0.0)
    v_out[...] = jnp.where(valid, v, 0.0)
    kk_out[...] = jnp.where(valid, kk, 0.0)
    b_out[...] = jnp.where(valid, kk * a, 0.0)
    g_out[...] = g


def _rwkv_proj(h2d, lp, mu, vec, wr, wk, wv, w1, w2, a1, a2, g1, g2, gsum, gexp):
    T, D = h2d.shape
    row_spec = pl.BlockSpec((TM, D), lambda i: (i, 0))
    prev_spec = pl.BlockSpec((SUBLANES, D), lambda i: (jnp.maximum(i * (TM // SUBLANES) - 1, 0), 0))
    ws = [mu, vec, wr, wk, wv, w1, w2, a1, a2, g1, g2, gsum, gexp]
    return pl.pallas_call(
        functools.partial(_rwkv_proj_kernel, lp=lp),
        grid=(T // TM,),
        in_specs=[row_spec, prev_spec] + [_full(w.shape) for w in ws],
        out_specs=[row_spec] * 7,
        out_shape=[jax.ShapeDtypeStruct((T, D), F32)] * 7,
        compiler_params=_params("parallel"),
        name="rwkv_proj",
    )(h2d, h2d, *ws)


def _wkv_chunk(r, w, cum, k, v, kk, b, S, c):
    C = WKV_CHUNK
    P = range(len(r))
    bf = lambda xs: [x.astype(BF16) for x in xs]
    each = lambda f, *ls: [f(*a) for a in zip(*ls)]

    def stack(x):
        return jnp.concatenate([jnp.where(c["m0"], x, 0.0), jnp.where(c["m0"], 0.0, x)], axis=0)

    f32 = lambda xs: [x.astype(F32) for x in xs]
    cat0 = lambda x, y: jnp.concatenate([x, y], axis=0)
    cat1 = lambda x, y: jnp.concatenate([x, y], axis=1)
    nt = lambda x, y: _dot(x, y, _NT)
    C2 = 2 * C

    tot = [cum[p][C - 1:C, :] for p in P]
    inv = [jnp.exp(-cum[p]) for p in P]
    dend = [jnp.exp(tot[p] - cum[p]) for p in P]
    kr_s = bf([cat0(stack(kk[p] * jnp.exp(cum[p] - w[p])), stack(r[p] * jnp.exp(cum[p]))) for p in P])
    bk_s = bf([cat0(stack(b[p] * inv[p]), stack(k[p] * inv[p])) for p in P])
    bh_s = bf([stack(b[p] * dend[p]) for p in P])
    kh_s = bf([stack(k[p] * dend[p]) for p in P])
    vs_t = bf([stack(v[p]).T for p in P])

    sc = each(nt, bk_s, kr_s)
    lab_t = [jnp.where(c["upper"], x[:C2, :C2], 0.0) for x in sc]
    arb_t = bf([jnp.where(c["upinc"], x[:C2, C2:], 0.0) for x in sc])
    lak_t = bf([jnp.where(c["upper"], x[C2:, :C2], 0.0) for x in sc])
    ark_t = bf([jnp.where(c["upinc"], x[C2:, C2:], 0.0) for x in sc])

    eye = c["eye"]
    ud = [jnp.where(c["blk"], x, 0.0) for x in lab_t]
    nu = bf([x - y for x, y in zip(lab_t, ud)])
    udb = bf(ud)
    u2b = bf(each(_dot, udb, udb))
    u2 = f32(u2b)
    t1 = each(_dot, bf([eye + x for x in u2]), [cat1(x, (eye - y).astype(BF16)) for x, y in zip(u2b, ud)])
    u4b = bf([x[:, :C2] - y for x, y in zip(t1, u2)])
    u4 = f32(u4b)
    t2 = each(_dot, bf([eye + x for x in u4]), [cat1(x, y[:, C2:].astype(BF16)) for x, y in zip(u4b, t1)])
    u8 = [x[:, :C2] - y for x, y in zip(t2, u4)]
    dinv = each(_dot, bf([eye + x for x in u8]), bf([x[:, C2:] for x in t2]))
    dinvb = bf(dinv)
    wzb = bf(each(_dot, dinvb, nu))
    t3 = each(_dot, wzb, [cat1(x, y) for x, y in zip(wzb, dinvb)])
    tt_t = bf(each(_dot, bf([eye + x[:, :C2] for x in t3]), bf([y - x[:, C2:] for x, y in zip(t3, dinv)])))

    sb = bf(S)
    p2 = each(_dot, vs_t, [cat1(x, y) for x, y in zip(lak_t, ark_t)])
    p3 = each(_dot, vs_t, kh_s)
    p1 = each(nt, sb, kr_s)
    u_t = bf(each(_dot, bf([-(x[:, :C2] + y[:, :C2]) for x, y in zip(p1, p2)]), tt_t))
    p4 = each(_dot, u_t, [cat1(x, y) for x, y in zip(arb_t, bh_s)])
    s_new = [S[p] * jnp.exp(tot[p]) + p4[p][:, C2:] + p3[p] for p in P]
    y_t = [p1[p][:, C2:] + p4[p][:, :C2] + p2[p][:, C2:] for p in P]
    return y_t, s_new


def _wkv_group_norm(y_t, hblk):
    C = WKV_CHUNK
    mean = jnp.sum(y_t, axis=0, keepdims=True) * (1.0 / RWKV_HEAD)
    yc = jnp.where(hblk, y_t - mean, 0.0)
    var = jnp.sum(yc * yc, axis=0, keepdims=True) * (1.0 / RWKV_HEAD)
    ys = (yc * lax.rsqrt(var + GN_EPS)).T
    return ys[:C] + ys[C:]


def _wkv_kernel(r_ref, w_ref, k_ref, v_ref, kk_ref, b_ref, g_ref, prm_ref, o_ref, yt_ref, *s_refs, lb):
    C = WKV_CHUNK

    @pl.when(pl.program_id(2) == 0)
    def _():
        for s_ref in s_refs:
            s_ref[...] = jnp.zeros_like(s_ref)

    ri = lax.broadcasted_iota(jnp.int32, (2 * C, 2 * C), 0)
    ci = lax.broadcasted_iota(jnp.int32, (2 * C, 2 * C), 1)
    lane = lax.broadcasted_iota(jnp.int32, (1, LANES), 1)
    ti = lax.broadcasted_iota(jnp.int32, (C, C), 0)
    tj = lax.broadcasted_iota(jnp.int32, (C, C), 1)
    consts = dict(
        m0=lane < RWKV_HEAD,
        upper=ri < ci,
        upinc=ri <= ci,
        blk=(ri // 16) == (ci // 16),
        hblk=(ri // RWKV_HEAD) == (ci // C),
        eye=(ri == ci).astype(F32),
    )
    m0 = consts["m0"]
    tril = (ti >= tj).astype(BF16)

    def head_sum(x):
        s0 = jnp.sum(jnp.where(m0, x, 0.0), axis=-1, keepdims=True)
        s1 = jnp.sum(jnp.where(m0, 0.0, x), axis=-1, keepdims=True)
        return jnp.where(m0, s0, s1)

    lns = [slice(p * LANES, (p + 1) * LANES) for p in range(WKV_PAIRS)]

    def recurrence(ci_):
        rows = pl.ds(pl.multiple_of(ci_ * C, C), C)
        ld = lambda ref: [ref[rows, ln] for ln in lns]
        r, w, k, v, kk, b = ld(r_ref), ld(w_ref), ld(k_ref), ld(v_ref), ld(kk_ref), ld(b_ref)
        cum_all = _dot_hilo_lhs(tril, w_ref[rows, :])
        cum = [cum_all[:, ln] for ln in lns]
        ys, s_new = _wkv_chunk(r, w, cum, k, v, kk, b, [s[...] for s in s_refs], consts)
        for p, ln in enumerate(lns):
            s_refs[p][...] = s_new[p]
            yt_ref[:, ln] = ys[p]

    def finish(ci_):
        rows = pl.ds(pl.multiple_of(ci_ * C, C), C)
        for ln in lns:
            r_k, lg, lb_ = prm_ref[0:1, ln], prm_ref[1:2, ln], prm_ref[2:3, ln]
            yn = _wkv_group_norm(yt_ref[:, ln], consts["hblk"])
            bonus = head_sum(r_ref[rows, ln] * k_ref[rows, ln] * r_k) * v_ref[rows, ln]
            o_ref[rows, ln] = ((yn * lg + lb_ + bonus) * g_ref[rows, ln]).astype(o_ref.dtype)

    def chunk(ci_, carry):
        finish(ci_ - 1)
        recurrence(ci_)
        return carry

    recurrence(0)
    lax.fori_loop(1, lb // C, chunk, 0)
    finish(lb // C - 1)


def _wkv_row_block(lp):
    nch = lp // WKV_CHUNK
    for d in (11, 8, 6, 4, 3, 2, 1):
        if nch % d == 0:
            return d * WKV_CHUNK
    return WKV_CHUNK


def _wkv(r, w, k, v, kk, b, g, prm):
    B, lp, D = r.shape
    lb = _wkv_row_block(lp)
    wl = WKV_PAIRS * LANES
    spec = pl.BlockSpec((None, lb, wl), lambda bi, pi, li: (bi, li, pi))
    return pl.pallas_call(
        functools.partial(_wkv_kernel, lb=lb),
        grid=(B, D // wl, lp // lb),
        in_specs=[spec] * 7 + [pl.BlockSpec((SUBLANES, wl), lambda bi, pi, li: (0, pi))],
        out_specs=spec,
        out_shape=jax.ShapeDtypeStruct((B, lp, D), BF16),
        scratch_shapes=[pltpu.VMEM((LANES, wl), F32)] + [pltpu.VMEM((LANES, LANES), F32)] * WKV_PAIRS,
        compiler_params=_params("parallel", "parallel", "arbitrary"),
        name="wkv7",
    )(r, w, k, v, kk, b, g, prm)


def _proj_ln_kernel(a_ref, w_ref, vec_ref, h_ref, o_ref):
    mix = _dot(a_ref[...], w_ref[...]) + vec_ref[0:1, :]
    o_ref[...] = _layer_norm(DEEPNORM_ALPHA * h_ref[...] + mix, vec_ref[1:2, :], vec_ref[2:3, :])


def _proj_ln(a2d, w, vec, h2d):
    T, D = h2d.shape
    K = a2d.shape[1]
    return pl.pallas_call(
        _proj_ln_kernel,
        grid=(T // TM,),
        in_specs=[pl.BlockSpec((TM, K), lambda i: (i, 0)), _full(w.shape), _full(vec.shape),
                  pl.BlockSpec((TM, D), lambda i: (i, 0))],
        out_specs=pl.BlockSpec((TM, D), lambda i: (i, 0)),
        out_shape=jax.ShapeDtypeStruct((T, D), F32),
        compiler_params=_params("parallel"),
        name="proj_ln",
    )(a2d, w, vec, h2d)


def _qkv_kernel(x_ref, wq_ref, wk_ref, wv_ref, bq_ref, bk_ref, bv_ref, q_out, k_out, v_out):
    x = x_ref[...].astype(BF16)
    q_out[...] = ((_dot(x, wq_ref[...]) + bq_ref[...]) * ATT_HEAD_DIM ** -0.5).astype(BF16)
    k_out[...] = (_dot(x, wk_ref[...]) + bk_ref[...]).astype(BF16)
    v_out[...] = (_dot(x, wv_ref[...]) + bv_ref[...]).astype(BF16)


def _qkv(h2d, wq, wkd, wvd, bq, bkd, bvd):
    T, D = h2d.shape
    nq, nk = wq.shape[1], wkd.shape[1]
    ws = [wq, wkd, wvd, bq, bkd, bvd]
    return pl.pallas_call(
        _qkv_kernel,
        grid=(T // TM,),
        in_specs=[pl.BlockSpec((TM, D), lambda i: (i, 0))] + [_full(w.shape) for w in ws],
        out_specs=[pl.BlockSpec((TM, nq), lambda i: (i, 0)), pl.BlockSpec((TM, nk), lambda i: (i, 0)),
                   pl.BlockSpec((TM, nk), lambda i: (i, 0))],
        out_shape=[jax.ShapeDtypeStruct((T, nq), BF16), jax.ShapeDtypeStruct((T, nk), BF16),
                   jax.ShapeDtypeStruct((T, nk), BF16)],
        compiler_params=_params("parallel"),
        name="qkv_proj",
    )(h2d, *ws)


def _attn_kernel(sink_ref, q_ref, k_ref, v_ref, bband_ref, bmeta_ref, bmm_ref, o_ref, *, nb):
    c = pl.program_id(0)
    BLK = ATT_BLOCK
    M0 = FRONT
    lane = lax.broadcasted_iota(jnp.int32, (1, LANES), 1)
    m0 = lane < ATT_HEAD_DIM
    qi = lax.broadcasted_iota(jnp.int32, (2 * BLK, BLK), 0) % BLK
    sj = lax.broadcasted_iota(jnp.int32, (2 * BLK, BLK), 1)
    cur_vis = sj <= qi
    prev_vis = sj > qi
    half = lax.broadcasted_iota(jnp.int32, (2 * BLK, 1), 0) < BLK

    def stack(x):
        z = jnp.zeros_like(x)
        return jnp.concatenate([jnp.where(m0, x, z), jnp.where(m0, z, x)], axis=0)

    HP = range(4)
    kl = [slice((hp // 2) * LANES, (hp // 2 + 1) * LANES) for hp in HP]
    ql = [slice(hp * LANES, (hp + 1) * LANES) for hp in HP]

    def rowmax(lgs, floor):
        by_width = {}
        for lg in lgs:
            by_width.setdefault(lg.shape[1], []).append(lg)
        tops = [jnp.max(functools.reduce(jnp.maximum, g), axis=-1, keepdims=True) for g in by_width.values()]
        return functools.reduce(jnp.maximum, tops, floor)

    def attend(segs, sinks, n):
        it = range(len(segs))
        mx = [rowmax([lg for lg, _ in segs[i]], sinks[i]) for i in it]
        ps = [[jnp.exp(lg - mx[i]).astype(BF16) for lg, _ in segs[i]] for i in it]
        one = jnp.ones((1, LANES), BF16)
        va = [[jnp.where(m0, vals, one) for _, vals in segs[i]] for i in it]
        vb = [[jnp.where(m0, one, vals) for _, vals in segs[i]] for i in it]
        oa = [sum(_dot(p[:n], v) for p, v in zip(ps[i], va[i])) for i in it]
        ob = [sum(_dot(p[n:], v) for p, v in zip(ps[i], vb[i])) for i in it]
        out = []
        for i in it:
            st = jnp.exp(sinks[i] - mx[i])
            den = pltpu.roll(jnp.where(m0, ob[i], oa[i]), ATT_HEAD_DIM, axis=1)
            out.append(jnp.where(m0, oa[i], ob[i]) / (den + jnp.where(m0, st[:n], st[n:])))
        return out

    o_ref[0:M0, :] = jnp.zeros((M0, o_ref.shape[1]), o_ref.dtype)
    s0 = [sink_ref[c * 8 + hp * 2] for hp in HP]
    s1 = [sink_ref[c * 8 + hp * 2 + 1] for hp in HP]
    k_meta = [k_ref[M0:M0 + N_META, kl[hp]] for hp in HP]
    v_meta = [v_ref[M0:M0 + N_META, kl[hp]] for hp in HP]

    first = lax.broadcasted_iota(jnp.int32, (2 * N_META, 1), 0) < N_META
    mi = lax.broadcasted_iota(jnp.int32, (2 * N_META, N_META), 0) % N_META
    mj = lax.broadcasted_iota(jnp.int32, (2 * N_META, N_META), 1)
    qm = [stack(q_ref[M0:M0 + N_META, ql[hp]]) for hp in HP]
    lg = [jnp.where(mj <= mi, _dot(qm[hp], k_meta[hp], _NT) + bmm_ref[hp], NEG) for hp in HP]
    om = attend([[(lg[hp], v_meta[hp])] for hp in HP], [jnp.where(first, s0[hp], s1[hp]) for hp in HP], N_META)
    for hp in HP:
        o_ref[M0:M0 + N_META, ql[hp]] = om[hp].astype(o_ref.dtype)

    sink_q = [jnp.where(half, s0[hp], s1[hp]) for hp in HP]

    unroll = 2 if nb % 2 == 0 else 1

    def blocks(jo, carry):
        items = [(jo * unroll + u, hp) for u in range(unroll) for hp in HP]
        start = [pl.multiple_of(M0 + N_META + j * BLK, 64) for j, _ in items]
        pstart = [pl.multiple_of(jnp.maximum(s - BLK, 0), 64) for s in start]
        it = range(len(items))
        qs = [stack(q_ref[pl.ds(start[i], BLK), ql[items[i][1]]]) for i in it]
        k_prev = [k_ref[pl.ds(pstart[i], BLK), kl[items[i][1]]] for i in it]
        k_cur = [k_ref[pl.ds(start[i], BLK), kl[items[i][1]]] for i in it]
        lg_meta = [_dot(qs[i], k_meta[items[i][1]], _NT) + bmeta_ref[items[i][1], items[i][0]] for i in it]
        lg_prev = [jnp.where(prev_vis & (items[i][0] > 0),
                             _dot(qs[i], k_prev[i], _NT) + bband_ref[items[i][1], :, 0:BLK], NEG) for i in it]
        lg_cur = [jnp.where(cur_vis, _dot(qs[i], k_cur[i], _NT) + bband_ref[items[i][1], :, BLK:2 * BLK], NEG)
                  for i in it]
        out = attend([[(lg_meta[i], v_meta[items[i][1]]),
                       (lg_prev[i], v_ref[pl.ds(pstart[i], BLK), kl[items[i][1]]]),
                       (lg_cur[i], v_ref[pl.ds(start[i], BLK), kl[items[i][1]]])] for i in it],
                     [sink_q[hp] for _, hp in items], BLK)
        for i in it:
            o_ref[pl.ds(start[i], BLK), ql[items[i][1]]] = out[i].astype(o_ref.dtype)
        return carry

    lax.fori_loop(0, nb // unroll, blocks, 0)


def _attention(q, kd, vd, sinks, bband, bmeta, bmm):
    B, lp, _ = q.shape
    nb = (lp - FRONT - N_META) // ATT_BLOCK
    qw, kw = 4 * LANES, 2 * LANES
    return pl.pallas_call(
        functools.partial(_attn_kernel, nb=nb),
        grid=(2, B),
        in_specs=[pl.BlockSpec(memory_space=pltpu.SMEM),
                  pl.BlockSpec((None, lp, qw), lambda c, b: (b, 0, c)),
                  pl.BlockSpec((None, lp, kw), lambda c, b: (b, 0, c)),
                  pl.BlockSpec((None, lp, kw), lambda c, b: (b, 0, c)),
                  pl.BlockSpec((4,) + bband.shape[1:], lambda c, b: (c, 0, 0)),
                  pl.BlockSpec((4,) + bmeta.shape[1:], lambda c, b: (c, 0, 0, 0)),
                  pl.BlockSpec((4,) + bmm.shape[1:], lambda c, b: (c, 0, 0))],
        out_specs=pl.BlockSpec((None, lp, qw), lambda c, b: (b, 0, c)),
        out_shape=jax.ShapeDtypeStruct((B, lp, D_MODEL), BF16),
        compiler_params=_params("parallel", "parallel"),
        name="swa_attention",
    )(sinks, q, kd, vd, bband, bmeta, bmm)


def _t5_bucket(dist):
    exact = N_BUCKETS // 2
    d = jnp.maximum(dist, 0)
    ratio = jnp.log(jnp.maximum(d, 1).astype(F32) / exact) / math.log(MAX_DISTANCE / exact)
    large = jnp.minimum(exact + (ratio * (N_BUCKETS - exact)).astype(jnp.int32), N_BUCKETS - 1)
    return jnp.where(d < exact, d, large)


def _bias_tables(rel_bias, nb):
    H, BLK = ATT_HEADS, ATT_BLOCK

    def lookup(dist):
        onehot = (_t5_bucket(dist)[..., None] == jnp.arange(N_BUCKETS)).astype(F32)
        return jnp.dot(onehot, rel_bias.astype(F32), precision=lax.Precision.HIGHEST)

    qi = jnp.arange(BLK)[:, None]
    band = lookup(qi + BLK - jnp.arange(2 * BLK)[None, :])
    band = jnp.moveaxis(band, -1, 0).reshape(H // 2, 2 * BLK, 2 * BLK)
    pos = jnp.arange(nb * BLK)[:, None]
    meta = lookup(N_META + pos - jnp.arange(N_META)[None, :])
    meta = meta.reshape(nb, BLK, N_META, H // 2, 2).transpose(3, 0, 4, 1, 2)
    meta = meta.reshape(H // 2, nb, 2 * BLK, N_META)
    pm = jnp.arange(N_META)
    mm = lookup(pm[:, None] - pm[None, :])
    mm = jnp.moveaxis(mm, -1, 0).reshape(H // 2, 2 * N_META, N_META)
    return band.astype(F32), meta.astype(F32), mm.astype(F32)


def _dispatch_geometry(a1):
    E = N_EXPERTS
    routed = (a1 > 0.0).astype(BF16)
    n_col = jnp.sum(routed.astype(F32), axis=1, keepdims=True)
    ch_col = jnp.floor((n_col + (MOE_CH - 1)) * (1.0 / MOE_CH))
    ei = lax.broadcasted_iota(jnp.int32, (E, E), 0)
    ej = lax.broadcasted_iota(jnp.int32, (E, E), 1)
    off_col = _dot((ei > ej).astype(BF16), jnp.broadcast_to(ch_col, (E, LANES)))[:, 0:1]
    n_row = _dot(jnp.ones((SUBLANES, TM), BF16), routed, _NT)
    ch_row = jnp.floor((n_row + (MOE_CH - 1)) * (1.0 / MOE_CH))
    off_row = _dot(ch_row, (ei < ej).astype(BF16))
    return ch_col, off_col, ch_row, off_row


def _select_experts(choice):
    E, G, EPG = N_EXPERTS, N_GROUPS, N_EXPERTS // N_GROUPS
    grp = choice.reshape(G, EPG, TM)
    sub = lax.broadcasted_iota(jnp.int32, (G, EPG, TM), 1)
    top1 = jnp.max(grp, axis=1, keepdims=True)
    first = jnp.min(jnp.where(grp == top1, sub, EPG), axis=1, keepdims=True)
    top2 = jnp.max(jnp.where(sub == first, -jnp.inf, grp), axis=1, keepdims=True)
    gscore = jnp.broadcast_to(top1 + top2, (G, EPG, TM))

    gi = lax.broadcasted_iota(jnp.int32, (G, EPG, TM), 0)
    keep = jnp.zeros((G, EPG, TM), jnp.bool_)
    for _ in range(TOPK_GROUPS):
        m = jnp.max(gscore, axis=0, keepdims=True)
        sel = gi == jnp.min(jnp.where(gscore == m, gi, G), axis=0, keepdims=True)
        keep = keep | sel
        gscore = jnp.where(sel, -jnp.inf, gscore)
    keep_e = keep.reshape(E, TM)

    cand = jnp.where(keep_e, choice, -jnp.inf)
    ei = lax.broadcasted_iota(jnp.int32, (E, TM), 0)
    routed = jnp.zeros((E, TM), jnp.bool_)
    for _ in range(TOP_K):
        m = jnp.max(cand, axis=0, keepdims=True)
        sel = ei == jnp.min(jnp.where(cand == m, ei, E), axis=0, keepdims=True)
        routed = routed | sel
        cand = jnp.where(sel, -jnp.inf, cand)
    return routed


def _route_kernel(x_ref, rt_ref, bias_ref, a1_out, a1t_out, gate_out, cnt_out, off_out, *, tiles):
    U = range(tiles)
    xs = [x_ref[u * TM:(u + 1) * TM, :] for u in U]
    scores = [_sigmoid(_dot3(rt_ref[...], x, _NT)) for x in xs]
    routed = [_select_experts(s + bias_ref[:, 0:1]) for s in scores]
    gate = [jnp.where(r, s, 0.0) for r, s in zip(routed, scores)]
    gate = [g / jnp.sum(g, axis=0, keepdims=True) * ROUTED_SCALE for g in gate]
    ti = lax.broadcasted_iota(jnp.int32, (TM, TM), 0)
    tj = lax.broadcasted_iota(jnp.int32, (TM, TM), 1)
    eye = (ti == tj).astype(BF16)
    routed_b = [r.astype(BF16) for r in routed]
    rank = [_dot(rb, (ti < tj).astype(BF16)) for rb in routed_b]
    rank_t = [_dot((tj < ti).astype(BF16), rb, _NT) for rb in routed_b]
    routed_t = [_dot(eye, rb, _NT) for rb in routed_b]
    gate_t = [_dot_exact_lhs(eye, g, _NT) for g in gate]
    for u in U:
        a1 = jnp.where(routed[u], rank[u] + 1.0, 0.0)
        _, _, ch_row, off_row = _dispatch_geometry(a1)
        a1_out[u] = a1
        a1t_out[u] = routed_t[u] * (rank_t[u] + 1.0)
        gate_out[u] = jnp.concatenate(_split2(gate_t[u]), axis=1)
        cnt_out[u] = ch_row
        off_out[u] = off_row


def _dispatch_kernel(x_ref, a1_ref, gate_ref, cnt_ref, off_ref, xs_out, gs_out):
    E = N_EXPERTS
    lo = jnp.concatenate([off_ref[0:1, :]] * 2, axis=1) * MOE_CH
    hi = lo + jnp.concatenate([cnt_ref[0:1, :]] * 2, axis=1) * MOE_CH
    first = lax.broadcasted_iota(jnp.int32, (1, 2 * E), 1) < E
    lo1 = jnp.where(first, lo, 0.0)
    xg = jnp.concatenate([x_ref[...].astype(BF16), gate_ref[...]], axis=1)
    a1b = a1_ref[...].astype(BF16)
    SB = MOE_SLOTS // 3
    for s0 in range(0, MOE_SLOTS, SB):
        si = (lax.broadcasted_iota(jnp.int32, (SB, 2 * E), 0) + s0).astype(F32)
        member = (si >= lo) & (si < hi)
        seg0 = jnp.sum(jnp.where(member, lo1, 0.0), axis=1, keepdims=True)
        rs1 = (lax.broadcasted_iota(jnp.int32, (SB, 1), 0) + (s0 + 1)).astype(F32) - seg0
        perm = _dot(member[:, :E], a1b) == rs1
        disp = _dot(perm, xg)
        xs_out[s0:s0 + SB, :] = disp[:, :D_MODEL].astype(BF16)
        gs_out[s0:s0 + SB, :] = jnp.sum(jnp.where(member, disp[:, D_MODEL:], 0.0), axis=1, keepdims=True)


def _router(h2d, router_t, bias_col):
    T, D = h2d.shape
    nt = T // TM
    E = N_EXPERTS
    tiles = next(t for t in (4, 3, 2, 1) if nt % t == 0)
    a1, a1t, gate, cnt, off = pl.pallas_call(
        functools.partial(_route_kernel, tiles=tiles),
        grid=(nt // tiles,),
        in_specs=[pl.BlockSpec((tiles * TM, D), lambda i: (i, 0)), _full(router_t.shape), _full(bias_col.shape)],
        out_specs=[pl.BlockSpec((tiles, E, TM), lambda i: (i, 0, 0)),
                   pl.BlockSpec((tiles, TM, E), lambda i: (i, 0, 0)),
                   pl.BlockSpec((tiles, TM, 2 * E), lambda i: (i, 0, 0)),
                   pl.BlockSpec((tiles, SUBLANES, E), lambda i: (i, 0, 0)),
                   pl.BlockSpec((tiles, SUBLANES, E), lambda i: (i, 0, 0))],
        out_shape=[jax.ShapeDtypeStruct((nt, E, TM), F32),
                   jax.ShapeDtypeStruct((nt, TM, E), F32),
                   jax.ShapeDtypeStruct((nt, TM, 2 * E), BF16),
                   jax.ShapeDtypeStruct((nt, SUBLANES, E), F32),
                   jax.ShapeDtypeStruct((nt, SUBLANES, E), F32)],
        compiler_params=_params("parallel"),
        name="moe_route",
    )(h2d, router_t, bias_col)
    xs, gs = pl.pallas_call(
        _dispatch_kernel,
        grid=(nt,),
        in_specs=[pl.BlockSpec((TM, D), lambda i: (i, 0)),
                  pl.BlockSpec((None, E, TM), lambda i: (i, 0, 0)),
                  pl.BlockSpec((None, TM, 2 * E), lambda i: (i, 0, 0)),
                  pl.BlockSpec((None, SUBLANES, E), lambda i: (i, 0, 0)),
                  pl.BlockSpec((None, SUBLANES, E), lambda i: (i, 0, 0))],
        out_specs=[pl.BlockSpec((None, MOE_SLOTS, D), lambda i: (i, 0, 0)),
                   pl.BlockSpec((None, MOE_SLOTS, 1), lambda i: (i, 0, 0))],
        out_shape=[jax.ShapeDtypeStruct((nt, MOE_SLOTS, D), BF16),
                   jax.ShapeDtypeStruct((nt, MOE_SLOTS, 1), F32)],
        compiler_params=_params("parallel"),
        name="moe_dispatch",
    )(h2d, a1, gate, cnt, off)
    return xs, gs, a1, a1t, cnt


def _expert_kernel(be_ref, ps_ref, nb_ref, src_ref, xs_hbm, wg_ref, wu_ref, wd_ref, ys_hbm,
                   xbuf, ybuf, wgu_bf, wd_bf, in_sem, out_sem, *, nblk):
    b = pl.program_id(0)
    slot = lax.rem(b, 2)

    def in_copy(src_chunk, sl, c):
        return pltpu.make_async_copy(xs_hbm.at[src_chunk], xbuf.at[sl, c], in_sem.at[sl])

    def out_copy(dst_chunk, sl, c):
        return pltpu.make_async_copy(ybuf.at[sl, c], ys_hbm.at[dst_chunk], out_sem.at[sl])

    def for_chunks(blk, fn):
        p0 = ps_ref[blk]
        n = nb_ref[blk]
        for c0 in range(0, MOE_CB, MOE_SEG):

            @pl.when(n >= c0 + MOE_SEG)
            def _():
                for c in range(c0, c0 + MOE_SEG):
                    fn(src_ref[p0 + c], c)

            @pl.when((n > c0) & (n < c0 + MOE_SEG))
            def _():
                def body(c, carry):
                    fn(src_ref[p0 + c], c)
                    return carry

                lax.fori_loop(c0, n, body, 0)

    @pl.when(b == 0)
    def _():
        xbuf[...] = jnp.zeros_like(xbuf)
        for_chunks(0, lambda s, c: in_copy(s, 0, c).start())

    @pl.when(b + 1 < nblk)
    def _():
        for_chunks(b + 1, lambda s, c: in_copy(s, 1 - slot, c).start())

    for_chunks(b, lambda s, c: in_copy(s, slot, c).wait())

    @pl.when(b >= 2)
    def _():
        for_chunks(b - 2, lambda s, c: out_copy(s, slot, c).wait())

    @pl.when((b == 0) | (be_ref[b] != be_ref[jnp.maximum(b - 1, 0)]))
    def _():
        wgu_bf[:, :EXPERT_FF] = wg_ref[...].astype(BF16)
        wgu_bf[:, EXPERT_FF:] = wu_ref[...].astype(BF16)
        wd_bf[...] = wd_ref[...].astype(BF16)

    for c0 in range(0, MOE_CB, MOE_SEG):

        @pl.when(nb_ref[b] > c0)
        def _():
            x = xbuf[slot, c0:c0 + MOE_SEG].reshape(MOE_SEG * MOE_CH, D_MODEL)
            hid = _dot(x, wgu_bf[...])
            act = _silu(hid[:, :EXPERT_FF]) * hid[:, EXPERT_FF:]
            y = _dot(act, wd_bf[...])
            ybuf[slot, c0:c0 + MOE_SEG] = y.astype(BF16).reshape(MOE_SEG, MOE_CH, D_MODEL)

    for_chunks(b, lambda s, c: out_copy(s, slot, c).start())

    @pl.when(b == nblk - 1)
    def _():
        for_chunks(b, lambda s, c: out_copy(s, slot, c).wait())

        @pl.when(b >= 1)
        def _():
            for_chunks(b - 1, lambda s, c: out_copy(s, 1 - slot, c).wait())


def _experts(xs, wg, wu, wd, layer, be, pstart, nbc, src):
    nt = xs.shape[0]
    nblk = be.shape[0]
    xs_c = xs.reshape(nt * MOE_CPT, MOE_CH, D_MODEL)
    by_expert = lambda b, be, ps, nb, src: (layer, be[b], 0, 0)
    grid_spec = pltpu.PrefetchScalarGridSpec(
        num_scalar_prefetch=4,
        grid=(nblk,),
        in_specs=[pl.BlockSpec(memory_space=pl.ANY),
                  pl.BlockSpec((None, None, D_MODEL, EXPERT_FF), by_expert),
                  pl.BlockSpec((None, None, D_MODEL, EXPERT_FF), by_expert),
                  pl.BlockSpec((None, None, EXPERT_FF, D_MODEL), by_expert)],
        out_specs=pl.BlockSpec(memory_space=pl.ANY),
        scratch_shapes=[pltpu.VMEM((2, MOE_CB, MOE_CH, D_MODEL), BF16),
                        pltpu.VMEM((2, MOE_CB, MOE_CH, D_MODEL), BF16),
                        pltpu.VMEM((D_MODEL, 2 * EXPERT_FF), BF16),
                        pltpu.VMEM((EXPERT_FF, D_MODEL), BF16),
                        pltpu.SemaphoreType.DMA((2,)), pltpu.SemaphoreType.DMA((2,))],
    )
    ys = pl.pallas_call(
        functools.partial(_expert_kernel, nblk=nblk),
        grid_spec=grid_spec,
        out_shape=jax.ShapeDtypeStruct(xs_c.shape, BF16),
        input_output_aliases={4: 0},
        compiler_params=_params("arbitrary"),
        name="moe_experts",
    )(be, pstart, nbc, src, xs_c, wg, wu, wd)
    return ys.reshape(nt, MOE_SLOTS, D_MODEL)


def _combine_kernel(h_ref, ys_ref, gs_ref, a1_ref, a1t_ref, wgu_ref, wd_ref, vec_ref, o_ref):
    E = N_EXPERTS
    h = h_ref[...]
    ch_col, off_col, _, _ = _dispatch_geometry(a1_ref[...])
    si = lax.broadcasted_iota(jnp.int32, (E, MOE_SLOTS), 1).astype(F32)
    lo = off_col * MOE_CH
    member_t = (si >= lo) & (si < lo + ch_col * MOE_CH)
    seg0 = jnp.sum(jnp.where(member_t, lo, 0.0), axis=0, keepdims=True)
    rs1 = lax.broadcasted_iota(jnp.int32, (1, MOE_SLOTS), 1).astype(F32) - seg0 + 1.0
    perm_t = _dot(a1t_ref[...], member_t) == rs1
    routed = _dot(perm_t.astype(BF16), ys_ref[...].astype(F32) * gs_ref[...])
    xb = h.astype(BF16)
    hid = _dot(xb, wgu_ref[...])
    shared = _dot(_silu(hid[:, :EXPERT_FF]) * hid[:, EXPERT_FF:], wd_ref[...])
    o_ref[...] = _layer_norm(DEEPNORM_ALPHA * h + (shared + routed), vec_ref[0:1, :], vec_ref[1:2, :])


def _combine(h2d, ys, gs, a1, a1t, wgu_s, wd_s, vec):
    T, D = h2d.shape
    nt = T // TM
    return pl.pallas_call(
        _combine_kernel,
        grid=(nt,),
        in_specs=[pl.BlockSpec((TM, D), lambda i: (i, 0)),
                  pl.BlockSpec((None, MOE_SLOTS, D), lambda i: (i, 0, 0)),
                  pl.BlockSpec((None, MOE_SLOTS, 1), lambda i: (i, 0, 0)),
                  pl.BlockSpec((None, N_EXPERTS, TM), lambda i: (i, 0, 0)),
                  pl.BlockSpec((None, TM, N_EXPERTS), lambda i: (i, 0, 0)),
                  _full(wgu_s.shape), _full(wd_s.shape), _full(vec.shape)],
        out_specs=pl.BlockSpec((TM, D), lambda i: (i, 0)),
        out_shape=jax.ShapeDtypeStruct((T, D), F32),
        compiler_params=_params("parallel"),
        name="moe_combine",
    )(h2d, ys, gs, a1, a1t, wgu_s, wd_s, vec)


def _dispatch_tables(chunks):
    nt, E = chunks.shape
    maxc = nt * MOE_CPT
    nblk = -(-maxc // MOE_CB) + E
    hp = lax.Precision.HIGHEST
    ch = chunks.astype(F32)
    off = jnp.cumsum(ch, axis=1) - ch
    wend = jnp.cumsum(ch, axis=0)
    cnt_e = wend[-1]
    cend = jnp.cumsum(cnt_e)
    base_e = cend - cnt_e
    srcbase = jnp.arange(nt, dtype=F32)[:, None] * MOE_CPT + off - (wend - ch)
    p = jnp.arange(maxc, dtype=F32)
    e_p = jnp.minimum(jnp.sum((cend[None, :] <= p[:, None]).astype(jnp.int32), axis=1), E - 1)
    oh_e = (e_p[:, None] == jnp.arange(E, dtype=jnp.int32)[None, :]).astype(F32)
    look = jnp.dot(oh_e, jnp.concatenate([wend.T, srcbase.T, base_e[:, None]], axis=1), precision=hp)
    q = p - look[:, 2 * nt]
    i_p = jnp.sum((look[:, :nt] <= q[:, None]).astype(jnp.int32), axis=1)
    oh_i = i_p[:, None] == jnp.arange(nt, dtype=jnp.int32)[None, :]
    src = jnp.sum(jnp.where(oh_i, look[:, nt:2 * nt], 0.0), axis=1) + q
    src = jnp.clip(src, 0, maxc - 1).astype(jnp.int32)
    nblk_e = jnp.floor((cnt_e + (MOE_CB - 1)) * (1.0 / MOE_CB))
    bend = jnp.cumsum(nblk_e)
    bidx = jnp.arange(nblk, dtype=F32)
    be = jnp.minimum(jnp.sum((bend[None, :] <= bidx[:, None]).astype(jnp.int32), axis=1), E - 1)
    oh_b = (be[:, None] == jnp.arange(E, dtype=jnp.int32)[None, :]).astype(F32)
    lookb = jnp.dot(oh_b, jnp.stack([bend - nblk_e, cnt_e, base_e], axis=1), precision=hp)
    q0 = (bidx - lookb[:, 0]) * MOE_CB
    nbc = jnp.clip(lookb[:, 1] - q0, 0, MOE_CB).astype(jnp.int32)
    pstart = jnp.clip(lookb[:, 2] + q0, 0, maxc - MOE_CB).astype(jnp.int32)
    return be.astype(jnp.int32), pstart, nbc, src


def _moe_layer(h2d, layer, router, bias, w_gate, w_up, w_down, sh_gate, sh_up, sh_down, ln_g, ln_b):
    router_t = router.T.astype(F32)
    bias_col = jnp.broadcast_to(bias.astype(F32)[:, None], (N_EXPERTS, LANES))
    xs, gs, a1, a1t, cnt = _router(h2d, router_t, bias_col)
    be, pstart, nbc, src = _dispatch_tables(cnt[:, 0, :].astype(jnp.int32))
    ys = _experts(xs, w_gate, w_up, w_down, layer, be, pstart, nbc, src)
    wgu_s = jnp.concatenate([sh_gate, sh_up], axis=-1).astype(BF16)
    vec = jnp.zeros((SUBLANES, D_MODEL), F32).at[0].set(ln_g).at[1].set(ln_b)
    return _combine(h2d, ys, gs, a1, a1t, wgu_s, sh_down.astype(BF16), vec)


def _rows(*vs):
    out = jnp.zeros((SUBLANES, vs[0].shape[-1]), F32)
    for i, v in enumerate(vs):
        out = out.at[i].set(v.astype(F32))
    return out


def kernel(x, meta, rel_bias, rwkv_mu, rwkv_w0, rwkv_w1, rwkv_w2, rwkv_a0, rwkv_a1, rwkv_a2, rwkv_g1, rwkv_g2, rwkv_k_k, rwkv_k_a, rwkv_r_k, rwkv_w_r, rwkv_w_k, rwkv_w_v, rwkv_w_o, rwkv_lnx_g, rwkv_lnx_b, attn_w_qkv, attn_b_qkv, attn_sinks, attn_w_o, attn_b_o, ln_mix_g, ln_mix_b, ln_ffn_g, ln_ffn_b, moe_router, moe_bias, moe_w_gate, moe_w_up, moe_w_down, shared_w_gate, shared_w_up, shared_w_down):
    B, S, D = x.shape
    assert D == D_MODEL and S % ATT_BLOCK == 0 and (FRONT + N_META) % WKV_CHUNK == 0
    lp = FRONT + N_META + S
    T = B * lp
    assert T % TM == 0 and lp >= TM
    h = jnp.concatenate([jnp.zeros((B, FRONT, D), x.dtype),
                         jnp.broadcast_to(meta[None].astype(x.dtype), (B, N_META, D)), x], axis=1)
    h = h.reshape(T, D)
    bf = lambda w: w.astype(BF16)

    H, N = RWKV_HEADS, RWKV_HEAD
    head_of = jnp.arange(D) // N
    gsum = (head_of[:, None] == jnp.arange(LANES)[None, :]).astype(BF16)
    gexp = gsum.T
    r, w, k, v, kk, b, g = _rwkv_proj(
        h, lp, _rows(*rwkv_mu[0]), _rows(rwkv_w0[0], rwkv_a0[0], rwkv_k_k[0], rwkv_k_a[0]),
        bf(rwkv_w_r[0]), bf(rwkv_w_k[0]), bf(rwkv_w_v[0]), bf(rwkv_w1[0]), bf(rwkv_w2[0]),
        bf(rwkv_a1[0]), bf(rwkv_a2[0]), bf(rwkv_g1[0]), bf(rwkv_g2[0]), gsum, gexp)
    to3 = lambda t: t.reshape(B, lp, D)
    prm = _rows(rwkv_r_k[0].reshape(D), rwkv_lnx_g[0], rwkv_lnx_b[0])
    o = _wkv(to3(r), to3(w), to3(k), to3(v), to3(kk), to3(b), to3(g), prm)
    h = _proj_ln(o.reshape(T, D), bf(rwkv_w_o[0]), _rows(jnp.zeros((D,), F32), ln_mix_g[0], ln_mix_b[0]), h)
    h = _moe_layer(h, 0, moe_router[0], moe_bias[0], moe_w_gate, moe_w_up, moe_w_down,
                   shared_w_gate[0], shared_w_up[0], shared_w_down[0], ln_ffn_g[0], ln_ffn_b[0])

    HD, KV = ATT_HEAD_DIM, ATT_KV_HEADS
    qw = ATT_HEADS * HD
    wqkv, bqkv = attn_w_qkv[0], attn_b_qkv[0]
    dup = lambda t: jnp.concatenate([t.reshape(-1, KV, 1, HD)] * 2, axis=2).reshape(t.shape[0], 2 * KV * HD)
    wq, wkd, wvd = wqkv[:, :qw], dup(wqkv[:, qw:qw + KV * HD]), dup(wqkv[:, qw + KV * HD:])
    bq, bkd, bvd = bqkv[None, :qw], dup(bqkv[None, qw:qw + KV * HD]), dup(bqkv[None, qw + KV * HD:])
    q, kd, vd = _qkv(h, bf(wq), bf(wkd), bf(wvd), bq.astype(F32), bkd.astype(F32), bvd.astype(F32))
    bband, bmeta, bmm = _bias_tables(rel_bias, S // ATT_BLOCK)
    o = _attention(q.reshape(B, lp, qw), kd.reshape(B, lp, -1), vd.reshape(B, lp, -1),
                   attn_sinks[0].astype(F32), bband, bmeta, bmm)
    h = _proj_ln(o.reshape(T, D), bf(attn_w_o[0]), _rows(attn_b_o[0], ln_mix_g[1], ln_mix_b[1]), h)
    h = _moe_layer(h, 1, moe_router[1], moe_bias[1], moe_w_gate, moe_w_up, moe_w_down,
                   shared_w_gate[1], shared_w_up[1], shared_w_down[1], ln_ffn_g[1], ln_ffn_b[1])
    return h.reshape(B, lp, D)[:, FRONT + N_META:]
```

```python
import functools
import math

import jax
import jax.numpy as jnp
from jax import lax
from jax.experimental import pallas as pl
from jax.experimental.pallas import tpu as pltpu

F32 = jnp.float32
BF16 = jnp.bfloat16

D_MODEL = 1024
DEPTH = 2
N_META = 16
RWKV_HEAD = 64
RWKV_HEADS = D_MODEL // RWKV_HEAD
GN_EPS = 64e-5
ATT_HEADS = 16
ATT_KV_HEADS = 4
ATT_HEAD_DIM = D_MODEL // ATT_HEADS
ATT_GROUP = ATT_HEADS // ATT_KV_HEADS
WINDOW = 128
ATT_BLOCK = 128
N_BUCKETS = 32
MAX_DISTANCE = 128
N_EXPERTS = 64
TOP_K = 8
N_GROUPS = 8
TOPK_GROUPS = 4
EXPERT_FF = 256
ROUTED_SCALE = 2.5
DEEPNORM_ALPHA = (2 * DEPTH) ** 0.25
LN_EPS = 1e-5

LANES = 128
SUBLANES = 8
BF16_ROWS = 16
VMEM_LIMIT = 56 * 1024 * 1024

FRONT = 48
TM = 256
WKV_CHUNK = 64
WKV_PAIRS = 8
MOE_CH = BF16_ROWS
MOE_SLOTS = TM * TOP_K + N_EXPERTS * MOE_CH
MOE_CPT = MOE_SLOTS // MOE_CH
MOE_CB = 128
MOE_SEG = 32
NEG = -1e30

_NN = (((1,), (0,)), ((), ()))
_NT = (((1,), (1,)), ((), ()))


def _dot(a, b, dn=_NN):
    return lax.dot_general(a.astype(BF16), b.astype(BF16), dn, preferred_element_type=F32)


def _split2(x):
    hi = x.astype(BF16)
    lo = (x - hi.astype(F32)).astype(BF16)
    return hi, lo


def _split3(x):
    h1 = x.astype(BF16)
    r1 = x - h1.astype(F32)
    h2 = r1.astype(BF16)
    h3 = (r1 - h2.astype(F32)).astype(BF16)
    return h1, h2, h3


def _dot_exact_lhs(a01, b, dn=_NN):
    a = a01.astype(BF16)
    return sum(lax.dot_general(a, p, dn, preferred_element_type=F32) for p in _split3(b))


def _dot_hilo_lhs(a01, b, dn=_NN):
    a = a01.astype(BF16)
    return sum(lax.dot_general(a, p, dn, preferred_element_type=F32) for p in _split2(b))


def _dot_hilo_rhs(a, b01, dn=_NN):
    b = b01.astype(BF16)
    return sum(lax.dot_general(p, b, dn, preferred_element_type=F32) for p in _split2(a))


def _dot3(a, b, dn=_NN):
    ah, al = _split2(a)
    bh, bl = _split2(b)
    d = lambda x, y: lax.dot_general(x, y, dn, preferred_element_type=F32)
    return d(ah, bh) + (d(ah, bl) + d(al, bh))


def _sigmoid(x):
    return 1.0 / (1.0 + jnp.exp(-x))


def _silu(x):
    return x * _sigmoid(x)


def _layer_norm(x, g, b):
    mu = jnp.mean(x, axis=-1, keepdims=True)
    xc = x - mu
    var = jnp.mean(xc * xc, axis=-1, keepdims=True)
    return xc * lax.rsqrt(var + LN_EPS) * g + b


def _full(shape):
    nd = len(shape)
    return pl.BlockSpec(shape, lambda *_: (0,) * nd)


def _params(*sem):
    return pltpu.CompilerParams(dimension_semantics=sem, vmem_limit_bytes=VMEM_LIMIT)


def _rwkv_proj_kernel(x_ref, xp_ref, mu_ref, vec_ref, wr_ref, wk_ref, wv_ref, w1_ref, w2_ref,
                      a1_ref, a2_ref, g1_ref, g2_ref, gsum_ref, gexp_ref,
                      r_out, w_out, k_out, v_out, kk_out, b_out, g_out, *, lp):
    i = pl.program_id(0)
    x = x_ref[...]
    row = lax.broadcasted_iota(jnp.int32, (TM, 1), 0)
    pos = lax.rem(i * TM, lp) + row
    pos = jnp.where(pos >= lp, pos - lp, pos)
    prev = jnp.where(row == 0, xp_ref[SUBLANES - 1:SUBLANES, :], pltpu.roll(x, 1, axis=0))
    prev = jnp.where(pos == FRONT, 0.0, prev)
    xx = prev - x
    valid = pos >= FRONT

    def mix(j):
        return (x + xx * mu_ref[j:j + 1, :]).astype(BF16)

    w0, a0, k_k, k_a = (vec_ref[j:j + 1, :] for j in range(4))
    r = _dot(mix(0), wr_ref[...])
    z = w0 + _dot(jnp.tanh(_dot(mix(1), w1_ref[...])), w2_ref[...])
    w = -math.exp(-0.5) * _sigmoid(z)
    k = _dot(mix(2), wk_ref[...])
    v = _dot(mix(3), wv_ref[...])
    a = _sigmoid(a0 + _dot(_dot(mix(4), a1_ref[...]), a2_ref[...]))
    g = _dot(_sigmoid(_dot(mix(5), g1_ref[...])), g2_ref[...])
    kk = k * k_k
    ssq = _dot_hilo_rhs(kk * kk, gsum_ref[...])
    kk = kk * lax.rsqrt(jnp.maximum(_dot_hilo_rhs(ssq, gexp_ref[...]), 1e-24))
    k = k * (1.0 + (a - 1.0) * k_a)
    r_out[...] = r
    w_out[...] = jnp.where(valid, w, 0.0)
    k_out[...] = jnp.where(valid, k, 0.0)
    v_out[...] = jnp.where(valid, v, 0.0)
    kk_out[...] = jnp.where(valid, kk, 0.0)
    b_out[...] = jnp.where(valid, kk * a, 0.0)
    g_out[...] = g


def _rwkv_proj(h2d, lp, mu, vec, wr, wk, wv, w1, w2, a1, a2, g1, g2, gsum, gexp):
    T, D = h2d.shape
    row_spec = pl.BlockSpec((TM, D), lambda i: (i, 0))
    prev_spec = pl.BlockSpec((SUBLANES, D), lambda i: (jnp.maximum(i * (TM // SUBLANES) - 1, 0), 0))
    ws = [mu, vec, wr, wk, wv, w1, w2, a1, a2, g1, g2, gsum, gexp]
    return pl.pallas_call(
        functools.partial(_rwkv_proj_kernel, lp=lp),
        grid=(T // TM,),
        in_specs=[row_spec, prev_spec] + [_full(w.shape) for w in ws],
        out_specs=[row_spec] * 7,
        out_shape=[jax.ShapeDtypeStruct((T, D), F32)] * 7,
        compiler_params=_params("parallel"),
        name="rwkv_proj",
    )(h2d, h2d, *ws)


def _wkv_chunk(r, w, cum, k, v, kk, b, S, c):
    C = WKV_CHUNK
    P = range(len(r))
    bf = lambda xs: [x.astype(BF16) for x in xs]
    each = lambda f, *ls: [f(*a) for a in zip(*ls)]

    def stack(x):
        return jnp.concatenate([jnp.where(c["m0"], x, 0.0), jnp.where(c["m0"], 0.0, x)], axis=0)

    f32 = lambda xs: [x.astype(F32) for x in xs]
    cat0 = lambda x, y: jnp.concatenate([x, y], axis=0)
    cat1 = lambda x, y: jnp.concatenate([x, y], axis=1)
    nt = lambda x, y: _dot(x, y, _NT)
    C2 = 2 * C

    tot = [cum[p][C - 1:C, :] for p in P]
    inv = [jnp.exp(-cum[p]) for p in P]
    dend = [jnp.exp(tot[p] - cum[p]) for p in P]
    kr_s = bf([cat0(stack(kk[p] * jnp.exp(cum[p] - w[p])), stack(r[p] * jnp.exp(cum[p]))) for p in P])
    bk_s = bf([cat0(stack(b[p] * inv[p]), stack(k[p] * inv[p])) for p in P])
    bh_s = bf([stack(b[p] * dend[p]) for p in P])
    kh_s = bf([stack(k[p] * dend[p]) for p in P])
    vs_t = bf([stack(v[p]).T for p in P])

    sc = each(nt, bk_s, kr_s)
    lab_t = [jnp.where(c["upper"], x[:C2, :C2], 0.0) for x in sc]
    arb_t = bf([jnp.where(c["upinc"], x[:C2, C2:], 0.0) for x in sc])
    lak_t = bf([jnp.where(c["upper"], x[C2:, :C2], 0.0) for x in sc])
    ark_t = bf([jnp.where(c["upinc"], x[C2:, C2:], 0.0) for x in sc])

    eye = c["eye"]
    ud = [jnp.where(c["blk"], x, 0.0) for x in lab_t]
    nu = bf([x - y for x, y in zip(lab_t, ud)])
    udb = bf(ud)
    u2b = bf(each(_dot, udb, udb))
    u2 = f32(u2b)
    t1 = each(_dot, bf([eye + x for x in u2]), [cat1(x, (eye - y).astype(BF16)) for x, y in zip(u2b, ud)])
    u4b = bf([x[:, :C2] - y for x, y in zip(t1, u2)])
    u4 = f32(u4b)
    t2 = each(_dot, bf([eye + x for x in u4]), [cat1(x, y[:, C2:].astype(BF16)) for x, y in zip(u4b, t1)])
    u8 = [x[:, :C2] - y for x, y in zip(t2, u4)]
    dinv = each(_dot, bf([eye + x for x in u8]), bf([x[:, C2:] for x in t2]))
    dinvb = bf(dinv)
    wzb = bf(each(_dot, dinvb, nu))
    t3 = each(_dot, wzb, [cat1(x, y) for x, y in zip(wzb, dinvb)])
    tt_t = bf(each(_dot, bf([eye + x[:, :C2] for x in t3]), bf([y - x[:, C2:] for x, y in zip(t3, dinv)])))

    sb = bf(S)
    p2 = each(_dot, vs_t, [cat1(x, y) for x, y in zip(lak_t, ark_t)])
    p3 = each(_dot, vs_t, kh_s)
    p1 = each(nt, sb, kr_s)
    u_t = bf(each(_dot, bf([-(x[:, :C2] + y[:, :C2]) for x, y in zip(p1, p2)]), tt_t))
    p4 = each(_dot, u_t, [cat1(x, y) for x, y in zip(arb_t, bh_s)])
    s_new = [S[p] * jnp.exp(tot[p]) + p4[p][:, C2:] + p3[p] for p in P]
    y_t = [p1[p][:, C2:] + p4[p][:, :C2] + p2[p][:, C2:] for p in P]
    return y_t, s_new


def _wkv_group_norm(y_t, hblk):
    C = WKV_CHUNK
    mean = jnp.sum(y_t, axis=0, keepdims=True) * (1.0 / RWKV_HEAD)
    yc = jnp.where(hblk, y_t - mean, 0.0)
    var = jnp.sum(yc * yc, axis=0, keepdims=True) * (1.0 / RWKV_HEAD)
    ys = (yc * lax.rsqrt(var + GN_EPS)).T
    return ys[:C] + ys[C:]


def _wkv_kernel(r_ref, w_ref, k_ref, v_ref, kk_ref, b_ref, g_ref, prm_ref, o_ref, yt_ref, *s_refs, lb):
    C = WKV_CHUNK

    @pl.when(pl.program_id(2) == 0)
    def _():
        for s_ref in s_refs:
            s_ref[...] = jnp.zeros_like(s_ref)

    ri = lax.broadcasted_iota(jnp.int32, (2 * C, 2 * C), 0)
    ci = lax.broadcasted_iota(jnp.int32, (2 * C, 2 * C), 1)
    lane = lax.broadcasted_iota(jnp.int32, (1, LANES), 1)
    ti = lax.broadcasted_iota(jnp.int32, (C, C), 0)
    tj = lax.broadcasted_iota(jnp.int32, (C, C), 1)
    consts = dict(
        m0=lane < RWKV_HEAD,
        upper=ri < ci,
        upinc=ri <= ci,
        blk=(ri // 16) == (ci // 16),
        hblk=(ri // RWKV_HEAD) == (ci // C),
        eye=(ri == ci).astype(F32),
    )
    m0 = consts["m0"]
    tril = (ti >= tj).astype(BF16)

    def head_sum(x):
        s0 = jnp.sum(jnp.where(m0, x, 0.0), axis=-1, keepdims=True)
        s1 = jnp.sum(jnp.where(m0, 0.0, x), axis=-1, keepdims=True)
        return jnp.where(m0, s0, s1)

    lns = [slice(p * LANES, (p + 1) * LANES) for p in range(WKV_PAIRS)]

    def recurrence(ci_):
        rows = pl.ds(pl.multiple_of(ci_ * C, C), C)
        ld = lambda ref: [ref[rows, ln] for ln in lns]
        r, w, k, v, kk, b = ld(r_ref), ld(w_ref), ld(k_ref), ld(v_ref), ld(kk_ref), ld(b_ref)
        cum_all = _dot_hilo_lhs(tril, w_ref[rows, :])
        cum = [cum_all[:, ln] for ln in lns]
        ys, s_new = _wkv_chunk(r, w, cum, k, v, kk, b, [s[...] for s in s_refs], consts)
        for p, ln in enumerate(lns):
            s_refs[p][...] = s_new[p]
            yt_ref[:, ln] = ys[p]

    def finish(ci_):
        rows = pl.ds(pl.multiple_of(ci_ * C, C), C)
        for ln in lns:
            r_k, lg, lb_ = prm_ref[0:1, ln], prm_ref[1:2, ln], prm_ref[2:3, ln]
            yn = _wkv_group_norm(yt_ref[:, ln], consts["hblk"])
            bonus = head_sum(r_ref[rows, ln] * k_ref[rows, ln] * r_k) * v_ref[rows, ln]
            o_ref[rows, ln] = ((yn * lg + lb_ + bonus) * g_ref[rows, ln]).astype(o_ref.dtype)

    def chunk(ci_, carry):
        finish(ci_ - 1)
        recurrence(ci_)
        return carry

    recurrence(0)
    lax.fori_loop(1, lb // C, chunk, 0)
    finish(lb // C - 1)


def _wkv_row_block(lp):
    nch = lp // WKV_CHUNK
    for d in (11, 8, 6, 4, 3, 2, 1):
        if nch % d == 0:
            return d * WKV_CHUNK
    return WKV_CHUNK


def _wkv(r, w, k, v, kk, b, g, prm):
    B, lp, D = r.shape
    lb = _wkv_row_block(lp)
    wl = WKV_PAIRS * LANES
    spec = pl.BlockSpec((None, lb, wl), lambda bi, pi, li: (bi, li, pi))
    return pl.pallas_call(
        functools.partial(_wkv_kernel, lb=lb),
        grid=(B, D // wl, lp // lb),
        in_specs=[spec] * 7 + [pl.BlockSpec((SUBLANES, wl), lambda bi, pi, li: (0, pi))],
        out_specs=spec,
        out_shape=jax.ShapeDtypeStruct((B, lp, D), BF16),
        scratch_shapes=[pltpu.VMEM((LANES, wl), F32)] + [pltpu.VMEM((LANES, LANES), F32)] * WKV_PAIRS,
        compiler_params=_params("parallel", "parallel", "arbitrary"),
        name="wkv7",
    )(r, w, k, v, kk, b, g, prm)


def _proj_ln_kernel(a_ref, w_ref, vec_ref, h_ref, o_ref):
    mix = _dot(a_ref[...], w_ref[...]) + vec_ref[0:1, :]
    o_ref[...] = _layer_norm(DEEPNORM_ALPHA * h_ref[...] + mix, vec_ref[1:2, :], vec_ref[2:3, :])


def _proj_rows(T):
    return 2 * TM if T % (2 * TM) == 0 else TM


def _proj_ln(a2d, w, vec, h2d):
    T, D = h2d.shape
    K = a2d.shape[1]
    tp = _proj_rows(T)
    return pl.pallas_call(
        _proj_ln_kernel,
        grid=(T // tp,),
        in_specs=[pl.BlockSpec((tp, K), lambda i: (i, 0)), _full(w.shape), _full(vec.shape),
                  pl.BlockSpec((tp, D), lambda i: (i, 0))],
        out_specs=pl.BlockSpec((tp, D), lambda i: (i, 0)),
        out_shape=jax.ShapeDtypeStruct((T, D), F32),
        compiler_params=_params("parallel"),
        name="proj_ln",
    )(a2d, w, vec, h2d)


def _qkv_kernel(x_ref, wq_ref, wk_ref, wv_ref, bq_ref, bk_ref, bv_ref, q_out, k_out, v_out):
    x = x_ref[...].astype(BF16)
    q_out[...] = ((_dot(x, wq_ref[...]) + bq_ref[...]) * ATT_HEAD_DIM ** -0.5).astype(BF16)
    k_out[...] = (_dot(x, wk_ref[...]) + bk_ref[...]).astype(BF16)
    v_out[...] = (_dot(x, wv_ref[...]) + bv_ref[...]).astype(BF16)


def _qkv(h2d, wq, wkd, wvd, bq, bkd, bvd):
    T, D = h2d.shape
    nq, nk = wq.shape[1], wkd.shape[1]
    ws = [wq, wkd, wvd, bq, bkd, bvd]
    tp = _proj_rows(T)
    return pl.pallas_call(
        _qkv_kernel,
        grid=(T // tp,),
        in_specs=[pl.BlockSpec((tp, D), lambda i: (i, 0))] + [_full(w.shape) for w in ws],
        out_specs=[pl.BlockSpec((tp, nq), lambda i: (i, 0)), pl.BlockSpec((tp, nk), lambda i: (i, 0)),
                   pl.BlockSpec((tp, nk), lambda i: (i, 0))],
        out_shape=[jax.ShapeDtypeStruct((T, nq), BF16), jax.ShapeDtypeStruct((T, nk), BF16),
                   jax.ShapeDtypeStruct((T, nk), BF16)],
        compiler_params=_params("parallel"),
        name="qkv_proj",
    )(h2d, *ws)


def _attn_kernel(sink_ref, q_ref, k_ref, v_ref, bband_ref, bmeta_ref, bmm_ref, o_ref, *, nb):
    c = pl.program_id(0)
    BLK = ATT_BLOCK
    M0 = FRONT
    lane = lax.broadcasted_iota(jnp.int32, (1, LANES), 1)
    m0 = lane < ATT_HEAD_DIM
    qi = lax.broadcasted_iota(jnp.int32, (2 * BLK, BLK), 0) % BLK
    sj = lax.broadcasted_iota(jnp.int32, (2 * BLK, BLK), 1)
    cur_vis = sj <= qi
    prev_vis = sj > qi
    half = lax.broadcasted_iota(jnp.int32, (2 * BLK, 1), 0) < BLK

    def stack(x):
        z = jnp.zeros_like(x)
        return jnp.concatenate([jnp.where(m0, x, z), jnp.where(m0, z, x)], axis=0)

    HP = range(4)
    kl = [slice((hp // 2) * LANES, (hp // 2 + 1) * LANES) for hp in HP]
    ql = [slice(hp * LANES, (hp + 1) * LANES) for hp in HP]

    def rowmax(lgs, floor):
        by_width = {}
        for lg in lgs:
            by_width.setdefault(lg.shape[1], []).append(lg)
        tops = [jnp.max(functools.reduce(jnp.maximum, g), axis=-1, keepdims=True) for g in by_width.values()]
        return functools.reduce(jnp.maximum, tops, floor)

    def attend(segs, sinks, n):
        it = range(len(segs))
        mx = [rowmax([lg for lg, _ in segs[i]], sinks[i]) for i in it]
        ps = [[jnp.exp(lg - mx[i]).astype(BF16) for lg, _ in segs[i]] for i in it]
        one = jnp.ones((1, LANES), BF16)
        va = [[jnp.where(m0, vals, one) for _, vals in segs[i]] for i in it]
        vb = [[jnp.where(m0, one, vals) for _, vals in segs[i]] for i in it]
        oa = [sum(_dot(p[:n], v) for p, v in zip(ps[i], va[i])) for i in it]
        ob = [sum(_dot(p[n:], v) for p, v in zip(ps[i], vb[i])) for i in it]
        out = []
        for i in it:
            st = jnp.exp(sinks[i] - mx[i])
            den = pltpu.roll(jnp.where(m0, ob[i], oa[i]), ATT_HEAD_DIM, axis=1)
            out.append(jnp.where(m0, oa[i], ob[i]) / (den + jnp.where(m0, st[:n], st[n:])))
        return out

    o_ref[0:M0, :] = jnp.zeros((M0, o_ref.shape[1]), o_ref.dtype)
    s0 = [sink_ref[c * 8 + hp * 2] for hp in HP]
    s1 = [sink_ref[c * 8 + hp * 2 + 1] for hp in HP]
    k_meta = [k_ref[M0:M0 + N_META, kl[hp]] for hp in HP]
    v_meta = [v_ref[M0:M0 + N_META, kl[hp]] for hp in HP]

    first = lax.broadcasted_iota(jnp.int32, (2 * N_META, 1), 0) < N_META
    mi = lax.broadcasted_iota(jnp.int32, (2 * N_META, N_META), 0) % N_META
    mj = lax.broadcasted_iota(jnp.int32, (2 * N_META, N_META), 1)
    qm = [stack(q_ref[M0:M0 + N_META, ql[hp]]) for hp in HP]
    lg = [jnp.where(mj <= mi, _dot(qm[hp], k_meta[hp], _NT) + bmm_ref[hp], NEG) for hp in HP]
    om = attend([[(lg[hp], v_meta[hp])] for hp in HP], [jnp.where(first, s0[hp], s1[hp]) for hp in HP], N_META)
    for hp in HP:
        o_ref[M0:M0 + N_META, ql[hp]] = om[hp].astype(o_ref.dtype)

    sink_q = [jnp.where(half, s0[hp], s1[hp]) for hp in HP]

    unroll = 2 if nb % 2 == 0 else 1

    def blocks(jo, carry):
        items = [(jo * unroll + u, hp) for u in range(unroll) for hp in HP]
        start = [pl.multiple_of(M0 + N_META + j * BLK, 64) for j, _ in items]
        pstart = [pl.multiple_of(jnp.maximum(s - BLK, 0), 64) for s in start]
        it = range(len(items))
        qs = [stack(q_ref[pl.ds(start[i], BLK), ql[items[i][1]]]) for i in it]
        k_prev = [k_ref[pl.ds(pstart[i], BLK), kl[items[i][1]]] for i in it]
        k_cur = [k_ref[pl.ds(start[i], BLK), kl[items[i][1]]] for i in it]
        lg_meta = [_dot(qs[i], k_meta[items[i][1]], _NT) + bmeta_ref[items[i][1], items[i][0]] for i in it]
        lg_prev = [jnp.where(prev_vis & (items[i][0] > 0),
                             _dot(qs[i], k_prev[i], _NT) + bband_ref[items[i][1], :, 0:BLK], NEG) for i in it]
        lg_cur = [jnp.where(cur_vis, _dot(qs[i], k_cur[i], _NT) + bband_ref[items[i][1], :, BLK:2 * BLK], NEG)
                  for i in it]
        out = attend([[(lg_meta[i], v_meta[items[i][1]]),
                       (lg_prev[i], v_ref[pl.ds(pstart[i], BLK), kl[items[i][1]]]),
                       (lg_cur[i], v_ref[pl.ds(start[i], BLK), kl[items[i][1]]])] for i in it],
                     [sink_q[hp] for _, hp in items], BLK)
        for i in it:
            o_ref[pl.ds(start[i], BLK), ql[items[i][1]]] = out[i].astype(o_ref.dtype)
        return carry

    lax.fori_loop(0, nb // unroll, blocks, 0)


def _attention(q, kd, vd, sinks, bband, bmeta, bmm):
    B, lp, _ = q.shape
    nb = (lp - FRONT - N_META) // ATT_BLOCK
    qw, kw = 4 * LANES, 2 * LANES
    return pl.pallas_call(
        functools.partial(_attn_kernel, nb=nb),
        grid=(2, B),
        in_specs=[pl.BlockSpec(memory_space=pltpu.SMEM),
                  pl.BlockSpec((None, lp, qw), lambda c, b: (b, 0, c)),
                  pl.BlockSpec((None, lp, kw), lambda c, b: (b, 0, c)),
                  pl.BlockSpec((None, lp, kw), lambda c, b: (b, 0, c)),
                  pl.BlockSpec((4,) + bband.shape[1:], lambda c, b: (c, 0, 0)),
                  pl.BlockSpec((4,) + bmeta.shape[1:], lambda c, b: (c, 0, 0, 0)),
                  pl.BlockSpec((4,) + bmm.shape[1:], lambda c, b: (c, 0, 0))],
        out_specs=pl.BlockSpec((None, lp, qw), lambda c, b: (b, 0, c)),
        out_shape=jax.ShapeDtypeStruct((B, lp, D_MODEL), BF16),
        compiler_params=_params("parallel", "parallel"),
        name="swa_attention",
    )(sinks, q, kd, vd, bband, bmeta, bmm)


def _t5_bucket(dist):
    exact = N_BUCKETS // 2
    d = jnp.maximum(dist, 0)
    ratio = jnp.log(jnp.maximum(d, 1).astype(F32) / exact) / math.log(MAX_DISTANCE / exact)
    large = jnp.minimum(exact + (ratio * (N_BUCKETS - exact)).astype(jnp.int32), N_BUCKETS - 1)
    return jnp.where(d < exact, d, large)


def _bias_tables(rel_bias, nb):
    H, BLK = ATT_HEADS, ATT_BLOCK

    def lookup(dist):
        onehot = (_t5_bucket(dist)[..., None] == jnp.arange(N_BUCKETS)).astype(F32)
        return jnp.dot(onehot, rel_bias.astype(F32), precision=lax.Precision.HIGHEST)

    qi = jnp.arange(BLK)[:, None]
    band = lookup(qi + BLK - jnp.arange(2 * BLK)[None, :])
    band = jnp.moveaxis(band, -1, 0).reshape(H // 2, 2 * BLK, 2 * BLK)
    pos = jnp.arange(nb * BLK)[:, None]
    meta = lookup(N_META + pos - jnp.arange(N_META)[None, :])
    meta = meta.reshape(nb, BLK, N_META, H // 2, 2).transpose(3, 0, 4, 1, 2)
    meta = meta.reshape(H // 2, nb, 2 * BLK, N_META)
    pm = jnp.arange(N_META)
    mm = lookup(pm[:, None] - pm[None, :])
    mm = jnp.moveaxis(mm, -1, 0).reshape(H // 2, 2 * N_META, N_META)
    return band.astype(F32), meta.astype(F32), mm.astype(F32)


def _dispatch_geometry(a1):
    E = N_EXPERTS
    routed = (a1 > 0.0).astype(BF16)
    n_col = jnp.sum(routed.astype(F32), axis=1, keepdims=True)
    ch_col = jnp.floor((n_col + (MOE_CH - 1)) * (1.0 / MOE_CH))
    ei = lax.broadcasted_iota(jnp.int32, (E, E), 0)
    ej = lax.broadcasted_iota(jnp.int32, (E, E), 1)
    off_col = _dot((ei > ej).astype(BF16), jnp.broadcast_to(ch_col, (E, LANES)))[:, 0:1]
    n_row = _dot(jnp.ones((SUBLANES, TM), BF16), routed, _NT)
    ch_row = jnp.floor((n_row + (MOE_CH - 1)) * (1.0 / MOE_CH))
    off_row = _dot(ch_row, (ei < ej).astype(BF16))
    return ch_col, off_col, ch_row, off_row


def _select_experts(choice):
    E, G, EPG = N_EXPERTS, N_GROUPS, N_EXPERTS // N_GROUPS
    grp = choice.reshape(G, EPG, TM)
    sub = lax.broadcasted_iota(jnp.int32, (G, EPG, TM), 1)
    top1 = jnp.max(grp, axis=1, keepdims=True)
    first = jnp.min(jnp.where(grp == top1, sub, EPG), axis=1, keepdims=True)
    top2 = jnp.max(jnp.where(sub == first, -jnp.inf, grp), axis=1, keepdims=True)
    gscore = jnp.broadcast_to(top1 + top2, (G, EPG, TM))

    gi = lax.broadcasted_iota(jnp.int32, (G, EPG, TM), 0)
    keep = jnp.zeros((G, EPG, TM), jnp.bool_)
    for _ in range(TOPK_GROUPS):
        m = jnp.max(gscore, axis=0, keepdims=True)
        sel = gi == jnp.min(jnp.where(gscore == m, gi, G), axis=0, keepdims=True)
        keep = keep | sel
        gscore = jnp.where(sel, -jnp.inf, gscore)
    keep_e = keep.reshape(E, TM)

    cand = jnp.where(keep_e, choice, -jnp.inf)
    ei = lax.broadcasted_iota(jnp.int32, (E, TM), 0)
    routed = jnp.zeros((E, TM), jnp.bool_)
    for _ in range(TOP_K):
        m = jnp.max(cand, axis=0, keepdims=True)
        sel = ei == jnp.min(jnp.where(cand == m, ei, E), axis=0, keepdims=True)
        routed = routed | sel
        cand = jnp.where(sel, -jnp.inf, cand)
    return routed


def _route_kernel(x_ref, rt_ref, bias_ref, a1_out, a1t_out, gate_out, cnt_out, off_out, *, tiles):
    U = range(tiles)
    xs = [x_ref[u * TM:(u + 1) * TM, :] for u in U]
    scores = [_sigmoid(_dot3(rt_ref[...], x, _NT)) for x in xs]
    routed = [_select_experts(s + bias_ref[:, 0:1]) for s in scores]
    gate = [jnp.where(r, s, 0.0) for r, s in zip(routed, scores)]
    gate = [g / jnp.sum(g, axis=0, keepdims=True) * ROUTED_SCALE for g in gate]
    ti = lax.broadcasted_iota(jnp.int32, (TM, TM), 0)
    tj = lax.broadcasted_iota(jnp.int32, (TM, TM), 1)
    eye = (ti == tj).astype(BF16)
    routed_b = [r.astype(BF16) for r in routed]
    rank = [_dot(rb, (ti < tj).astype(BF16)) for rb in routed_b]
    rank_t = [_dot((tj < ti).astype(BF16), rb, _NT) for rb in routed_b]
    routed_t = [_dot(eye, rb, _NT) for rb in routed_b]
    gate_t = [_dot_exact_lhs(eye, g, _NT) for g in gate]
    for u in U:
        a1 = jnp.where(routed[u], rank[u] + 1.0, 0.0)
        _, _, ch_row, off_row = _dispatch_geometry(a1)
        a1_out[u] = a1
        a1t_out[u] = routed_t[u] * (rank_t[u] + 1.0)
        gate_out[u] = jnp.concatenate(_split2(gate_t[u]), axis=1)
        cnt_out[u] = ch_row
        off_out[u] = off_row


def _dispatch_kernel(x_ref, a1_ref, gate_ref, cnt_ref, off_ref, xs_out, gs_out):
    E = N_EXPERTS
    lo = jnp.concatenate([off_ref[0:1, :]] * 2, axis=1) * MOE_CH
    hi = lo + jnp.concatenate([cnt_ref[0:1, :]] * 2, axis=1) * MOE_CH
    first = lax.broadcasted_iota(jnp.int32, (1, 2 * E), 1) < E
    lo1 = jnp.where(first, lo, 0.0)
    xg = jnp.concatenate([x_ref[...].astype(BF16), gate_ref[...]], axis=1)
    a1b = a1_ref[...].astype(BF16)
    SB = MOE_SLOTS // 3
    for s0 in range(0, MOE_SLOTS, SB):
        si = (lax.broadcasted_iota(jnp.int32, (SB, 2 * E), 0) + s0).astype(F32)
        member = (si >= lo) & (si < hi)
        seg0 = jnp.sum(jnp.where(member, lo1, 0.0), axis=1, keepdims=True)
        rs1 = (lax.broadcasted_iota(jnp.int32, (SB, 1), 0) + (s0 + 1)).astype(F32) - seg0
        perm = _dot(member[:, :E], a1b) == rs1
        disp = _dot(perm, xg)
        xs_out[s0:s0 + SB, :] = disp[:, :D_MODEL].astype(BF16)
        gs_out[s0:s0 + SB, :] = jnp.sum(jnp.where(member, disp[:, D_MODEL:], 0.0), axis=1, keepdims=True)


def _router(h2d, router_t, bias_col):
    T, D = h2d.shape
    nt = T // TM
    E = N_EXPERTS
    tiles = next(t for t in (4, 3, 2, 1) if nt % t == 0)
    a1, a1t, gate, cnt, off = pl.pallas_call(
        functools.partial(_route_kernel, tiles=tiles),
        grid=(nt // tiles,),
        in_specs=[pl.BlockSpec((tiles * TM, D), lambda i: (i, 0)), _full(router_t.shape), _full(bias_col.shape)],
        out_specs=[pl.BlockSpec((tiles, E, TM), lambda i: (i, 0, 0)),
                   pl.BlockSpec((tiles, TM, E), lambda i: (i, 0, 0)),
                   pl.BlockSpec((tiles, TM, 2 * E), lambda i: (i, 0, 0)),
                   pl.BlockSpec((tiles, SUBLANES, E), lambda i: (i, 0, 0)),
                   pl.BlockSpec((tiles, SUBLANES, E), lambda i: (i, 0, 0))],
        out_shape=[jax.ShapeDtypeStruct((nt, E, TM), F32),
                   jax.ShapeDtypeStruct((nt, TM, E), F32),
                   jax.ShapeDtypeStruct((nt, TM, 2 * E), BF16),
                   jax.ShapeDtypeStruct((nt, SUBLANES, E), F32),
                   jax.ShapeDtypeStruct((nt, SUBLANES, E), F32)],
        compiler_params=_params("parallel"),
        name="moe_route",
    )(h2d, router_t, bias_col)
    xs, gs = pl.pallas_call(
        _dispatch_kernel,
        grid=(nt,),
        in_specs=[pl.BlockSpec((TM, D), lambda i: (i, 0)),
                  pl.BlockSpec((None, E, TM), lambda i: (i, 0, 0)),
                  pl.BlockSpec((None, TM, 2 * E), lambda i: (i, 0, 0)),
                  pl.BlockSpec((None, SUBLANES, E), lambda i: (i, 0, 0)),
                  pl.BlockSpec((None, SUBLANES, E), lambda i: (i, 0, 0))],
        out_specs=[pl.BlockSpec((None, MOE_SLOTS, D), lambda i: (i, 0, 0)),
                   pl.BlockSpec((None, MOE_SLOTS, 1), lambda i: (i, 0, 0))],
        out_shape=[jax.ShapeDtypeStruct((nt, MOE_SLOTS, D), BF16),
                   jax.ShapeDtypeStruct((nt, MOE_SLOTS, 1), F32)],
        compiler_params=_params("parallel"),
        name="moe_dispatch",
    )(h2d, a1, gate, cnt, off)
    return xs, gs, a1, a1t, cnt


def _expert_kernel(be_ref, ps_ref, nb_ref, src_ref, xs_hbm, wg_ref, wu_ref, wd_ref, ys_hbm,
                   xbuf, ybuf, wgu_bf, wd_bf, in_sem, out_sem, *, nblk):
    b = pl.program_id(0)
    slot = lax.rem(b, 2)

    def in_copy(src_chunk, sl, c):
        return pltpu.make_async_copy(xs_hbm.at[src_chunk], xbuf.at[sl, c], in_sem.at[sl])

    def out_copy(dst_chunk, sl, c):
        return pltpu.make_async_copy(ybuf.at[sl, c], ys_hbm.at[dst_chunk], out_sem.at[sl])

    def for_chunks(blk, fn):
        p0 = ps_ref[blk]
        n = nb_ref[blk]
        for c0 in range(0, MOE_CB, MOE_SEG):

            @pl.when(n >= c0 + MOE_SEG)
            def _():
                for c in range(c0, c0 + MOE_SEG):
                    fn(src_ref[p0 + c], c)

            @pl.when((n > c0) & (n < c0 + MOE_SEG))
            def _():
                def body(c, carry):
                    fn(src_ref[p0 + c], c)
                    return carry

                lax.fori_loop(c0, n, body, 0)

    @pl.when(b == 0)
    def _():
        xbuf[...] = jnp.zeros_like(xbuf)
        for_chunks(0, lambda s, c: in_copy(s, 0, c).start())

    @pl.when(b + 1 < nblk)
    def _():
        for_chunks(b + 1, lambda s, c: in_copy(s, 1 - slot, c).start())

    for_chunks(b, lambda s, c: in_copy(s, slot, c).wait())

    @pl.when(b >= 2)
    def _():
        for_chunks(b - 2, lambda s, c: out_copy(s, slot, c).wait())

    @pl.when((b == 0) | (be_ref[b] != be_ref[jnp.maximum(b - 1, 0)]))
    def _():
        wgu_bf[:, :EXPERT_FF] = wg_ref[...].astype(BF16)
        wgu_bf[:, EXPERT_FF:] = wu_ref[...].astype(BF16)
        wd_bf[...] = wd_ref[...].astype(BF16)

    for c0 in range(0, MOE_CB, MOE_SEG):

        @pl.when(nb_ref[b] > c0)
        def _():
            x = xbuf[slot, c0:c0 + MOE_SEG].reshape(MOE_SEG * MOE_CH, D_MODEL)
            hid = _dot(x, wgu_bf[...])
            act = _silu(hid[:, :EXPERT_FF]) * hid[:, EXPERT_FF:]
            y = _dot(act, wd_bf[...])
            ybuf[slot, c0:c0 + MOE_SEG] = y.astype(BF16).reshape(MOE_SEG, MOE_CH, D_MODEL)

    for_chunks(b, lambda s, c: out_copy(s, slot, c).start())

    @pl.when(b == nblk - 1)
    def _():
        for_chunks(b, lambda s, c: out_copy(s, slot, c).wait())

        @pl.when(b >= 1)
        def _():
            for_chunks(b - 1, lambda s, c: out_copy(s, 1 - slot, c).wait())


def _experts(xs, wg, wu, wd, layer, be, pstart, nbc, src):
    nt = xs.shape[0]
    nblk = be.shape[0]
    xs_c = xs.reshape(nt * MOE_CPT, MOE_CH, D_MODEL)
    by_expert = lambda b, be, ps, nb, src: (layer, be[b], 0, 0)
    grid_spec = pltpu.PrefetchScalarGridSpec(
        num_scalar_prefetch=4,
        grid=(nblk,),
        in_specs=[pl.BlockSpec(memory_space=pl.ANY),
                  pl.BlockSpec((None, None, D_MODEL, EXPERT_FF), by_expert),
                  pl.BlockSpec((None, None, D_MODEL, EXPERT_FF), by_expert),
                  pl.BlockSpec((None, None, EXPERT_FF, D_MODEL), by_expert)],
        out_specs=pl.BlockSpec(memory_space=pl.ANY),
        scratch_shapes=[pltpu.VMEM((2, MOE_CB, MOE_CH, D_MODEL), BF16),
                        pltpu.VMEM((2, MOE_CB, MOE_CH, D_MODEL), BF16),
                        pltpu.VMEM((D_MODEL, 2 * EXPERT_FF), BF16),
                        pltpu.VMEM((EXPERT_FF, D_MODEL), BF16),
                        pltpu.SemaphoreType.DMA((2,)), pltpu.SemaphoreType.DMA((2,))],
    )
    ys = pl.pallas_call(
        functools.partial(_expert_kernel, nblk=nblk),
        grid_spec=grid_spec,
        out_shape=jax.ShapeDtypeStruct(xs_c.shape, BF16),
        input_output_aliases={4: 0},
        compiler_params=_params("arbitrary"),
        name="moe_experts",
    )(be, pstart, nbc, src, xs_c, wg, wu, wd)
    return ys.reshape(nt, MOE_SLOTS, D_MODEL)


def _combine_kernel(h_ref, ys_ref, gs_ref, a1_ref, a1t_ref, wgu_ref, wd_ref, vec_ref, o_ref):
    E = N_EXPERTS
    h = h_ref[...]
    ch_col, off_col, _, _ = _dispatch_geometry(a1_ref[...])
    si = lax.broadcasted_iota(jnp.int32, (E, MOE_SLOTS), 1).astype(F32)
    lo = off_col * MOE_CH
    member_t = (si >= lo) & (si < lo + ch_col * MOE_CH)
    seg0 = jnp.sum(jnp.where(member_t, lo, 0.0), axis=0, keepdims=True)
    rs1 = lax.broadcasted_iota(jnp.int32, (1, MOE_SLOTS), 1).astype(F32) - seg0 + 1.0
    perm_t = _dot(a1t_ref[...], member_t) == rs1
    routed = _dot(perm_t.astype(BF16), ys_ref[...].astype(F32) * gs_ref[...])
    xb = h.astype(BF16)
    hid = _dot(xb, wgu_ref[...])
    shared = _dot(_silu(hid[:, :EXPERT_FF]) * hid[:, EXPERT_FF:], wd_ref[...])
    o_ref[...] = _layer_norm(DEEPNORM_ALPHA * h + (shared + routed), vec_ref[0:1, :], vec_ref[1:2, :])


def _combine(h2d, ys, gs, a1, a1t, wgu_s, wd_s, vec):
    T, D = h2d.shape
    nt = T // TM
    return pl.pallas_call(
        _combine_kernel,
        grid=(nt,),
        in_specs=[pl.BlockSpec((TM, D), lambda i: (i, 0)),
                  pl.BlockSpec((None, MOE_SLOTS, D), lambda i: (i, 0, 0)),
                  pl.BlockSpec((None, MOE_SLOTS, 1), lambda i: (i, 0, 0)),
                  pl.BlockSpec((None, N_EXPERTS, TM), lambda i: (i, 0, 0)),
                  pl.BlockSpec((None, TM, N_EXPERTS), lambda i: (i, 0, 0)),
                  _full(wgu_s.shape), _full(wd_s.shape), _full(vec.shape)],
        out_specs=pl.BlockSpec((TM, D), lambda i: (i, 0)),
        out_shape=jax.ShapeDtypeStruct((T, D), F32),
        compiler_params=_params("parallel"),
        name="moe_combine",
    )(h2d, ys, gs, a1, a1t, wgu_s, wd_s, vec)


def _dispatch_tables(chunks):
    nt, E = chunks.shape
    maxc = nt * MOE_CPT
    nblk = -(-maxc // MOE_CB) + E
    hp = lax.Precision.HIGHEST
    ch = chunks.astype(F32)
    off = jnp.cumsum(ch, axis=1) - ch
    wend = jnp.cumsum(ch, axis=0)
    cnt_e = wend[-1]
    cend = jnp.cumsum(cnt_e)
    base_e = cend - cnt_e
    srcbase = jnp.arange(nt, dtype=F32)[:, None] * MOE_CPT + off - (wend - ch)
    p = jnp.arange(maxc, dtype=F32)
    e_p = jnp.minimum(jnp.sum((cend[None, :] <= p[:, None]).astype(jnp.int32), axis=1), E - 1)
    oh_e = (e_p[:, None] == jnp.arange(E, dtype=jnp.int32)[None, :]).astype(F32)
    look = jnp.dot(oh_e, jnp.concatenate([wend.T, srcbase.T, base_e[:, None]], axis=1), precision=hp)
    q = p - look[:, 2 * nt]
    i_p = jnp.sum((look[:, :nt] <= q[:, None]).astype(jnp.int32), axis=1)
    oh_i = i_p[:, None] == jnp.arange(nt, dtype=jnp.int32)[None, :]
    src = jnp.sum(jnp.where(oh_i, look[:, nt:2 * nt], 0.0), axis=1) + q
    src = jnp.clip(src, 0, maxc - 1).astype(jnp.int32)
    nblk_e = jnp.floor((cnt_e + (MOE_CB - 1)) * (1.0 / MOE_CB))
    bend = jnp.cumsum(nblk_e)
    bidx = jnp.arange(nblk, dtype=F32)
    be = jnp.minimum(jnp.sum((bend[None, :] <= bidx[:, None]).astype(jnp.int32), axis=1), E - 1)
    oh_b = (be[:, None] == jnp.arange(E, dtype=jnp.int32)[None, :]).astype(F32)
    lookb = jnp.dot(oh_b, jnp.stack([bend - nblk_e, cnt_e, base_e], axis=1), precision=hp)
    q0 = (bidx - lookb[:, 0]) * MOE_CB
    nbc = jnp.clip(lookb[:, 1] - q0, 0, MOE_CB).astype(jnp.int32)
    pstart = jnp.clip(lookb[:, 2] + q0, 0, maxc - MOE_CB).astype(jnp.int32)
    return be.astype(jnp.int32), pstart, nbc, src


def _moe_layer(h2d, layer, router, bias, w_gate, w_up, w_down, sh_gate, sh_up, sh_down, ln_g, ln_b):
    router_t = router.T.astype(F32)
    bias_col = jnp.broadcast_to(bias.astype(F32)[:, None], (N_EXPERTS, LANES))
    xs, gs, a1, a1t, cnt = _router(h2d, router_t, bias_col)
    be, pstart, nbc, src = _dispatch_tables(cnt[:, 0, :].astype(jnp.int32))
    ys = _experts(xs, w_gate, w_up, w_down, layer, be, pstart, nbc, src)
    wgu_s = jnp.concatenate([sh_gate, sh_up], axis=-1).astype(BF16)
    vec = jnp.zeros((SUBLANES, D_MODEL), F32).at[0].set(ln_g).at[1].set(ln_b)
    return _combine(h2d, ys, gs, a1, a1t, wgu_s, sh_down.astype(BF16), vec)


def _rows(*vs):
    out = jnp.zeros((SUBLANES, vs[0].shape[-1]), F32)
    for i, v in enumerate(vs):
        out = out.at[i].set(v.astype(F32))
    return out


def kernel(x, meta, rel_bias, rwkv_mu, rwkv_w0, rwkv_w1, rwkv_w2, rwkv_a0, rwkv_a1, rwkv_a2, rwkv_g1, rwkv_g2, rwkv_k_k, rwkv_k_a, rwkv_r_k, rwkv_w_r, rwkv_w_k, rwkv_w_v, rwkv_w_o, rwkv_lnx_g, rwkv_lnx_b, attn_w_qkv, attn_b_qkv, attn_sinks, attn_w_o, attn_b_o, ln_mix_g, ln_mix_b, ln_ffn_g, ln_ffn_b, moe_router, moe_bias, moe_w_gate, moe_w_up, moe_w_down, shared_w_gate, shared_w_up, shared_w_down):
    B, S, D = x.shape
    assert D == D_MODEL and S % ATT_BLOCK == 0 and (FRONT + N_META) % WKV_CHUNK == 0
    lp = FRONT + N_META + S
    T = B * lp
    assert T % TM == 0 and lp >= TM
    h = jnp.concatenate([jnp.zeros((B, FRONT, D), x.dtype),
                         jnp.broadcast_to(meta[None].astype(x.dtype), (B, N_META, D)), x], axis=1)
    h = h.reshape(T, D)
    bf = lambda w: w.astype(BF16)

    H, N = RWKV_HEADS, RWKV_HEAD
    head_of = jnp.arange(D) // N
    gsum = (head_of[:, None] == jnp.arange(LANES)[None, :]).astype(BF16)
    gexp = gsum.T
    r, w, k, v, kk, b, g = _rwkv_proj(
        h, lp, _rows(*rwkv_mu[0]), _rows(rwkv_w0[0], rwkv_a0[0], rwkv_k_k[0], rwkv_k_a[0]),
        bf(rwkv_w_r[0]), bf(rwkv_w_k[0]), bf(rwkv_w_v[0]), bf(rwkv_w1[0]), bf(rwkv_w2[0]),
        bf(rwkv_a1[0]), bf(rwkv_a2[0]), bf(rwkv_g1[0]), bf(rwkv_g2[0]), gsum, gexp)
    to3 = lambda t: t.reshape(B, lp, D)
    prm = _rows(rwkv_r_k[0].reshape(D), rwkv_lnx_g[0], rwkv_lnx_b[0])
    o = _wkv(to3(r), to3(w), to3(k), to3(v), to3(kk), to3(b), to3(g), prm)
    h = _proj_ln(o.reshape(T, D), bf(rwkv_w_o[0]), _rows(jnp.zeros((D,), F32), ln_mix_g[0], ln_mix_b[0]), h)
    h = _moe_layer(h, 0, moe_router[0], moe_bias[0], moe_w_gate, moe_w_up, moe_w_down,
                   shared_w_gate[0], shared_w_up[0], shared_w_down[0], ln_ffn_g[0], ln_ffn_b[0])

    HD, KV = ATT_HEAD_DIM, ATT_KV_HEADS
    qw = ATT_HEADS * HD
    wqkv, bqkv = attn_w_qkv[0], attn_b_qkv[0]
    dup = lambda t: jnp.concatenate([t.reshape(-1, KV, 1, HD)] * 2, axis=2).reshape(t.shape[0], 2 * KV * HD)
    wq, wkd, wvd = wqkv[:, :qw], dup(wqkv[:, qw:qw + KV * HD]), dup(wqkv[:, qw + KV * HD:])
    bq, bkd, bvd = bqkv[None, :qw], dup(bqkv[None, qw:qw + KV * HD]), dup(bqkv[None, qw + KV * HD:])
    q, kd, vd = _qkv(h, bf(wq), bf(wkd), bf(wvd), bq.astype(F32), bkd.astype(F32), bvd.astype(F32))
    bband, bmeta, bmm = _bias_tables(rel_bias, S // ATT_BLOCK)
    o = _attention(q.reshape(B, lp, qw), kd.reshape(B, lp, -1), vd.reshape(B, lp, -1),
                   attn_sinks[0].astype(F32), bband, bmeta, bmm)
    h = _proj_ln(o.reshape(T, D), bf(attn_w_o[0]), _rows(attn_b_o[0], ln_mix_g[1], ln_mix_b[1]), h)
    h = _moe_layer(h, 1, moe_router[1], moe_bias[1], moe_w_gate, moe_w_up, moe_w_down,
                   shared_w_gate[1], shared_w_up[1], shared_w_down[1], ln_ffn_g[1], ln_ffn_b[1])
    return h.reshape(B, lp, D)[:, FRONT + N_META:]
```

```python
import functools
import math

import jax
import jax.numpy as jnp
from jax import lax
from jax.experimental import pallas as pl
from jax.experimental.pallas import tpu as pltpu

F32 = jnp.float32
BF16 = jnp.bfloat16

D_MODEL = 1024
DEPTH = 2
N_META = 16
RWKV_HEAD = 64
RWKV_HEADS = D_MODEL // RWKV_HEAD
GN_EPS = 64e-5
ATT_HEADS = 16
ATT_KV_HEADS = 4
ATT_HEAD_DIM = D_MODEL // ATT_HEADS
ATT_GROUP = ATT_HEADS // ATT_KV_HEADS
WINDOW = 128
ATT_BLOCK = 128
N_BUCKETS = 32
MAX_DISTANCE = 128
N_EXPERTS = 64
TOP_K = 8
N_GROUPS = 8
TOPK_GROUPS = 4
EXPERT_FF = 256
ROUTED_SCALE = 2.5
DEEPNORM_ALPHA = (2 * DEPTH) ** 0.25
LN_EPS = 1e-5

LANES = 128
SUBLANES = 8
BF16_ROWS = 16
VMEM_LIMIT = 56 * 1024 * 1024

FRONT = 48
TM = 256
WKV_CHUNK = 64
WKV_PAIRS = 8
WKV_BATCH = 2
WKV_SUB = 16
ATT_STEP_KV = 2
ATT_STEP_PAIRS = ATT_STEP_KV * ATT_GROUP // 2
ROW_ALIGN = FRONT + N_META
MOE_CH = BF16_ROWS
MOE_SLOTS = TM * TOP_K + N_EXPERTS * MOE_CH
MOE_CPT = MOE_SLOTS // MOE_CH
MOE_CB = 128
MOE_SEG = 32
MOE_SLOT_SPLITS = 3
NEG = -1e30

_NN = (((1,), (0,)), ((), ()))
_NT = (((1,), (1,)), ((), ()))


def _dot(a, b, dn=_NN):
    return lax.dot_general(a.astype(BF16), b.astype(BF16), dn, preferred_element_type=F32)


def _split2(x):
    hi = x.astype(BF16)
    lo = (x - hi.astype(F32)).astype(BF16)
    return hi, lo


def _split3(x):
    h1 = x.astype(BF16)
    r1 = x - h1.astype(F32)
    h2 = r1.astype(BF16)
    h3 = (r1 - h2.astype(F32)).astype(BF16)
    return h1, h2, h3


def _dot_exact_lhs(a01, b, dn=_NN):
    a = a01.astype(BF16)
    return sum(lax.dot_general(a, p, dn, preferred_element_type=F32) for p in _split3(b))


def _dot_hilo_lhs(a01, b, dn=_NN):
    a = a01.astype(BF16)
    return sum(lax.dot_general(a, p, dn, preferred_element_type=F32) for p in _split2(b))


def _dot_hilo_rhs(a, b01, dn=_NN):
    b = b01.astype(BF16)
    return sum(lax.dot_general(p, b, dn, preferred_element_type=F32) for p in _split2(a))


def _dot3(a, b, dn=_NN):
    ah, al = _split2(a)
    bh, bl = _split2(b)
    d = lambda x, y: lax.dot_general(x, y, dn, preferred_element_type=F32)
    return d(ah, bh) + (d(ah, bl) + d(al, bh))


def _sigmoid(x):
    return 1.0 / (1.0 + jnp.exp(-x))


def _silu(x):
    return x * _sigmoid(x)


def _layer_norm(x, g, b):
    mu = jnp.mean(x, axis=-1, keepdims=True)
    xc = x - mu
    var = jnp.mean(xc * xc, axis=-1, keepdims=True)
    return xc * lax.rsqrt(var + LN_EPS) * g + b


def _full(shape):
    nd = len(shape)
    return pl.BlockSpec(shape, lambda *_: (0,) * nd)


def _params(*sem):
    return pltpu.CompilerParams(dimension_semantics=sem, vmem_limit_bytes=VMEM_LIMIT)


def _rwkv_proj_kernel(x_ref, xp_ref, mu_ref, vec_ref, wr_ref, wk_ref, wv_ref, w1_ref, w2_ref,
                      a1_ref, a2_ref, g1_ref, g2_ref, gsum_ref, gexp_ref,
                      r_out, w_out, k_out, v_out, kk_out, b_out, g_out, *, lp):
    i = pl.program_id(0)
    x = x_ref[...]
    rows = x.shape[0]
    row = lax.broadcasted_iota(jnp.int32, (rows, 1), 0)
    pos = lax.rem(i * rows, lp) + row
    pos = jnp.where(pos >= lp, pos - lp, pos)
    prev = jnp.where(row == 0, xp_ref[SUBLANES - 1:SUBLANES, :], pltpu.roll(x, 1, axis=0))
    prev = jnp.where(pos == FRONT, 0.0, prev)
    xx = prev - x
    valid = pos >= FRONT

    def mix(j):
        return (x + xx * mu_ref[j:j + 1, :]).astype(BF16)

    w0, a0, k_k, k_a = (vec_ref[j:j + 1, :] for j in range(4))
    r = _dot(mix(0), wr_ref[...])
    z = w0 + _dot(jnp.tanh(_dot(mix(1), w1_ref[...])), w2_ref[...])
    w = -math.exp(-0.5) * _sigmoid(z)
    k = _dot(mix(2), wk_ref[...])
    v = _dot(mix(3), wv_ref[...])
    a = _sigmoid(a0 + _dot(_dot(mix(4), a1_ref[...]), a2_ref[...]))
    g = _dot(_sigmoid(_dot(mix(5), g1_ref[...])), g2_ref[...])
    kk = k * k_k
    ssq = _dot_hilo_rhs(kk * kk, gsum_ref[...])
    kk = kk * lax.rsqrt(jnp.maximum(_dot_hilo_rhs(ssq, gexp_ref[...]), 1e-24))
    k = k * (1.0 + (a - 1.0) * k_a)
    r_out[...] = r
    w_out[...] = jnp.where(valid, w, 0.0)
    k_out[...] = jnp.where(valid, k, 0.0)
    v_out[...] = jnp.where(valid, v, 0.0)
    kk_out[...] = jnp.where(valid, kk, 0.0)
    b_out[...] = jnp.where(valid, kk * a, 0.0)
    g_out[...] = g


def _rwkv_proj(h2d, lp, mu, vec, wr, wk, wv, w1, w2, a1, a2, g1, g2, gsum, gexp):
    T, D = h2d.shape
    tp = _proj_rows(T)
    assert tp <= lp
    row_spec = pl.BlockSpec((tp, D), lambda i: (i, 0))
    prev_spec = pl.BlockSpec((SUBLANES, D), lambda i: (jnp.maximum(i * (tp // SUBLANES) - 1, 0), 0))
    ws = [mu, vec, wr, wk, wv, w1, w2, a1, a2, g1, g2, gsum, gexp]
    once = lambda w: pl.BlockSpec(w.shape, lambda i: (0,) * w.ndim, pipeline_mode=pl.Buffered(1))
    return pl.pallas_call(
        functools.partial(_rwkv_proj_kernel, lp=lp),
        grid=(T // tp,),
        in_specs=[row_spec, prev_spec] + [once(w) for w in ws],
        out_specs=[row_spec] * 7,
        out_shape=[jax.ShapeDtypeStruct((T, D), F32)] * 7,
        compiler_params=_params("parallel"),
        name="rwkv_proj",
    )(h2d, h2d, *ws)


def _wkv_chunk(r, w, cum, k, v, kk, b, S, c):
    C = WKV_CHUNK
    P = range(len(r))
    bf = lambda xs: [x.astype(BF16) for x in xs]
    each = lambda f, *ls: [f(*a) for a in zip(*ls)]

    def stack(x):
        return jnp.concatenate([jnp.where(c["m0"], x, 0.0), jnp.where(c["m0"], 0.0, x)], axis=0)

    f32 = lambda xs: [x.astype(F32) for x in xs]
    cat0 = lambda x, y: jnp.concatenate([x, y], axis=0)
    cat1 = lambda x, y: jnp.concatenate([x, y], axis=1)
    nt = lambda x, y: _dot(x, y, _NT)
    C2 = 2 * C

    tot = [cum[p][C - 1:C, :] for p in P]
    inv = [jnp.exp(-cum[p]) for p in P]
    dend = [jnp.exp(tot[p] - cum[p]) for p in P]
    kr_s = bf([cat0(stack(kk[p] * jnp.exp(cum[p] - w[p])), stack(r[p] * jnp.exp(cum[p]))) for p in P])
    bk_s = bf([cat0(stack(b[p] * inv[p]), stack(k[p] * inv[p])) for p in P])
    bh_s = bf([stack(b[p] * dend[p]) for p in P])
    kh_s = bf([stack(k[p] * dend[p]) for p in P])
    vs_t = bf([stack(v[p]).T for p in P])

    sc = each(nt, bk_s, kr_s)
    lab_t = [jnp.where(c["upper"], x[:C2, :C2], 0.0) for x in sc]
    arb_t = bf([jnp.where(c["upinc"], x[:C2, C2:], 0.0) for x in sc])
    lak_t = bf([jnp.where(c["upper"], x[C2:, :C2], 0.0) for x in sc])
    ark_t = bf([jnp.where(c["upinc"], x[C2:, C2:], 0.0) for x in sc])

    eye = c["eye"]
    ud = [jnp.where(c["blk"], x, 0.0) for x in lab_t]
    nu = bf([x - y for x, y in zip(lab_t, ud)])
    udb = bf(ud)
    u2b = bf(each(_dot, udb, udb))
    u2 = f32(u2b)
    t1 = each(_dot, bf([eye + x for x in u2]), [cat1(x, (eye - y).astype(BF16)) for x, y in zip(u2b, ud)])
    u4b = bf([x[:, :C2] - y for x, y in zip(t1, u2)])
    u4 = f32(u4b)
    t2 = each(_dot, bf([eye + x for x in u4]), [cat1(x, y[:, C2:].astype(BF16)) for x, y in zip(u4b, t1)])
    u8 = [x[:, :C2] - y for x, y in zip(t2, u4)]
    dinv = each(_dot, bf([eye + x for x in u8]), bf([x[:, C2:] for x in t2]))
    dinvb = bf(dinv)
    wzb = bf(each(_dot, dinvb, nu))
    t3 = each(_dot, wzb, [cat1(x, y) for x, y in zip(wzb, dinvb)])
    tt_t = bf(each(_dot, bf([eye + x[:, :C2] for x in t3]), bf([y - x[:, C2:] for x, y in zip(t3, dinv)])))

    sb = bf(S)
    p2 = each(_dot, vs_t, [cat1(x, y) for x, y in zip(lak_t, ark_t)])
    p3 = each(_dot, vs_t, kh_s)
    p1 = each(nt, sb, kr_s)
    u_t = bf(each(_dot, bf([-(x[:, :C2] + y[:, :C2]) for x, y in zip(p1, p2)]), tt_t))
    p4 = each(_dot, u_t, [cat1(x, y) for x, y in zip(arb_t, bh_s)])
    s_new = [S[p] * jnp.exp(tot[p]) + p4[p][:, C2:] + p3[p] for p in P]
    y_t = [p1[p][:, C2:] + p4[p][:, :C2] + p2[p][:, C2:] for p in P]
    return y_t, s_new


def _wkv_group_norm(y_t, hblk):
    C = WKV_CHUNK
    mean = jnp.sum(y_t, axis=0, keepdims=True) * (1.0 / RWKV_HEAD)
    yc = jnp.where(hblk, y_t - mean, 0.0)
    var = jnp.sum(yc * yc, axis=0, keepdims=True) * (1.0 / RWKV_HEAD)
    ys = (yc * lax.rsqrt(var + GN_EPS)).T
    return ys[:C] + ys[C:]


def _wkv_kernel(r_ref, w_ref, k_ref, v_ref, kk_ref, b_ref, g_ref, prm_ref, o_ref, yt_ref, *s_refs, lb):
    C = WKV_CHUNK

    @pl.when(pl.program_id(2) == 0)
    def _():
        for s_ref in s_refs:
            s_ref[...] = jnp.zeros_like(s_ref)

    ri = lax.broadcasted_iota(jnp.int32, (2 * C, 2 * C), 0)
    ci = lax.broadcasted_iota(jnp.int32, (2 * C, 2 * C), 1)
    lane = lax.broadcasted_iota(jnp.int32, (1, LANES), 1)
    ti = lax.broadcasted_iota(jnp.int32, (C, C), 0)
    tj = lax.broadcasted_iota(jnp.int32, (C, C), 1)
    consts = dict(
        m0=lane < RWKV_HEAD,
        upper=ri < ci,
        upinc=ri <= ci,
        blk=(ri // WKV_SUB) == (ci // WKV_SUB),
        hblk=(ri // RWKV_HEAD) == (ci // C),
        eye=(ri == ci).astype(F32),
    )
    m0 = consts["m0"]
    tril = (ti >= tj).astype(BF16)

    def head_sum(x):
        s0 = jnp.sum(jnp.where(m0, x, 0.0), axis=-1, keepdims=True)
        s1 = jnp.sum(jnp.where(m0, 0.0, x), axis=-1, keepdims=True)
        return jnp.where(m0, s0, s1)

    nbat = r_ref.shape[0]
    items = [(bi, slice(p * LANES, (p + 1) * LANES)) for bi in range(nbat) for p in range(WKV_PAIRS)]

    def recurrence(ci_):
        rows = pl.ds(pl.multiple_of(ci_ * C, C), C)
        ld = lambda ref: [ref[bi, rows, ln] for bi, ln in items]
        r, w, k, v, kk, b = ld(r_ref), ld(w_ref), ld(k_ref), ld(v_ref), ld(kk_ref), ld(b_ref)
        cum_all = [_dot_hilo_lhs(tril, w_ref[bi, rows, :]) for bi in range(nbat)]
        cum = [cum_all[bi][:, ln] for bi, ln in items]
        ys, s_new = _wkv_chunk(r, w, cum, k, v, kk, b, [s[...] for s in s_refs], consts)
        for i, (bi, ln) in enumerate(items):
            s_refs[i][...] = s_new[i]
            yt_ref[bi, :, ln] = ys[i]

    def finish(ci_):
        rows = pl.ds(pl.multiple_of(ci_ * C, C), C)
        for bi, ln in items:
            r_k, lg, lb_ = prm_ref[0:1, ln], prm_ref[1:2, ln], prm_ref[2:3, ln]
            yn = _wkv_group_norm(yt_ref[bi, :, ln], consts["hblk"])
            bonus = head_sum(r_ref[bi, rows, ln] * k_ref[bi, rows, ln] * r_k) * v_ref[bi, rows, ln]
            o_ref[bi, rows, ln] = ((yn * lg + lb_ + bonus) * g_ref[bi, rows, ln]).astype(o_ref.dtype)

    def chunk(ci_, carry):
        finish(ci_ - 1)
        recurrence(ci_)
        return carry

    recurrence(0)
    lax.fori_loop(1, lb // C, chunk, 0)
    finish(lb // C - 1)


def _wkv_row_block(lp, nbat):
    nch = lp // WKV_CHUNK
    for d in ((11, 8, 6, 4, 3, 2, 1) if nbat == 1 else (4, 3, 2, 1)):
        if nch % d == 0:
            return d * WKV_CHUNK
    return WKV_CHUNK


def _wkv(r, w, k, v, kk, b, g, prm):
    B, lp, D = r.shape
    nbat = WKV_BATCH if B % WKV_BATCH == 0 else 1
    lb = _wkv_row_block(lp, nbat)
    wl = WKV_PAIRS * LANES
    spec = pl.BlockSpec((nbat, lb, wl), lambda bi, pi, li: (bi, li, pi))
    return pl.pallas_call(
        functools.partial(_wkv_kernel, lb=lb),
        grid=(B // nbat, D // wl, lp // lb),
        in_specs=[spec] * 7 + [pl.BlockSpec((SUBLANES, wl), lambda bi, pi, li: (0, pi))],
        out_specs=spec,
        out_shape=jax.ShapeDtypeStruct((B, lp, D), BF16),
        scratch_shapes=[pltpu.VMEM((nbat, LANES, wl), F32)]
        + [pltpu.VMEM((LANES, LANES), F32)] * (WKV_PAIRS * nbat),
        compiler_params=_params("parallel", "parallel", "arbitrary"),
        name="wkv7",
    )(r, w, k, v, kk, b, g, prm)


def _proj_ln_kernel(a_ref, w_ref, vec_ref, h_ref, o_ref):
    mix = _dot(a_ref[...], w_ref[...]) + vec_ref[0:1, :]
    o_ref[...] = _layer_norm(DEEPNORM_ALPHA * h_ref[...] + mix, vec_ref[1:2, :], vec_ref[2:3, :])


def _proj_rows(T):
    return 2 * TM if T % (2 * TM) == 0 else TM


def _proj_ln(a2d, w, vec, h2d):
    T, D = h2d.shape
    K = a2d.shape[1]
    tp = _proj_rows(T)
    return pl.pallas_call(
        _proj_ln_kernel,
        grid=(T // tp,),
        in_specs=[pl.BlockSpec((tp, K), lambda i: (i, 0)), _full(w.shape), _full(vec.shape),
                  pl.BlockSpec((tp, D), lambda i: (i, 0))],
        out_specs=pl.BlockSpec((tp, D), lambda i: (i, 0)),
        out_shape=jax.ShapeDtypeStruct((T, D), F32),
        compiler_params=_params("parallel"),
        name="proj_ln",
    )(a2d, w, vec, h2d)


def _qkv_kernel(x_ref, wq_ref, wk_ref, wv_ref, bq_ref, bk_ref, bv_ref, q_out, k_out, v_out):
    x = x_ref[...].astype(BF16)
    q_out[...] = ((_dot(x, wq_ref[...]) + bq_ref[...]) * ATT_HEAD_DIM ** -0.5).astype(BF16)
    k_out[...] = (_dot(x, wk_ref[...]) + bk_ref[...]).astype(BF16)
    v_out[...] = (_dot(x, wv_ref[...]) + bv_ref[...]).astype(BF16)


def _qkv(h2d, wq, wkd, wvd, bq, bkd, bvd):
    T, D = h2d.shape
    nq, nk = wq.shape[1], wkd.shape[1]
    ws = [wq, wkd, wvd, bq, bkd, bvd]
    tp = _proj_rows(T)
    return pl.pallas_call(
        _qkv_kernel,
        grid=(T // tp,),
        in_specs=[pl.BlockSpec((tp, D), lambda i: (i, 0))] + [_full(w.shape) for w in ws],
        out_specs=[pl.BlockSpec((tp, nq), lambda i: (i, 0)), pl.BlockSpec((tp, nk), lambda i: (i, 0)),
                   pl.BlockSpec((tp, nk), lambda i: (i, 0))],
        out_shape=[jax.ShapeDtypeStruct((T, nq), BF16), jax.ShapeDtypeStruct((T, nk), BF16),
                   jax.ShapeDtypeStruct((T, nk), BF16)],
        compiler_params=_params("parallel"),
        name="qkv_proj",
    )(h2d, *ws)


def _attn_kernel(sink_ref, q_ref, k_ref, v_ref, bband_ref, bmeta_ref, bmm_ref, o_ref, *, nb):
    c = pl.program_id(0)
    BLK = ATT_BLOCK
    M0 = FRONT
    lane = lax.broadcasted_iota(jnp.int32, (1, LANES), 1)
    m0 = lane < ATT_HEAD_DIM
    qi = lax.broadcasted_iota(jnp.int32, (2 * BLK, BLK), 0) % BLK
    sj = lax.broadcasted_iota(jnp.int32, (2 * BLK, BLK), 1)
    cur_vis = sj <= qi
    prev_vis = sj > qi
    half = lax.broadcasted_iota(jnp.int32, (2 * BLK, 1), 0) < BLK

    def stack(x):
        z = jnp.zeros_like(x)
        return jnp.concatenate([jnp.where(m0, x, z), jnp.where(m0, z, x)], axis=0)

    HP = range(ATT_STEP_PAIRS)
    ppk = ATT_GROUP // 2
    kl = [slice((hp // ppk) * LANES, (hp // ppk + 1) * LANES) for hp in HP]
    ql = [slice(hp * LANES, (hp + 1) * LANES) for hp in HP]

    def rowmax(lgs, floor):
        by_width = {}
        for lg in lgs:
            by_width.setdefault(lg.shape[1], []).append(lg)
        tops = [jnp.max(functools.reduce(jnp.maximum, g), axis=-1, keepdims=True) for g in by_width.values()]
        return functools.reduce(jnp.maximum, tops, floor)

    def attend(segs, sinks, n):
        it = range(len(segs))
        mx = [rowmax([lg for lg, _ in segs[i]], sinks[i]) for i in it]
        ps = [[jnp.exp(lg - mx[i]).astype(BF16) for lg, _ in segs[i]] for i in it]
        one = jnp.ones((1, LANES), BF16)
        va = [[jnp.where(m0, vals, one) for _, vals in segs[i]] for i in it]
        vb = [[jnp.where(m0, one, vals) for _, vals in segs[i]] for i in it]
        oa = [sum(_dot(p[:n], v) for p, v in zip(ps[i], va[i])) for i in it]
        ob = [sum(_dot(p[n:], v) for p, v in zip(ps[i], vb[i])) for i in it]
        out = []
        for i in it:
            st = jnp.exp(sinks[i] - mx[i])
            den = pltpu.roll(jnp.where(m0, ob[i], oa[i]), ATT_HEAD_DIM, axis=1)
            out.append(jnp.where(m0, oa[i], ob[i]) / (den + jnp.where(m0, st[:n], st[n:])))
        return out

    o_ref[0:M0, :] = jnp.zeros((M0, o_ref.shape[1]), o_ref.dtype)
    s0 = [sink_ref[(c * ATT_STEP_PAIRS + hp) * 2] for hp in HP]
    s1 = [sink_ref[(c * ATT_STEP_PAIRS + hp) * 2 + 1] for hp in HP]
    k_meta = [k_ref[M0:M0 + N_META, kl[hp]] for hp in HP]
    v_meta = [v_ref[M0:M0 + N_META, kl[hp]] for hp in HP]

    first = lax.broadcasted_iota(jnp.int32, (2 * N_META, 1), 0) < N_META
    mi = lax.broadcasted_iota(jnp.int32, (2 * N_META, N_META), 0) % N_META
    mj = lax.broadcasted_iota(jnp.int32, (2 * N_META, N_META), 1)
    qm = [stack(q_ref[M0:M0 + N_META, ql[hp]]) for hp in HP]
    lg = [jnp.where(mj <= mi, _dot(qm[hp], k_meta[hp], _NT) + bmm_ref[hp], NEG) for hp in HP]
    om = attend([[(lg[hp], v_meta[hp])] for hp in HP], [jnp.where(first, s0[hp], s1[hp]) for hp in HP], N_META)
    for hp in HP:
        o_ref[M0:M0 + N_META, ql[hp]] = om[hp].astype(o_ref.dtype)

    sink_q = [jnp.where(half, s0[hp], s1[hp]) for hp in HP]

    unroll = 2 if nb % 2 == 0 else 1

    def blocks(jo, carry):
        items = [(jo * unroll + u, hp) for u in range(unroll) for hp in HP]
        start = [pl.multiple_of(ROW_ALIGN + j * BLK, ROW_ALIGN) for j, _ in items]
        pstart = [pl.multiple_of(jnp.maximum(s - BLK, 0), ROW_ALIGN) for s in start]
        it = range(len(items))
        qs = [stack(q_ref[pl.ds(start[i], BLK), ql[items[i][1]]]) for i in it]
        k_prev = [k_ref[pl.ds(pstart[i], BLK), kl[items[i][1]]] for i in it]
        k_cur = [k_ref[pl.ds(start[i], BLK), kl[items[i][1]]] for i in it]
        lg_meta = [_dot(qs[i], k_meta[items[i][1]], _NT) + bmeta_ref[items[i][1], items[i][0]] for i in it]
        lg_prev = [jnp.where(prev_vis & (items[i][0] > 0),
                             _dot(qs[i], k_prev[i], _NT) + bband_ref[items[i][1], :, 0:BLK], NEG) for i in it]
        lg_cur = [jnp.where(cur_vis, _dot(qs[i], k_cur[i], _NT) + bband_ref[items[i][1], :, BLK:2 * BLK], NEG)
                  for i in it]
        out = attend([[(lg_meta[i], v_meta[items[i][1]]),
                       (lg_prev[i], v_ref[pl.ds(pstart[i], BLK), kl[items[i][1]]]),
                       (lg_cur[i], v_ref[pl.ds(start[i], BLK), kl[items[i][1]]])] for i in it],
                     [sink_q[hp] for _, hp in items], BLK)
        for i in it:
            o_ref[pl.ds(start[i], BLK), ql[items[i][1]]] = out[i].astype(o_ref.dtype)
        return carry

    lax.fori_loop(0, nb // unroll, blocks, 0)


def _attention(q, kd, vd, sinks, bband, bmeta, bmm):
    B, lp, _ = q.shape
    nb = (lp - FRONT - N_META) // ATT_BLOCK
    assert WINDOW == ATT_BLOCK and ROW_ALIGN % BF16_ROWS == 0 and ATT_KV_HEADS % ATT_STEP_KV == 0
    np_, qw, kw = ATT_STEP_PAIRS, ATT_STEP_PAIRS * LANES, ATT_STEP_KV * LANES
    return pl.pallas_call(
        functools.partial(_attn_kernel, nb=nb),
        grid=(ATT_KV_HEADS // ATT_STEP_KV, B),
        in_specs=[pl.BlockSpec(memory_space=pltpu.SMEM),
                  pl.BlockSpec((None, lp, qw), lambda c, b: (b, 0, c)),
                  pl.BlockSpec((None, lp, kw), lambda c, b: (b, 0, c)),
                  pl.BlockSpec((None, lp, kw), lambda c, b: (b, 0, c)),
                  pl.BlockSpec((np_,) + bband.shape[1:], lambda c, b: (c, 0, 0)),
                  pl.BlockSpec((np_,) + bmeta.shape[1:], lambda c, b: (c, 0, 0, 0)),
                  pl.BlockSpec((np_,) + bmm.shape[1:], lambda c, b: (c, 0, 0))],
        out_specs=pl.BlockSpec((None, lp, qw), lambda c, b: (b, 0, c)),
        out_shape=jax.ShapeDtypeStruct((B, lp, D_MODEL), BF16),
        compiler_params=_params("parallel", "parallel"),
        name="swa_attention",
    )(sinks, q, kd, vd, bband, bmeta, bmm)


def _t5_bucket(dist):
    exact = N_BUCKETS // 2
    d = jnp.maximum(dist, 0)
    ratio = jnp.log(jnp.maximum(d, 1).astype(F32) / exact) / math.log(MAX_DISTANCE / exact)
    large = jnp.minimum(exact + (ratio * (N_BUCKETS - exact)).astype(jnp.int32), N_BUCKETS - 1)
    return jnp.where(d < exact, d, large)


def _bias_tables(rel_bias, nb):
    H, BLK = ATT_HEADS, ATT_BLOCK

    def lookup(dist):
        onehot = (_t5_bucket(dist)[..., None] == jnp.arange(N_BUCKETS)).astype(F32)
        return jnp.dot(onehot, rel_bias.astype(F32), precision=lax.Precision.HIGHEST)

    qi = jnp.arange(BLK)[:, None]
    band = lookup(qi + BLK - jnp.arange(2 * BLK)[None, :])
    band = jnp.moveaxis(band, -1, 0).reshape(H // 2, 2 * BLK, 2 * BLK)
    pos = jnp.arange(nb * BLK)[:, None]
    meta = lookup(N_META + pos - jnp.arange(N_META)[None, :])
    meta = meta.reshape(nb, BLK, N_META, H // 2, 2).transpose(3, 0, 4, 1, 2)
    meta = meta.reshape(H // 2, nb, 2 * BLK, N_META)
    pm = jnp.arange(N_META)
    mm = lookup(pm[:, None] - pm[None, :])
    mm = jnp.moveaxis(mm, -1, 0).reshape(H // 2, 2 * N_META, N_META)
    return band.astype(F32), meta.astype(F32), mm.astype(F32)


def _dispatch_geometry(a1):
    E = N_EXPERTS
    routed = (a1 > 0.0).astype(BF16)
    n_col = jnp.sum(routed.astype(F32), axis=1, keepdims=True)
    ch_col = jnp.floor((n_col + (MOE_CH - 1)) * (1.0 / MOE_CH))
    ei = lax.broadcasted_iota(jnp.int32, (E, E), 0)
    ej = lax.broadcasted_iota(jnp.int32, (E, E), 1)
    off_col = _dot((ei > ej).astype(BF16), jnp.broadcast_to(ch_col, (E, LANES)))[:, 0:1]
    n_row = _dot(jnp.ones((SUBLANES, TM), BF16), routed, _NT)
    ch_row = jnp.floor((n_row + (MOE_CH - 1)) * (1.0 / MOE_CH))
    off_row = _dot(ch_row, (ei < ej).astype(BF16))
    return ch_col, off_col, ch_row, off_row


def _select_experts(choice):
    E, G, EPG = N_EXPERTS, N_GROUPS, N_EXPERTS // N_GROUPS
    grp = choice.reshape(G, EPG, TM)
    sub = lax.broadcasted_iota(jnp.int32, (G, EPG, TM), 1)
    top1 = jnp.max(grp, axis=1, keepdims=True)
    first = jnp.min(jnp.where(grp == top1, sub, EPG), axis=1, keepdims=True)
    top2 = jnp.max(jnp.where(sub == first, -jnp.inf, grp), axis=1, keepdims=True)
    gscore = jnp.broadcast_to(top1 + top2, (G, EPG, TM))

    gi = lax.broadcasted_iota(jnp.int32, (G, EPG, TM), 0)
    keep = jnp.zeros((G, EPG, TM), jnp.bool_)
    for _ in range(TOPK_GROUPS):
        m = jnp.max(gscore, axis=0, keepdims=True)
        sel = gi == jnp.min(jnp.where(gscore == m, gi, G), axis=0, keepdims=True)
        keep = keep | sel
        gscore = jnp.where(sel, -jnp.inf, gscore)
    keep_e = keep.reshape(E, TM)

    cand = jnp.where(keep_e, choice, -jnp.inf)
    ei = lax.broadcasted_iota(jnp.int32, (E, TM), 0)
    routed = jnp.zeros((E, TM), jnp.bool_)
    for _ in range(TOP_K):
        m = jnp.max(cand, axis=0, keepdims=True)
        sel = ei == jnp.min(jnp.where(cand == m, ei, E), axis=0, keepdims=True)
        routed = routed | sel
        cand = jnp.where(sel, -jnp.inf, cand)
    return routed


def _route_kernel(x_ref, rt_ref, bias_ref, a1_out, a1t_out, gate_out, cnt_out, off_out, *, tiles):
    U = range(tiles)
    xs = [x_ref[u * TM:(u + 1) * TM, :] for u in U]
    scores = [_sigmoid(_dot3(rt_ref[...], x, _NT)) for x in xs]
    routed = [_select_experts(s + bias_ref[:, 0:1]) for s in scores]
    gate = [jnp.where(r, s, 0.0) for r, s in zip(routed, scores)]
    gate = [g / jnp.sum(g, axis=0, keepdims=True) * ROUTED_SCALE for g in gate]
    ti = lax.broadcasted_iota(jnp.int32, (TM, TM), 0)
    tj = lax.broadcasted_iota(jnp.int32, (TM, TM), 1)
    eye = (ti == tj).astype(BF16)
    routed_b = [r.astype(BF16) for r in routed]
    rank = [_dot(rb, (ti < tj).astype(BF16)) for rb in routed_b]
    rank_t = [_dot((tj < ti).astype(BF16), rb, _NT) for rb in routed_b]
    routed_t = [_dot(eye, rb, _NT) for rb in routed_b]
    gate_t = [_dot_exact_lhs(eye, g, _NT) for g in gate]
    for u in U:
        a1 = jnp.where(routed[u], rank[u] + 1.0, 0.0)
        _, _, ch_row, off_row = _dispatch_geometry(a1)
        a1_out[u] = a1
        a1t_out[u] = routed_t[u] * (rank_t[u] + 1.0)
        gate_out[u] = jnp.concatenate(_split2(gate_t[u]), axis=1)
        cnt_out[u] = ch_row
        off_out[u] = off_row


def _dispatch_kernel(x_ref, a1_ref, gate_ref, cnt_ref, off_ref, xs_out, gs_out):
    E = N_EXPERTS
    lo = jnp.concatenate([off_ref[0:1, :]] * 2, axis=1) * MOE_CH
    hi = lo + jnp.concatenate([cnt_ref[0:1, :]] * 2, axis=1) * MOE_CH
    first = lax.broadcasted_iota(jnp.int32, (1, 2 * E), 1) < E
    lo1 = jnp.where(first, lo, 0.0)
    xg = jnp.concatenate([x_ref[...].astype(BF16), gate_ref[...]], axis=1)
    a1b = a1_ref[...].astype(BF16)
    SB = MOE_SLOTS // MOE_SLOT_SPLITS
    for s0 in range(0, MOE_SLOTS, SB):
        si = (lax.broadcasted_iota(jnp.int32, (SB, 2 * E), 0) + s0).astype(F32)
        member = (si >= lo) & (si < hi)
        seg0 = jnp.sum(jnp.where(member, lo1, 0.0), axis=1, keepdims=True)
        rs1 = (lax.broadcasted_iota(jnp.int32, (SB, 1), 0) + (s0 + 1)).astype(F32) - seg0
        perm = _dot(member[:, :E], a1b) == rs1
        disp = _dot(perm, xg)
        xs_out[s0:s0 + SB, :] = disp[:, :D_MODEL].astype(BF16)
        gs_out[s0:s0 + SB, :] = jnp.sum(jnp.where(member, disp[:, D_MODEL:], 0.0), axis=1, keepdims=True)


def _router(h2d, router_t, bias_col):
    T, D = h2d.shape
    nt = T // TM
    E = N_EXPERTS
    tiles = next(t for t in (4, 3, 2, 1) if nt % t == 0)
    a1, a1t, gate, cnt, off = pl.pallas_call(
        functools.partial(_route_kernel, tiles=tiles),
        grid=(nt // tiles,),
        in_specs=[pl.BlockSpec((tiles * TM, D), lambda i: (i, 0)), _full(router_t.shape), _full(bias_col.shape)],
        out_specs=[pl.BlockSpec((tiles, E, TM), lambda i: (i, 0, 0)),
                   pl.BlockSpec((tiles, TM, E), lambda i: (i, 0, 0)),
                   pl.BlockSpec((tiles, TM, 2 * E), lambda i: (i, 0, 0)),
                   pl.BlockSpec((tiles, SUBLANES, E), lambda i: (i, 0, 0)),
                   pl.BlockSpec((tiles, SUBLANES, E), lambda i: (i, 0, 0))],
        out_shape=[jax.ShapeDtypeStruct((nt, E, TM), F32),
                   jax.ShapeDtypeStruct((nt, TM, E), F32),
                   jax.ShapeDtypeStruct((nt, TM, 2 * E), BF16),
                   jax.ShapeDtypeStruct((nt, SUBLANES, E), F32),
                   jax.ShapeDtypeStruct((nt, SUBLANES, E), F32)],
        compiler_params=_params("parallel"),
        name="moe_route",
    )(h2d, router_t, bias_col)
    xs, gs = pl.pallas_call(
        _dispatch_kernel,
        grid=(nt,),
        in_specs=[pl.BlockSpec((TM, D), lambda i: (i, 0)),
                  pl.BlockSpec((None, E, TM), lambda i: (i, 0, 0)),
                  pl.BlockSpec((None, TM, 2 * E), lambda i: (i, 0, 0)),
                  pl.BlockSpec((None, SUBLANES, E), lambda i: (i, 0, 0)),
                  pl.BlockSpec((None, SUBLANES, E), lambda i: (i, 0, 0))],
        out_specs=[pl.BlockSpec((None, MOE_SLOTS, D), lambda i: (i, 0, 0)),
                   pl.BlockSpec((None, MOE_SLOTS, 1), lambda i: (i, 0, 0))],
        out_shape=[jax.ShapeDtypeStruct((nt, MOE_SLOTS, D), BF16),
                   jax.ShapeDtypeStruct((nt, MOE_SLOTS, 1), F32)],
        compiler_params=_params("parallel"),
        name="moe_dispatch",
    )(h2d, a1, gate, cnt, off)
    return xs, gs, a1, a1t, cnt


def _expert_kernel(be_ref, ps_ref, nb_ref, src_ref, xs_hbm, wg_ref, wu_ref, wd_ref, ys_hbm,
                   xbuf, ybuf, wgu_bf, wd_bf, in_sem, out_sem, *, nblk):
    b = pl.program_id(0)
    slot = lax.rem(b, 2)

    def in_copy(src_chunk, sl, c):
        return pltpu.make_async_copy(xs_hbm.at[src_chunk], xbuf.at[sl, c], in_sem.at[sl])

    def out_copy(dst_chunk, sl, c):
        return pltpu.make_async_copy(ybuf.at[sl, c], ys_hbm.at[dst_chunk], out_sem.at[sl])

    def for_chunks(blk, fn):
        p0 = ps_ref[blk]
        n = nb_ref[blk]
        for c0 in range(0, MOE_CB, MOE_SEG):

            @pl.when(n >= c0 + MOE_SEG)
            def _():
                for c in range(c0, c0 + MOE_SEG):
                    fn(src_ref[p0 + c], c)

            @pl.when((n > c0) & (n < c0 + MOE_SEG))
            def _():
                def body(c, carry):
                    fn(src_ref[p0 + c], c)
                    return carry

                lax.fori_loop(c0, n, body, 0)

    @pl.when(b == 0)
    def _():
        xbuf[...] = jnp.zeros_like(xbuf)
        for_chunks(0, lambda s, c: in_copy(s, 0, c).start())

    @pl.when(b + 1 < nblk)
    def _():
        for_chunks(b + 1, lambda s, c: in_copy(s, 1 - slot, c).start())

    for_chunks(b, lambda s, c: in_copy(s, slot, c).wait())

    @pl.when(b >= 2)
    def _():
        for_chunks(b - 2, lambda s, c: out_copy(s, slot, c).wait())

    @pl.when((b == 0) | (be_ref[b] != be_ref[jnp.maximum(b - 1, 0)]))
    def _():
        wgu_bf[:, :EXPERT_FF] = wg_ref[...].astype(BF16)
        wgu_bf[:, EXPERT_FF:] = wu_ref[...].astype(BF16)
        wd_bf[...] = wd_ref[...].astype(BF16)

    for c0 in range(0, MOE_CB, MOE_SEG):

        @pl.when(nb_ref[b] > c0)
        def _():
            x = xbuf[slot, c0:c0 + MOE_SEG].reshape(MOE_SEG * MOE_CH, D_MODEL)
            hid = _dot(x, wgu_bf[...])
            act = _silu(hid[:, :EXPERT_FF]) * hid[:, EXPERT_FF:]
            y = _dot(act, wd_bf[...])
            ybuf[slot, c0:c0 + MOE_SEG] = y.astype(BF16).reshape(MOE_SEG, MOE_CH, D_MODEL)

    for_chunks(b, lambda s, c: out_copy(s, slot, c).start())

    @pl.when(b == nblk - 1)
    def _():
        for_chunks(b, lambda s, c: out_copy(s, slot, c).wait())

        @pl.when(b >= 1)
        def _():
            for_chunks(b - 1, lambda s, c: out_copy(s, 1 - slot, c).wait())


def _experts(xs, wg, wu, wd, layer, be, pstart, nbc, src):
    nt = xs.shape[0]
    nblk = be.shape[0]
    xs_c = xs.reshape(nt * MOE_CPT, MOE_CH, D_MODEL)
    by_expert = lambda b, be, ps, nb, src: (layer, be[b], 0, 0)
    grid_spec = pltpu.PrefetchScalarGridSpec(
        num_scalar_prefetch=4,
        grid=(nblk,),
        in_specs=[pl.BlockSpec(memory_space=pl.ANY),
                  pl.BlockSpec((None, None, D_MODEL, EXPERT_FF), by_expert),
                  pl.BlockSpec((None, None, D_MODEL, EXPERT_FF), by_expert),
                  pl.BlockSpec((None, None, EXPERT_FF, D_MODEL), by_expert)],
        out_specs=pl.BlockSpec(memory_space=pl.ANY),
        scratch_shapes=[pltpu.VMEM((2, MOE_CB, MOE_CH, D_MODEL), BF16),
                        pltpu.VMEM((2, MOE_CB, MOE_CH, D_MODEL), BF16),
                        pltpu.VMEM((D_MODEL, 2 * EXPERT_FF), BF16),
                        pltpu.VMEM((EXPERT_FF, D_MODEL), BF16),
                        pltpu.SemaphoreType.DMA((2,)), pltpu.SemaphoreType.DMA((2,))],
    )
    ys = pl.pallas_call(
        functools.partial(_expert_kernel, nblk=nblk),
        grid_spec=grid_spec,
        out_shape=jax.ShapeDtypeStruct(xs_c.shape, BF16),
        input_output_aliases={4: 0},
        compiler_params=_params("arbitrary"),
        name="moe_experts",
    )(be, pstart, nbc, src, xs_c, wg, wu, wd)
    return ys.reshape(nt, MOE_SLOTS, D_MODEL)


def _combine_kernel(h_ref, ys_ref, gs_ref, a1_ref, a1t_ref, wgu_ref, wd_ref, vec_ref, o_ref):
    E = N_EXPERTS
    h = h_ref[...]
    ch_col, off_col, _, _ = _dispatch_geometry(a1_ref[...])
    si = lax.broadcasted_iota(jnp.int32, (E, MOE_SLOTS), 1).astype(F32)
    lo = off_col * MOE_CH
    member_t = (si >= lo) & (si < lo + ch_col * MOE_CH)
    seg0 = jnp.sum(jnp.where(member_t, lo, 0.0), axis=0, keepdims=True)
    rs1 = lax.broadcasted_iota(jnp.int32, (1, MOE_SLOTS), 1).astype(F32) - seg0 + 1.0
    perm_t = _dot(a1t_ref[...], member_t) == rs1
    routed = _dot(perm_t.astype(BF16), ys_ref[...].astype(F32) * gs_ref[...])
    xb = h.astype(BF16)
    hid = _dot(xb, wgu_ref[...])
    shared = _dot(_silu(hid[:, :EXPERT_FF]) * hid[:, EXPERT_FF:], wd_ref[...])
    o_ref[...] = _layer_norm(DEEPNORM_ALPHA * h + (shared + routed), vec_ref[0:1, :], vec_ref[1:2, :])


def _combine(h2d, ys, gs, a1, a1t, wgu_s, wd_s, vec):
    T, D = h2d.shape
    nt = T // TM
    return pl.pallas_call(
        _combine_kernel,
        grid=(nt,),
        in_specs=[pl.BlockSpec((TM, D), lambda i: (i, 0)),
                  pl.BlockSpec((None, MOE_SLOTS, D), lambda i: (i, 0, 0)),
                  pl.BlockSpec((None, MOE_SLOTS, 1), lambda i: (i, 0, 0)),
                  pl.BlockSpec((None, N_EXPERTS, TM), lambda i: (i, 0, 0)),
                  pl.BlockSpec((None, TM, N_EXPERTS), lambda i: (i, 0, 0)),
                  _full(wgu_s.shape), _full(wd_s.shape), _full(vec.shape)],
        out_specs=pl.BlockSpec((TM, D), lambda i: (i, 0)),
        out_shape=jax.ShapeDtypeStruct((T, D), F32),
        compiler_params=_params("parallel"),
        name="moe_combine",
    )(h2d, ys, gs, a1, a1t, wgu_s, wd_s, vec)


def _dispatch_tables(chunks):
    nt, E = chunks.shape
    maxc = nt * MOE_CPT
    nblk = -(-maxc // MOE_CB) + E
    hp = lax.Precision.HIGHEST
    ch = chunks.astype(F32)
    off = jnp.cumsum(ch, axis=1) - ch
    wend = jnp.cumsum(ch, axis=0)
    cnt_e = wend[-1]
    cend = jnp.cumsum(cnt_e)
    base_e = cend - cnt_e
    srcbase = jnp.arange(nt, dtype=F32)[:, None] * MOE_CPT + off - (wend - ch)
    p = jnp.arange(maxc, dtype=F32)
    e_p = jnp.minimum(jnp.sum((cend[None, :] <= p[:, None]).astype(jnp.int32), axis=1), E - 1)
    oh_e = (e_p[:, None] == jnp.arange(E, dtype=jnp.int32)[None, :]).astype(F32)
    look = jnp.dot(oh_e, jnp.concatenate([wend.T, srcbase.T, base_e[:, None]], axis=1), precision=hp)
    q = p - look[:, 2 * nt]
    i_p = jnp.sum((look[:, :nt] <= q[:, None]).astype(jnp.int32), axis=1)
    oh_i = i_p[:, None] == jnp.arange(nt, dtype=jnp.int32)[None, :]
    src = jnp.sum(jnp.where(oh_i, look[:, nt:2 * nt], 0.0), axis=1) + q
    src = jnp.clip(src, 0, maxc - 1).astype(jnp.int32)
    nblk_e = jnp.floor((cnt_e + (MOE_CB - 1)) * (1.0 / MOE_CB))
    bend = jnp.cumsum(nblk_e)
    bidx = jnp.arange(nblk, dtype=F32)
    be = jnp.minimum(jnp.sum((bend[None, :] <= bidx[:, None]).astype(jnp.int32), axis=1), E - 1)
    oh_b = (be[:, None] == jnp.arange(E, dtype=jnp.int32)[None, :]).astype(F32)
    lookb = jnp.dot(oh_b, jnp.stack([bend - nblk_e, cnt_e, base_e], axis=1), precision=hp)
    q0 = (bidx - lookb[:, 0]) * MOE_CB
    nbc = jnp.clip(lookb[:, 1] - q0, 0, MOE_CB).astype(jnp.int32)
    pstart = jnp.clip(lookb[:, 2] + q0, 0, maxc - MOE_CB).astype(jnp.int32)
    return be.astype(jnp.int32), pstart, nbc, src


def _moe_layer(h2d, layer, router, bias, w_gate, w_up, w_down, sh_gate, sh_up, sh_down, ln_g, ln_b):
    router_t = router.T.astype(F32)
    bias_col = jnp.broadcast_to(bias.astype(F32)[:, None], (N_EXPERTS, LANES))
    xs, gs, a1, a1t, cnt = _router(h2d, router_t, bias_col)
    be, pstart, nbc, src = _dispatch_tables(cnt[:, 0, :].astype(jnp.int32))
    ys = _experts(xs, w_gate, w_up, w_down, layer, be, pstart, nbc, src)
    wgu_s = jnp.concatenate([sh_gate, sh_up], axis=-1).astype(BF16)
    vec = jnp.zeros((SUBLANES, D_MODEL), F32).at[0].set(ln_g).at[1].set(ln_b)
    return _combine(h2d, ys, gs, a1, a1t, wgu_s, sh_down.astype(BF16), vec)


def _rows(*vs):
    out = jnp.zeros((SUBLANES, vs[0].shape[-1]), F32)
    for i, v in enumerate(vs):
        out = out.at[i].set(v.astype(F32))
    return out


def kernel(x, meta, rel_bias, rwkv_mu, rwkv_w0, rwkv_w1, rwkv_w2, rwkv_a0, rwkv_a1, rwkv_a2, rwkv_g1, rwkv_g2, rwkv_k_k, rwkv_k_a, rwkv_r_k, rwkv_w_r, rwkv_w_k, rwkv_w_v, rwkv_w_o, rwkv_lnx_g, rwkv_lnx_b, attn_w_qkv, attn_b_qkv, attn_sinks, attn_w_o, attn_b_o, ln_mix_g, ln_mix_b, ln_ffn_g, ln_ffn_b, moe_router, moe_bias, moe_w_gate, moe_w_up, moe_w_down, shared_w_gate, shared_w_up, shared_w_down):
    B, S, D = x.shape
    assert D == D_MODEL and S % ATT_BLOCK == 0 and (FRONT + N_META) % WKV_CHUNK == 0
    lp = FRONT + N_META + S
    T = B * lp
    assert T % TM == 0 and lp >= TM
    h = jnp.concatenate([jnp.zeros((B, FRONT, D), x.dtype),
                         jnp.broadcast_to(meta[None].astype(x.dtype), (B, N_META, D)), x], axis=1)
    h = h.reshape(T, D)
    bf = lambda w: w.astype(BF16)

    H, N = RWKV_HEADS, RWKV_HEAD
    head_of = jnp.arange(D) // N
    gsum = (head_of[:, None] == jnp.arange(LANES)[None, :]).astype(BF16)
    gexp = gsum.T
    r, w, k, v, kk, b, g = _rwkv_proj(
        h, lp, _rows(*rwkv_mu[0]), _rows(rwkv_w0[0], rwkv_a0[0], rwkv_k_k[0], rwkv_k_a[0]),
        bf(rwkv_w_r[0]), bf(rwkv_w_k[0]), bf(rwkv_w_v[0]), bf(rwkv_w1[0]), bf(rwkv_w2[0]),
        bf(rwkv_a1[0]), bf(rwkv_a2[0]), bf(rwkv_g1[0]), bf(rwkv_g2[0]), gsum, gexp)
    to3 = lambda t: t.reshape(B, lp, D)
    prm = _rows(rwkv_r_k[0].reshape(D), rwkv_lnx_g[0], rwkv_lnx_b[0])
    o = _wkv(to3(r), to3(w), to3(k), to3(v), to3(kk), to3(b), to3(g), prm)
    h = _proj_ln(o.reshape(T, D), bf(rwkv_w_o[0]), _rows(jnp.zeros((D,), F32), ln_mix_g[0], ln_mix_b[0]), h)
    h = _moe_layer(h, 0, moe_router[0], moe_bias[0], moe_w_gate, moe_w_up, moe_w_down,
                   shared_w_gate[0], shared_w_up[0], shared_w_down[0], ln_ffn_g[0], ln_ffn_b[0])

    HD, KV = ATT_HEAD_DIM, ATT_KV_HEADS
    qw = ATT_HEADS * HD
    wqkv, bqkv = attn_w_qkv[0], attn_b_qkv[0]
    dup = lambda t: jnp.concatenate([t.reshape(-1, KV, 1, HD)] * 2, axis=2).reshape(t.shape[0], 2 * KV * HD)
    wq, wkd, wvd = wqkv[:, :qw], dup(wqkv[:, qw:qw + KV * HD]), dup(wqkv[:, qw + KV * HD:])
    bq, bkd, bvd = bqkv[None, :qw], dup(bqkv[None, qw:qw + KV * HD]), dup(bqkv[None, qw + KV * HD:])
    q, kd, vd = _qkv(h, bf(wq), bf(wkd), bf(wvd), bq.astype(F32), bkd.astype(F32), bvd.astype(F32))
    bband, bmeta, bmm = _bias_tables(rel_bias, S // ATT_BLOCK)
    o = _attention(q.reshape(B, lp, qw), kd.reshape(B, lp, -1), vd.reshape(B, lp, -1),
                   attn_sinks[0].astype(F32), bband, bmeta, bmm)
    h = _proj_ln(o.reshape(T, D), bf(attn_w_o[0]), _rows(attn_b_o[0], ln_mix_g[1], ln_mix_b[1]), h)
    h = _moe_layer(h, 1, moe_router[1], moe_bias[1], moe_w_gate, moe_w_up, moe_w_down,
                   shared_w_gate[1], shared_w_up[1], shared_w_down[1], ln_ffn_g[1], ln_ffn_b[1])
    return h.reshape(B, lp, D)[:, FRONT + N_META:]
```

```python
import functools
import math

import jax
import jax.numpy as jnp
from jax import lax
from jax.experimental import pallas as pl
from jax.experimental.pallas import tpu as pltpu

F32 = jnp.float32
BF16 = jnp.bfloat16

D_MODEL = 1024
DEPTH = 2
N_META = 16
RWKV_HEAD = 64
RWKV_HEADS = D_MODEL // RWKV_HEAD
GN_EPS = 64e-5
ATT_HEADS = 16
ATT_KV_HEADS = 4
ATT_HEAD_DIM = D_MODEL // ATT_HEADS
ATT_GROUP = ATT_HEADS // ATT_KV_HEADS
WINDOW = 128
ATT_BLOCK = 128
N_BUCKETS = 32
MAX_DISTANCE = 128
N_EXPERTS = 64
TOP_K = 8
N_GROUPS = 8
TOPK_GROUPS = 4
EXPERT_FF = 256
ROUTED_SCALE = 2.5
DEEPNORM_ALPHA = (2 * DEPTH) ** 0.25
LN_EPS = 1e-5

LANES = 128
SUBLANES = 8
BF16_ROWS = 16
VMEM_LIMIT = 56 * 1024 * 1024

FRONT = 48
TM = 256
WKV_CHUNK = 64
WKV_PAIRS = 8
WKV_BATCH = 2
WKV_SUB = 16
ATT_STEP_KV = 2
ATT_STEP_PAIRS = ATT_STEP_KV * ATT_GROUP // 2
ROW_ALIGN = FRONT + N_META
MOE_CH = BF16_ROWS
MOE_SLOTS = TM * TOP_K + N_EXPERTS * MOE_CH
MOE_CPT = MOE_SLOTS // MOE_CH
MOE_NL = D_MODEL // LANES
MOE_CB = 128
MOE_SEG = 32
MOE_SLOT_SPLITS = 3
NEG = -1e30

_NN = (((1,), (0,)), ((), ()))
_NT = (((1,), (1,)), ((), ()))


def _dot(a, b, dn=_NN):
    return lax.dot_general(a.astype(BF16), b.astype(BF16), dn, preferred_element_type=F32)


def _split2(x):
    hi = x.astype(BF16)
    lo = (x - hi.astype(F32)).astype(BF16)
    return hi, lo


def _split3(x):
    h1 = x.astype(BF16)
    r1 = x - h1.astype(F32)
    h2 = r1.astype(BF16)
    h3 = (r1 - h2.astype(F32)).astype(BF16)
    return h1, h2, h3


def _dot_exact_lhs(a01, b, dn=_NN):
    a = a01.astype(BF16)
    return sum(lax.dot_general(a, p, dn, preferred_element_type=F32) for p in _split3(b))


def _dot_hilo_lhs(a01, b, dn=_NN):
    a = a01.astype(BF16)
    return sum(lax.dot_general(a, p, dn, preferred_element_type=F32) for p in _split2(b))


def _dot_hilo_rhs(a, b01, dn=_NN):
    b = b01.astype(BF16)
    return sum(lax.dot_general(p, b, dn, preferred_element_type=F32) for p in _split2(a))


def _dot3(a, b, dn=_NN):
    ah, al = _split2(a)
    bh, bl = _split2(b)
    d = lambda x, y: lax.dot_general(x, y, dn, preferred_element_type=F32)
    return d(ah, bh) + (d(ah, bl) + d(al, bh))


def _sigmoid(x):
    return 1.0 / (1.0 + jnp.exp(-x))


def _silu(x):
    return x * _sigmoid(x)


def _layer_norm(x, g, b):
    mu = jnp.mean(x, axis=-1, keepdims=True)
    xc = x - mu
    var = jnp.mean(xc * xc, axis=-1, keepdims=True)
    return xc * lax.rsqrt(var + LN_EPS) * g + b


def _full(shape):
    nd = len(shape)
    return pl.BlockSpec(shape, lambda *_: (0,) * nd)


def _params(*sem):
    return pltpu.CompilerParams(dimension_semantics=sem, vmem_limit_bytes=VMEM_LIMIT)


def _rwkv_proj_kernel(x_ref, xp_ref, mu_ref, vec_ref, wr_ref, wk_ref, wv_ref, w1_ref, w2_ref,
                      a1_ref, a2_ref, g1_ref, g2_ref, gsum_ref, gexp_ref,
                      r_out, w_out, k_out, v_out, kk_out, b_out, g_out, *, lp):
    i = pl.program_id(0)
    x = x_ref[...]
    rows = x.shape[0]
    row = lax.broadcasted_iota(jnp.int32, (rows, 1), 0)
    pos = lax.rem(i * rows, lp) + row
    pos = jnp.where(pos >= lp, pos - lp, pos)
    prev = jnp.where(row == 0, xp_ref[SUBLANES - 1:SUBLANES, :], pltpu.roll(x, 1, axis=0))
    prev = jnp.where(pos == FRONT, 0.0, prev)
    xx = prev - x
    valid = pos >= FRONT

    def mix(j):
        return (x + xx * mu_ref[j:j + 1, :]).astype(BF16)

    w0, a0, k_k, k_a = (vec_ref[j:j + 1, :] for j in range(4))
    r = _dot(mix(0), wr_ref[...])
    z = w0 + _dot(jnp.tanh(_dot(mix(1), w1_ref[...])), w2_ref[...])
    w = -math.exp(-0.5) * _sigmoid(z)
    k = _dot(mix(2), wk_ref[...])
    v = _dot(mix(3), wv_ref[...])
    a = _sigmoid(a0 + _dot(_dot(mix(4), a1_ref[...]), a2_ref[...]))
    g = _dot(_sigmoid(_dot(mix(5), g1_ref[...])), g2_ref[...])
    kk = k * k_k
    ssq = _dot_hilo_rhs(kk * kk, gsum_ref[...])
    kk = kk * lax.rsqrt(jnp.maximum(_dot_hilo_rhs(ssq, gexp_ref[...]), 1e-24))
    k = k * (1.0 + (a - 1.0) * k_a)
    r_out[...] = r
    w_out[...] = jnp.where(valid, w, 0.0)
    k_out[...] = jnp.where(valid, k, 0.0)
    v_out[...] = jnp.where(valid, v, 0.0)
    kk_out[...] = jnp.where(valid, kk, 0.0)
    b_out[...] = jnp.where(valid, kk * a, 0.0)
    g_out[...] = g


def _rwkv_proj(h2d, lp, mu, vec, wr, wk, wv, w1, w2, a1, a2, g1, g2, gsum, gexp):
    T, D = h2d.shape
    tp = _proj_rows(T)
    assert tp <= lp
    row_spec = pl.BlockSpec((tp, D), lambda i: (i, 0))
    prev_spec = pl.BlockSpec((SUBLANES, D), lambda i: (jnp.maximum(i * (tp // SUBLANES) - 1, 0), 0))
    ws = [mu, vec, wr, wk, wv, w1, w2, a1, a2, g1, g2, gsum, gexp]
    once = lambda w: pl.BlockSpec(w.shape, lambda i: (0,) * w.ndim, pipeline_mode=pl.Buffered(1))
    return pl.pallas_call(
        functools.partial(_rwkv_proj_kernel, lp=lp),
        grid=(T // tp,),
        in_specs=[row_spec, prev_spec] + [once(w) for w in ws],
        out_specs=[row_spec] * 7,
        out_shape=[jax.ShapeDtypeStruct((T, D), F32)] * 7,
        compiler_params=_params("parallel"),
        name="rwkv_proj",
    )(h2d, h2d, *ws)


def _wkv_chunk(r, w, cum, k, v, kk, b, S, c):
    C = WKV_CHUNK
    P = range(len(r))
    bf = lambda xs: [x.astype(BF16) for x in xs]
    each = lambda f, *ls: [f(*a) for a in zip(*ls)]

    def stack(x):
        return jnp.concatenate([jnp.where(c["m0"], x, 0.0), jnp.where(c["m0"], 0.0, x)], axis=0)

    f32 = lambda xs: [x.astype(F32) for x in xs]
    cat0 = lambda x, y: jnp.concatenate([x, y], axis=0)
    cat1 = lambda x, y: jnp.concatenate([x, y], axis=1)
    nt = lambda x, y: _dot(x, y, _NT)
    C2 = 2 * C

    tot = [cum[p][C - 1:C, :] for p in P]
    inv = [jnp.exp(-cum[p]) for p in P]
    dend = [jnp.exp(tot[p] - cum[p]) for p in P]
    kr_s = bf([cat0(stack(kk[p] * jnp.exp(cum[p] - w[p])), stack(r[p] * jnp.exp(cum[p]))) for p in P])
    bk_s = bf([cat0(stack(b[p] * inv[p]), stack(k[p] * inv[p])) for p in P])
    bh_s = bf([stack(b[p] * dend[p]) for p in P])
    kh_s = bf([stack(k[p] * dend[p]) for p in P])
    vs_t = bf([stack(v[p]).T for p in P])

    sc = each(nt, bk_s, kr_s)
    lab_t = [jnp.where(c["upper"], x[:C2, :C2], 0.0) for x in sc]
    arb_t = bf([jnp.where(c["upinc"], x[:C2, C2:], 0.0) for x in sc])
    lak_t = bf([jnp.where(c["upper"], x[C2:, :C2], 0.0) for x in sc])
    ark_t = bf([jnp.where(c["upinc"], x[C2:, C2:], 0.0) for x in sc])

    eye = c["eye"]
    ud = [jnp.where(c["blk"], x, 0.0) for x in lab_t]
    nu = bf([x - y for x, y in zip(lab_t, ud)])
    udb = bf(ud)
    u2b = bf(each(_dot, udb, udb))
    u2 = f32(u2b)
    t1 = each(_dot, bf([eye + x for x in u2]), [cat1(x, (eye - y).astype(BF16)) for x, y in zip(u2b, ud)])
    u4b = bf([x[:, :C2] - y for x, y in zip(t1, u2)])
    u4 = f32(u4b)
    t2 = each(_dot, bf([eye + x for x in u4]), [cat1(x, y[:, C2:].astype(BF16)) for x, y in zip(u4b, t1)])
    u8 = [x[:, :C2] - y for x, y in zip(t2, u4)]
    dinv = each(_dot, bf([eye + x for x in u8]), bf([x[:, C2:] for x in t2]))
    dinvb = bf(dinv)
    wzb = bf(each(_dot, dinvb, nu))
    t3 = each(_dot, wzb, [cat1(x, y) for x, y in zip(wzb, dinvb)])
    tt_t = bf(each(_dot, bf([eye + x[:, :C2] for x in t3]), bf([y - x[:, C2:] for x, y in zip(t3, dinv)])))

    sb = bf(S)
    p2 = each(_dot, vs_t, [cat1(x, y) for x, y in zip(lak_t, ark_t)])
    p3 = each(_dot, vs_t, kh_s)
    p1 = each(nt, sb, kr_s)
    u_t = bf(each(_dot, bf([-(x[:, :C2] + y[:, :C2]) for x, y in zip(p1, p2)]), tt_t))
    p4 = each(_dot, u_t, [cat1(x, y) for x, y in zip(arb_t, bh_s)])
    s_new = [S[p] * jnp.exp(tot[p]) + p4[p][:, C2:] + p3[p] for p in P]
    y_t = [p1[p][:, C2:] + p4[p][:, :C2] + p2[p][:, C2:] for p in P]
    return y_t, s_new


def _wkv_group_norm(y_t, hblk):
    C = WKV_CHUNK
    mean = jnp.sum(y_t, axis=0, keepdims=True) * (1.0 / RWKV_HEAD)
    yc = jnp.where(hblk, y_t - mean, 0.0)
    var = jnp.sum(yc * yc, axis=0, keepdims=True) * (1.0 / RWKV_HEAD)
    ys = (yc * lax.rsqrt(var + GN_EPS)).T
    return ys[:C] + ys[C:]


def _wkv_kernel(r_ref, w_ref, k_ref, v_ref, kk_ref, b_ref, g_ref, prm_ref, o_ref, yt_ref, *s_refs, lb):
    C = WKV_CHUNK

    @pl.when(pl.program_id(2) == 0)
    def _():
        for s_ref in s_refs:
            s_ref[...] = jnp.zeros_like(s_ref)

    ri = lax.broadcasted_iota(jnp.int32, (2 * C, 2 * C), 0)
    ci = lax.broadcasted_iota(jnp.int32, (2 * C, 2 * C), 1)
    lane = lax.broadcasted_iota(jnp.int32, (1, LANES), 1)
    ti = lax.broadcasted_iota(jnp.int32, (C, C), 0)
    tj = lax.broadcasted_iota(jnp.int32, (C, C), 1)
    consts = dict(
        m0=lane < RWKV_HEAD,
        upper=ri < ci,
        upinc=ri <= ci,
        blk=(ri // WKV_SUB) == (ci // WKV_SUB),
        hblk=(ri // RWKV_HEAD) == (ci // C),
        eye=(ri == ci).astype(F32),
    )
    m0 = consts["m0"]
    tril = (ti >= tj).astype(BF16)

    def head_sum(x):
        s0 = jnp.sum(jnp.where(m0, x, 0.0), axis=-1, keepdims=True)
        s1 = jnp.sum(jnp.where(m0, 0.0, x), axis=-1, keepdims=True)
        return jnp.where(m0, s0, s1)

    nbat = r_ref.shape[0]
    items = [(bi, slice(p * LANES, (p + 1) * LANES)) for bi in range(nbat) for p in range(WKV_PAIRS)]

    def recurrence(ci_):
        rows = pl.ds(pl.multiple_of(ci_ * C, C), C)
        ld = lambda ref: [ref[bi, rows, ln] for bi, ln in items]
        r, w, k, v, kk, b = ld(r_ref), ld(w_ref), ld(k_ref), ld(v_ref), ld(kk_ref), ld(b_ref)
        cum_all = [_dot_hilo_lhs(tril, w_ref[bi, rows, :]) for bi in range(nbat)]
        cum = [cum_all[bi][:, ln] for bi, ln in items]
        ys, s_new = _wkv_chunk(r, w, cum, k, v, kk, b, [s[...] for s in s_refs], consts)
        for i, (bi, ln) in enumerate(items):
            s_refs[i][...] = s_new[i]
            yt_ref[bi, :, ln] = ys[i]

    def finish(ci_):
        rows = pl.ds(pl.multiple_of(ci_ * C, C), C)
        for bi, ln in items:
            r_k, lg, lb_ = prm_ref[0:1, ln], prm_ref[1:2, ln], prm_ref[2:3, ln]
            yn = _wkv_group_norm(yt_ref[bi, :, ln], consts["hblk"])
            bonus = head_sum(r_ref[bi, rows, ln] * k_ref[bi, rows, ln] * r_k) * v_ref[bi, rows, ln]
            o_ref[bi, rows, ln] = ((yn * lg + lb_ + bonus) * g_ref[bi, rows, ln]).astype(o_ref.dtype)

    def chunk(ci_, carry):
        finish(ci_ - 1)
        recurrence(ci_)
        return carry

    recurrence(0)
    lax.fori_loop(1, lb // C, chunk, 0)
    finish(lb // C - 1)


def _wkv_row_block(lp, nbat):
    nch = lp // WKV_CHUNK
    for d in ((11, 8, 6, 4, 3, 2, 1) if nbat == 1 else (4, 3, 2, 1)):
        if nch % d == 0:
            return d * WKV_CHUNK
    return WKV_CHUNK


def _wkv(r, w, k, v, kk, b, g, prm):
    B, lp, D = r.shape
    nbat = WKV_BATCH if B % WKV_BATCH == 0 else 1
    lb = _wkv_row_block(lp, nbat)
    wl = WKV_PAIRS * LANES
    spec = pl.BlockSpec((nbat, lb, wl), lambda bi, pi, li: (bi, li, pi))
    return pl.pallas_call(
        functools.partial(_wkv_kernel, lb=lb),
        grid=(B // nbat, D // wl, lp // lb),
        in_specs=[spec] * 7 + [pl.BlockSpec((SUBLANES, wl), lambda bi, pi, li: (0, pi))],
        out_specs=spec,
        out_shape=jax.ShapeDtypeStruct((B, lp, D), BF16),
        scratch_shapes=[pltpu.VMEM((nbat, LANES, wl), F32)]
        + [pltpu.VMEM((LANES, LANES), F32)] * (WKV_PAIRS * nbat),
        compiler_params=_params("parallel", "parallel", "arbitrary"),
        name="wkv7",
    )(r, w, k, v, kk, b, g, prm)


def _proj_ln_kernel(a_ref, w_ref, vec_ref, h_ref, o_ref):
    mix = _dot(a_ref[...], w_ref[...]) + vec_ref[0:1, :]
    o_ref[...] = _layer_norm(DEEPNORM_ALPHA * h_ref[...] + mix, vec_ref[1:2, :], vec_ref[2:3, :])


def _proj_rows(T):
    return 2 * TM if T % (2 * TM) == 0 else TM


def _proj_ln(a2d, w, vec, h2d):
    T, D = h2d.shape
    K = a2d.shape[1]
    tp = _proj_rows(T)
    return pl.pallas_call(
        _proj_ln_kernel,
        grid=(T // tp,),
        in_specs=[pl.BlockSpec((tp, K), lambda i: (i, 0)), _full(w.shape), _full(vec.shape),
                  pl.BlockSpec((tp, D), lambda i: (i, 0))],
        out_specs=pl.BlockSpec((tp, D), lambda i: (i, 0)),
        out_shape=jax.ShapeDtypeStruct((T, D), F32),
        compiler_params=_params("parallel"),
        name="proj_ln",
    )(a2d, w, vec, h2d)


def _qkv_kernel(x_ref, wq_ref, wk_ref, wv_ref, bq_ref, bk_ref, bv_ref, q_out, k_out, v_out):
    x = x_ref[...].astype(BF16)
    q_out[...] = ((_dot(x, wq_ref[...]) + bq_ref[...]) * ATT_HEAD_DIM ** -0.5).astype(BF16)
    k_out[...] = (_dot(x, wk_ref[...]) + bk_ref[...]).astype(BF16)
    v_out[...] = (_dot(x, wv_ref[...]) + bv_ref[...]).astype(BF16)


def _qkv(h2d, wq, wkd, wvd, bq, bkd, bvd):
    T, D = h2d.shape
    nq, nk = wq.shape[1], wkd.shape[1]
    ws = [wq, wkd, wvd, bq, bkd, bvd]
    tp = _proj_rows(T)
    return pl.pallas_call(
        _qkv_kernel,
        grid=(T // tp,),
        in_specs=[pl.BlockSpec((tp, D), lambda i: (i, 0))] + [_full(w.shape) for w in ws],
        out_specs=[pl.BlockSpec((tp, nq), lambda i: (i, 0)), pl.BlockSpec((tp, nk), lambda i: (i, 0)),
                   pl.BlockSpec((tp, nk), lambda i: (i, 0))],
        out_shape=[jax.ShapeDtypeStruct((T, nq), BF16), jax.ShapeDtypeStruct((T, nk), BF16),
                   jax.ShapeDtypeStruct((T, nk), BF16)],
        compiler_params=_params("parallel"),
        name="qkv_proj",
    )(h2d, *ws)


def _attn_kernel(sink_ref, q_ref, k_ref, v_ref, bband_ref, bmeta_ref, bmm_ref, o_ref, *, nb):
    c = pl.program_id(0)
    BLK = ATT_BLOCK
    M0 = FRONT
    lane = lax.broadcasted_iota(jnp.int32, (1, LANES), 1)
    m0 = lane < ATT_HEAD_DIM
    qi = lax.broadcasted_iota(jnp.int32, (2 * BLK, BLK), 0) % BLK
    sj = lax.broadcasted_iota(jnp.int32, (2 * BLK, BLK), 1)
    cur_vis = sj <= qi
    prev_vis = sj > qi
    half = lax.broadcasted_iota(jnp.int32, (2 * BLK, 1), 0) < BLK

    def stack(x):
        z = jnp.zeros_like(x)
        return jnp.concatenate([jnp.where(m0, x, z), jnp.where(m0, z, x)], axis=0)

    HP = range(ATT_STEP_PAIRS)
    ppk = ATT_GROUP // 2
    kl = [slice((hp // ppk) * LANES, (hp // ppk + 1) * LANES) for hp in HP]
    ql = [slice(hp * LANES, (hp + 1) * LANES) for hp in HP]

    def rowmax(lgs, floor):
        by_width = {}
        for lg in lgs:
            by_width.setdefault(lg.shape[1], []).append(lg)
        tops = [jnp.max(functools.reduce(jnp.maximum, g), axis=-1, keepdims=True) for g in by_width.values()]
        return functools.reduce(jnp.maximum, tops, floor)

    def attend(segs, sinks, n):
        it = range(len(segs))
        mx = [rowmax([lg for lg, _ in segs[i]], sinks[i]) for i in it]
        ps = [[jnp.exp(lg - mx[i]).astype(BF16) for lg, _ in segs[i]] for i in it]
        one = jnp.ones((1, LANES), BF16)
        va = [[jnp.where(m0, vals, one) for _, vals in segs[i]] for i in it]
        vb = [[jnp.where(m0, one, vals) for _, vals in segs[i]] for i in it]
        oa = [sum(_dot(p[:n], v) for p, v in zip(ps[i], va[i])) for i in it]
        ob = [sum(_dot(p[n:], v) for p, v in zip(ps[i], vb[i])) for i in it]
        out = []
        for i in it:
            st = jnp.exp(sinks[i] - mx[i])
            den = pltpu.roll(jnp.where(m0, ob[i], oa[i]), ATT_HEAD_DIM, axis=1)
            out.append(jnp.where(m0, oa[i], ob[i]) / (den + jnp.where(m0, st[:n], st[n:])))
        return out

    o_ref[0:M0, :] = jnp.zeros((M0, o_ref.shape[1]), o_ref.dtype)
    s0 = [sink_ref[(c * ATT_STEP_PAIRS + hp) * 2] for hp in HP]
    s1 = [sink_ref[(c * ATT_STEP_PAIRS + hp) * 2 + 1] for hp in HP]
    k_meta = [k_ref[M0:M0 + N_META, kl[hp]] for hp in HP]
    v_meta = [v_ref[M0:M0 + N_META, kl[hp]] for hp in HP]

    first = lax.broadcasted_iota(jnp.int32, (2 * N_META, 1), 0) < N_META
    mi = lax.broadcasted_iota(jnp.int32, (2 * N_META, N_META), 0) % N_META
    mj = lax.broadcasted_iota(jnp.int32, (2 * N_META, N_META), 1)
    qm = [stack(q_ref[M0:M0 + N_META, ql[hp]]) for hp in HP]
    lg = [jnp.where(mj <= mi, _dot(qm[hp], k_meta[hp], _NT) + bmm_ref[hp], NEG) for hp in HP]
    om = attend([[(lg[hp], v_meta[hp])] for hp in HP], [jnp.where(first, s0[hp], s1[hp]) for hp in HP], N_META)
    for hp in HP:
        o_ref[M0:M0 + N_META, ql[hp]] = om[hp].astype(o_ref.dtype)

    sink_q = [jnp.where(half, s0[hp], s1[hp]) for hp in HP]

    unroll = 2 if nb % 2 == 0 else 1

    def blocks(jo, carry):
        items = [(jo * unroll + u, hp) for u in range(unroll) for hp in HP]
        start = [pl.multiple_of(ROW_ALIGN + j * BLK, ROW_ALIGN) for j, _ in items]
        pstart = [pl.multiple_of(jnp.maximum(s - BLK, 0), ROW_ALIGN) for s in start]
        it = range(len(items))
        qs = [stack(q_ref[pl.ds(start[i], BLK), ql[items[i][1]]]) for i in it]
        k_prev = [k_ref[pl.ds(pstart[i], BLK), kl[items[i][1]]] for i in it]
        k_cur = [k_ref[pl.ds(start[i], BLK), kl[items[i][1]]] for i in it]
        lg_meta = [_dot(qs[i], k_meta[items[i][1]], _NT) + bmeta_ref[items[i][1], items[i][0]] for i in it]
        lg_prev = [jnp.where(prev_vis & (items[i][0] > 0),
                             _dot(qs[i], k_prev[i], _NT) + bband_ref[items[i][1], :, 0:BLK], NEG) for i in it]
        lg_cur = [jnp.where(cur_vis, _dot(qs[i], k_cur[i], _NT) + bband_ref[items[i][1], :, BLK:2 * BLK], NEG)
                  for i in it]
        out = attend([[(lg_meta[i], v_meta[items[i][1]]),
                       (lg_prev[i], v_ref[pl.ds(pstart[i], BLK), kl[items[i][1]]]),
                       (lg_cur[i], v_ref[pl.ds(start[i], BLK), kl[items[i][1]]])] for i in it],
                     [sink_q[hp] for _, hp in items], BLK)
        for i in it:
            o_ref[pl.ds(start[i], BLK), ql[items[i][1]]] = out[i].astype(o_ref.dtype)
        return carry

    lax.fori_loop(0, nb // unroll, blocks, 0)


def _attention(q, kd, vd, sinks, bband, bmeta, bmm):
    B, lp, _ = q.shape
    nb = (lp - FRONT - N_META) // ATT_BLOCK
    assert WINDOW == ATT_BLOCK and ROW_ALIGN % BF16_ROWS == 0 and ATT_KV_HEADS % ATT_STEP_KV == 0
    np_, qw, kw = ATT_STEP_PAIRS, ATT_STEP_PAIRS * LANES, ATT_STEP_KV * LANES
    return pl.pallas_call(
        functools.partial(_attn_kernel, nb=nb),
        grid=(ATT_KV_HEADS // ATT_STEP_KV, B),
        in_specs=[pl.BlockSpec(memory_space=pltpu.SMEM),
                  pl.BlockSpec((None, lp, qw), lambda c, b: (b, 0, c)),
                  pl.BlockSpec((None, lp, kw), lambda c, b: (b, 0, c)),
                  pl.BlockSpec((None, lp, kw), lambda c, b: (b, 0, c)),
                  pl.BlockSpec((np_,) + bband.shape[1:], lambda c, b: (c, 0, 0)),
                  pl.BlockSpec((np_,) + bmeta.shape[1:], lambda c, b: (c, 0, 0, 0)),
                  pl.BlockSpec((np_,) + bmm.shape[1:], lambda c, b: (c, 0, 0))],
        out_specs=pl.BlockSpec((None, lp, qw), lambda c, b: (b, 0, c)),
        out_shape=jax.ShapeDtypeStruct((B, lp, D_MODEL), BF16),
        compiler_params=_params("parallel", "parallel"),
        name="swa_attention",
    )(sinks, q, kd, vd, bband, bmeta, bmm)


def _t5_bucket(dist):
    exact = N_BUCKETS // 2
    d = jnp.maximum(dist, 0)
    ratio = jnp.log(jnp.maximum(d, 1).astype(F32) / exact) / math.log(MAX_DISTANCE / exact)
    large = jnp.minimum(exact + (ratio * (N_BUCKETS - exact)).astype(jnp.int32), N_BUCKETS - 1)
    return jnp.where(d < exact, d, large)


def _bias_tables(rel_bias, nb):
    H, BLK = ATT_HEADS, ATT_BLOCK

    def lookup(dist):
        onehot = (_t5_bucket(dist)[..., None] == jnp.arange(N_BUCKETS)).astype(F32)
        return jnp.dot(onehot, rel_bias.astype(F32), precision=lax.Precision.HIGHEST)

    qi = jnp.arange(BLK)[:, None]
    band = lookup(qi + BLK - jnp.arange(2 * BLK)[None, :])
    band = jnp.moveaxis(band, -1, 0).reshape(H // 2, 2 * BLK, 2 * BLK)
    pos = jnp.arange(nb * BLK)[:, None]
    meta = lookup(N_META + pos - jnp.arange(N_META)[None, :])
    meta = meta.reshape(nb, BLK, N_META, H // 2, 2).transpose(3, 0, 4, 1, 2)
    meta = meta.reshape(H // 2, nb, 2 * BLK, N_META)
    pm = jnp.arange(N_META)
    mm = lookup(pm[:, None] - pm[None, :])
    mm = jnp.moveaxis(mm, -1, 0).reshape(H // 2, 2 * N_META, N_META)
    return band.astype(F32), meta.astype(F32), mm.astype(F32)


def _dispatch_geometry(a1):
    E = N_EXPERTS
    routed = (a1 > 0.0).astype(BF16)
    n_col = jnp.sum(routed.astype(F32), axis=1, keepdims=True)
    ch_col = jnp.floor((n_col + (MOE_CH - 1)) * (1.0 / MOE_CH))
    ei = lax.broadcasted_iota(jnp.int32, (E, E), 0)
    ej = lax.broadcasted_iota(jnp.int32, (E, E), 1)
    off_col = _dot((ei > ej).astype(BF16), jnp.broadcast_to(ch_col, (E, LANES)))[:, 0:1]
    n_row = _dot(jnp.ones((SUBLANES, TM), BF16), routed, _NT)
    ch_row = jnp.floor((n_row + (MOE_CH - 1)) * (1.0 / MOE_CH))
    off_row = _dot(ch_row, (ei < ej).astype(BF16))
    return ch_col, off_col, ch_row, off_row


def _select_experts(choice):
    E, G, EPG = N_EXPERTS, N_GROUPS, N_EXPERTS // N_GROUPS
    grp = choice.reshape(G, EPG, TM)
    sub = lax.broadcasted_iota(jnp.int32, (G, EPG, TM), 1)
    top1 = jnp.max(grp, axis=1, keepdims=True)
    first = jnp.min(jnp.where(grp == top1, sub, EPG), axis=1, keepdims=True)
    top2 = jnp.max(jnp.where(sub == first, -jnp.inf, grp), axis=1, keepdims=True)
    gscore = jnp.broadcast_to(top1 + top2, (G, EPG, TM))

    gi = lax.broadcasted_iota(jnp.int32, (G, EPG, TM), 0)
    keep = jnp.zeros((G, EPG, TM), jnp.bool_)
    for _ in range(TOPK_GROUPS):
        m = jnp.max(gscore, axis=0, keepdims=True)
        sel = gi == jnp.min(jnp.where(gscore == m, gi, G), axis=0, keepdims=True)
        keep = keep | sel
        gscore = jnp.where(sel, -jnp.inf, gscore)
    keep_e = keep.reshape(E, TM)

    cand = jnp.where(keep_e, choice, -jnp.inf)
    ei = lax.broadcasted_iota(jnp.int32, (E, TM), 0)
    routed = jnp.zeros((E, TM), jnp.bool_)
    for _ in range(TOP_K):
        m = jnp.max(cand, axis=0, keepdims=True)
        sel = ei == jnp.min(jnp.where(cand == m, ei, E), axis=0, keepdims=True)
        routed = routed | sel
        cand = jnp.where(sel, -jnp.inf, cand)
    return routed


def _route_kernel(x_ref, rt_ref, bias_ref, a1_out, a1t_out, gate_out, cnt_out, off_out, *, tiles):
    U = range(tiles)
    xs = [x_ref[u * TM:(u + 1) * TM, :] for u in U]
    scores = [_sigmoid(_dot3(rt_ref[...], x, _NT)) for x in xs]
    routed = [_select_experts(s + bias_ref[:, 0:1]) for s in scores]
    gate = [jnp.where(r, s, 0.0) for r, s in zip(routed, scores)]
    gate = [g / jnp.sum(g, axis=0, keepdims=True) * ROUTED_SCALE for g in gate]
    ti = lax.broadcasted_iota(jnp.int32, (TM, TM), 0)
    tj = lax.broadcasted_iota(jnp.int32, (TM, TM), 1)
    eye = (ti == tj).astype(BF16)
    routed_b = [r.astype(BF16) for r in routed]
    rank = [_dot(rb, (ti < tj).astype(BF16)) for rb in routed_b]
    rank_t = [_dot((tj < ti).astype(BF16), rb, _NT) for rb in routed_b]
    routed_t = [_dot(eye, rb, _NT) for rb in routed_b]
    gate_t = [_dot_exact_lhs(eye, g, _NT) for g in gate]
    for u in U:
        a1 = jnp.where(routed[u], rank[u] + 1.0, 0.0)
        _, _, ch_row, off_row = _dispatch_geometry(a1)
        a1_out[u] = a1
        a1t_out[u] = routed_t[u] * (rank_t[u] + 1.0)
        gate_out[u] = jnp.concatenate(_split2(gate_t[u]), axis=1)
        cnt_out[u] = ch_row
        off_out[u] = off_row


def _dispatch_kernel(x_ref, a1_ref, gate_ref, cnt_ref, off_ref, xs_out, gs_out):
    E = N_EXPERTS
    lo = jnp.concatenate([off_ref[0:1, :]] * 2, axis=1) * MOE_CH
    hi = lo + jnp.concatenate([cnt_ref[0:1, :]] * 2, axis=1) * MOE_CH
    first = lax.broadcasted_iota(jnp.int32, (1, 2 * E), 1) < E
    lo1 = jnp.where(first, lo, 0.0)
    xg = jnp.concatenate([x_ref[...].astype(BF16), gate_ref[...]], axis=1)
    a1b = a1_ref[...].astype(BF16)
    SB = MOE_SLOTS // MOE_SLOT_SPLITS
    for s0 in range(0, MOE_SLOTS, SB):
        si = (lax.broadcasted_iota(jnp.int32, (SB, 2 * E), 0) + s0).astype(F32)
        member = (si >= lo) & (si < hi)
        seg0 = jnp.sum(jnp.where(member, lo1, 0.0), axis=1, keepdims=True)
        rs1 = (lax.broadcasted_iota(jnp.int32, (SB, 1), 0) + (s0 + 1)).astype(F32) - seg0
        perm = _dot(member[:, :E], a1b) == rs1
        disp = _dot(perm, xg)
        for j in range(MOE_NL):
            piece = disp[:, j * LANES:(j + 1) * LANES].astype(BF16)
            xs_out[s0 // MOE_CH:(s0 + SB) // MOE_CH, j] = piece.reshape(SB // MOE_CH, MOE_CH, LANES)
        gs_out[s0:s0 + SB, :] = jnp.sum(jnp.where(member, disp[:, D_MODEL:], 0.0), axis=1, keepdims=True)


def _router(h2d, router_t, bias_col):
    T, D = h2d.shape
    nt = T // TM
    E = N_EXPERTS
    tiles = next(t for t in (4, 3, 2, 1) if nt % t == 0)
    a1, a1t, gate, cnt, off = pl.pallas_call(
        functools.partial(_route_kernel, tiles=tiles),
        grid=(nt // tiles,),
        in_specs=[pl.BlockSpec((tiles * TM, D), lambda i: (i, 0)), _full(router_t.shape), _full(bias_col.shape)],
        out_specs=[pl.BlockSpec((tiles, E, TM), lambda i: (i, 0, 0)),
                   pl.BlockSpec((tiles, TM, E), lambda i: (i, 0, 0)),
                   pl.BlockSpec((tiles, TM, 2 * E), lambda i: (i, 0, 0)),
                   pl.BlockSpec((tiles, SUBLANES, E), lambda i: (i, 0, 0)),
                   pl.BlockSpec((tiles, SUBLANES, E), lambda i: (i, 0, 0))],
        out_shape=[jax.ShapeDtypeStruct((nt, E, TM), F32),
                   jax.ShapeDtypeStruct((nt, TM, E), F32),
                   jax.ShapeDtypeStruct((nt, TM, 2 * E), BF16),
                   jax.ShapeDtypeStruct((nt, SUBLANES, E), F32),
                   jax.ShapeDtypeStruct((nt, SUBLANES, E), F32)],
        compiler_params=_params("parallel"),
        name="moe_route",
    )(h2d, router_t, bias_col)
    xs, gs = pl.pallas_call(
        _dispatch_kernel,
        grid=(nt,),
        in_specs=[pl.BlockSpec((TM, D), lambda i: (i, 0)),
                  pl.BlockSpec((None, E, TM), lambda i: (i, 0, 0)),
                  pl.BlockSpec((None, TM, 2 * E), lambda i: (i, 0, 0)),
                  pl.BlockSpec((None, SUBLANES, E), lambda i: (i, 0, 0)),
                  pl.BlockSpec((None, SUBLANES, E), lambda i: (i, 0, 0))],
        out_specs=[pl.BlockSpec((None, MOE_CPT, MOE_NL, MOE_CH, LANES), lambda i: (i, 0, 0, 0, 0)),
                   pl.BlockSpec((None, MOE_SLOTS, 1), lambda i: (i, 0, 0))],
        out_shape=[jax.ShapeDtypeStruct((nt, MOE_CPT, MOE_NL, MOE_CH, LANES), BF16),
                   jax.ShapeDtypeStruct((nt, MOE_SLOTS, 1), F32)],
        compiler_params=_params("parallel"),
        name="moe_dispatch",
    )(h2d, a1, gate, cnt, off)
    return xs, gs, a1, a1t, cnt


def _expert_kernel(be_ref, ps_ref, nb_ref, src_ref, xs_hbm, wg_ref, wu_ref, wd_ref, ys_hbm,
                   xbuf, ybuf, wgu_bf, wd_bf, in_sem, out_sem, *, nblk):
    b = pl.program_id(0)
    slot = lax.rem(b, 2)

    def in_copy(src_chunk, sl, c):
        return pltpu.make_async_copy(xs_hbm.at[src_chunk], xbuf.at[sl, c], in_sem.at[sl])

    def out_copy(dst_chunk, sl, c):
        return pltpu.make_async_copy(ybuf.at[sl, c], ys_hbm.at[dst_chunk], out_sem.at[sl])

    def for_chunks(blk, fn):
        p0 = ps_ref[blk]
        n = nb_ref[blk]
        for c0 in range(0, MOE_CB, MOE_SEG):

            @pl.when(n >= c0 + MOE_SEG)
            def _():
                for c in range(c0, c0 + MOE_SEG):
                    fn(src_ref[p0 + c], c)

            @pl.when((n > c0) & (n < c0 + MOE_SEG))
            def _():
                def body(c, carry):
                    fn(src_ref[p0 + c], c)
                    return carry

                lax.fori_loop(c0, n, body, 0)

    @pl.when(b == 0)
    def _():
        xbuf[...] = jnp.zeros_like(xbuf)
        for_chunks(0, lambda s, c: in_copy(s, 0, c).start())

    @pl.when(b + 1 < nblk)
    def _():
        for_chunks(b + 1, lambda s, c: in_copy(s, 1 - slot, c).start())

    for_chunks(b, lambda s, c: in_copy(s, slot, c).wait())

    @pl.when(b >= 2)
    def _():
        for_chunks(b - 2, lambda s, c: out_copy(s, slot, c).wait())

    @pl.when((b == 0) | (be_ref[b] != be_ref[jnp.maximum(b - 1, 0)]))
    def _():
        wgu_bf[:, :EXPERT_FF] = wg_ref[...].astype(BF16)
        wgu_bf[:, EXPERT_FF:] = wu_ref[...].astype(BF16)
        wd_bf[...] = wd_ref[...].astype(BF16)

    for c0 in range(0, MOE_CB, MOE_SEG):

        @pl.when(nb_ref[b] > c0)
        def _():
            x = jnp.concatenate([xbuf[slot, c0:c0 + MOE_SEG, j].reshape(MOE_SEG * MOE_CH, LANES)
                                 for j in range(MOE_NL)], axis=1)
            hid = _dot(x, wgu_bf[...])
            act = _silu(hid[:, :EXPERT_FF]) * hid[:, EXPERT_FF:]
            y = _dot(act, wd_bf[...]).astype(BF16)
            for j in range(MOE_NL):
                ybuf[slot, c0:c0 + MOE_SEG, j] = y[:, j * LANES:(j + 1) * LANES].reshape(MOE_SEG, MOE_CH, LANES)

    for_chunks(b, lambda s, c: out_copy(s, slot, c).start())

    @pl.when(b == nblk - 1)
    def _():
        for_chunks(b, lambda s, c: out_copy(s, slot, c).wait())

        @pl.when(b >= 1)
        def _():
            for_chunks(b - 1, lambda s, c: out_copy(s, 1 - slot, c).wait())


def _experts(xs, wg, wu, wd, layer, be, pstart, nbc, src):
    nt = xs.shape[0]
    nblk = be.shape[0]
    xs_c = xs.reshape(nt * MOE_CPT, MOE_NL, MOE_CH, LANES)
    by_expert = lambda b, be, ps, nb, src: (layer, be[b], 0, 0)
    grid_spec = pltpu.PrefetchScalarGridSpec(
        num_scalar_prefetch=4,
        grid=(nblk,),
        in_specs=[pl.BlockSpec(memory_space=pl.ANY),
                  pl.BlockSpec((None, None, D_MODEL, EXPERT_FF), by_expert),
                  pl.BlockSpec((None, None, D_MODEL, EXPERT_FF), by_expert),
                  pl.BlockSpec((None, None, EXPERT_FF, D_MODEL), by_expert)],
        out_specs=pl.BlockSpec(memory_space=pl.ANY),
        scratch_shapes=[pltpu.VMEM((2, MOE_CB, MOE_NL, MOE_CH, LANES), BF16),
                        pltpu.VMEM((2, MOE_CB, MOE_NL, MOE_CH, LANES), BF16),
                        pltpu.VMEM((D_MODEL, 2 * EXPERT_FF), BF16),
                        pltpu.VMEM((EXPERT_FF, D_MODEL), BF16),
                        pltpu.SemaphoreType.DMA((2,)), pltpu.SemaphoreType.DMA((2,))],
    )
    ys = pl.pallas_call(
        functools.partial(_expert_kernel, nblk=nblk),
        grid_spec=grid_spec,
        out_shape=jax.ShapeDtypeStruct(xs_c.shape, BF16),
        input_output_aliases={4: 0},
        compiler_params=_params("arbitrary"),
        name="moe_experts",
    )(be, pstart, nbc, src, xs_c, wg, wu, wd)
    return ys.reshape(xs.shape)


def _combine_kernel(h_ref, ys_ref, gs_ref, a1_ref, a1t_ref, wgu_ref, wd_ref, vec_ref, o_ref):
    E = N_EXPERTS
    h = h_ref[...]
    ch_col, off_col, _, _ = _dispatch_geometry(a1_ref[...])
    si = lax.broadcasted_iota(jnp.int32, (E, MOE_SLOTS), 1).astype(F32)
    lo = off_col * MOE_CH
    member_t = (si >= lo) & (si < lo + ch_col * MOE_CH)
    seg0 = jnp.sum(jnp.where(member_t, lo, 0.0), axis=0, keepdims=True)
    rs1 = lax.broadcasted_iota(jnp.int32, (1, MOE_SLOTS), 1).astype(F32) - seg0 + 1.0
    perm_t = _dot(a1t_ref[...], member_t) == rs1
    ys = jnp.concatenate([ys_ref[:, j].reshape(MOE_SLOTS, LANES) for j in range(MOE_NL)], axis=1)
    routed = _dot(perm_t.astype(BF16), ys.astype(F32) * gs_ref[...])
    xb = h.astype(BF16)
    hid = _dot(xb, wgu_ref[...])
    shared = _dot(_silu(hid[:, :EXPERT_FF]) * hid[:, EXPERT_FF:], wd_ref[...])
    o_ref[...] = _layer_norm(DEEPNORM_ALPHA * h + (shared + routed), vec_ref[0:1, :], vec_ref[1:2, :])


def _combine(h2d, ys, gs, a1, a1t, wgu_s, wd_s, vec):
    T, D = h2d.shape
    nt = T // TM
    return pl.pallas_call(
        _combine_kernel,
        grid=(nt,),
        in_specs=[pl.BlockSpec((TM, D), lambda i: (i, 0)),
                  pl.BlockSpec((None, MOE_CPT, MOE_NL, MOE_CH, LANES), lambda i: (i, 0, 0, 0, 0)),
                  pl.BlockSpec((None, MOE_SLOTS, 1), lambda i: (i, 0, 0)),
                  pl.BlockSpec((None, N_EXPERTS, TM), lambda i: (i, 0, 0)),
                  pl.BlockSpec((None, TM, N_EXPERTS), lambda i: (i, 0, 0)),
                  _full(wgu_s.shape), _full(wd_s.shape), _full(vec.shape)],
        out_specs=pl.BlockSpec((TM, D), lambda i: (i, 0)),
        out_shape=jax.ShapeDtypeStruct((T, D), F32),
        compiler_params=_params("parallel"),
        name="moe_combine",
    )(h2d, ys, gs, a1, a1t, wgu_s, wd_s, vec)


def _dispatch_tables(chunks):
    nt, E = chunks.shape
    maxc = nt * MOE_CPT
    nblk = -(-maxc // MOE_CB) + E
    hp = lax.Precision.HIGHEST
    ch = chunks.astype(F32)
    off = jnp.cumsum(ch, axis=1) - ch
    wend = jnp.cumsum(ch, axis=0)
    cnt_e = wend[-1]
    cend = jnp.cumsum(cnt_e)
    base_e = cend - cnt_e
    srcbase = jnp.arange(nt, dtype=F32)[:, None] * MOE_CPT + off - (wend - ch)
    p = jnp.arange(maxc, dtype=F32)
    e_p = jnp.minimum(jnp.sum((cend[None, :] <= p[:, None]).astype(jnp.int32), axis=1), E - 1)
    oh_e = (e_p[:, None] == jnp.arange(E, dtype=jnp.int32)[None, :]).astype(F32)
    look = jnp.dot(oh_e, jnp.concatenate([wend.T, srcbase.T, base_e[:, None]], axis=1), precision=hp)
    q = p - look[:, 2 * nt]
    i_p = jnp.sum((look[:, :nt] <= q[:, None]).astype(jnp.int32), axis=1)
    oh_i = i_p[:, None] == jnp.arange(nt, dtype=jnp.int32)[None, :]
    src = jnp.sum(jnp.where(oh_i, look[:, nt:2 * nt], 0.0), axis=1) + q
    src = jnp.clip(src, 0, maxc - 1).astype(jnp.int32)
    nblk_e = jnp.floor((cnt_e + (MOE_CB - 1)) * (1.0 / MOE_CB))
    bend = jnp.cumsum(nblk_e)
    bidx = jnp.arange(nblk, dtype=F32)
    be = jnp.minimum(jnp.sum((bend[None, :] <= bidx[:, None]).astype(jnp.int32), axis=1), E - 1)
    oh_b = (be[:, None] == jnp.arange(E, dtype=jnp.int32)[None, :]).astype(F32)
    lookb = jnp.dot(oh_b, jnp.stack([bend - nblk_e, cnt_e, base_e], axis=1), precision=hp)
    q0 = (bidx - lookb[:, 0]) * MOE_CB
    nbc = jnp.clip(lookb[:, 1] - q0, 0, MOE_CB).astype(jnp.int32)
    pstart = jnp.clip(lookb[:, 2] + q0, 0, maxc - MOE_CB).astype(jnp.int32)
    return be.astype(jnp.int32), pstart, nbc, src


def _moe_layer(h2d, layer, router, bias, w_gate, w_up, w_down, sh_gate, sh_up, sh_down, ln_g, ln_b):
    router_t = router.T.astype(F32)
    bias_col = jnp.broadcast_to(bias.astype(F32)[:, None], (N_EXPERTS, LANES))
    xs, gs, a1, a1t, cnt = _router(h2d, router_t, bias_col)
    be, pstart, nbc, src = _dispatch_tables(cnt[:, 0, :].astype(jnp.int32))
    ys = _experts(xs, w_gate, w_up, w_down, layer, be, pstart, nbc, src)
    wgu_s = jnp.concatenate([sh_gate, sh_up], axis=-1).astype(BF16)
    vec = jnp.zeros((SUBLANES, D_MODEL), F32).at[0].set(ln_g).at[1].set(ln_b)
    return _combine(h2d, ys, gs, a1, a1t, wgu_s, sh_down.astype(BF16), vec)


def _rows(*vs):
    out = jnp.zeros((SUBLANES, vs[0].shape[-1]), F32)
    for i, v in enumerate(vs):
        out = out.at[i].set(v.astype(F32))
    return out


def kernel(x, meta, rel_bias, rwkv_mu, rwkv_w0, rwkv_w1, rwkv_w2, rwkv_a0, rwkv_a1, rwkv_a2, rwkv_g1, rwkv_g2, rwkv_k_k, rwkv_k_a, rwkv_r_k, rwkv_w_r, rwkv_w_k, rwkv_w_v, rwkv_w_o, rwkv_lnx_g, rwkv_lnx_b, attn_w_qkv, attn_b_qkv, attn_sinks, attn_w_o, attn_b_o, ln_mix_g, ln_mix_b, ln_ffn_g, ln_ffn_b, moe_router, moe_bias, moe_w_gate, moe_w_up, moe_w_down, shared_w_gate, shared_w_up, shared_w_down):
    B, S, D = x.shape
    assert D == D_MODEL and S % ATT_BLOCK == 0 and (FRONT + N_META) % WKV_CHUNK == 0
    lp = FRONT + N_META + S
    T = B * lp
    assert T % TM == 0 and lp >= TM
    h = jnp.concatenate([jnp.zeros((B, FRONT, D), x.dtype),
                         jnp.broadcast_to(meta[None].astype(x.dtype), (B, N_META, D)), x], axis=1)
    h = h.reshape(T, D)
    bf = lambda w: w.astype(BF16)

    H, N = RWKV_HEADS, RWKV_HEAD
    head_of = jnp.arange(D) // N
    gsum = (head_of[:, None] == jnp.arange(LANES)[None, :]).astype(BF16)
    gexp = gsum.T
    r, w, k, v, kk, b, g = _rwkv_proj(
        h, lp, _rows(*rwkv_mu[0]), _rows(rwkv_w0[0], rwkv_a0[0], rwkv_k_k[0], rwkv_k_a[0]),
        bf(rwkv_w_r[0]), bf(rwkv_w_k[0]), bf(rwkv_w_v[0]), bf(rwkv_w1[0]), bf(rwkv_w2[0]),
        bf(rwkv_a1[0]), bf(rwkv_a2[0]), bf(rwkv_g1[0]), bf(rwkv_g2[0]), gsum, gexp)
    to3 = lambda t: t.reshape(B, lp, D)
    prm = _rows(rwkv_r_k[0].reshape(D), rwkv_lnx_g[0], rwkv_lnx_b[0])
    o = _wkv(to3(r), to3(w), to3(k), to3(v), to3(kk), to3(b), to3(g), prm)
    h = _proj_ln(o.reshape(T, D), bf(rwkv_w_o[0]), _rows(jnp.zeros((D,), F32), ln_mix_g[0], ln_mix_b[0]), h)
    h = _moe_layer(h, 0, moe_router[0], moe_bias[0], moe_w_gate, moe_w_up, moe_w_down,
                   shared_w_gate[0], shared_w_up[0], shared_w_down[0], ln_ffn_g[0], ln_ffn_b[0])

    HD, KV = ATT_HEAD_DIM, ATT_KV_HEADS
    qw = ATT_HEADS * HD
    wqkv, bqkv = attn_w_qkv[0], attn_b_qkv[0]
    dup = lambda t: jnp.concatenate([t.reshape(-1, KV, 1, HD)] * 2, axis=2).reshape(t.shape[0], 2 * KV * HD)
    wq, wkd, wvd = wqkv[:, :qw], dup(wqkv[:, qw:qw + KV * HD]), dup(wqkv[:, qw + KV * HD:])
    bq, bkd, bvd = bqkv[None, :qw], dup(bqkv[None, qw:qw + KV * HD]), dup(bqkv[None, qw + KV * HD:])
    q, kd, vd = _qkv(h, bf(wq), bf(wkd), bf(wvd), bq.astype(F32), bkd.astype(F32), bvd.astype(F32))
    bband, bmeta, bmm = _bias_tables(rel_bias, S // ATT_BLOCK)
    o = _attention(q.reshape(B, lp, qw), kd.reshape(B, lp, -1), vd.reshape(B, lp, -1),
                   attn_sinks[0].astype(F32), bband, bmeta, bmm)
    h = _proj_ln(o.reshape(T, D), bf(attn_w_o[0]), _rows(attn_b_o[0], ln_mix_g[1], ln_mix_b[1]), h)
    h = _moe_layer(h, 1, moe_router[1], moe_bias[1], moe_w_gate, moe_w_up, moe_w_down,
                   shared_w_gate[1], shared_w_up[1], shared_w_down[1], ln_ffn_g[1], ln_ffn_b[1])
    return h.reshape(B, lp, D)[:, FRONT + N_META:]
```

```python
import functools
import math

import jax
import jax.numpy as jnp
from jax import lax
from jax.experimental import pallas as pl
from jax.experimental.pallas import tpu as pltpu

F32 = jnp.float32
BF16 = jnp.bfloat16

D_MODEL = 1024
DEPTH = 2
N_META = 16
RWKV_HEAD = 64
RWKV_HEADS = D_MODEL // RWKV_HEAD
GN_EPS = 64e-5
ATT_HEADS = 16
ATT_KV_HEADS = 4
ATT_HEAD_DIM = D_MODEL // ATT_HEADS
ATT_GROUP = ATT_HEADS // ATT_KV_HEADS
WINDOW = 128
ATT_BLOCK = 128
N_BUCKETS = 32
MAX_DISTANCE = 128
N_EXPERTS = 64
TOP_K = 8
N_GROUPS = 8
TOPK_GROUPS = 4
EXPERT_FF = 256
ROUTED_SCALE = 2.5
DEEPNORM_ALPHA = (2 * DEPTH) ** 0.25
LN_EPS = 1e-5

LANES = 128
SUBLANES = 8
BF16_ROWS = 16
VMEM_LIMIT = 56 * 1024 * 1024

FRONT = 48
TM = 256
WKV_CHUNK = 64
WKV_PAIRS = 8
WKV_BATCH = 2
WKV_SUB = 16
ATT_STEP_KV = 2
ATT_STEP_PAIRS = ATT_STEP_KV * ATT_GROUP // 2
ROW_ALIGN = FRONT + N_META
MOE_CH = BF16_ROWS
MOE_SLOTS = TM * TOP_K + N_EXPERTS * MOE_CH
MOE_CPT = MOE_SLOTS // MOE_CH
MOE_NL = D_MODEL // LANES
MOE_CB = 128
MOE_SEG = 32
MOE_SLOT_SPLITS = 3
NEG = -1e30

_NN = (((1,), (0,)), ((), ()))
_NT = (((1,), (1,)), ((), ()))


def _dot(a, b, dn=_NN):
    return lax.dot_general(a.astype(BF16), b.astype(BF16), dn, preferred_element_type=F32)


def _split2(x):
    hi = x.astype(BF16)
    lo = (x - hi.astype(F32)).astype(BF16)
    return hi, lo


def _split3(x):
    h1 = x.astype(BF16)
    r1 = x - h1.astype(F32)
    h2 = r1.astype(BF16)
    h3 = (r1 - h2.astype(F32)).astype(BF16)
    return h1, h2, h3


def _dot_exact_lhs(a01, b, dn=_NN):
    a = a01.astype(BF16)
    return sum(lax.dot_general(a, p, dn, preferred_element_type=F32) for p in _split3(b))


def _dot_hilo_lhs(a01, b, dn=_NN):
    a = a01.astype(BF16)
    return sum(lax.dot_general(a, p, dn, preferred_element_type=F32) for p in _split2(b))


def _dot_hilo_rhs(a, b01, dn=_NN):
    b = b01.astype(BF16)
    return sum(lax.dot_general(p, b, dn, preferred_element_type=F32) for p in _split2(a))


def _dot3(a, b, dn=_NN):
    ah, al = _split2(a)
    bh, bl = _split2(b)
    d = lambda x, y: lax.dot_general(x, y, dn, preferred_element_type=F32)
    return d(ah, bh) + (d(ah, bl) + d(al, bh))


def _sigmoid(x):
    return 1.0 / (1.0 + jnp.exp(-x))


def _silu(x):
    return x * _sigmoid(x)


def _layer_norm(x, g, b):
    mu = jnp.mean(x, axis=-1, keepdims=True)
    xc = x - mu
    var = jnp.mean(xc * xc, axis=-1, keepdims=True)
    return xc * lax.rsqrt(var + LN_EPS) * g + b


def _full(shape):
    nd = len(shape)
    return pl.BlockSpec(shape, lambda *_: (0,) * nd)


def _params(*sem):
    return pltpu.CompilerParams(dimension_semantics=sem, vmem_limit_bytes=VMEM_LIMIT)


def _rwkv_proj_kernel(x_ref, xp_ref, mu_ref, vec_ref, wr_ref, wk_ref, wv_ref, w1_ref, w2_ref,
                      a1_ref, a2_ref, g1_ref, g2_ref, gsum_ref, gexp_ref,
                      r_out, w_out, k_out, v_out, kk_out, b_out, g_out, *, lp):
    i = pl.program_id(0)
    x = x_ref[...]
    rows = x.shape[0]
    row = lax.broadcasted_iota(jnp.int32, (rows, 1), 0)
    pos = lax.rem(i * rows, lp) + row
    pos = jnp.where(pos >= lp, pos - lp, pos)
    prev = jnp.where(row == 0, xp_ref[SUBLANES - 1:SUBLANES, :], pltpu.roll(x, 1, axis=0))
    prev = jnp.where(pos == FRONT, 0.0, prev)
    xx = prev - x
    valid = pos >= FRONT

    def mix(j):
        return (x + xx * mu_ref[j:j + 1, :]).astype(BF16)

    w0, a0, k_k, k_a = (vec_ref[j:j + 1, :] for j in range(4))
    r = _dot(mix(0), wr_ref[...])
    z = w0 + _dot(jnp.tanh(_dot(mix(1), w1_ref[...])), w2_ref[...])
    w = -math.exp(-0.5) * _sigmoid(z)
    k = _dot(mix(2), wk_ref[...])
    v = _dot(mix(3), wv_ref[...])
    a = _sigmoid(a0 + _dot(_dot(mix(4), a1_ref[...]), a2_ref[...]))
    g = _dot(_sigmoid(_dot(mix(5), g1_ref[...])), g2_ref[...])
    kk = k * k_k
    ssq = _dot_hilo_rhs(kk * kk, gsum_ref[...])
    kk = kk * lax.rsqrt(jnp.maximum(_dot_hilo_rhs(ssq, gexp_ref[...]), 1e-24))
    k = k * (1.0 + (a - 1.0) * k_a)
    r_out[...] = r
    w_out[...] = jnp.where(valid, w, 0.0)
    k_out[...] = jnp.where(valid, k, 0.0)
    v_out[...] = jnp.where(valid, v, 0.0)
    kk_out[...] = jnp.where(valid, kk, 0.0)
    b_out[...] = jnp.where(valid, kk * a, 0.0)
    g_out[...] = g


def _rwkv_proj(h2d, lp, mu, vec, wr, wk, wv, w1, w2, a1, a2, g1, g2, gsum, gexp):
    T, D = h2d.shape
    tp = _proj_rows(T)
    assert tp <= lp
    row_spec = pl.BlockSpec((tp, D), lambda i: (i, 0))
    prev_spec = pl.BlockSpec((SUBLANES, D), lambda i: (jnp.maximum(i * (tp // SUBLANES) - 1, 0), 0))
    ws = [mu, vec, wr, wk, wv, w1, w2, a1, a2, g1, g2, gsum, gexp]
    once = lambda w: pl.BlockSpec(w.shape, lambda i: (0,) * w.ndim, pipeline_mode=pl.Buffered(1))
    return pl.pallas_call(
        functools.partial(_rwkv_proj_kernel, lp=lp),
        grid=(T // tp,),
        in_specs=[row_spec, prev_spec] + [once(w) for w in ws],
        out_specs=[row_spec] * 7,
        out_shape=[jax.ShapeDtypeStruct((T, D), F32)] * 7,
        compiler_params=_params("parallel"),
        name="rwkv_proj",
    )(h2d, h2d, *ws)


def _wkv_chunk(r, w, cum, k, v, kk, b, S, c):
    C = WKV_CHUNK
    P = range(len(r))
    bf = lambda xs: [x.astype(BF16) for x in xs]
    each = lambda f, *ls: [f(*a) for a in zip(*ls)]

    def stack(x):
        return jnp.concatenate([jnp.where(c["m0"], x, 0.0), jnp.where(c["m0"], 0.0, x)], axis=0)

    f32 = lambda xs: [x.astype(F32) for x in xs]
    cat0 = lambda x, y: jnp.concatenate([x, y], axis=0)
    cat1 = lambda x, y: jnp.concatenate([x, y], axis=1)
    nt = lambda x, y: _dot(x, y, _NT)
    C2 = 2 * C

    tot = [cum[p][C - 1:C, :] for p in P]
    inv = [jnp.exp(-cum[p]) for p in P]
    dend = [jnp.exp(tot[p] - cum[p]) for p in P]
    kr_s = bf([cat0(stack(kk[p] * jnp.exp(cum[p] - w[p])), stack(r[p] * jnp.exp(cum[p]))) for p in P])
    bk_s = bf([cat0(stack(b[p] * inv[p]), stack(k[p] * inv[p])) for p in P])
    bh_s = bf([stack(b[p] * dend[p]) for p in P])
    kh_s = bf([stack(k[p] * dend[p]) for p in P])
    vs_t = bf([stack(v[p]).T for p in P])

    sc = each(nt, bk_s, kr_s)
    lab_t = [jnp.where(c["upper"], x[:C2, :C2], 0.0) for x in sc]
    arb_t = bf([jnp.where(c["upinc"], x[:C2, C2:], 0.0) for x in sc])
    lak_t = bf([jnp.where(c["upper"], x[C2:, :C2], 0.0) for x in sc])
    ark_t = bf([jnp.where(c["upinc"], x[C2:, C2:], 0.0) for x in sc])

    eye = c["eye"]
    ud = [jnp.where(c["blk"], x, 0.0) for x in lab_t]
    nu = bf([x - y for x, y in zip(lab_t, ud)])
    udb = bf(ud)
    u2b = bf(each(_dot, udb, udb))
    u2 = f32(u2b)
    t1 = each(_dot, bf([eye + x for x in u2]), [cat1(x, (eye - y).astype(BF16)) for x, y in zip(u2b, ud)])
    u4b = bf([x[:, :C2] - y for x, y in zip(t1, u2)])
    u4 = f32(u4b)
    t2 = each(_dot, bf([eye + x for x in u4]), [cat1(x, y[:, C2:].astype(BF16)) for x, y in zip(u4b, t1)])
    u8 = [x[:, :C2] - y for x, y in zip(t2, u4)]
    dinv = each(_dot, bf([eye + x for x in u8]), bf([x[:, C2:] for x in t2]))
    dinvb = bf(dinv)
    wzb = bf(each(_dot, dinvb, nu))
    t3 = each(_dot, wzb, [cat1(x, y) for x, y in zip(wzb, dinvb)])
    tt_t = bf(each(_dot, bf([eye + x[:, :C2] for x in t3]), bf([y - x[:, C2:] for x, y in zip(t3, dinv)])))

    sb = bf(S)
    p2 = each(_dot, vs_t, [cat1(x, y) for x, y in zip(lak_t, ark_t)])
    p3 = each(_dot, vs_t, kh_s)
    p1 = each(nt, sb, kr_s)
    u_t = bf(each(_dot, bf([-(x[:, :C2] + y[:, :C2]) for x, y in zip(p1, p2)]), tt_t))
    p4 = each(_dot, u_t, [cat1(x, y) for x, y in zip(arb_t, bh_s)])
    s_new = [S[p] * jnp.exp(tot[p]) + p4[p][:, C2:] + p3[p] for p in P]
    y_t = [p1[p][:, C2:] + p4[p][:, :C2] + p2[p][:, C2:] for p in P]
    return y_t, s_new


def _wkv_group_norm(y_t, hblk):
    C = WKV_CHUNK
    mean = jnp.sum(y_t, axis=0, keepdims=True) * (1.0 / RWKV_HEAD)
    yc = jnp.where(hblk, y_t - mean, 0.0)
    var = jnp.sum(yc * yc, axis=0, keepdims=True) * (1.0 / RWKV_HEAD)
    ys = (yc * lax.rsqrt(var + GN_EPS)).T
    return ys[:C] + ys[C:]


def _wkv_kernel(r_ref, w_ref, k_ref, v_ref, kk_ref, b_ref, g_ref, prm_ref, o_ref, yt_ref, *s_refs, lb):
    C = WKV_CHUNK

    @pl.when(pl.program_id(2) == 0)
    def _():
        for s_ref in s_refs:
            s_ref[...] = jnp.zeros_like(s_ref)

    ri = lax.broadcasted_iota(jnp.int32, (2 * C, 2 * C), 0)
    ci = lax.broadcasted_iota(jnp.int32, (2 * C, 2 * C), 1)
    lane = lax.broadcasted_iota(jnp.int32, (1, LANES), 1)
    ti = lax.broadcasted_iota(jnp.int32, (C, C), 0)
    tj = lax.broadcasted_iota(jnp.int32, (C, C), 1)
    consts = dict(
        m0=lane < RWKV_HEAD,
        upper=ri < ci,
        upinc=ri <= ci,
        blk=(ri // WKV_SUB) == (ci // WKV_SUB),
        hblk=(ri // RWKV_HEAD) == (ci // C),
        eye=(ri == ci).astype(F32),
    )
    m0 = consts["m0"]
    tril = (ti >= tj).astype(BF16)

    def head_sum(x):
        s0 = jnp.sum(jnp.where(m0, x, 0.0), axis=-1, keepdims=True)
        s1 = jnp.sum(jnp.where(m0, 0.0, x), axis=-1, keepdims=True)
        return jnp.where(m0, s0, s1)

    nbat = r_ref.shape[0]
    items = [(bi, slice(p * LANES, (p + 1) * LANES)) for bi in range(nbat) for p in range(WKV_PAIRS)]

    def recurrence(ci_):
        rows = pl.ds(pl.multiple_of(ci_ * C, C), C)
        ld = lambda ref: [ref[bi, rows, ln] for bi, ln in items]
        r, w, k, v, kk, b = ld(r_ref), ld(w_ref), ld(k_ref), ld(v_ref), ld(kk_ref), ld(b_ref)
        cum_all = [_dot_hilo_lhs(tril, w_ref[bi, rows, :]) for bi in range(nbat)]
        cum = [cum_all[bi][:, ln] for bi, ln in items]
        ys, s_new = _wkv_chunk(r, w, cum, k, v, kk, b, [s[...] for s in s_refs], consts)
        for i, (bi, ln) in enumerate(items):
            s_refs[i][...] = s_new[i]
            yt_ref[bi, :, ln] = ys[i]

    def finish(ci_):
        rows = pl.ds(pl.multiple_of(ci_ * C, C), C)
        for bi, ln in items:
            r_k, lg, lb_ = prm_ref[0:1, ln], prm_ref[1:2, ln], prm_ref[2:3, ln]
            yn = _wkv_group_norm(yt_ref[bi, :, ln], consts["hblk"])
            bonus = head_sum(r_ref[bi, rows, ln] * k_ref[bi, rows, ln] * r_k) * v_ref[bi, rows, ln]
            o_ref[bi, rows, ln] = ((yn * lg + lb_ + bonus) * g_ref[bi, rows, ln]).astype(o_ref.dtype)

    def chunk(ci_, carry):
        finish(ci_ - 1)
        recurrence(ci_)
        return carry

    recurrence(0)
    lax.fori_loop(1, lb // C, chunk, 0)
    finish(lb // C - 1)


def _wkv_row_block(lp, nbat):
    nch = lp // WKV_CHUNK
    for d in ((11, 8, 6, 4, 3, 2, 1) if nbat == 1 else (4, 3, 2, 1)):
        if nch % d == 0:
            return d * WKV_CHUNK
    return WKV_CHUNK


def _wkv(r, w, k, v, kk, b, g, prm):
    B, lp, D = r.shape
    nbat = WKV_BATCH if B % WKV_BATCH == 0 else 1
    lb = _wkv_row_block(lp, nbat)
    wl = WKV_PAIRS * LANES
    spec = pl.BlockSpec((nbat, lb, wl), lambda bi, pi, li: (bi, li, pi))
    return pl.pallas_call(
        functools.partial(_wkv_kernel, lb=lb),
        grid=(B // nbat, D // wl, lp // lb),
        in_specs=[spec] * 7 + [pl.BlockSpec((SUBLANES, wl), lambda bi, pi, li: (0, pi))],
        out_specs=spec,
        out_shape=jax.ShapeDtypeStruct((B, lp, D), BF16),
        scratch_shapes=[pltpu.VMEM((nbat, LANES, wl), F32)]
        + [pltpu.VMEM((LANES, LANES), F32)] * (WKV_PAIRS * nbat),
        compiler_params=_params("parallel", "parallel", "arbitrary"),
        name="wkv7",
    )(r, w, k, v, kk, b, g, prm)


def _proj_ln_kernel(a_ref, w_ref, vec_ref, h_ref, o_ref):
    mix = _dot(a_ref[...], w_ref[...]) + vec_ref[0:1, :]
    o_ref[...] = _layer_norm(DEEPNORM_ALPHA * h_ref[...] + mix, vec_ref[1:2, :], vec_ref[2:3, :])


def _proj_rows(T):
    return 2 * TM if T % (2 * TM) == 0 else TM


def _proj_ln(a2d, w, vec, h2d):
    T, D = h2d.shape
    K = a2d.shape[1]
    tp = _proj_rows(T)
    return pl.pallas_call(
        _proj_ln_kernel,
        grid=(T // tp,),
        in_specs=[pl.BlockSpec((tp, K), lambda i: (i, 0)), _full(w.shape), _full(vec.shape),
                  pl.BlockSpec((tp, D), lambda i: (i, 0))],
        out_specs=pl.BlockSpec((tp, D), lambda i: (i, 0)),
        out_shape=jax.ShapeDtypeStruct((T, D), F32),
        compiler_params=_params("parallel"),
        name="proj_ln",
    )(a2d, w, vec, h2d)


def _qkv_kernel(x_ref, wq_ref, wk_ref, wv_ref, bq_ref, bk_ref, bv_ref, q_out, k_out, v_out):
    x = x_ref[...].astype(BF16)
    q_out[...] = ((_dot(x, wq_ref[...]) + bq_ref[...]) * ATT_HEAD_DIM ** -0.5).astype(BF16)
    k_out[...] = (_dot(x, wk_ref[...]) + bk_ref[...]).astype(BF16)
    v_out[...] = (_dot(x, wv_ref[...]) + bv_ref[...]).astype(BF16)


def _qkv(h2d, wq, wkd, wvd, bq, bkd, bvd):
    T, D = h2d.shape
    nq, nk = wq.shape[1], wkd.shape[1]
    ws = [wq, wkd, wvd, bq, bkd, bvd]
    tp = _proj_rows(T)
    return pl.pallas_call(
        _qkv_kernel,
        grid=(T // tp,),
        in_specs=[pl.BlockSpec((tp, D), lambda i: (i, 0))] + [_full(w.shape) for w in ws],
        out_specs=[pl.BlockSpec((tp, nq), lambda i: (i, 0)), pl.BlockSpec((tp, nk), lambda i: (i, 0)),
                   pl.BlockSpec((tp, nk), lambda i: (i, 0))],
        out_shape=[jax.ShapeDtypeStruct((T, nq), BF16), jax.ShapeDtypeStruct((T, nk), BF16),
                   jax.ShapeDtypeStruct((T, nk), BF16)],
        compiler_params=_params("parallel"),
        name="qkv_proj",
    )(h2d, *ws)


def _attn_kernel(sink_ref, q_ref, k_ref, v_ref, bband_ref, bmeta_ref, bmm_ref, o_ref, *, nb):
    c = pl.program_id(0)
    BLK = ATT_BLOCK
    M0 = FRONT
    lane = lax.broadcasted_iota(jnp.int32, (1, LANES), 1)
    m0 = lane < ATT_HEAD_DIM
    qi = lax.broadcasted_iota(jnp.int32, (2 * BLK, BLK), 0) % BLK
    sj = lax.broadcasted_iota(jnp.int32, (2 * BLK, BLK), 1)
    cur_vis = sj <= qi
    prev_vis = sj > qi
    half = lax.broadcasted_iota(jnp.int32, (2 * BLK, 1), 0) < BLK

    def stack(x):
        z = jnp.zeros_like(x)
        return jnp.concatenate([jnp.where(m0, x, z), jnp.where(m0, z, x)], axis=0)

    HP = range(ATT_STEP_PAIRS)
    ppk = ATT_GROUP // 2
    kl = [slice((hp // ppk) * LANES, (hp // ppk + 1) * LANES) for hp in HP]
    ql = [slice(hp * LANES, (hp + 1) * LANES) for hp in HP]

    def rowmax(lgs, floor):
        by_width = {}
        for lg in lgs:
            by_width.setdefault(lg.shape[1], []).append(lg)
        tops = [jnp.max(functools.reduce(jnp.maximum, g), axis=-1, keepdims=True) for g in by_width.values()]
        return functools.reduce(jnp.maximum, tops, floor)

    def attend(segs, sinks, n):
        it = range(len(segs))
        mx = [rowmax([lg for lg, _ in segs[i]], sinks[i]) for i in it]
        ps = [[jnp.exp(lg - mx[i]).astype(BF16) for lg, _ in segs[i]] for i in it]
        one = jnp.ones((1, LANES), BF16)
        va = [[jnp.where(m0, vals, one) for _, vals in segs[i]] for i in it]
        vb = [[jnp.where(m0, one, vals) for _, vals in segs[i]] for i in it]
        oa = [sum(_dot(p[:n], v) for p, v in zip(ps[i], va[i])) for i in it]
        ob = [sum(_dot(p[n:], v) for p, v in zip(ps[i], vb[i])) for i in it]
        out = []
        for i in it:
            st = jnp.exp(sinks[i] - mx[i])
            den = pltpu.roll(jnp.where(m0, ob[i], oa[i]), ATT_HEAD_DIM, axis=1)
            out.append(jnp.where(m0, oa[i], ob[i]) / (den + jnp.where(m0, st[:n], st[n:])))
        return out

    o_ref[0:M0, :] = jnp.zeros((M0, o_ref.shape[1]), o_ref.dtype)
    s0 = [sink_ref[(c * ATT_STEP_PAIRS + hp) * 2] for hp in HP]
    s1 = [sink_ref[(c * ATT_STEP_PAIRS + hp) * 2 + 1] for hp in HP]
    k_meta = [k_ref[M0:M0 + N_META, kl[hp]] for hp in HP]
    v_meta = [v_ref[M0:M0 + N_META, kl[hp]] for hp in HP]

    first = lax.broadcasted_iota(jnp.int32, (2 * N_META, 1), 0) < N_META
    mi = lax.broadcasted_iota(jnp.int32, (2 * N_META, N_META), 0) % N_META
    mj = lax.broadcasted_iota(jnp.int32, (2 * N_META, N_META), 1)
    qm = [stack(q_ref[M0:M0 + N_META, ql[hp]]) for hp in HP]
    lg = [jnp.where(mj <= mi, _dot(qm[hp], k_meta[hp], _NT) + bmm_ref[hp], NEG) for hp in HP]
    om = attend([[(lg[hp], v_meta[hp])] for hp in HP], [jnp.where(first, s0[hp], s1[hp]) for hp in HP], N_META)
    for hp in HP:
        o_ref[M0:M0 + N_META, ql[hp]] = om[hp].astype(o_ref.dtype)

    sink_q = [jnp.where(half, s0[hp], s1[hp]) for hp in HP]

    unroll = 2 if nb % 2 == 0 else 1

    def blocks(jo, carry):
        items = [(jo * unroll + u, hp) for u in range(unroll) for hp in HP]
        start = [pl.multiple_of(ROW_ALIGN + j * BLK, ROW_ALIGN) for j, _ in items]
        pstart = [pl.multiple_of(jnp.maximum(s - BLK, 0), ROW_ALIGN) for s in start]
        it = range(len(items))
        qs = [stack(q_ref[pl.ds(start[i], BLK), ql[items[i][1]]]) for i in it]
        k_prev = [k_ref[pl.ds(pstart[i], BLK), kl[items[i][1]]] for i in it]
        k_cur = [k_ref[pl.ds(start[i], BLK), kl[items[i][1]]] for i in it]
        lg_meta = [_dot(qs[i], k_meta[items[i][1]], _NT) + bmeta_ref[items[i][1], items[i][0]] for i in it]
        lg_prev = [jnp.where(prev_vis & (items[i][0] > 0),
                             _dot(qs[i], k_prev[i], _NT) + bband_ref[items[i][1], :, 0:BLK], NEG) for i in it]
        lg_cur = [jnp.where(cur_vis, _dot(qs[i], k_cur[i], _NT) + bband_ref[items[i][1], :, BLK:2 * BLK], NEG)
                  for i in it]
        out = attend([[(lg_meta[i], v_meta[items[i][1]]),
                       (lg_prev[i], v_ref[pl.ds(pstart[i], BLK), kl[items[i][1]]]),
                       (lg_cur[i], v_ref[pl.ds(start[i], BLK), kl[items[i][1]]])] for i in it],
                     [sink_q[hp] for _, hp in items], BLK)
        for i in it:
            o_ref[pl.ds(start[i], BLK), ql[items[i][1]]] = out[i].astype(o_ref.dtype)
        return carry

    lax.fori_loop(0, nb // unroll, blocks, 0)


def _attention(q, kd, vd, sinks, bband, bmeta, bmm):
    B, lp, _ = q.shape
    nb = (lp - FRONT - N_META) // ATT_BLOCK
    assert WINDOW == ATT_BLOCK and ROW_ALIGN % BF16_ROWS == 0 and ATT_KV_HEADS % ATT_STEP_KV == 0
    np_, qw, kw = ATT_STEP_PAIRS, ATT_STEP_PAIRS * LANES, ATT_STEP_KV * LANES
    return pl.pallas_call(
        functools.partial(_attn_kernel, nb=nb),
        grid=(ATT_KV_HEADS // ATT_STEP_KV, B),
        in_specs=[pl.BlockSpec(memory_space=pltpu.SMEM),
                  pl.BlockSpec((None, lp, qw), lambda c, b: (b, 0, c)),
                  pl.BlockSpec((None, lp, kw), lambda c, b: (b, 0, c)),
                  pl.BlockSpec((None, lp, kw), lambda c, b: (b, 0, c)),
                  pl.BlockSpec((np_,) + bband.shape[1:], lambda c, b: (c, 0, 0)),
                  pl.BlockSpec((np_,) + bmeta.shape[1:], lambda c, b: (c, 0, 0, 0)),
                  pl.BlockSpec((np_,) + bmm.shape[1:], lambda c, b: (c, 0, 0))],
        out_specs=pl.BlockSpec((None, lp, qw), lambda c, b: (b, 0, c)),
        out_shape=jax.ShapeDtypeStruct((B, lp, D_MODEL), BF16),
        compiler_params=_params("parallel", "parallel"),
        name="swa_attention",
    )(sinks, q, kd, vd, bband, bmeta, bmm)


def _t5_bucket(dist):
    exact = N_BUCKETS // 2
    d = jnp.maximum(dist, 0)
    ratio = jnp.log(jnp.maximum(d, 1).astype(F32) / exact) / math.log(MAX_DISTANCE / exact)
    large = jnp.minimum(exact + (ratio * (N_BUCKETS - exact)).astype(jnp.int32), N_BUCKETS - 1)
    return jnp.where(d < exact, d, large)


def _bias_tables(rel_bias, nb):
    H, BLK = ATT_HEADS, ATT_BLOCK

    def lookup(dist):
        onehot = (_t5_bucket(dist)[..., None] == jnp.arange(N_BUCKETS)).astype(F32)
        return jnp.dot(onehot, rel_bias.astype(F32), precision=lax.Precision.HIGHEST)

    qi = jnp.arange(BLK)[:, None]
    band = lookup(qi + BLK - jnp.arange(2 * BLK)[None, :])
    band = jnp.moveaxis(band, -1, 0).reshape(H // 2, 2 * BLK, 2 * BLK)
    pos = jnp.arange(nb * BLK)[:, None]
    meta = lookup(N_META + pos - jnp.arange(N_META)[None, :])
    meta = meta.reshape(nb, BLK, N_META, H // 2, 2).transpose(3, 0, 4, 1, 2)
    meta = meta.reshape(H // 2, nb, 2 * BLK, N_META)
    pm = jnp.arange(N_META)
    mm = lookup(pm[:, None] - pm[None, :])
    mm = jnp.moveaxis(mm, -1, 0).reshape(H // 2, 2 * N_META, N_META)
    return band.astype(F32), meta.astype(F32), mm.astype(F32)


def _dispatch_geometry(a1):
    E = N_EXPERTS
    routed = (a1 > 0.0).astype(BF16)
    n_col = jnp.sum(routed.astype(F32), axis=1, keepdims=True)
    ch_col = jnp.floor((n_col + (MOE_CH - 1)) * (1.0 / MOE_CH))
    ei = lax.broadcasted_iota(jnp.int32, (E, E), 0)
    ej = lax.broadcasted_iota(jnp.int32, (E, E), 1)
    off_col = _dot((ei > ej).astype(BF16), jnp.broadcast_to(ch_col, (E, LANES)))[:, 0:1]
    n_row = _dot(jnp.ones((SUBLANES, TM), BF16), routed, _NT)
    ch_row = jnp.floor((n_row + (MOE_CH - 1)) * (1.0 / MOE_CH))
    off_row = _dot(ch_row, (ei < ej).astype(BF16))
    return ch_col, off_col, ch_row, off_row


def _select_experts(choice):
    E, G, EPG = N_EXPERTS, N_GROUPS, N_EXPERTS // N_GROUPS
    grp = choice.reshape(G, EPG, TM)
    sub = lax.broadcasted_iota(jnp.int32, (G, EPG, TM), 1)
    top1 = jnp.max(grp, axis=1, keepdims=True)
    first = jnp.min(jnp.where(grp == top1, sub, EPG), axis=1, keepdims=True)
    top2 = jnp.max(jnp.where(sub == first, -jnp.inf, grp), axis=1, keepdims=True)
    gscore = jnp.broadcast_to(top1 + top2, (G, EPG, TM))

    gi = lax.broadcasted_iota(jnp.int32, (G, EPG, TM), 0)
    keep = jnp.zeros((G, EPG, TM), jnp.bool_)
    for _ in range(TOPK_GROUPS):
        m = jnp.max(gscore, axis=0, keepdims=True)
        sel = gi == jnp.min(jnp.where(gscore == m, gi, G), axis=0, keepdims=True)
        keep = keep | sel
        gscore = jnp.where(sel, -jnp.inf, gscore)
    keep_e = keep.reshape(E, TM)

    cand = jnp.where(keep_e, choice, -jnp.inf)
    ei = lax.broadcasted_iota(jnp.int32, (E, TM), 0)
    routed = jnp.zeros((E, TM), jnp.bool_)
    for _ in range(TOP_K):
        m = jnp.max(cand, axis=0, keepdims=True)
        sel = ei == jnp.min(jnp.where(cand == m, ei, E), axis=0, keepdims=True)
        routed = routed | sel
        cand = jnp.where(sel, -jnp.inf, cand)
    return routed


def _route_kernel(x_ref, rt_ref, bias_ref, a1_out, a1t_out, gate_out, cnt_out, off_out, *, tiles):
    U = range(tiles)
    xs = [x_ref[u * TM:(u + 1) * TM, :] for u in U]
    scores = [_sigmoid(_dot3(rt_ref[...], x, _NT)) for x in xs]
    routed = [_select_experts(s + bias_ref[:, 0:1]) for s in scores]
    gate = [jnp.where(r, s, 0.0) for r, s in zip(routed, scores)]
    gate = [g / jnp.sum(g, axis=0, keepdims=True) * ROUTED_SCALE for g in gate]
    ti = lax.broadcasted_iota(jnp.int32, (TM, TM), 0)
    tj = lax.broadcasted_iota(jnp.int32, (TM, TM), 1)
    eye = (ti == tj).astype(BF16)
    routed_b = [r.astype(BF16) for r in routed]
    rank = [_dot(rb, (ti < tj).astype(BF16)) for rb in routed_b]
    rank_t = [_dot((tj < ti).astype(BF16), rb, _NT) for rb in routed_b]
    routed_t = [_dot(eye, rb, _NT) for rb in routed_b]
    gate_t = [_dot_exact_lhs(eye, g, _NT) for g in gate]
    for u in U:
        a1 = jnp.where(routed[u], rank[u] + 1.0, 0.0)
        _, _, ch_row, off_row = _dispatch_geometry(a1)
        a1_out[u] = a1
        a1t_out[u] = routed_t[u] * (rank_t[u] + 1.0)
        gate_out[u] = jnp.concatenate(_split2(gate_t[u]), axis=1)
        cnt_out[u] = ch_row
        off_out[u] = off_row


def _dispatch_kernel(x_ref, a1_ref, gate_ref, cnt_ref, off_ref, xs_out, gs_out):
    E = N_EXPERTS
    lo = jnp.concatenate([off_ref[0:1, :]] * 2, axis=1) * MOE_CH
    hi = lo + jnp.concatenate([cnt_ref[0:1, :]] * 2, axis=1) * MOE_CH
    first = lax.broadcasted_iota(jnp.int32, (1, 2 * E), 1) < E
    lo1 = jnp.where(first, lo, 0.0)
    xg = jnp.concatenate([x_ref[...].astype(BF16), gate_ref[...]], axis=1)
    a1b = a1_ref[...].astype(BF16)
    SB = MOE_SLOTS // MOE_SLOT_SPLITS
    eye = (lax.broadcasted_iota(jnp.int32, (LANES, LANES), 0)
           == lax.broadcasted_iota(jnp.int32, (LANES, LANES), 1))
    for s0 in range(0, MOE_SLOTS, SB):
        si = (lax.broadcasted_iota(jnp.int32, (SB, 2 * E), 0) + s0).astype(F32)
        member = (si >= lo) & (si < hi)
        seg0 = jnp.sum(jnp.where(member, lo1, 0.0), axis=1, keepdims=True)
        rs1 = (lax.broadcasted_iota(jnp.int32, (SB, 1), 0) + (s0 + 1)).astype(F32) - seg0
        perm = _dot(member[:, :E], a1b) == rs1
        disp = _dot(perm, xg)
        for j in range(MOE_NL):
            piece = disp[:, j * LANES:(j + 1) * LANES].astype(BF16)
            xs_out[s0 // MOE_CH:(s0 + SB) // MOE_CH, j] = piece.reshape(SB // MOE_CH, MOE_CH, LANES)
        gcol = jnp.sum(jnp.where(member, disp[:, D_MODEL:], 0.0), axis=1, keepdims=True)
        for r0 in range(0, SB, LANES):
            grow = jnp.sum(jnp.where(eye, gcol[r0:r0 + LANES], 0.0), axis=0, keepdims=True)
            gs_out[:, s0 + r0:s0 + r0 + LANES] = jnp.broadcast_to(grow, (SUBLANES, LANES))


def _router(h2d, router_t, bias_col):
    T, D = h2d.shape
    nt = T // TM
    E = N_EXPERTS
    tiles = next(t for t in (4, 3, 2, 1) if nt % t == 0)
    a1, a1t, gate, cnt, off = pl.pallas_call(
        functools.partial(_route_kernel, tiles=tiles),
        grid=(nt // tiles,),
        in_specs=[pl.BlockSpec((tiles * TM, D), lambda i: (i, 0)), _full(router_t.shape), _full(bias_col.shape)],
        out_specs=[pl.BlockSpec((tiles, E, TM), lambda i: (i, 0, 0)),
                   pl.BlockSpec((tiles, TM, E), lambda i: (i, 0, 0)),
                   pl.BlockSpec((tiles, TM, 2 * E), lambda i: (i, 0, 0)),
                   pl.BlockSpec((tiles, SUBLANES, E), lambda i: (i, 0, 0)),
                   pl.BlockSpec((tiles, SUBLANES, E), lambda i: (i, 0, 0))],
        out_shape=[jax.ShapeDtypeStruct((nt, E, TM), F32),
                   jax.ShapeDtypeStruct((nt, TM, E), F32),
                   jax.ShapeDtypeStruct((nt, TM, 2 * E), BF16),
                   jax.ShapeDtypeStruct((nt, SUBLANES, E), F32),
                   jax.ShapeDtypeStruct((nt, SUBLANES, E), F32)],
        compiler_params=_params("parallel"),
        name="moe_route",
    )(h2d, router_t, bias_col)
    xs, gs = pl.pallas_call(
        _dispatch_kernel,
        grid=(nt,),
        in_specs=[pl.BlockSpec((TM, D), lambda i: (i, 0)),
                  pl.BlockSpec((None, E, TM), lambda i: (i, 0, 0)),
                  pl.BlockSpec((None, TM, 2 * E), lambda i: (i, 0, 0)),
                  pl.BlockSpec((None, SUBLANES, E), lambda i: (i, 0, 0)),
                  pl.BlockSpec((None, SUBLANES, E), lambda i: (i, 0, 0))],
        out_specs=[pl.BlockSpec((None, MOE_CPT, MOE_NL, MOE_CH, LANES), lambda i: (i, 0, 0, 0, 0)),
                   pl.BlockSpec((None, SUBLANES, MOE_SLOTS), lambda i: (i, 0, 0))],
        out_shape=[jax.ShapeDtypeStruct((nt, MOE_CPT, MOE_NL, MOE_CH, LANES), BF16),
                   jax.ShapeDtypeStruct((nt, SUBLANES, MOE_SLOTS), F32)],
        compiler_params=_params("parallel"),
        name="moe_dispatch",
    )(h2d, a1, gate, cnt, off)
    return xs, gs, a1, a1t, cnt


def _expert_kernel(be_ref, ps_ref, nb_ref, src_ref, xs_hbm, wg_ref, wu_ref, wd_ref, ys_hbm,
                   xbuf, ybuf, wgu_bf, wd_bf, in_sem, out_sem, *, nblk):
    b = pl.program_id(0)
    slot = lax.rem(b, 2)

    def in_copy(src_chunk, sl, c):
        return pltpu.make_async_copy(xs_hbm.at[src_chunk], xbuf.at[sl, c], in_sem.at[sl])

    def out_copy(dst_chunk, sl, c):
        return pltpu.make_async_copy(ybuf.at[sl, c], ys_hbm.at[dst_chunk], out_sem.at[sl])

    def for_chunks(blk, fn):
        p0 = ps_ref[blk]
        n = nb_ref[blk]
        for c0 in range(0, MOE_CB, MOE_SEG):

            @pl.when(n >= c0 + MOE_SEG)
            def _():
                for c in range(c0, c0 + MOE_SEG):
                    fn(src_ref[p0 + c], c)

            @pl.when((n > c0) & (n < c0 + MOE_SEG))
            def _():
                def body(c, carry):
                    fn(src_ref[p0 + c], c)
                    return carry

                lax.fori_loop(c0, n, body, 0)

    @pl.when(b == 0)
    def _():
        xbuf[...] = jnp.zeros_like(xbuf)
        for_chunks(0, lambda s, c: in_copy(s, 0, c).start())

    @pl.when(b + 1 < nblk)
    def _():
        for_chunks(b + 1, lambda s, c: in_copy(s, 1 - slot, c).start())

    for_chunks(b, lambda s, c: in_copy(s, slot, c).wait())

    @pl.when(b >= 2)
    def _():
        for_chunks(b - 2, lambda s, c: out_copy(s, slot, c).wait())

    @pl.when((b == 0) | (be_ref[b] != be_ref[jnp.maximum(b - 1, 0)]))
    def _():
        wgu_bf[:, :EXPERT_FF] = wg_ref[...].astype(BF16)
        wgu_bf[:, EXPERT_FF:] = wu_ref[...].astype(BF16)
        wd_bf[...] = wd_ref[...].astype(BF16)

    for c0 in range(0, MOE_CB, MOE_SEG):

        @pl.when(nb_ref[b] > c0)
        def _():
            x = jnp.concatenate([xbuf[slot, c0:c0 + MOE_SEG, j].reshape(MOE_SEG * MOE_CH, LANES)
                                 for j in range(MOE_NL)], axis=1)
            hid = _dot(x, wgu_bf[...])
            act = _silu(hid[:, :EXPERT_FF]) * hid[:, EXPERT_FF:]
            y = _dot(act, wd_bf[...]).astype(BF16)
            for j in range(MOE_NL):
                ybuf[slot, c0:c0 + MOE_SEG, j] = y[:, j * LANES:(j + 1) * LANES].reshape(MOE_SEG, MOE_CH, LANES)

    for_chunks(b, lambda s, c: out_copy(s, slot, c).start())

    @pl.when(b == nblk - 1)
    def _():
        for_chunks(b, lambda s, c: out_copy(s, slot, c).wait())

        @pl.when(b >= 1)
        def _():
            for_chunks(b - 1, lambda s, c: out_copy(s, 1 - slot, c).wait())


def _experts(xs, wg, wu, wd, layer, be, pstart, nbc, src):
    nt = xs.shape[0]
    nblk = be.shape[0]
    xs_c = xs.reshape(nt * MOE_CPT, MOE_NL, MOE_CH, LANES)
    by_expert = lambda b, be, ps, nb, src: (layer, be[b], 0, 0)
    grid_spec = pltpu.PrefetchScalarGridSpec(
        num_scalar_prefetch=4,
        grid=(nblk,),
        in_specs=[pl.BlockSpec(memory_space=pl.ANY),
                  pl.BlockSpec((None, None, D_MODEL, EXPERT_FF), by_expert),
                  pl.BlockSpec((None, None, D_MODEL, EXPERT_FF), by_expert),
                  pl.BlockSpec((None, None, EXPERT_FF, D_MODEL), by_expert)],
        out_specs=pl.BlockSpec(memory_space=pl.ANY),
        scratch_shapes=[pltpu.VMEM((2, MOE_CB, MOE_NL, MOE_CH, LANES), BF16),
                        pltpu.VMEM((2, MOE_CB, MOE_NL, MOE_CH, LANES), BF16),
                        pltpu.VMEM((D_MODEL, 2 * EXPERT_FF), BF16),
                        pltpu.VMEM((EXPERT_FF, D_MODEL), BF16),
                        pltpu.SemaphoreType.DMA((2,)), pltpu.SemaphoreType.DMA((2,))],
    )
    ys = pl.pallas_call(
        functools.partial(_expert_kernel, nblk=nblk),
        grid_spec=grid_spec,
        out_shape=jax.ShapeDtypeStruct(xs_c.shape, BF16),
        input_output_aliases={4: 0},
        compiler_params=_params("arbitrary"),
        name="moe_experts",
    )(be, pstart, nbc, src, xs_c, wg, wu, wd)
    return ys.reshape(xs.shape)


def _combine_kernel(h_ref, ys_ref, gs_ref, a1_ref, a1t_ref, wgu_ref, wd_ref, vec_ref, o_ref):
    E = N_EXPERTS
    h = h_ref[...]
    ch_col, off_col, _, _ = _dispatch_geometry(a1_ref[...])
    si = lax.broadcasted_iota(jnp.int32, (E, MOE_SLOTS), 1).astype(F32)
    lo = off_col * MOE_CH
    member_t = (si >= lo) & (si < lo + ch_col * MOE_CH)
    seg0 = jnp.sum(jnp.where(member_t, lo, 0.0), axis=0, keepdims=True)
    rs1 = lax.broadcasted_iota(jnp.int32, (1, MOE_SLOTS), 1).astype(F32) - seg0 + 1.0
    perm_t = _dot(a1t_ref[...], member_t) == rs1
    ys = jnp.concatenate([ys_ref[:, j].reshape(MOE_SLOTS, LANES) for j in range(MOE_NL)], axis=1)
    eye = (lax.broadcasted_iota(jnp.int32, (LANES, LANES), 0)
           == lax.broadcasted_iota(jnp.int32, (LANES, LANES), 1))
    gs = jnp.concatenate([jnp.sum(jnp.where(eye, gs_ref[0:1, s0:s0 + LANES], 0.0), axis=1, keepdims=True)
                          for s0 in range(0, MOE_SLOTS, LANES)], axis=0)
    routed = _dot(perm_t.astype(BF16), ys.astype(F32) * gs)
    xb = h.astype(BF16)
    hid = _dot(xb, wgu_ref[...])
    shared = _dot(_silu(hid[:, :EXPERT_FF]) * hid[:, EXPERT_FF:], wd_ref[...])
    o_ref[...] = _layer_norm(DEEPNORM_ALPHA * h + (shared + routed), vec_ref[0:1, :], vec_ref[1:2, :])


def _combine(h2d, ys, gs, a1, a1t, wgu_s, wd_s, vec):
    T, D = h2d.shape
    nt = T // TM
    return pl.pallas_call(
        _combine_kernel,
        grid=(nt,),
        in_specs=[pl.BlockSpec((TM, D), lambda i: (i, 0)),
                  pl.BlockSpec((None, MOE_CPT, MOE_NL, MOE_CH, LANES), lambda i: (i, 0, 0, 0, 0)),
                  pl.BlockSpec((None, SUBLANES, MOE_SLOTS), lambda i: (i, 0, 0)),
                  pl.BlockSpec((None, N_EXPERTS, TM), lambda i: (i, 0, 0)),
                  pl.BlockSpec((None, TM, N_EXPERTS), lambda i: (i, 0, 0)),
                  _full(wgu_s.shape), _full(wd_s.shape), _full(vec.shape)],
        out_specs=pl.BlockSpec((TM, D), lambda i: (i, 0)),
        out_shape=jax.ShapeDtypeStruct((T, D), F32),
        compiler_params=_params("parallel"),
        name="moe_combine",
    )(h2d, ys, gs, a1, a1t, wgu_s, wd_s, vec)


def _dispatch_tables(chunks):
    nt, E = chunks.shape
    maxc = nt * MOE_CPT
    nblk = -(-maxc // MOE_CB) + E
    hp = lax.Precision.HIGHEST
    ch = chunks.astype(F32)
    off = jnp.cumsum(ch, axis=1) - ch
    wend = jnp.cumsum(ch, axis=0)
    cnt_e = wend[-1]
    cend = jnp.cumsum(cnt_e)
    base_e = cend - cnt_e
    srcbase = jnp.arange(nt, dtype=F32)[:, None] * MOE_CPT + off - (wend - ch)
    p = jnp.arange(maxc, dtype=F32)
    e_p = jnp.minimum(jnp.sum((cend[None, :] <= p[:, None]).astype(jnp.int32), axis=1), E - 1)
    oh_e = (e_p[:, None] == jnp.arange(E, dtype=jnp.int32)[None, :]).astype(F32)
    look = jnp.dot(oh_e, jnp.concatenate([wend.T, srcbase.T, base_e[:, None]], axis=1), precision=hp)
    q = p - look[:, 2 * nt]
    i_p = jnp.sum((look[:, :nt] <= q[:, None]).astype(jnp.int32), axis=1)
    oh_i = i_p[:, None] == jnp.arange(nt, dtype=jnp.int32)[None, :]
    src = jnp.sum(jnp.where(oh_i, look[:, nt:2 * nt], 0.0), axis=1) + q
    src = jnp.clip(src, 0, maxc - 1).astype(jnp.int32)
    nblk_e = jnp.floor((cnt_e + (MOE_CB - 1)) * (1.0 / MOE_CB))
    bend = jnp.cumsum(nblk_e)
    bidx = jnp.arange(nblk, dtype=F32)
    be = jnp.minimum(jnp.sum((bend[None, :] <= bidx[:, None]).astype(jnp.int32), axis=1), E - 1)
    oh_b = (be[:, None] == jnp.arange(E, dtype=jnp.int32)[None, :]).astype(F32)
    lookb = jnp.dot(oh_b, jnp.stack([bend - nblk_e, cnt_e, base_e], axis=1), precision=hp)
    q0 = (bidx - lookb[:, 0]) * MOE_CB
    nbc = jnp.clip(lookb[:, 1] - q0, 0, MOE_CB).astype(jnp.int32)
    pstart = jnp.clip(lookb[:, 2] + q0, 0, maxc - MOE_CB).astype(jnp.int32)
    return be.astype(jnp.int32), pstart, nbc, src


def _moe_layer(h2d, layer, router, bias, w_gate, w_up, w_down, sh_gate, sh_up, sh_down, ln_g, ln_b):
    router_t = router.T.astype(F32)
    bias_col = jnp.broadcast_to(bias.astype(F32)[:, None], (N_EXPERTS, LANES))
    xs, gs, a1, a1t, cnt = _router(h2d, router_t, bias_col)
    be, pstart, nbc, src = _dispatch_tables(cnt[:, 0, :].astype(jnp.int32))
    ys = _experts(xs, w_gate, w_up, w_down, layer, be, pstart, nbc, src)
    wgu_s = jnp.concatenate([sh_gate, sh_up], axis=-1).astype(BF16)
    vec = jnp.zeros((SUBLANES, D_MODEL), F32).at[0].set(ln_g).at[1].set(ln_b)
    return _combine(h2d, ys, gs, a1, a1t, wgu_s, sh_down.astype(BF16), vec)


def _rows(*vs):
    out = jnp.zeros((SUBLANES, vs[0].shape[-1]), F32)
    for i, v in enumerate(vs):
        out = out.at[i].set(v.astype(F32))
    return out


def kernel(x, meta, rel_bias, rwkv_mu, rwkv_w0, rwkv_w1, rwkv_w2, rwkv_a0, rwkv_a1, rwkv_a2, rwkv_g1, rwkv_g2, rwkv_k_k, rwkv_k_a, rwkv_r_k, rwkv_w_r, rwkv_w_k, rwkv_w_v, rwkv_w_o, rwkv_lnx_g, rwkv_lnx_b, attn_w_qkv, attn_b_qkv, attn_sinks, attn_w_o, attn_b_o, ln_mix_g, ln_mix_b, ln_ffn_g, ln_ffn_b, moe_router, moe_bias, moe_w_gate, moe_w_up, moe_w_down, shared_w_gate, shared_w_up, shared_w_down):
    B, S, D = x.shape
    assert D == D_MODEL and S % ATT_BLOCK == 0 and (FRONT + N_META) % WKV_CHUNK == 0
    lp = FRONT + N_META + S
    T = B * lp
    assert T % TM == 0 and lp >= TM
    h = jnp.concatenate([jnp.zeros((B, FRONT, D), x.dtype),
                         jnp.broadcast_to(meta[None].astype(x.dtype), (B, N_META, D)), x], axis=1)
    h = h.reshape(T, D)
    bf = lambda w: w.astype(BF16)

    H, N = RWKV_HEADS, RWKV_HEAD
    head_of = jnp.arange(D) // N
    gsum = (head_of[:, None] == jnp.arange(LANES)[None, :]).astype(BF16)
    gexp = gsum.T
    r, w, k, v, kk, b, g = _rwkv_proj(
        h, lp, _rows(*rwkv_mu[0]), _rows(rwkv_w0[0], rwkv_a0[0], rwkv_k_k[0], rwkv_k_a[0]),
        bf(rwkv_w_r[0]), bf(rwkv_w_k[0]), bf(rwkv_w_v[0]), bf(rwkv_w1[0]), bf(rwkv_w2[0]),
        bf(rwkv_a1[0]), bf(rwkv_a2[0]), bf(rwkv_g1[0]), bf(rwkv_g2[0]), gsum, gexp)
    to3 = lambda t: t.reshape(B, lp, D)
    prm = _rows(rwkv_r_k[0].reshape(D), rwkv_lnx_g[0], rwkv_lnx_b[0])
    o = _wkv(to3(r), to3(w), to3(k), to3(v), to3(kk), to3(b), to3(g), prm)
    h = _proj_ln(o.reshape(T, D), bf(rwkv_w_o[0]), _rows(jnp.zeros((D,), F32), ln_mix_g[0], ln_mix_b[0]), h)
    h = _moe_layer(h, 0, moe_router[0], moe_bias[0], moe_w_gate, moe_w_up, moe_w_down,
                   shared_w_gate[0], shared_w_up[0], shared_w_down[0], ln_ffn_g[0], ln_ffn_b[0])

    HD, KV = ATT_HEAD_DIM, ATT_KV_HEADS
    qw = ATT_HEADS * HD
    wqkv, bqkv = attn_w_qkv[0], attn_b_qkv[0]
    dup = lambda t: jnp.concatenate([t.reshape(-1, KV, 1, HD)] * 2, axis=2).reshape(t.shape[0], 2 * KV * HD)
    wq, wkd, wvd = wqkv[:, :qw], dup(wqkv[:, qw:qw + KV * HD]), dup(wqkv[:, qw + KV * HD:])
    bq, bkd, bvd = bqkv[None, :qw], dup(bqkv[None, qw:qw + KV * HD]), dup(bqkv[None, qw + KV * HD:])
    q, kd, vd = _qkv(h, bf(wq), bf(wkd), bf(wvd), bq.astype(F32), bkd.astype(F32), bvd.astype(F32))
    bband, bmeta, bmm = _bias_tables(rel_bias, S // ATT_BLOCK)
    o = _attention(q.reshape(B, lp, qw), kd.reshape(B, lp, -1), vd.reshape(B, lp, -1),
                   attn_sinks[0].astype(F32), bband, bmeta, bmm)
    h = _proj_ln(o.reshape(T, D), bf(attn_w_o[0]), _rows(attn_b_o[0], ln_mix_g[1], ln_mix_b[1]), h)
    h = _moe_layer(h, 1, moe_router[1], moe_bias[1], moe_w_gate, moe_w_up, moe_w_down,
                   shared_w_gate[1], shared_w_up[1], shared_w_down[1], ln_ffn_g[1], ln_ffn_b[1])
    return h.reshape(B, lp, D)[:, FRONT + N_META:]
```

```python
import functools
import math

import jax
import jax.numpy as jnp
from jax import lax
from jax.experimental import pallas as pl
from jax.experimental.pallas import tpu as pltpu

F32 = jnp.float32
BF16 = jnp.bfloat16

D_MODEL = 1024
DEPTH = 2
N_META = 16
RWKV_HEAD = 64
RWKV_HEADS = D_MODEL // RWKV_HEAD
GN_EPS = 64e-5
ATT_HEADS = 16
ATT_KV_HEADS = 4
ATT_HEAD_DIM = D_MODEL // ATT_HEADS
ATT_GROUP = ATT_HEADS // ATT_KV_HEADS
WINDOW = 128
ATT_BLOCK = 128
N_BUCKETS = 32
MAX_DISTANCE = 128
N_EXPERTS = 64
TOP_K = 8
N_GROUPS = 8
TOPK_GROUPS = 4
EXPERT_FF = 256
ROUTED_SCALE = 2.5
DEEPNORM_ALPHA = (2 * DEPTH) ** 0.25
LN_EPS = 1e-5

LANES = 128
SUBLANES = 8
BF16_ROWS = 16
VMEM_LIMIT = 56 * 1024 * 1024

FRONT = 48
TM = 256
WKV_CHUNK = 64
WKV_PAIRS = 8
WKV_BATCH = 2
WKV_SUB = 16
ATT_STEP_KV = 2
ATT_STEP_PAIRS = ATT_STEP_KV * ATT_GROUP // 2
ROW_ALIGN = FRONT + N_META
MOE_CH = BF16_ROWS
MOE_SLOTS = TM * TOP_K + N_EXPERTS * MOE_CH
MOE_CPT = MOE_SLOTS // MOE_CH
MOE_NL = D_MODEL // LANES
MOE_CB = 128
MOE_SEG = 32
MOE_SLOT_SPLITS = 3
NEG = -1e30

_NN = (((1,), (0,)), ((), ()))
_NT = (((1,), (1,)), ((), ()))


def _dot(a, b, dn=_NN):
    return lax.dot_general(a.astype(BF16), b.astype(BF16), dn, preferred_element_type=F32)


def _split2(x):
    hi = x.astype(BF16)
    lo = (x - hi.astype(F32)).astype(BF16)
    return hi, lo


def _split3(x):
    h1 = x.astype(BF16)
    r1 = x - h1.astype(F32)
    h2 = r1.astype(BF16)
    h3 = (r1 - h2.astype(F32)).astype(BF16)
    return h1, h2, h3


def _dot_exact_lhs(a01, b, dn=_NN):
    a = a01.astype(BF16)
    return sum(lax.dot_general(a, p, dn, preferred_element_type=F32) for p in _split3(b))


def _dot_hilo_lhs(a01, b, dn=_NN):
    a = a01.astype(BF16)
    return sum(lax.dot_general(a, p, dn, preferred_element_type=F32) for p in _split2(b))


def _dot_hilo_rhs(a, b01, dn=_NN):
    b = b01.astype(BF16)
    return sum(lax.dot_general(p, b, dn, preferred_element_type=F32) for p in _split2(a))


def _dot3(a, b, dn=_NN):
    ah, al = _split2(a)
    bh, bl = _split2(b)
    d = lambda x, y: lax.dot_general(x, y, dn, preferred_element_type=F32)
    return d(ah, bh) + (d(ah, bl) + d(al, bh))


def _sigmoid(x):
    return 1.0 / (1.0 + jnp.exp(-x))


def _silu(x):
    return x * _sigmoid(x)


def _layer_norm(x, g, b):
    mu = jnp.mean(x, axis=-1, keepdims=True)
    xc = x - mu
    var = jnp.mean(xc * xc, axis=-1, keepdims=True)
    return xc * lax.rsqrt(var + LN_EPS) * g + b


def _full(shape):
    nd = len(shape)
    return pl.BlockSpec(shape, lambda *_: (0,) * nd)


def _params(*sem):
    return pltpu.CompilerParams(dimension_semantics=sem, vmem_limit_bytes=VMEM_LIMIT)


def _rwkv_proj_kernel(x_ref, xp_ref, mu_ref, vec_ref, wr_ref, wk_ref, wv_ref, w1_ref, w2_ref,
                      a1_ref, a2_ref, g1_ref, g2_ref, gsum_ref, gexp_ref,
                      r_out, w_out, k_out, v_out, kk_out, b_out, g_out, *, lp):
    i = pl.program_id(0)
    x = x_ref[...]
    rows = x.shape[0]
    row = lax.broadcasted_iota(jnp.int32, (rows, 1), 0)
    pos = lax.rem(i * rows, lp) + row
    pos = jnp.where(pos >= lp, pos - lp, pos)
    prev = jnp.where(row == 0, xp_ref[SUBLANES - 1:SUBLANES, :], pltpu.roll(x, 1, axis=0))
    prev = jnp.where(pos == FRONT, 0.0, prev)
    xx = prev - x
    valid = pos >= FRONT

    def mix(j):
        return (x + xx * mu_ref[j:j + 1, :]).astype(BF16)

    w0, a0, k_k, k_a = (vec_ref[j:j + 1, :] for j in range(4))
    r = _dot(mix(0), wr_ref[...])
    z = w0 + _dot(jnp.tanh(_dot(mix(1), w1_ref[...])), w2_ref[...])
    w = -math.exp(-0.5) * _sigmoid(z)
    k = _dot(mix(2), wk_ref[...])
    v = _dot(mix(3), wv_ref[...])
    a = _sigmoid(a0 + _dot(_dot(mix(4), a1_ref[...]), a2_ref[...]))
    g = _dot(_sigmoid(_dot(mix(5), g1_ref[...])), g2_ref[...])
    kk = k * k_k
    ssq = _dot_hilo_rhs(kk * kk, gsum_ref[...])
    kk = kk * lax.rsqrt(jnp.maximum(_dot_hilo_rhs(ssq, gexp_ref[...]), 1e-24))
    k = k * (1.0 + (a - 1.0) * k_a)
    r_out[...] = r
    w_out[...] = jnp.where(valid, w, 0.0)
    k_out[...] = jnp.where(valid, k, 0.0)
    v_out[...] = jnp.where(valid, v, 0.0)
    kk_out[...] = jnp.where(valid, kk, 0.0)
    b_out[...] = jnp.where(valid, kk * a, 0.0)
    g_out[...] = g


def _rwkv_proj(h2d, lp, mu, vec, wr, wk, wv, w1, w2, a1, a2, g1, g2, gsum, gexp):
    T, D = h2d.shape
    tp = _proj_rows(T)
    assert tp <= lp
    row_spec = pl.BlockSpec((tp, D), lambda i: (i, 0))
    prev_spec = pl.BlockSpec((SUBLANES, D), lambda i: (jnp.maximum(i * (tp // SUBLANES) - 1, 0), 0))
    ws = [mu, vec, wr, wk, wv, w1, w2, a1, a2, g1, g2, gsum, gexp]
    once = lambda w: pl.BlockSpec(w.shape, lambda i: (0,) * w.ndim, pipeline_mode=pl.Buffered(1))
    return pl.pallas_call(
        functools.partial(_rwkv_proj_kernel, lp=lp),
        grid=(T // tp,),
        in_specs=[row_spec, prev_spec] + [once(w) for w in ws],
        out_specs=[row_spec] * 7,
        out_shape=[jax.ShapeDtypeStruct((T, D), F32)] * 7,
        compiler_params=_params("parallel"),
        name="rwkv_proj",
    )(h2d, h2d, *ws)


def _wkv_chunk(r, w, cum, k, v, kk, b, S, c):
    C = WKV_CHUNK
    P = range(len(r))
    bf = lambda xs: [x.astype(BF16) for x in xs]
    each = lambda f, *ls: [f(*a) for a in zip(*ls)]

    def stack(x):
        return jnp.concatenate([jnp.where(c["m0"], x, 0.0), jnp.where(c["m0"], 0.0, x)], axis=0)

    f32 = lambda xs: [x.astype(F32) for x in xs]
    cat0 = lambda x, y: jnp.concatenate([x, y], axis=0)
    cat1 = lambda x, y: jnp.concatenate([x, y], axis=1)
    nt = lambda x, y: _dot(x, y, _NT)
    C2 = 2 * C

    tot = [cum[p][C - 1:C, :] for p in P]
    inv = [jnp.exp(-cum[p]) for p in P]
    dend = [jnp.exp(tot[p] - cum[p]) for p in P]
    kr_s = bf([cat0(stack(kk[p] * jnp.exp(cum[p] - w[p])), stack(r[p] * jnp.exp(cum[p]))) for p in P])
    bk_s = bf([cat0(stack(b[p] * inv[p]), stack(k[p] * inv[p])) for p in P])
    bh_s = bf([stack(b[p] * dend[p]) for p in P])
    kh_s = bf([stack(k[p] * dend[p]) for p in P])
    vs_t = bf([stack(v[p]).T for p in P])

    sc = each(nt, bk_s, kr_s)
    lab_t = [jnp.where(c["upper"], x[:C2, :C2], 0.0) for x in sc]
    arb_t = bf([jnp.where(c["upinc"], x[:C2, C2:], 0.0) for x in sc])
    lak_t = bf([jnp.where(c["upper"], x[C2:, :C2], 0.0) for x in sc])
    ark_t = bf([jnp.where(c["upinc"], x[C2:, C2:], 0.0) for x in sc])

    eye = c["eye"]
    ud = [jnp.where(c["blk"], x, 0.0) for x in lab_t]
    nu = bf([x - y for x, y in zip(lab_t, ud)])
    udb = bf(ud)
    u2b = bf(each(_dot, udb, udb))
    u2 = f32(u2b)
    t1 = each(_dot, bf([eye + x for x in u2]), [cat1(x, (eye - y).astype(BF16)) for x, y in zip(u2b, ud)])
    u4b = bf([x[:, :C2] - y for x, y in zip(t1, u2)])
    u4 = f32(u4b)
    t2 = each(_dot, bf([eye + x for x in u4]), [cat1(x, y[:, C2:].astype(BF16)) for x, y in zip(u4b, t1)])
    u8 = [x[:, :C2] - y for x, y in zip(t2, u4)]
    dinv = each(_dot, bf([eye + x for x in u8]), bf([x[:, C2:] for x in t2]))
    dinvb = bf(dinv)
    wzb = bf(each(_dot, dinvb, nu))
    t3 = each(_dot, wzb, [cat1(x, y) for x, y in zip(wzb, dinvb)])
    tt_t = bf(each(_dot, bf([eye + x[:, :C2] for x in t3]), bf([y - x[:, C2:] for x, y in zip(t3, dinv)])))

    sb = bf(S)
    p2 = each(_dot, vs_t, [cat1(x, y) for x, y in zip(lak_t, ark_t)])
    p3 = each(_dot, vs_t, kh_s)
    p1 = each(nt, sb, kr_s)
    u_t = bf(each(_dot, bf([-(x[:, :C2] + y[:, :C2]) for x, y in zip(p1, p2)]), tt_t))
    p4 = each(_dot, u_t, [cat1(x, y) for x, y in zip(arb_t, bh_s)])
    s_new = [S[p] * jnp.exp(tot[p]) + p4[p][:, C2:] + p3[p] for p in P]
    y_t = [p1[p][:, C2:] + p4[p][:, :C2] + p2[p][:, C2:] for p in P]
    return y_t, s_new


def _wkv_group_norm(y_t, hblk):
    C = WKV_CHUNK
    mean = jnp.sum(y_t, axis=0, keepdims=True) * (1.0 / RWKV_HEAD)
    yc = jnp.where(hblk, y_t - mean, 0.0)
    var = jnp.sum(yc * yc, axis=0, keepdims=True) * (1.0 / RWKV_HEAD)
    ys = (yc * lax.rsqrt(var + GN_EPS)).T
    return ys[:C] + ys[C:]


def _wkv_kernel(r_ref, w_ref, k_ref, v_ref, kk_ref, b_ref, g_ref, prm_ref, o_ref, yt_ref, *s_refs, lb):
    C = WKV_CHUNK

    @pl.when(pl.program_id(2) == 0)
    def _():
        for s_ref in s_refs:
            s_ref[...] = jnp.zeros_like(s_ref)

    ri = lax.broadcasted_iota(jnp.int32, (2 * C, 2 * C), 0)
    ci = lax.broadcasted_iota(jnp.int32, (2 * C, 2 * C), 1)
    lane = lax.broadcasted_iota(jnp.int32, (1, LANES), 1)
    ti = lax.broadcasted_iota(jnp.int32, (C, C), 0)
    tj = lax.broadcasted_iota(jnp.int32, (C, C), 1)
    consts = dict(
        m0=lane < RWKV_HEAD,
        upper=ri < ci,
        upinc=ri <= ci,
        blk=(ri // WKV_SUB) == (ci // WKV_SUB),
        hblk=(ri // RWKV_HEAD) == (ci // C),
        eye=(ri == ci).astype(F32),
    )
    m0 = consts["m0"]
    tril = (ti >= tj).astype(BF16)

    def head_sum(x):
        s0 = jnp.sum(jnp.where(m0, x, 0.0), axis=-1, keepdims=True)
        s1 = jnp.sum(jnp.where(m0, 0.0, x), axis=-1, keepdims=True)
        return jnp.where(m0, s0, s1)

    nbat = r_ref.shape[0]
    items = [(bi, slice(p * LANES, (p + 1) * LANES)) for bi in range(nbat) for p in range(WKV_PAIRS)]

    def recurrence(ci_):
        rows = pl.ds(pl.multiple_of(ci_ * C, C), C)
        ld = lambda ref: [ref[bi, rows, ln] for bi, ln in items]
        r, w, k, v, kk, b = ld(r_ref), ld(w_ref), ld(k_ref), ld(v_ref), ld(kk_ref), ld(b_ref)
        cum_all = [_dot_hilo_lhs(tril, w_ref[bi, rows, :]) for bi in range(nbat)]
        cum = [cum_all[bi][:, ln] for bi, ln in items]
        ys, s_new = _wkv_chunk(r, w, cum, k, v, kk, b, [s[...] for s in s_refs], consts)
        for i, (bi, ln) in enumerate(items):
            s_refs[i][...] = s_new[i]
            yt_ref[bi, :, ln] = ys[i]

    def finish(ci_):
        rows = pl.ds(pl.multiple_of(ci_ * C, C), C)
        for bi, ln in items:
            r_k, lg, lb_ = prm_ref[0:1, ln], prm_ref[1:2, ln], prm_ref[2:3, ln]
            yn = _wkv_group_norm(yt_ref[bi, :, ln], consts["hblk"])
            bonus = head_sum(r_ref[bi, rows, ln] * k_ref[bi, rows, ln] * r_k) * v_ref[bi, rows, ln]
            o_ref[bi, rows, ln] = ((yn * lg + lb_ + bonus) * g_ref[bi, rows, ln]).astype(o_ref.dtype)

    def chunk(ci_, carry):
        finish(ci_ - 1)
        recurrence(ci_)
        return carry

    recurrence(0)
    lax.fori_loop(1, lb // C, chunk, 0)
    finish(lb // C - 1)


def _wkv_row_block(lp, nbat):
    nch = lp // WKV_CHUNK
    for d in ((11, 8, 6, 4, 3, 2, 1) if nbat == 1 else (4, 3, 2, 1)):
        if nch % d == 0:
            return d * WKV_CHUNK
    return WKV_CHUNK


def _wkv(r, w, k, v, kk, b, g, prm):
    B, lp, D = r.shape
    nbat = WKV_BATCH if B % WKV_BATCH == 0 else 1
    lb = _wkv_row_block(lp, nbat)
    wl = WKV_PAIRS * LANES
    spec = pl.BlockSpec((nbat, lb, wl), lambda bi, pi, li: (bi, li, pi))
    return pl.pallas_call(
        functools.partial(_wkv_kernel, lb=lb),
        grid=(B // nbat, D // wl, lp // lb),
        in_specs=[spec] * 7 + [pl.BlockSpec((SUBLANES, wl), lambda bi, pi, li: (0, pi))],
        out_specs=spec,
        out_shape=jax.ShapeDtypeStruct((B, lp, D), BF16),
        scratch_shapes=[pltpu.VMEM((nbat, LANES, wl), F32)]
        + [pltpu.VMEM((LANES, LANES), F32)] * (WKV_PAIRS * nbat),
        compiler_params=_params("parallel", "parallel", "arbitrary"),
        name="wkv7",
    )(r, w, k, v, kk, b, g, prm)


def _proj_ln_kernel(a_ref, w_ref, vec_ref, h_ref, o_ref):
    mix = _dot(a_ref[...], w_ref[...]) + vec_ref[0:1, :]
    o_ref[...] = _layer_norm(DEEPNORM_ALPHA * h_ref[...] + mix, vec_ref[1:2, :], vec_ref[2:3, :])


def _proj_rows(T):
    return 2 * TM if T % (2 * TM) == 0 else TM


def _proj_ln(a2d, w, vec, h2d):
    T, D = h2d.shape
    K = a2d.shape[1]
    tp = _proj_rows(T)
    return pl.pallas_call(
        _proj_ln_kernel,
        grid=(T // tp,),
        in_specs=[pl.BlockSpec((tp, K), lambda i: (i, 0)), _full(w.shape), _full(vec.shape),
                  pl.BlockSpec((tp, D), lambda i: (i, 0))],
        out_specs=pl.BlockSpec((tp, D), lambda i: (i, 0)),
        out_shape=jax.ShapeDtypeStruct((T, D), F32),
        compiler_params=_params("parallel"),
        name="proj_ln",
    )(a2d, w, vec, h2d)


def _qkv_kernel(x_ref, wq_ref, wk_ref, wv_ref, bq_ref, bk_ref, bv_ref, q_out, k_out, v_out):
    x = x_ref[...].astype(BF16)
    q_out[...] = ((_dot(x, wq_ref[...]) + bq_ref[...]) * ATT_HEAD_DIM ** -0.5).astype(BF16)
    k_out[...] = (_dot(x, wk_ref[...]) + bk_ref[...]).astype(BF16)
    v_out[...] = (_dot(x, wv_ref[...]) + bv_ref[...]).astype(BF16)


def _qkv(h2d, wq, wkd, wvd, bq, bkd, bvd):
    T, D = h2d.shape
    nq, nk = wq.shape[1], wkd.shape[1]
    ws = [wq, wkd, wvd, bq, bkd, bvd]
    tp = _proj_rows(T)
    return pl.pallas_call(
        _qkv_kernel,
        grid=(T // tp,),
        in_specs=[pl.BlockSpec((tp, D), lambda i: (i, 0))] + [_full(w.shape) for w in ws],
        out_specs=[pl.BlockSpec((tp, nq), lambda i: (i, 0)), pl.BlockSpec((tp, nk), lambda i: (i, 0)),
                   pl.BlockSpec((tp, nk), lambda i: (i, 0))],
        out_shape=[jax.ShapeDtypeStruct((T, nq), BF16), jax.ShapeDtypeStruct((T, nk), BF16),
                   jax.ShapeDtypeStruct((T, nk), BF16)],
        compiler_params=_params("parallel"),
        name="qkv_proj",
    )(h2d, *ws)


def _attn_kernel(sink_ref, q_ref, k_ref, v_ref, bband_ref, bmeta_ref, bmm_ref, o_ref, *, nb):
    c = pl.program_id(0)
    BLK = ATT_BLOCK
    M0 = FRONT
    lane = lax.broadcasted_iota(jnp.int32, (1, LANES), 1)
    m0 = lane < ATT_HEAD_DIM
    qi = lax.broadcasted_iota(jnp.int32, (2 * BLK, BLK), 0) % BLK
    sj = lax.broadcasted_iota(jnp.int32, (2 * BLK, BLK), 1)
    cur_vis = sj <= qi
    prev_vis = sj > qi
    half = lax.broadcasted_iota(jnp.int32, (2 * BLK, 1), 0) < BLK

    def stack(x):
        z = jnp.zeros_like(x)
        return jnp.concatenate([jnp.where(m0, x, z), jnp.where(m0, z, x)], axis=0)

    HP = range(ATT_STEP_PAIRS)
    ppk = ATT_GROUP // 2
    kl = [slice((hp // ppk) * LANES, (hp // ppk + 1) * LANES) for hp in HP]
    ql = [slice(hp * LANES, (hp + 1) * LANES) for hp in HP]

    def rowmax(lgs, floor):
        by_width = {}
        for lg in lgs:
            by_width.setdefault(lg.shape[1], []).append(lg)
        tops = [jnp.max(functools.reduce(jnp.maximum, g), axis=-1, keepdims=True) for g in by_width.values()]
        return functools.reduce(jnp.maximum, tops, floor)

    def attend(segs, sinks, n):
        it = range(len(segs))
        mx = [rowmax([lg for lg, _ in segs[i]], sinks[i]) for i in it]
        ps = [[jnp.exp(lg - mx[i]).astype(BF16) for lg, _ in segs[i]] for i in it]
        one = jnp.ones((1, LANES), BF16)
        va = [[jnp.where(m0, vals, one) for _, vals in segs[i]] for i in it]
        vb = [[jnp.where(m0, one, vals) for _, vals in segs[i]] for i in it]
        oa = [sum(_dot(p[:n], v) for p, v in zip(ps[i], va[i])) for i in it]
        ob = [sum(_dot(p[n:], v) for p, v in zip(ps[i], vb[i])) for i in it]
        out = []
        for i in it:
            st = jnp.exp(sinks[i] - mx[i])
            den = pltpu.roll(jnp.where(m0, ob[i], oa[i]), ATT_HEAD_DIM, axis=1)
            out.append(jnp.where(m0, oa[i], ob[i]) / (den + jnp.where(m0, st[:n], st[n:])))
        return out

    o_ref[0:M0, :] = jnp.zeros((M0, o_ref.shape[1]), o_ref.dtype)
    s0 = [sink_ref[(c * ATT_STEP_PAIRS + hp) * 2] for hp in HP]
    s1 = [sink_ref[(c * ATT_STEP_PAIRS + hp) * 2 + 1] for hp in HP]
    k_meta = [k_ref[M0:M0 + N_META, kl[hp]] for hp in HP]
    v_meta = [v_ref[M0:M0 + N_META, kl[hp]] for hp in HP]

    first = lax.broadcasted_iota(jnp.int32, (2 * N_META, 1), 0) < N_META
    mi = lax.broadcasted_iota(jnp.int32, (2 * N_META, N_META), 0) % N_META
    mj = lax.broadcasted_iota(jnp.int32, (2 * N_META, N_META), 1)
    qm = [stack(q_ref[M0:M0 + N_META, ql[hp]]) for hp in HP]
    lg = [jnp.where(mj <= mi, _dot(qm[hp], k_meta[hp], _NT) + bmm_ref[hp], NEG) for hp in HP]
    om = attend([[(lg[hp], v_meta[hp])] for hp in HP], [jnp.where(first, s0[hp], s1[hp]) for hp in HP], N_META)
    for hp in HP:
        o_ref[M0:M0 + N_META, ql[hp]] = om[hp].astype(o_ref.dtype)

    sink_q = [jnp.where(half, s0[hp], s1[hp]) for hp in HP]

    unroll = 2 if nb % 2 == 0 else 1

    def blocks(jo, carry):
        items = [(jo * unroll + u, hp) for u in range(unroll) for hp in HP]
        start = [pl.multiple_of(ROW_ALIGN + j * BLK, ROW_ALIGN) for j, _ in items]
        pstart = [pl.multiple_of(jnp.maximum(s - BLK, 0), ROW_ALIGN) for s in start]
        it = range(len(items))
        qs = [stack(q_ref[pl.ds(start[i], BLK), ql[items[i][1]]]) for i in it]
        k_prev = [k_ref[pl.ds(pstart[i], BLK), kl[items[i][1]]] for i in it]
        k_cur = [k_ref[pl.ds(start[i], BLK), kl[items[i][1]]] for i in it]
        lg_meta = [_dot(qs[i], k_meta[items[i][1]], _NT) + bmeta_ref[items[i][1], items[i][0]] for i in it]
        lg_prev = [jnp.where(prev_vis & (items[i][0] > 0),
                             _dot(qs[i], k_prev[i], _NT) + bband_ref[items[i][1], :, 0:BLK], NEG) for i in it]
        lg_cur = [jnp.where(cur_vis, _dot(qs[i], k_cur[i], _NT) + bband_ref[items[i][1], :, BLK:2 * BLK], NEG)
                  for i in it]
        out = attend([[(lg_meta[i], v_meta[items[i][1]]),
                       (lg_prev[i], v_ref[pl.ds(pstart[i], BLK), kl[items[i][1]]]),
                       (lg_cur[i], v_ref[pl.ds(start[i], BLK), kl[items[i][1]]])] for i in it],
                     [sink_q[hp] for _, hp in items], BLK)
        for i in it:
            o_ref[pl.ds(start[i], BLK), ql[items[i][1]]] = out[i].astype(o_ref.dtype)
        return carry

    lax.fori_loop(0, nb // unroll, blocks, 0)


def _attention(q, kd, vd, sinks, bband, bmeta, bmm):
    B, lp, _ = q.shape
    nb = (lp - FRONT - N_META) // ATT_BLOCK
    assert WINDOW == ATT_BLOCK and ROW_ALIGN % BF16_ROWS == 0 and ATT_KV_HEADS % ATT_STEP_KV == 0
    np_, qw, kw = ATT_STEP_PAIRS, ATT_STEP_PAIRS * LANES, ATT_STEP_KV * LANES
    return pl.pallas_call(
        functools.partial(_attn_kernel, nb=nb),
        grid=(ATT_KV_HEADS // ATT_STEP_KV, B),
        in_specs=[pl.BlockSpec(memory_space=pltpu.SMEM),
                  pl.BlockSpec((None, lp, qw), lambda c, b: (b, 0, c)),
                  pl.BlockSpec((None, lp, kw), lambda c, b: (b, 0, c)),
                  pl.BlockSpec((None, lp, kw), lambda c, b: (b, 0, c)),
                  pl.BlockSpec((np_,) + bband.shape[1:], lambda c, b: (c, 0, 0)),
                  pl.BlockSpec((np_,) + bmeta.shape[1:], lambda c, b: (c, 0, 0, 0)),
                  pl.BlockSpec((np_,) + bmm.shape[1:], lambda c, b: (c, 0, 0))],
        out_specs=pl.BlockSpec((None, lp, qw), lambda c, b: (b, 0, c)),
        out_shape=jax.ShapeDtypeStruct((B, lp, D_MODEL), BF16),
        compiler_params=_params("parallel", "parallel"),
        name="swa_attention",
    )(sinks, q, kd, vd, bband, bmeta, bmm)


def _t5_bucket(dist):
    exact = N_BUCKETS // 2
    d = jnp.maximum(dist, 0)
    ratio = jnp.log(jnp.maximum(d, 1).astype(F32) / exact) / math.log(MAX_DISTANCE / exact)
    large = jnp.minimum(exact + (ratio * (N_BUCKETS - exact)).astype(jnp.int32), N_BUCKETS - 1)
    return jnp.where(d < exact, d, large)


def _bias_tables(rel_bias, nb):
    H, BLK = ATT_HEADS, ATT_BLOCK

    def lookup(dist):
        onehot = (_t5_bucket(dist)[..., None] == jnp.arange(N_BUCKETS)).astype(F32)
        return jnp.dot(onehot, rel_bias.astype(F32), precision=lax.Precision.HIGHEST)

    qi = jnp.arange(BLK)[:, None]
    band = lookup(qi + BLK - jnp.arange(2 * BLK)[None, :])
    band = jnp.moveaxis(band, -1, 0).reshape(H // 2, 2 * BLK, 2 * BLK)
    pos = jnp.arange(nb * BLK)[:, None]
    meta = lookup(N_META + pos - jnp.arange(N_META)[None, :])
    meta = meta.reshape(nb, BLK, N_META, H // 2, 2).transpose(3, 0, 4, 1, 2)
    meta = meta.reshape(H // 2, nb, 2 * BLK, N_META)
    pm = jnp.arange(N_META)
    mm = lookup(pm[:, None] - pm[None, :])
    mm = jnp.moveaxis(mm, -1, 0).reshape(H // 2, 2 * N_META, N_META)
    return band.astype(F32), meta.astype(F32), mm.astype(F32)


def _dispatch_geometry(a1):
    E = N_EXPERTS
    routed = (a1 > 0.0).astype(BF16)
    n_col = jnp.sum(routed.astype(F32), axis=1, keepdims=True)
    ch_col = jnp.floor((n_col + (MOE_CH - 1)) * (1.0 / MOE_CH))
    ei = lax.broadcasted_iota(jnp.int32, (E, E), 0)
    ej = lax.broadcasted_iota(jnp.int32, (E, E), 1)
    off_col = _dot((ei > ej).astype(BF16), jnp.broadcast_to(ch_col, (E, LANES)))[:, 0:1]
    n_row = _dot(jnp.ones((SUBLANES, TM), BF16), routed, _NT)
    ch_row = jnp.floor((n_row + (MOE_CH - 1)) * (1.0 / MOE_CH))
    off_row = _dot(ch_row, (ei < ej).astype(BF16))
    return ch_col, off_col, ch_row, off_row


def _select_experts(choice):
    E, G, EPG = N_EXPERTS, N_GROUPS, N_EXPERTS // N_GROUPS
    grp = choice.reshape(G, EPG, TM)
    sub = lax.broadcasted_iota(jnp.int32, (G, EPG, TM), 1)
    top1 = jnp.max(grp, axis=1, keepdims=True)
    first = jnp.min(jnp.where(grp == top1, sub, EPG), axis=1, keepdims=True)
    top2 = jnp.max(jnp.where(sub == first, -jnp.inf, grp), axis=1, keepdims=True)
    gscore = jnp.broadcast_to(top1 + top2, (G, EPG, TM))

    gi = lax.broadcasted_iota(jnp.int32, (G, EPG, TM), 0)
    keep = jnp.zeros((G, EPG, TM), jnp.bool_)
    for _ in range(TOPK_GROUPS):
        m = jnp.max(gscore, axis=0, keepdims=True)
        sel = gi == jnp.min(jnp.where(gscore == m, gi, G), axis=0, keepdims=True)
        keep = keep | sel
        gscore = jnp.where(sel, -jnp.inf, gscore)
    keep_e = keep.reshape(E, TM)

    cand = jnp.where(keep_e, choice, -jnp.inf)
    ei = lax.broadcasted_iota(jnp.int32, (E, TM), 0)
    routed = jnp.zeros((E, TM), jnp.bool_)
    for _ in range(TOP_K):
        m = jnp.max(cand, axis=0, keepdims=True)
        sel = ei == jnp.min(jnp.where(cand == m, ei, E), axis=0, keepdims=True)
        routed = routed | sel
        cand = jnp.where(sel, -jnp.inf, cand)
    return routed


def _route_kernel(x_ref, rt_ref, bias_ref, a1_out, a1t_out, gate_out, cnt_out, off_out, *, tiles):
    U = range(tiles)
    xs = [x_ref[u * TM:(u + 1) * TM, :] for u in U]
    scores = [_sigmoid(_dot3(rt_ref[...], x, _NT)) for x in xs]
    routed = [_select_experts(s + bias_ref[:, 0:1]) for s in scores]
    gate = [jnp.where(r, s, 0.0) for r, s in zip(routed, scores)]
    gate = [g / jnp.sum(g, axis=0, keepdims=True) * ROUTED_SCALE for g in gate]
    ti = lax.broadcasted_iota(jnp.int32, (TM, TM), 0)
    tj = lax.broadcasted_iota(jnp.int32, (TM, TM), 1)
    eye = (ti == tj).astype(BF16)
    routed_b = [r.astype(BF16) for r in routed]
    rank = [_dot(rb, (ti < tj).astype(BF16)) for rb in routed_b]
    rank_t = [_dot((tj < ti).astype(BF16), rb, _NT) for rb in routed_b]
    routed_t = [_dot(eye, rb, _NT) for rb in routed_b]
    gate_t = [_dot_exact_lhs(eye, g, _NT) for g in gate]
    for u in U:
        a1 = jnp.where(routed[u], rank[u] + 1.0, 0.0)
        _, _, ch_row, off_row = _dispatch_geometry(a1)
        a1_out[u] = a1
        a1t_out[u] = routed_t[u] * (rank_t[u] + 1.0)
        gate_out[u] = jnp.concatenate(_split2(gate_t[u]), axis=1)
        cnt_out[u] = ch_row
        off_out[u] = off_row


def _dispatch_kernel(x_ref, a1_ref, gate_ref, cnt_ref, off_ref, xs_out, gs_out):
    E = N_EXPERTS
    lo = jnp.concatenate([off_ref[0:1, :]] * 2, axis=1) * MOE_CH
    hi = lo + jnp.concatenate([cnt_ref[0:1, :]] * 2, axis=1) * MOE_CH
    first = lax.broadcasted_iota(jnp.int32, (1, 2 * E), 1) < E
    lo1 = jnp.where(first, lo, 0.0)
    xg = jnp.concatenate([x_ref[...].astype(BF16), gate_ref[...]], axis=1)
    a1b = a1_ref[...].astype(BF16)
    SB = MOE_SLOTS // MOE_SLOT_SPLITS
    eye = (lax.broadcasted_iota(jnp.int32, (LANES, LANES), 0)
           == lax.broadcasted_iota(jnp.int32, (LANES, LANES), 1))
    for s0 in range(0, MOE_SLOTS, SB):
        si = (lax.broadcasted_iota(jnp.int32, (SB, 2 * E), 0) + s0).astype(F32)
        member = (si >= lo) & (si < hi)
        seg0 = jnp.sum(jnp.where(member, lo1, 0.0), axis=1, keepdims=True)
        rs1 = (lax.broadcasted_iota(jnp.int32, (SB, 1), 0) + (s0 + 1)).astype(F32) - seg0
        perm = _dot(member[:, :E], a1b) == rs1
        disp = _dot(perm, xg)
        for j in range(MOE_NL):
            piece = disp[:, j * LANES:(j + 1) * LANES].astype(BF16)
            xs_out[s0 // MOE_CH:(s0 + SB) // MOE_CH, j] = piece.reshape(SB // MOE_CH, MOE_CH, LANES)
        gcol = jnp.sum(jnp.where(member, disp[:, D_MODEL:], 0.0), axis=1, keepdims=True)
        for r0 in range(0, SB, LANES):
            grow = jnp.sum(jnp.where(eye, gcol[r0:r0 + LANES], 0.0), axis=0, keepdims=True)
            gs_out[:, s0 + r0:s0 + r0 + LANES] = jnp.broadcast_to(grow, (SUBLANES, LANES))


def _router(h2d, router_t, bias_col):
    T, D = h2d.shape
    nt = T // TM
    E = N_EXPERTS
    tiles = next(t for t in (4, 3, 2, 1) if nt % t == 0)
    a1, a1t, gate, cnt, off = pl.pallas_call(
        functools.partial(_route_kernel, tiles=tiles),
        grid=(nt // tiles,),
        in_specs=[pl.BlockSpec((tiles * TM, D), lambda i: (i, 0)), _full(router_t.shape), _full(bias_col.shape)],
        out_specs=[pl.BlockSpec((tiles, E, TM), lambda i: (i, 0, 0)),
                   pl.BlockSpec((tiles, TM, E), lambda i: (i, 0, 0)),
                   pl.BlockSpec((tiles, TM, 2 * E), lambda i: (i, 0, 0)),
                   pl.BlockSpec((tiles, SUBLANES, E), lambda i: (i, 0, 0)),
                   pl.BlockSpec((tiles, SUBLANES, E), lambda i: (i, 0, 0))],
        out_shape=[jax.ShapeDtypeStruct((nt, E, TM), F32),
                   jax.ShapeDtypeStruct((nt, TM, E), F32),
                   jax.ShapeDtypeStruct((nt, TM, 2 * E), BF16),
                   jax.ShapeDtypeStruct((nt, SUBLANES, E), F32),
                   jax.ShapeDtypeStruct((nt, SUBLANES, E), F32)],
        compiler_params=_params("parallel"),
        name="moe_route",
    )(h2d, router_t, bias_col)
    xs, gs = pl.pallas_call(
        _dispatch_kernel,
        grid=(nt,),
        in_specs=[pl.BlockSpec((TM, D), lambda i: (i, 0)),
                  pl.BlockSpec((None, E, TM), lambda i: (i, 0, 0)),
                  pl.BlockSpec((None, TM, 2 * E), lambda i: (i, 0, 0)),
                  pl.BlockSpec((None, SUBLANES, E), lambda i: (i, 0, 0)),
                  pl.BlockSpec((None, SUBLANES, E), lambda i: (i, 0, 0))],
        out_specs=[pl.BlockSpec((None, MOE_CPT, MOE_NL, MOE_CH, LANES), lambda i: (i, 0, 0, 0, 0)),
                   pl.BlockSpec((None, SUBLANES, MOE_SLOTS), lambda i: (i, 0, 0))],
        out_shape=[jax.ShapeDtypeStruct((nt, MOE_CPT, MOE_NL, MOE_CH, LANES), BF16),
                   jax.ShapeDtypeStruct((nt, SUBLANES, MOE_SLOTS), F32)],
        compiler_params=_params("parallel"),
        name="moe_dispatch",
    )(h2d, a1, gate, cnt, off)
    return xs, gs, a1, a1t, cnt


def _expert_kernel(be_ref, ps_ref, nb_ref, src_ref, xs_hbm, wg_ref, wu_ref, wd_ref, ys_hbm,
                   xbuf, ybuf, wgu_bf, wd_bf, in_sem, out_sem, *, nblk):
    b = pl.program_id(0)
    slot = lax.rem(b, 2)

    def in_copy(src_chunk, sl, c):
        return pltpu.make_async_copy(xs_hbm.at[src_chunk], xbuf.at[sl, c], in_sem.at[sl])

    def out_copy(dst_chunk, sl, c):
        return pltpu.make_async_copy(ybuf.at[sl, c], ys_hbm.at[dst_chunk], out_sem.at[sl])

    def for_chunks(blk, fn):
        p0 = ps_ref[blk]
        n = nb_ref[blk]
        for c0 in range(0, MOE_CB, MOE_SEG):

            @pl.when(n >= c0 + MOE_SEG)
            def _():
                for c in range(c0, c0 + MOE_SEG):
                    fn(src_ref[p0 + c], c)

            @pl.when((n > c0) & (n < c0 + MOE_SEG))
            def _():
                def body(c, carry):
                    fn(src_ref[p0 + c], c)
                    return carry

                lax.fori_loop(c0, n, body, 0)

    @pl.when(b == 0)
    def _():
        xbuf[...] = jnp.zeros_like(xbuf)
        for_chunks(0, lambda s, c: in_copy(s, 0, c).start())

    @pl.when(b + 1 < nblk)
    def _():
        for_chunks(b + 1, lambda s, c: in_copy(s, 1 - slot, c).start())

    for_chunks(b, lambda s, c: in_copy(s, slot, c).wait())

    @pl.when(b >= 2)
    def _():
        for_chunks(b - 2, lambda s, c: out_copy(s, slot, c).wait())

    @pl.when((b == 0) | (be_ref[b] != be_ref[jnp.maximum(b - 1, 0)]))
    def _():
        wgu_bf[:, :EXPERT_FF] = wg_ref[...].astype(BF16)
        wgu_bf[:, EXPERT_FF:] = wu_ref[...].astype(BF16)
        wd_bf[...] = wd_ref[...].astype(BF16)

    def swiglu(c0):
        x = jnp.concatenate([xbuf[slot, c0:c0 + MOE_SEG, j].reshape(MOE_SEG * MOE_CH, LANES)
                             for j in range(MOE_NL)], axis=1)
        hid = _dot(x, wgu_bf[...])
        act = _silu(hid[:, :EXPERT_FF]) * hid[:, EXPERT_FF:]
        y = _dot(act, wd_bf[...]).astype(BF16)
        for j in range(MOE_NL):
            ybuf[slot, c0:c0 + MOE_SEG, j] = y[:, j * LANES:(j + 1) * LANES].reshape(MOE_SEG, MOE_CH, LANES)

    @pl.when(nb_ref[b] == MOE_CB)
    def _():
        for c0 in range(0, MOE_CB, MOE_SEG):
            swiglu(c0)

    @pl.when(nb_ref[b] < MOE_CB)
    def _():
        for c0 in range(0, MOE_CB, MOE_SEG):

            @pl.when(nb_ref[b] > c0)
            def _():
                swiglu(c0)

    for_chunks(b, lambda s, c: out_copy(s, slot, c).start())

    @pl.when(b == nblk - 1)
    def _():
        for_chunks(b, lambda s, c: out_copy(s, slot, c).wait())

        @pl.when(b >= 1)
        def _():
            for_chunks(b - 1, lambda s, c: out_copy(s, 1 - slot, c).wait())


def _experts(xs, wg, wu, wd, layer, be, pstart, nbc, src):
    nt = xs.shape[0]
    nblk = be.shape[0]
    xs_c = xs.reshape(nt * MOE_CPT, MOE_NL, MOE_CH, LANES)
    by_expert = lambda b, be, ps, nb, src: (layer, be[b], 0, 0)
    grid_spec = pltpu.PrefetchScalarGridSpec(
        num_scalar_prefetch=4,
        grid=(nblk,),
        in_specs=[pl.BlockSpec(memory_space=pl.ANY),
                  pl.BlockSpec((None, None, D_MODEL, EXPERT_FF), by_expert),
                  pl.BlockSpec((None, None, D_MODEL, EXPERT_FF), by_expert),
                  pl.BlockSpec((None, None, EXPERT_FF, D_MODEL), by_expert)],
        out_specs=pl.BlockSpec(memory_space=pl.ANY),
        scratch_shapes=[pltpu.VMEM((2, MOE_CB, MOE_NL, MOE_CH, LANES), BF16),
                        pltpu.VMEM((2, MOE_CB, MOE_NL, MOE_CH, LANES), BF16),
                        pltpu.VMEM((D_MODEL, 2 * EXPERT_FF), BF16),
                        pltpu.VMEM((EXPERT_FF, D_MODEL), BF16),
                        pltpu.SemaphoreType.DMA((2,)), pltpu.SemaphoreType.DMA((2,))],
    )
    ys = pl.pallas_call(
        functools.partial(_expert_kernel, nblk=nblk),
        grid_spec=grid_spec,
        out_shape=jax.ShapeDtypeStruct(xs_c.shape, BF16),
        input_output_aliases={4: 0},
        compiler_params=_params("arbitrary"),
        name="moe_experts",
    )(be, pstart, nbc, src, xs_c, wg, wu, wd)
    return ys.reshape(xs.shape)


def _combine_kernel(h_ref, ys_ref, gs_ref, a1_ref, a1t_ref, wgu_ref, wd_ref, vec_ref, o_ref):
    E = N_EXPERTS
    h = h_ref[...]
    ch_col, off_col, _, _ = _dispatch_geometry(a1_ref[...])
    si = lax.broadcasted_iota(jnp.int32, (E, MOE_SLOTS), 1).astype(F32)
    lo = off_col * MOE_CH
    member_t = (si >= lo) & (si < lo + ch_col * MOE_CH)
    seg0 = jnp.sum(jnp.where(member_t, lo, 0.0), axis=0, keepdims=True)
    rs1 = lax.broadcasted_iota(jnp.int32, (1, MOE_SLOTS), 1).astype(F32) - seg0 + 1.0
    perm_t = _dot(a1t_ref[...], member_t) == rs1
    ys = jnp.concatenate([ys_ref[:, j].reshape(MOE_SLOTS, LANES) for j in range(MOE_NL)], axis=1)
    eye = (lax.broadcasted_iota(jnp.int32, (LANES, LANES), 0)
           == lax.broadcasted_iota(jnp.int32, (LANES, LANES), 1))
    gs = jnp.concatenate([jnp.sum(jnp.where(eye, gs_ref[0:1, s0:s0 + LANES], 0.0), axis=1, keepdims=True)
                          for s0 in range(0, MOE_SLOTS, LANES)], axis=0)
    routed = _dot(perm_t.astype(BF16), ys.astype(F32) * gs)
    xb = h.astype(BF16)
    hid = _dot(xb, wgu_ref[...])
    shared = _dot(_silu(hid[:, :EXPERT_FF]) * hid[:, EXPERT_FF:], wd_ref[...])
    o_ref[...] = _layer_norm(DEEPNORM_ALPHA * h + (shared + routed), vec_ref[0:1, :], vec_ref[1:2, :])


def _combine(h2d, ys, gs, a1, a1t, wgu_s, wd_s, vec):
    T, D = h2d.shape
    nt = T // TM
    return pl.pallas_call(
        _combine_kernel,
        grid=(nt,),
        in_specs=[pl.BlockSpec((TM, D), lambda i: (i, 0)),
                  pl.BlockSpec((None, MOE_CPT, MOE_NL, MOE_CH, LANES), lambda i: (i, 0, 0, 0, 0)),
                  pl.BlockSpec((None, SUBLANES, MOE_SLOTS), lambda i: (i, 0, 0)),
                  pl.BlockSpec((None, N_EXPERTS, TM), lambda i: (i, 0, 0)),
                  pl.BlockSpec((None, TM, N_EXPERTS), lambda i: (i, 0, 0)),
                  _full(wgu_s.shape), _full(wd_s.shape), _full(vec.shape)],
        out_specs=pl.BlockSpec((TM, D), lambda i: (i, 0)),
        out_shape=jax.ShapeDtypeStruct((T, D), F32),
        compiler_params=_params("parallel"),
        name="moe_combine",
    )(h2d, ys, gs, a1, a1t, wgu_s, wd_s, vec)


def _dispatch_tables(chunks):
    nt, E = chunks.shape
    maxc = nt * MOE_CPT
    nblk = -(-maxc // MOE_CB) + E
    hp = lax.Precision.HIGHEST
    ch = chunks.astype(F32)
    off = jnp.cumsum(ch, axis=1) - ch
    wend = jnp.cumsum(ch, axis=0)
    cnt_e = wend[-1]
    cend = jnp.cumsum(cnt_e)
    base_e = cend - cnt_e
    srcbase = jnp.arange(nt, dtype=F32)[:, None] * MOE_CPT + off - (wend - ch)
    p = jnp.arange(maxc, dtype=F32)
    e_p = jnp.minimum(jnp.sum((cend[None, :] <= p[:, None]).astype(jnp.int32), axis=1), E - 1)
    oh_e = (e_p[:, None] == jnp.arange(E, dtype=jnp.int32)[None, :]).astype(F32)
    look = jnp.dot(oh_e, jnp.concatenate([wend.T, srcbase.T, base_e[:, None]], axis=1), precision=hp)
    q = p - look[:, 2 * nt]
    i_p = jnp.sum((look[:, :nt] <= q[:, None]).astype(jnp.int32), axis=1)
    oh_i = i_p[:, None] == jnp.arange(nt, dtype=jnp.int32)[None, :]
    src = jnp.sum(jnp.where(oh_i, look[:, nt:2 * nt], 0.0), axis=1) + q
    src = jnp.clip(src, 0, maxc - 1).astype(jnp.int32)
    nblk_e = jnp.floor((cnt_e + (MOE_CB - 1)) * (1.0 / MOE_CB))
    bend = jnp.cumsum(nblk_e)
    bidx = jnp.arange(nblk, dtype=F32)
    be = jnp.minimum(jnp.sum((bend[None, :] <= bidx[:, None]).astype(jnp.int32), axis=1), E - 1)
    oh_b = (be[:, None] == jnp.arange(E, dtype=jnp.int32)[None, :]).astype(F32)
    lookb = jnp.dot(oh_b, jnp.stack([bend - nblk_e, cnt_e, base_e], axis=1), precision=hp)
    q0 = (bidx - lookb[:, 0]) * MOE_CB
    nbc = jnp.clip(lookb[:, 1] - q0, 0, MOE_CB).astype(jnp.int32)
    pstart = jnp.clip(lookb[:, 2] + q0, 0, maxc - MOE_CB).astype(jnp.int32)
    return be.astype(jnp.int32), pstart, nbc, src


def _moe_layer(h2d, layer, router, bias, w_gate, w_up, w_down, sh_gate, sh_up, sh_down, ln_g, ln_b):
    router_t = router.T.astype(F32)
    bias_col = jnp.broadcast_to(bias.astype(F32)[:, None], (N_EXPERTS, LANES))
    xs, gs, a1, a1t, cnt = _router(h2d, router_t, bias_col)
    be, pstart, nbc, src = _dispatch_tables(cnt[:, 0, :].astype(jnp.int32))
    ys = _experts(xs, w_gate, w_up, w_down, layer, be, pstart, nbc, src)
    wgu_s = jnp.concatenate([sh_gate, sh_up], axis=-1).astype(BF16)
    vec = jnp.zeros((SUBLANES, D_MODEL), F32).at[0].set(ln_g).at[1].set(ln_b)
    return _combine(h2d, ys, gs, a1, a1t, wgu_s, sh_down.astype(BF16), vec)


def _rows(*vs):
    out = jnp.zeros((SUBLANES, vs[0].shape[-1]), F32)
    for i, v in enumerate(vs):
        out = out.at[i].set(v.astype(F32))
    return out


def kernel(x, meta, rel_bias, rwkv_mu, rwkv_w0, rwkv_w1, rwkv_w2, rwkv_a0, rwkv_a1, rwkv_a2, rwkv_g1, rwkv_g2, rwkv_k_k, rwkv_k_a, rwkv_r_k, rwkv_w_r, rwkv_w_k, rwkv_w_v, rwkv_w_o, rwkv_lnx_g, rwkv_lnx_b, attn_w_qkv, attn_b_qkv, attn_sinks, attn_w_o, attn_b_o, ln_mix_g, ln_mix_b, ln_ffn_g, ln_ffn_b, moe_router, moe_bias, moe_w_gate, moe_w_up, moe_w_down, shared_w_gate, shared_w_up, shared_w_down):
    B, S, D = x.shape
    assert D == D_MODEL and S % ATT_BLOCK == 0 and (FRONT + N_META) % WKV_CHUNK == 0
    lp = FRONT + N_META + S
    T = B * lp
    assert T % TM == 0 and lp >= TM
    h = jnp.concatenate([jnp.zeros((B, FRONT, D), x.dtype),
                         jnp.broadcast_to(meta[None].astype(x.dtype), (B, N_META, D)), x], axis=1)
    h = h.reshape(T, D)
    bf = lambda w: w.astype(BF16)

    H, N = RWKV_HEADS, RWKV_HEAD
    head_of = jnp.arange(D) // N
    gsum = (head_of[:, None] == jnp.arange(LANES)[None, :]).astype(BF16)
    gexp = gsum.T
    r, w, k, v, kk, b, g = _rwkv_proj(
        h, lp, _rows(*rwkv_mu[0]), _rows(rwkv_w0[0], rwkv_a0[0], rwkv_k_k[0], rwkv_k_a[0]),
        bf(rwkv_w_r[0]), bf(rwkv_w_k[0]), bf(rwkv_w_v[0]), bf(rwkv_w1[0]), bf(rwkv_w2[0]),
        bf(rwkv_a1[0]), bf(rwkv_a2[0]), bf(rwkv_g1[0]), bf(rwkv_g2[0]), gsum, gexp)
    to3 = lambda t: t.reshape(B, lp, D)
    prm = _rows(rwkv_r_k[0].reshape(D), rwkv_lnx_g[0], rwkv_lnx_b[0])
    o = _wkv(to3(r), to3(w), to3(k), to3(v), to3(kk), to3(b), to3(g), prm)
    h = _proj_ln(o.reshape(T, D), bf(rwkv_w_o[0]), _rows(jnp.zeros((D,), F32), ln_mix_g[0], ln_mix_b[0]), h)
    h = _moe_layer(h, 0, moe_router[0], moe_bias[0], moe_w_gate, moe_w_up, moe_w_down,
                   shared_w_gate[0], shared_w_up[0], shared_w_down[0], ln_ffn_g[0], ln_ffn_b[0])

    HD, KV = ATT_HEAD_DIM, ATT_KV_HEADS
    qw = ATT_HEADS * HD
    wqkv, bqkv = attn_w_qkv[0], attn_b_qkv[0]
    dup = lambda t: jnp.concatenate([t.reshape(-1, KV, 1, HD)] * 2, axis=2).reshape(t.shape[0], 2 * KV * HD)
    wq, wkd, wvd = wqkv[:, :qw], dup(wqkv[:, qw:qw + KV * HD]), dup(wqkv[:, qw + KV * HD:])
    bq, bkd, bvd = bqkv[None, :qw], dup(bqkv[None, qw:qw + KV * HD]), dup(bqkv[None, qw + KV * HD:])
    q, kd, vd = _qkv(h, bf(wq), bf(wkd), bf(wvd), bq.astype(F32), bkd.astype(F32), bvd.astype(F32))
    bband, bmeta, bmm = _bias_tables(rel_bias, S // ATT_BLOCK)
    o = _attention(q.reshape(B, lp, qw), kd.reshape(B, lp, -1), vd.reshape(B, lp, -1),
                   attn_sinks[0].astype(F32), bband, bmeta, bmm)
    h = _proj_ln(o.reshape(T, D), bf(attn_w_o[0]), _rows(attn_b_o[0], ln_mix_g[1], ln_mix_b[1]), h)
    h = _moe_layer(h, 1, moe_router[1], moe_bias[1], moe_w_gate, moe_w_up, moe_w_down,
                   shared_w_gate[1], shared_w_up[1], shared_w_down[1], ln_ffn_g[1], ln_ffn_b[1])
    return h.reshape(B, lp, D)[:, FRONT + N_META:]
```

```python
import functools
import math

import jax
import jax.numpy as jnp
from jax import lax
from jax.experimental import pallas as pl
from jax.experimental.pallas import tpu as pltpu

F32 = jnp.float32
BF16 = jnp.bfloat16

D_MODEL = 1024
DEPTH = 2
N_META = 16
RWKV_HEAD = 64
RWKV_HEADS = D_MODEL // RWKV_HEAD
GN_EPS = 64e-5
ATT_HEADS = 16
ATT_KV_HEADS = 4
ATT_HEAD_DIM = D_MODEL // ATT_HEADS
ATT_GROUP = ATT_HEADS // ATT_KV_HEADS
WINDOW = 128
ATT_BLOCK = 128
N_BUCKETS = 32
MAX_DISTANCE = 128
N_EXPERTS = 64
TOP_K = 8
N_GROUPS = 8
TOPK_GROUPS = 4
EXPERT_FF = 256
ROUTED_SCALE = 2.5
DEEPNORM_ALPHA = (2 * DEPTH) ** 0.25
LN_EPS = 1e-5

LANES = 128
SUBLANES = 8
BF16_ROWS = 16
VMEM_LIMIT = 56 * 1024 * 1024

FRONT = 48
TM = 256
WKV_CHUNK = 64
WKV_PAIRS = 8
WKV_BATCH = 2
WKV_SUB = 16
ATT_STEP_KV = 2
ATT_STEP_PAIRS = ATT_STEP_KV * ATT_GROUP // 2
ROW_ALIGN = FRONT + N_META
MOE_CH = BF16_ROWS
MOE_SLOTS = TM * TOP_K + N_EXPERTS * MOE_CH
MOE_CPT = MOE_SLOTS // MOE_CH
MOE_NL = D_MODEL // LANES
MOE_CB = 128
MOE_SEG = 32
MOE_SLOT_SPLITS = 3
NEG = -1e30

_NN = (((1,), (0,)), ((), ()))
_NT = (((1,), (1,)), ((), ()))


def _dot(a, b, dn=_NN):
    return lax.dot_general(a.astype(BF16), b.astype(BF16), dn, preferred_element_type=F32)


def _split2(x):
    hi = x.astype(BF16)
    lo = (x - hi.astype(F32)).astype(BF16)
    return hi, lo


def _split3(x):
    h1 = x.astype(BF16)
    r1 = x - h1.astype(F32)
    h2 = r1.astype(BF16)
    h3 = (r1 - h2.astype(F32)).astype(BF16)
    return h1, h2, h3


def _dot_exact_lhs(a01, b, dn=_NN):
    a = a01.astype(BF16)
    return sum(lax.dot_general(a, p, dn, preferred_element_type=F32) for p in _split3(b))


def _dot_hilo_lhs(a01, b, dn=_NN):
    a = a01.astype(BF16)
    return sum(lax.dot_general(a, p, dn, preferred_element_type=F32) for p in _split2(b))


def _dot_hilo_rhs(a, b01, dn=_NN):
    b = b01.astype(BF16)
    return sum(lax.dot_general(p, b, dn, preferred_element_type=F32) for p in _split2(a))


def _dot3(a, b, dn=_NN):
    ah, al = _split2(a)
    bh, bl = _split2(b)
    d = lambda x, y: lax.dot_general(x, y, dn, preferred_element_type=F32)
    return d(ah, bh) + (d(ah, bl) + d(al, bh))


def _sigmoid(x):
    return 1.0 / (1.0 + jnp.exp(-x))


def _silu(x):
    return x * _sigmoid(x)


def _layer_norm(x, g, b):
    mu = jnp.mean(x, axis=-1, keepdims=True)
    xc = x - mu
    var = jnp.mean(xc * xc, axis=-1, keepdims=True)
    return xc * lax.rsqrt(var + LN_EPS) * g + b


def _full(shape):
    nd = len(shape)
    return pl.BlockSpec(shape, lambda *_: (0,) * nd)


def _params(*sem):
    return pltpu.CompilerParams(dimension_semantics=sem, vmem_limit_bytes=VMEM_LIMIT)


def _rwkv_proj_kernel(x_ref, xp_ref, mu_ref, vec_ref, wr_ref, wk_ref, wv_ref, w1_ref, w2_ref,
                      a1_ref, a2_ref, g1_ref, g2_ref, gsum_ref, gexp_ref,
                      r_out, w_out, k_out, v_out, kk_out, b_out, g_out, *, lp):
    i = pl.program_id(0)
    x = x_ref[...]
    rows = x.shape[0]
    row = lax.broadcasted_iota(jnp.int32, (rows, 1), 0)
    pos = lax.rem(i * rows, lp) + row
    pos = jnp.where(pos >= lp, pos - lp, pos)
    prev = jnp.where(row == 0, xp_ref[SUBLANES - 1:SUBLANES, :], pltpu.roll(x, 1, axis=0))
    prev = jnp.where(pos == FRONT, 0.0, prev)
    xx = prev - x
    valid = pos >= FRONT

    def mix(j):
        return (x + xx * mu_ref[j:j + 1, :]).astype(BF16)

    w0, a0, k_k, k_a = (vec_ref[j:j + 1, :] for j in range(4))
    r = _dot(mix(0), wr_ref[...])
    z = w0 + _dot(jnp.tanh(_dot(mix(1), w1_ref[...])), w2_ref[...])
    w = -math.exp(-0.5) * _sigmoid(z)
    k = _dot(mix(2), wk_ref[...])
    v = _dot(mix(3), wv_ref[...])
    a = _sigmoid(a0 + _dot(_dot(mix(4), a1_ref[...]), a2_ref[...]))
    g = _dot(_sigmoid(_dot(mix(5), g1_ref[...])), g2_ref[...])
    kk = k * k_k
    ssq = _dot_hilo_rhs(kk * kk, gsum_ref[...])
    kk = kk * lax.rsqrt(jnp.maximum(_dot_hilo_rhs(ssq, gexp_ref[...]), 1e-24))
    k = k * (1.0 + (a - 1.0) * k_a)
    r_out[...] = r
    w_out[...] = jnp.where(valid, w, 0.0)
    k_out[...] = jnp.where(valid, k, 0.0)
    v_out[...] = jnp.where(valid, v, 0.0)
    kk_out[...] = jnp.where(valid, kk, 0.0)
    b_out[...] = jnp.where(valid, kk * a, 0.0)
    g_out[...] = g


def _rwkv_proj(h2d, lp, mu, vec, wr, wk, wv, w1, w2, a1, a2, g1, g2, gsum, gexp):
    T, D = h2d.shape
    tp = _proj_rows(T)
    assert tp <= lp
    row_spec = pl.BlockSpec((tp, D), lambda i: (i, 0))
    prev_spec = pl.BlockSpec((SUBLANES, D), lambda i: (jnp.maximum(i * (tp // SUBLANES) - 1, 0), 0))
    ws = [mu, vec, wr, wk, wv, w1, w2, a1, a2, g1, g2, gsum, gexp]
    once = lambda w: pl.BlockSpec(w.shape, lambda i: (0,) * w.ndim, pipeline_mode=pl.Buffered(1))
    return pl.pallas_call(
        functools.partial(_rwkv_proj_kernel, lp=lp),
        grid=(T // tp,),
        in_specs=[row_spec, prev_spec] + [once(w) for w in ws],
        out_specs=[row_spec] * 7,
        out_shape=[jax.ShapeDtypeStruct((T, D), F32)] * 7,
        compiler_params=_params("parallel"),
        name="rwkv_proj",
    )(h2d, h2d, *ws)


def _wkv_chunk(r, w, cum, k, v, kk, b, S, c):
    C = WKV_CHUNK
    P = range(len(r))
    bf = lambda xs: [x.astype(BF16) for x in xs]
    each = lambda f, *ls: [f(*a) for a in zip(*ls)]

    def stack(x):
        return jnp.concatenate([jnp.where(c["m0"], x, 0.0), jnp.where(c["m0"], 0.0, x)], axis=0)

    f32 = lambda xs: [x.astype(F32) for x in xs]
    cat0 = lambda x, y: jnp.concatenate([x, y], axis=0)
    cat1 = lambda x, y: jnp.concatenate([x, y], axis=1)
    nt = lambda x, y: _dot(x, y, _NT)
    C2 = 2 * C

    tot = [cum[p][C - 1:C, :] for p in P]
    inv = [jnp.exp(-cum[p]) for p in P]
    dend = [jnp.exp(tot[p] - cum[p]) for p in P]
    kr_s = bf([cat0(stack(kk[p] * jnp.exp(cum[p] - w[p])), stack(r[p] * jnp.exp(cum[p]))) for p in P])
    bk_s = bf([cat0(stack(b[p] * inv[p]), stack(k[p] * inv[p])) for p in P])
    bh_s = bf([stack(b[p] * dend[p]) for p in P])
    kh_s = bf([stack(k[p] * dend[p]) for p in P])
    vs_t = bf([stack(v[p]).T for p in P])

    sc = each(nt, bk_s, kr_s)
    lab_t = [jnp.where(c["upper"], x[:C2, :C2], 0.0) for x in sc]
    arb_t = bf([jnp.where(c["upinc"], x[:C2, C2:], 0.0) for x in sc])
    lak_t = bf([jnp.where(c["upper"], x[C2:, :C2], 0.0) for x in sc])
    ark_t = bf([jnp.where(c["upinc"], x[C2:, C2:], 0.0) for x in sc])

    eye = c["eye"]
    ud = [jnp.where(c["blk"], x, 0.0) for x in lab_t]
    nu = bf([x - y for x, y in zip(lab_t, ud)])
    udb = bf(ud)
    u2b = bf(each(_dot, udb, udb))
    u2 = f32(u2b)
    t1 = each(_dot, bf([eye + x for x in u2]), [cat1(x, (eye - y).astype(BF16)) for x, y in zip(u2b, ud)])
    u4b = bf([x[:, :C2] - y for x, y in zip(t1, u2)])
    u4 = f32(u4b)
    t2 = each(_dot, bf([eye + x for x in u4]), [cat1(x, y[:, C2:].astype(BF16)) for x, y in zip(u4b, t1)])
    u8 = [x[:, :C2] - y for x, y in zip(t2, u4)]
    dinv = each(_dot, bf([eye + x for x in u8]), bf([x[:, C2:] for x in t2]))
    dinvb = bf(dinv)
    wzb = bf(each(_dot, dinvb, nu))
    t3 = each(_dot, wzb, [cat1(x, y) for x, y in zip(wzb, dinvb)])
    tt_t = bf(each(_dot, bf([eye + x[:, :C2] for x in t3]), bf([y - x[:, C2:] for x, y in zip(t3, dinv)])))

    sb = bf(S)
    p2 = each(_dot, vs_t, [cat1(x, y) for x, y in zip(lak_t, ark_t)])
    p3 = each(_dot, vs_t, kh_s)
    p1 = each(nt, sb, kr_s)
    u_t = bf(each(_dot, bf([-(x[:, :C2] + y[:, :C2]) for x, y in zip(p1, p2)]), tt_t))
    p4 = each(_dot, u_t, [cat1(x, y) for x, y in zip(arb_t, bh_s)])
    s_new = [S[p] * jnp.exp(tot[p]) + p4[p][:, C2:] + p3[p] for p in P]
    y_t = [p1[p][:, C2:] + p4[p][:, :C2] + p2[p][:, C2:] for p in P]
    return y_t, s_new


def _wkv_group_norm(y_t, hblk):
    C = WKV_CHUNK
    mean = jnp.sum(y_t, axis=0, keepdims=True) * (1.0 / RWKV_HEAD)
    yc = jnp.where(hblk, y_t - mean, 0.0)
    var = jnp.sum(yc * yc, axis=0, keepdims=True) * (1.0 / RWKV_HEAD)
    ys = (yc * lax.rsqrt(var + GN_EPS)).T
    return ys[:C] + ys[C:]


def _wkv_kernel(r_ref, w_ref, k_ref, v_ref, kk_ref, b_ref, g_ref, prm_ref, o_ref, yt_ref, *s_refs, lb):
    C = WKV_CHUNK

    @pl.when(pl.program_id(2) == 0)
    def _():
        for s_ref in s_refs:
            s_ref[...] = jnp.zeros_like(s_ref)

    ri = lax.broadcasted_iota(jnp.int32, (2 * C, 2 * C), 0)
    ci = lax.broadcasted_iota(jnp.int32, (2 * C, 2 * C), 1)
    lane = lax.broadcasted_iota(jnp.int32, (1, LANES), 1)
    ti = lax.broadcasted_iota(jnp.int32, (C, C), 0)
    tj = lax.broadcasted_iota(jnp.int32, (C, C), 1)
    consts = dict(
        m0=lane < RWKV_HEAD,
        upper=ri < ci,
        upinc=ri <= ci,
        blk=(ri // WKV_SUB) == (ci // WKV_SUB),
        hblk=(ri // RWKV_HEAD) == (ci // C),
        eye=(ri == ci).astype(F32),
    )
    m0 = consts["m0"]
    tril = (ti >= tj).astype(BF16)

    def head_sum(x):
        s0 = jnp.sum(jnp.where(m0, x, 0.0), axis=-1, keepdims=True)
        s1 = jnp.sum(jnp.where(m0, 0.0, x), axis=-1, keepdims=True)
        return jnp.where(m0, s0, s1)

    nbat = r_ref.shape[0]
    items = [(bi, slice(p * LANES, (p + 1) * LANES)) for bi in range(nbat) for p in range(WKV_PAIRS)]

    def recurrence(ci_):
        rows = pl.ds(pl.multiple_of(ci_ * C, C), C)
        ld = lambda ref: [ref[bi, rows, ln] for bi, ln in items]
        r, w, k, v, kk, b = ld(r_ref), ld(w_ref), ld(k_ref), ld(v_ref), ld(kk_ref), ld(b_ref)
        cum_all = [_dot_hilo_lhs(tril, w_ref[bi, rows, :]) for bi in range(nbat)]
        cum = [cum_all[bi][:, ln] for bi, ln in items]
        ys, s_new = _wkv_chunk(r, w, cum, k, v, kk, b, [s[...] for s in s_refs], consts)
        for i, (bi, ln) in enumerate(items):
            s_refs[i][...] = s_new[i]
            yt_ref[bi, :, ln] = ys[i]

    def finish(ci_):
        rows = pl.ds(pl.multiple_of(ci_ * C, C), C)
        for bi, ln in items:
            r_k, lg, lb_ = prm_ref[0:1, ln], prm_ref[1:2, ln], prm_ref[2:3, ln]
            yn = _wkv_group_norm(yt_ref[bi, :, ln], consts["hblk"])
            bonus = head_sum(r_ref[bi, rows, ln] * k_ref[bi, rows, ln] * r_k) * v_ref[bi, rows, ln]
            o_ref[bi, rows, ln] = ((yn * lg + lb_ + bonus) * g_ref[bi, rows, ln]).astype(o_ref.dtype)

    def chunk(ci_, carry):
        finish(ci_ - 1)
        recurrence(ci_)
        return carry

    recurrence(0)
    lax.fori_loop(1, lb // C, chunk, 0)
    finish(lb // C - 1)


def _wkv_row_block(lp, nbat):
    nch = lp // WKV_CHUNK
    for d in ((11, 8, 6, 4, 3, 2, 1) if nbat == 1 else (4, 3, 2, 1)):
        if nch % d == 0:
            return d * WKV_CHUNK
    return WKV_CHUNK


def _wkv(r, w, k, v, kk, b, g, prm):
    B, lp, D = r.shape
    nbat = WKV_BATCH if B % WKV_BATCH == 0 else 1
    lb = _wkv_row_block(lp, nbat)
    wl = WKV_PAIRS * LANES
    spec = pl.BlockSpec((nbat, lb, wl), lambda bi, pi, li: (bi, li, pi))
    return pl.pallas_call(
        functools.partial(_wkv_kernel, lb=lb),
        grid=(B // nbat, D // wl, lp // lb),
        in_specs=[spec] * 7 + [pl.BlockSpec((SUBLANES, wl), lambda bi, pi, li: (0, pi))],
        out_specs=spec,
        out_shape=jax.ShapeDtypeStruct((B, lp, D), BF16),
        scratch_shapes=[pltpu.VMEM((nbat, LANES, wl), F32)]
        + [pltpu.VMEM((LANES, LANES), F32)] * (WKV_PAIRS * nbat),
        compiler_params=_params("parallel", "parallel", "arbitrary"),
        name="wkv7",
    )(r, w, k, v, kk, b, g, prm)


def _proj_ln_kernel(a_ref, w_ref, vec_ref, h_ref, o_ref):
    mix = _dot(a_ref[...], w_ref[...]) + vec_ref[0:1, :]
    o_ref[...] = _layer_norm(DEEPNORM_ALPHA * h_ref[...] + mix, vec_ref[1:2, :], vec_ref[2:3, :])


def _proj_rows(T):
    return 2 * TM if T % (2 * TM) == 0 else TM


def _proj_ln(a2d, w, vec, h2d):
    T, D = h2d.shape
    K = a2d.shape[1]
    tp = _proj_rows(T)
    return pl.pallas_call(
        _proj_ln_kernel,
        grid=(T // tp,),
        in_specs=[pl.BlockSpec((tp, K), lambda i: (i, 0)), _full(w.shape), _full(vec.shape),
                  pl.BlockSpec((tp, D), lambda i: (i, 0))],
        out_specs=pl.BlockSpec((tp, D), lambda i: (i, 0)),
        out_shape=jax.ShapeDtypeStruct((T, D), F32),
        compiler_params=_params("parallel"),
        name="proj_ln",
    )(a2d, w, vec, h2d)


def _qkv_kernel(x_ref, wq_ref, wk_ref, wv_ref, bq_ref, bk_ref, bv_ref, q_out, k_out, v_out):
    x = x_ref[...].astype(BF16)
    q_out[...] = ((_dot(x, wq_ref[...]) + bq_ref[...]) * ATT_HEAD_DIM ** -0.5).astype(BF16)
    k_out[...] = (_dot(x, wk_ref[...]) + bk_ref[...]).astype(BF16)
    v_out[...] = (_dot(x, wv_ref[...]) + bv_ref[...]).astype(BF16)


def _qkv(h2d, wq, wkd, wvd, bq, bkd, bvd):
    T, D = h2d.shape
    nq, nk = wq.shape[1], wkd.shape[1]
    ws = [wq, wkd, wvd, bq, bkd, bvd]
    tp = _proj_rows(T)
    return pl.pallas_call(
        _qkv_kernel,
        grid=(T // tp,),
        in_specs=[pl.BlockSpec((tp, D), lambda i: (i, 0))] + [_full(w.shape) for w in ws],
        out_specs=[pl.BlockSpec((tp, nq), lambda i: (i, 0)), pl.BlockSpec((tp, nk), lambda i: (i, 0)),
                   pl.BlockSpec((tp, nk), lambda i: (i, 0))],
        out_shape=[jax.ShapeDtypeStruct((T, nq), BF16), jax.ShapeDtypeStruct((T, nk), BF16),
                   jax.ShapeDtypeStruct((T, nk), BF16)],
        compiler_params=_params("parallel"),
        name="qkv_proj",
    )(h2d, *ws)


def _attn_kernel(sink_ref, q_ref, k_ref, v_ref, bband_ref, bmeta_ref, bmm_ref, o_ref, *, nb):
    c = pl.program_id(0)
    BLK = ATT_BLOCK
    M0 = FRONT
    lane = lax.broadcasted_iota(jnp.int32, (1, LANES), 1)
    m0 = lane < ATT_HEAD_DIM
    qi = lax.broadcasted_iota(jnp.int32, (2 * BLK, BLK), 0) % BLK
    sj = lax.broadcasted_iota(jnp.int32, (2 * BLK, BLK), 1)
    cur_vis = sj <= qi
    prev_vis = sj > qi
    half = lax.broadcasted_iota(jnp.int32, (2 * BLK, 1), 0) < BLK

    def stack(x):
        z = jnp.zeros_like(x)
        return jnp.concatenate([jnp.where(m0, x, z), jnp.where(m0, z, x)], axis=0)

    HP = range(ATT_STEP_PAIRS)
    ppk = ATT_GROUP // 2
    kl = [slice((hp // ppk) * LANES, (hp // ppk + 1) * LANES) for hp in HP]
    ql = [slice(hp * LANES, (hp + 1) * LANES) for hp in HP]

    def rowmax(lgs, floor):
        by_width = {}
        for lg in lgs:
            by_width.setdefault(lg.shape[1], []).append(lg)
        tops = [jnp.max(functools.reduce(jnp.maximum, g), axis=-1, keepdims=True) for g in by_width.values()]
        return functools.reduce(jnp.maximum, tops, floor)

    def attend(segs, sinks, n):
        it = range(len(segs))
        mx = [rowmax([lg for lg, _ in segs[i]], sinks[i]) for i in it]
        ps = [[jnp.exp(lg - mx[i]).astype(BF16) for lg, _ in segs[i]] for i in it]
        one = jnp.ones((1, LANES), BF16)
        va = [[jnp.where(m0, vals, one) for _, vals in segs[i]] for i in it]
        vb = [[jnp.where(m0, one, vals) for _, vals in segs[i]] for i in it]
        oa = [sum(_dot(p[:n], v) for p, v in zip(ps[i], va[i])) for i in it]
        ob = [sum(_dot(p[n:], v) for p, v in zip(ps[i], vb[i])) for i in it]
        out = []
        for i in it:
            st = jnp.exp(sinks[i] - mx[i])
            den = pltpu.roll(jnp.where(m0, ob[i], oa[i]), ATT_HEAD_DIM, axis=1)
            out.append(jnp.where(m0, oa[i], ob[i]) / (den + jnp.where(m0, st[:n], st[n:])))
        return out

    o_ref[0:M0, :] = jnp.zeros((M0, o_ref.shape[1]), o_ref.dtype)
    s0 = [sink_ref[(c * ATT_STEP_PAIRS + hp) * 2] for hp in HP]
    s1 = [sink_ref[(c * ATT_STEP_PAIRS + hp) * 2 + 1] for hp in HP]
    k_meta = [k_ref[M0:M0 + N_META, kl[hp]] for hp in HP]
    v_meta = [v_ref[M0:M0 + N_META, kl[hp]] for hp in HP]

    first = lax.broadcasted_iota(jnp.int32, (2 * N_META, 1), 0) < N_META
    mi = lax.broadcasted_iota(jnp.int32, (2 * N_META, N_META), 0) % N_META
    mj = lax.broadcasted_iota(jnp.int32, (2 * N_META, N_META), 1)
    qm = [stack(q_ref[M0:M0 + N_META, ql[hp]]) for hp in HP]
    lg = [jnp.where(mj <= mi, _dot(qm[hp], k_meta[hp], _NT) + bmm_ref[hp], NEG) for hp in HP]
    om = attend([[(lg[hp], v_meta[hp])] for hp in HP], [jnp.where(first, s0[hp], s1[hp]) for hp in HP], N_META)
    for hp in HP:
        o_ref[M0:M0 + N_META, ql[hp]] = om[hp].astype(o_ref.dtype)

    sink_q = [jnp.where(half, s0[hp], s1[hp]) for hp in HP]

    unroll = 2 if nb % 2 == 0 else 1

    def blocks(jo, carry):
        items = [(jo * unroll + u, hp) for u in range(unroll) for hp in HP]
        start = [pl.multiple_of(ROW_ALIGN + j * BLK, ROW_ALIGN) for j, _ in items]
        pstart = [pl.multiple_of(jnp.maximum(s - BLK, 0), ROW_ALIGN) for s in start]
        it = range(len(items))
        qs = [stack(q_ref[pl.ds(start[i], BLK), ql[items[i][1]]]) for i in it]
        k_prev = [k_ref[pl.ds(pstart[i], BLK), kl[items[i][1]]] for i in it]
        k_cur = [k_ref[pl.ds(start[i], BLK), kl[items[i][1]]] for i in it]
        lg_meta = [_dot(qs[i], k_meta[items[i][1]], _NT) + bmeta_ref[items[i][1], items[i][0]] for i in it]
        lg_prev = [jnp.where(prev_vis & (items[i][0] > 0),
                             _dot(qs[i], k_prev[i], _NT) + bband_ref[items[i][1], :, 0:BLK], NEG) for i in it]
        lg_cur = [jnp.where(cur_vis, _dot(qs[i], k_cur[i], _NT) + bband_ref[items[i][1], :, BLK:2 * BLK], NEG)
                  for i in it]
        out = attend([[(lg_meta[i], v_meta[items[i][1]]),
                       (lg_prev[i], v_ref[pl.ds(pstart[i], BLK), kl[items[i][1]]]),
                       (lg_cur[i], v_ref[pl.ds(start[i], BLK), kl[items[i][1]]])] for i in it],
                     [sink_q[hp] for _, hp in items], BLK)
        for i in it:
            o_ref[pl.ds(start[i], BLK), ql[items[i][1]]] = out[i].astype(o_ref.dtype)
        return carry

    lax.fori_loop(0, nb // unroll, blocks, 0)


def _attention(q, kd, vd, sinks, bband, bmeta, bmm):
    B, lp, _ = q.shape
    nb = (lp - FRONT - N_META) // ATT_BLOCK
    assert WINDOW == ATT_BLOCK and ROW_ALIGN % BF16_ROWS == 0 and ATT_KV_HEADS % ATT_STEP_KV == 0
    np_, qw, kw = ATT_STEP_PAIRS, ATT_STEP_PAIRS * LANES, ATT_STEP_KV * LANES
    return pl.pallas_call(
        functools.partial(_attn_kernel, nb=nb),
        grid=(ATT_KV_HEADS // ATT_STEP_KV, B),
        in_specs=[pl.BlockSpec(memory_space=pltpu.SMEM),
                  pl.BlockSpec((None, lp, qw), lambda c, b: (b, 0, c)),
                  pl.BlockSpec((None, lp, kw), lambda c, b: (b, 0, c)),
                  pl.BlockSpec((None, lp, kw), lambda c, b: (b, 0, c)),
                  pl.BlockSpec((np_,) + bband.shape[1:], lambda c, b: (c, 0, 0)),
                  pl.BlockSpec((np_,) + bmeta.shape[1:], lambda c, b: (c, 0, 0, 0)),
                  pl.BlockSpec((np_,) + bmm.shape[1:], lambda c, b: (c, 0, 0))],
        out_specs=pl.BlockSpec((None, lp, qw), lambda c, b: (b, 0, c)),
        out_shape=jax.ShapeDtypeStruct((B, lp, D_MODEL), BF16),
        compiler_params=_params("parallel", "parallel"),
        name="swa_attention",
    )(sinks, q, kd, vd, bband, bmeta, bmm)


def _t5_bucket(dist):
    exact = N_BUCKETS // 2
    d = jnp.maximum(dist, 0)
    ratio = jnp.log(jnp.maximum(d, 1).astype(F32) / exact) / math.log(MAX_DISTANCE / exact)
    large = jnp.minimum(exact + (ratio * (N_BUCKETS - exact)).astype(jnp.int32), N_BUCKETS - 1)
    return jnp.where(d < exact, d, large)


def _bias_tables(rel_bias, nb):
    H, BLK = ATT_HEADS, ATT_BLOCK

    def lookup(dist):
        onehot = (_t5_bucket(dist)[..., None] == jnp.arange(N_BUCKETS)).astype(F32)
        return jnp.dot(onehot, rel_bias.astype(F32), precision=lax.Precision.HIGHEST)

    qi = jnp.arange(BLK)[:, None]
    band = lookup(qi + BLK - jnp.arange(2 * BLK)[None, :])
    band = jnp.moveaxis(band, -1, 0).reshape(H // 2, 2 * BLK, 2 * BLK)
    pos = jnp.arange(nb * BLK)[:, None]
    meta = lookup(N_META + pos - jnp.arange(N_META)[None, :])
    meta = meta.reshape(nb, BLK, N_META, H // 2, 2).transpose(3, 0, 4, 1, 2)
    meta = meta.reshape(H // 2, nb, 2 * BLK, N_META)
    pm = jnp.arange(N_META)
    mm = lookup(pm[:, None] - pm[None, :])
    mm = jnp.moveaxis(mm, -1, 0).reshape(H // 2, 2 * N_META, N_META)
    return band.astype(F32), meta.astype(F32), mm.astype(F32)


def _dispatch_geometry(a1):
    E = N_EXPERTS
    routed = (a1 > 0.0).astype(BF16)
    n_col = jnp.sum(routed.astype(F32), axis=1, keepdims=True)
    ch_col = jnp.floor((n_col + (MOE_CH - 1)) * (1.0 / MOE_CH))
    ei = lax.broadcasted_iota(jnp.int32, (E, E), 0)
    ej = lax.broadcasted_iota(jnp.int32, (E, E), 1)
    off_col = _dot((ei > ej).astype(BF16), jnp.broadcast_to(ch_col, (E, LANES)))[:, 0:1]
    n_row = _dot(jnp.ones((SUBLANES, TM), BF16), routed, _NT)
    ch_row = jnp.floor((n_row + (MOE_CH - 1)) * (1.0 / MOE_CH))
    off_row = _dot(ch_row, (ei < ej).astype(BF16))
    return ch_col, off_col, ch_row, off_row


def _select_experts(choice):
    E, G, EPG = N_EXPERTS, N_GROUPS, N_EXPERTS // N_GROUPS
    grp = choice.reshape(G, EPG, TM)
    sub = lax.broadcasted_iota(jnp.int32, (G, EPG, TM), 1)
    top1 = jnp.max(grp, axis=1, keepdims=True)
    first = jnp.min(jnp.where(grp == top1, sub, EPG), axis=1, keepdims=True)
    top2 = jnp.max(jnp.where(sub == first, -jnp.inf, grp), axis=1, keepdims=True)
    gscore = jnp.broadcast_to(top1 + top2, (G, EPG, TM))

    gi = lax.broadcasted_iota(jnp.int32, (G, EPG, TM), 0)
    keep = jnp.zeros((G, EPG, TM), jnp.bool_)
    for _ in range(TOPK_GROUPS):
        m = jnp.max(gscore, axis=0, keepdims=True)
        sel = gi == jnp.min(jnp.where(gscore == m, gi, G), axis=0, keepdims=True)
        keep = keep | sel
        gscore = jnp.where(sel, -jnp.inf, gscore)
    keep_e = keep.reshape(E, TM)

    cand = jnp.where(keep_e, choice, -jnp.inf)
    ei = lax.broadcasted_iota(jnp.int32, (E, TM), 0)
    routed = jnp.zeros((E, TM), jnp.bool_)
    for _ in range(TOP_K):
        m = jnp.max(cand, axis=0, keepdims=True)
        sel = ei == jnp.min(jnp.where(cand == m, ei, E), axis=0, keepdims=True)
        routed = routed | sel
        cand = jnp.where(sel, -jnp.inf, cand)
    return routed


def _route_kernel(x_ref, rt_ref, bias_ref, a1_out, a1t_out, gate_out, cnt_out, off_out, *, tiles):
    U = range(tiles)
    xs = [x_ref[u * TM:(u + 1) * TM, :] for u in U]
    scores = [_sigmoid(_dot3(rt_ref[...], x, _NT)) for x in xs]
    routed = [_select_experts(s + bias_ref[:, 0:1]) for s in scores]
    gate = [jnp.where(r, s, 0.0) for r, s in zip(routed, scores)]
    gate = [g / jnp.sum(g, axis=0, keepdims=True) * ROUTED_SCALE for g in gate]
    ti = lax.broadcasted_iota(jnp.int32, (TM, TM), 0)
    tj = lax.broadcasted_iota(jnp.int32, (TM, TM), 1)
    eye = (ti == tj).astype(BF16)
    routed_b = [r.astype(BF16) for r in routed]
    rank = [_dot(rb, (ti < tj).astype(BF16)) for rb in routed_b]
    rank_t = [_dot((tj < ti).astype(BF16), rb, _NT) for rb in routed_b]
    routed_t = [_dot(eye, rb, _NT) for rb in routed_b]
    gate_t = [_dot_exact_lhs(eye, g, _NT) for g in gate]
    for u in U:
        a1 = jnp.where(routed[u], rank[u] + 1.0, 0.0)
        _, _, ch_row, off_row = _dispatch_geometry(a1)
        a1_out[u] = a1
        a1t_out[u] = routed_t[u] * (rank_t[u] + 1.0)
        gate_out[u] = jnp.concatenate(_split2(gate_t[u]), axis=1)
        cnt_out[u] = ch_row
        off_out[u] = off_row


def _dispatch_kernel(x_ref, a1_ref, gate_ref, cnt_ref, off_ref, xs_out, gs_out):
    E = N_EXPERTS
    lo = jnp.concatenate([off_ref[0:1, :]] * 2, axis=1) * MOE_CH
    hi = lo + jnp.concatenate([cnt_ref[0:1, :]] * 2, axis=1) * MOE_CH
    first = lax.broadcasted_iota(jnp.int32, (1, 2 * E), 1) < E
    lo1 = jnp.where(first, lo, 0.0)
    xg = jnp.concatenate([x_ref[...].astype(BF16), gate_ref[...]], axis=1)
    a1b = a1_ref[...].astype(BF16)
    SB = MOE_SLOTS // MOE_SLOT_SPLITS
    eye = (lax.broadcasted_iota(jnp.int32, (LANES, LANES), 0)
           == lax.broadcasted_iota(jnp.int32, (LANES, LANES), 1))
    for s0 in range(0, MOE_SLOTS, SB):
        si = (lax.broadcasted_iota(jnp.int32, (SB, 2 * E), 0) + s0).astype(F32)
        member = (si >= lo) & (si < hi)
        seg0 = jnp.sum(jnp.where(member, lo1, 0.0), axis=1, keepdims=True)
        rs1 = (lax.broadcasted_iota(jnp.int32, (SB, 1), 0) + (s0 + 1)).astype(F32) - seg0
        perm = _dot(member[:, :E], a1b) == rs1
        disp = _dot(perm, xg)
        for j in range(MOE_NL):
            piece = disp[:, j * LANES:(j + 1) * LANES].astype(BF16)
            xs_out[s0 // MOE_CH:(s0 + SB) // MOE_CH, j] = piece.reshape(SB // MOE_CH, MOE_CH, LANES)
        gcol = jnp.sum(jnp.where(member, disp[:, D_MODEL:], 0.0), axis=1, keepdims=True)
        for r0 in range(0, SB, LANES):
            grow = jnp.sum(jnp.where(eye, gcol[r0:r0 + LANES], 0.0), axis=0, keepdims=True)
            gs_out[:, s0 + r0:s0 + r0 + LANES] = jnp.broadcast_to(grow, (SUBLANES, LANES))


def _router(h2d, router_t, bias_col):
    T, D = h2d.shape
    nt = T // TM
    E = N_EXPERTS
    tiles = next(t for t in (4, 3, 2, 1) if nt % t == 0)
    a1, a1t, gate, cnt, off = pl.pallas_call(
        functools.partial(_route_kernel, tiles=tiles),
        grid=(nt // tiles,),
        in_specs=[pl.BlockSpec((tiles * TM, D), lambda i: (i, 0)), _full(router_t.shape), _full(bias_col.shape)],
        out_specs=[pl.BlockSpec((tiles, E, TM), lambda i: (i, 0, 0)),
                   pl.BlockSpec((tiles, TM, E), lambda i: (i, 0, 0)),
                   pl.BlockSpec((tiles, TM, 2 * E), lambda i: (i, 0, 0)),
                   pl.BlockSpec((tiles, SUBLANES, E), lambda i: (i, 0, 0)),
                   pl.BlockSpec((tiles, SUBLANES, E), lambda i: (i, 0, 0))],
        out_shape=[jax.ShapeDtypeStruct((nt, E, TM), F32),
                   jax.ShapeDtypeStruct((nt, TM, E), F32),
                   jax.ShapeDtypeStruct((nt, TM, 2 * E), BF16),
                   jax.ShapeDtypeStruct((nt, SUBLANES, E), F32),
                   jax.ShapeDtypeStruct((nt, SUBLANES, E), F32)],
        compiler_params=_params("parallel"),
        name="moe_route",
    )(h2d, router_t, bias_col)
    xs, gs = pl.pallas_call(
        _dispatch_kernel,
        grid=(nt,),
        in_specs=[pl.BlockSpec((TM, D), lambda i: (i, 0)),
                  pl.BlockSpec((None, E, TM), lambda i: (i, 0, 0)),
                  pl.BlockSpec((None, TM, 2 * E), lambda i: (i, 0, 0)),
                  pl.BlockSpec((None, SUBLANES, E), lambda i: (i, 0, 0)),
                  pl.BlockSpec((None, SUBLANES, E), lambda i: (i, 0, 0))],
        out_specs=[pl.BlockSpec((None, MOE_CPT, MOE_NL, MOE_CH, LANES), lambda i: (i, 0, 0, 0, 0)),
                   pl.BlockSpec((None, SUBLANES, MOE_SLOTS), lambda i: (i, 0, 0))],
        out_shape=[jax.ShapeDtypeStruct((nt, MOE_CPT, MOE_NL, MOE_CH, LANES), BF16),
                   jax.ShapeDtypeStruct((nt, SUBLANES, MOE_SLOTS), F32)],
        compiler_params=_params("parallel"),
        name="moe_dispatch",
    )(h2d, a1, gate, cnt, off)
    return xs, gs, a1, a1t, cnt


def _expert_kernel(be_ref, ps_ref, nb_ref, src_ref, xs_hbm, wg_ref, wu_ref, wd_ref, ys_hbm,
                   xbuf, ybuf, wgu_bf, wd_bf, in_sem, out_sem, *, nblk):
    b = pl.program_id(0)
    slot = lax.rem(b, 2)

    def in_copy(src_chunk, sl, c):
        return pltpu.make_async_copy(xs_hbm.at[src_chunk], xbuf.at[sl, c], in_sem.at[sl])

    def out_copy(dst_chunk, sl, c):
        return pltpu.make_async_copy(ybuf.at[sl, c], ys_hbm.at[dst_chunk], out_sem.at[sl])

    def for_chunks(blk, fn):
        p0 = ps_ref[blk]
        n = nb_ref[blk]
        for c0 in range(0, MOE_CB, MOE_SEG):

            @pl.when(n >= c0 + MOE_SEG)
            def _():
                for c in range(c0, c0 + MOE_SEG):
                    fn(src_ref[p0 + c], c)

            @pl.when((n > c0) & (n < c0 + MOE_SEG))
            def _():
                def body(c, carry):
                    fn(src_ref[p0 + c], c)
                    return carry

                lax.fori_loop(c0, n, body, 0)

    @pl.when(b == 0)
    def _():
        xbuf[...] = jnp.zeros_like(xbuf)
        for_chunks(0, lambda s, c: in_copy(s, 0, c).start())

    @pl.when(b + 1 < nblk)
    def _():
        for_chunks(b + 1, lambda s, c: in_copy(s, 1 - slot, c).start())

    for_chunks(b, lambda s, c: in_copy(s, slot, c).wait())

    @pl.when(b >= 2)
    def _():
        for_chunks(b - 2, lambda s, c: out_copy(s, slot, c).wait())

    @pl.when((b == 0) | (be_ref[b] != be_ref[jnp.maximum(b - 1, 0)]))
    def _():
        wgu_bf[:, :EXPERT_FF] = wg_ref[...].astype(BF16)
        wgu_bf[:, EXPERT_FF:] = wu_ref[...].astype(BF16)
        wd_bf[...] = wd_ref[...].astype(BF16)

    def swiglu(c0):
        x = jnp.concatenate([xbuf[slot, c0:c0 + MOE_SEG, j].reshape(MOE_SEG * MOE_CH, LANES)
                             for j in range(MOE_NL)], axis=1)
        hid = _dot(x, wgu_bf[...])
        act = _silu(hid[:, :EXPERT_FF]) * hid[:, EXPERT_FF:]
        y = _dot(act, wd_bf[...]).astype(BF16)
        for j in range(MOE_NL):
            ybuf[slot, c0:c0 + MOE_SEG, j] = y[:, j * LANES:(j + 1) * LANES].reshape(MOE_SEG, MOE_CH, LANES)

    @pl.when(nb_ref[b] == MOE_CB)
    def _():
        for c0 in range(0, MOE_CB, MOE_SEG):
            swiglu(c0)

    @pl.when(nb_ref[b] < MOE_CB)
    def _():
        for c0 in range(0, MOE_CB, MOE_SEG):

            @pl.when(nb_ref[b] > c0)
            def _():
                swiglu(c0)

    for_chunks(b, lambda s, c: out_copy(s, slot, c).start())

    @pl.when(b == nblk - 1)
    def _():
        for_chunks(b, lambda s, c: out_copy(s, slot, c).wait())

        @pl.when(b >= 1)
        def _():
            for_chunks(b - 1, lambda s, c: out_copy(s, 1 - slot, c).wait())


def _experts(xs, wg, wu, wd, layer, be, pstart, nbc, src):
    nt = xs.shape[0]
    nblk = be.shape[0]
    xs_c = xs.reshape(nt * MOE_CPT, MOE_NL, MOE_CH, LANES)
    by_expert = lambda b, be, ps, nb, src: (layer, be[b], 0, 0)
    grid_spec = pltpu.PrefetchScalarGridSpec(
        num_scalar_prefetch=4,
        grid=(nblk,),
        in_specs=[pl.BlockSpec(memory_space=pl.ANY),
                  pl.BlockSpec((None, None, D_MODEL, EXPERT_FF), by_expert),
                  pl.BlockSpec((None, None, D_MODEL, EXPERT_FF), by_expert),
                  pl.BlockSpec((None, None, EXPERT_FF, D_MODEL), by_expert)],
        out_specs=pl.BlockSpec(memory_space=pl.ANY),
        scratch_shapes=[pltpu.VMEM((2, MOE_CB, MOE_NL, MOE_CH, LANES), BF16),
                        pltpu.VMEM((2, MOE_CB, MOE_NL, MOE_CH, LANES), BF16),
                        pltpu.VMEM((D_MODEL, 2 * EXPERT_FF), BF16),
                        pltpu.VMEM((EXPERT_FF, D_MODEL), BF16),
                        pltpu.SemaphoreType.DMA((2,)), pltpu.SemaphoreType.DMA((2,))],
    )
    ys = pl.pallas_call(
        functools.partial(_expert_kernel, nblk=nblk),
        grid_spec=grid_spec,
        out_shape=jax.ShapeDtypeStruct(xs_c.shape, BF16),
        input_output_aliases={4: 0},
        compiler_params=_params("arbitrary"),
        name="moe_experts",
    )(be, pstart, nbc, src, xs_c, wg, wu, wd)
    return ys.reshape(xs.shape)


def _combine_kernel(h_ref, ys_ref, gs_ref, a1_ref, a1t_ref, wgu_ref, wd_ref, vec_ref, o_ref):
    for u in range(ys_ref.shape[0]):
        rows = slice(u * TM, (u + 1) * TM)
        o_ref[rows, :] = _combine_tile(h_ref[rows, :], ys_ref.at[u], gs_ref.at[u], a1_ref[u], a1t_ref[u],
                                       wgu_ref, wd_ref, vec_ref)


def _combine_tile(h, ys_ref, gs_ref, a1, a1t, wgu_ref, wd_ref, vec_ref):
    E = N_EXPERTS
    ch_col, off_col, _, _ = _dispatch_geometry(a1)
    si = lax.broadcasted_iota(jnp.int32, (E, MOE_SLOTS), 1).astype(F32)
    lo = off_col * MOE_CH
    member_t = (si >= lo) & (si < lo + ch_col * MOE_CH)
    seg0 = jnp.sum(jnp.where(member_t, lo, 0.0), axis=0, keepdims=True)
    rs1 = lax.broadcasted_iota(jnp.int32, (1, MOE_SLOTS), 1).astype(F32) - seg0 + 1.0
    perm_t = _dot(a1t, member_t) == rs1
    ys = jnp.concatenate([ys_ref[:, j].reshape(MOE_SLOTS, LANES) for j in range(MOE_NL)], axis=1)
    eye = (lax.broadcasted_iota(jnp.int32, (LANES, LANES), 0)
           == lax.broadcasted_iota(jnp.int32, (LANES, LANES), 1))
    gs = jnp.concatenate([jnp.sum(jnp.where(eye, gs_ref[0:1, s0:s0 + LANES], 0.0), axis=1, keepdims=True)
                          for s0 in range(0, MOE_SLOTS, LANES)], axis=0)
    routed = _dot(perm_t.astype(BF16), ys.astype(F32) * gs)
    xb = h.astype(BF16)
    hid = _dot(xb, wgu_ref[...])
    shared = _dot(_silu(hid[:, :EXPERT_FF]) * hid[:, EXPERT_FF:], wd_ref[...])
    return _layer_norm(DEEPNORM_ALPHA * h + (shared + routed), vec_ref[0:1, :], vec_ref[1:2, :])


def _combine(h2d, ys, gs, a1, a1t, wgu_s, wd_s, vec):
    T, D = h2d.shape
    nt = T // TM
    tiles = 2 if nt % 2 == 0 else 1
    return pl.pallas_call(
        _combine_kernel,
        grid=(nt // tiles,),
        in_specs=[pl.BlockSpec((tiles * TM, D), lambda i: (i, 0)),
                  pl.BlockSpec((tiles, MOE_CPT, MOE_NL, MOE_CH, LANES), lambda i: (i, 0, 0, 0, 0)),
                  pl.BlockSpec((tiles, SUBLANES, MOE_SLOTS), lambda i: (i, 0, 0)),
                  pl.BlockSpec((tiles, N_EXPERTS, TM), lambda i: (i, 0, 0)),
                  pl.BlockSpec((tiles, TM, N_EXPERTS), lambda i: (i, 0, 0)),
                  _full(wgu_s.shape), _full(wd_s.shape), _full(vec.shape)],
        out_specs=pl.BlockSpec((tiles * TM, D), lambda i: (i, 0)),
        out_shape=jax.ShapeDtypeStruct((T, D), F32),
        compiler_params=_params("parallel"),
        name="moe_combine",
    )(h2d, ys, gs, a1, a1t, wgu_s, wd_s, vec)


def _dispatch_tables(chunks):
    nt, E = chunks.shape
    maxc = nt * MOE_CPT
    nblk = -(-maxc // MOE_CB) + E
    hp = lax.Precision.HIGHEST
    ch = chunks.astype(F32)
    off = jnp.cumsum(ch, axis=1) - ch
    wend = jnp.cumsum(ch, axis=0)
    cnt_e = wend[-1]
    cend = jnp.cumsum(cnt_e)
    base_e = cend - cnt_e
    srcbase = jnp.arange(nt, dtype=F32)[:, None] * MOE_CPT + off - (wend - ch)
    p = jnp.arange(maxc, dtype=F32)
    e_p = jnp.minimum(jnp.sum((cend[None, :] <= p[:, None]).astype(jnp.int32), axis=1), E - 1)
    oh_e = (e_p[:, None] == jnp.arange(E, dtype=jnp.int32)[None, :]).astype(F32)
    look = jnp.dot(oh_e, jnp.concatenate([wend.T, srcbase.T, base_e[:, None]], axis=1), precision=hp)
    q = p - look[:, 2 * nt]
    i_p = jnp.sum((look[:, :nt] <= q[:, None]).astype(jnp.int32), axis=1)
    oh_i = i_p[:, None] == jnp.arange(nt, dtype=jnp.int32)[None, :]
    src = jnp.sum(jnp.where(oh_i, look[:, nt:2 * nt], 0.0), axis=1) + q
    src = jnp.clip(src, 0, maxc - 1).astype(jnp.int32)
    nblk_e = jnp.floor((cnt_e + (MOE_CB - 1)) * (1.0 / MOE_CB))
    bend = jnp.cumsum(nblk_e)
    bidx = jnp.arange(nblk, dtype=F32)
    be = jnp.minimum(jnp.sum((bend[None, :] <= bidx[:, None]).astype(jnp.int32), axis=1), E - 1)
    oh_b = (be[:, None] == jnp.arange(E, dtype=jnp.int32)[None, :]).astype(F32)
    lookb = jnp.dot(oh_b, jnp.stack([bend - nblk_e, cnt_e, base_e], axis=1), precision=hp)
    q0 = (bidx - lookb[:, 0]) * MOE_CB
    nbc = jnp.clip(lookb[:, 1] - q0, 0, MOE_CB).astype(jnp.int32)
    pstart = jnp.clip(lookb[:, 2] + q0, 0, maxc - MOE_CB).astype(jnp.int32)
    return be.astype(jnp.int32), pstart, nbc, src


def _moe_layer(h2d, layer, router, bias, w_gate, w_up, w_down, sh_gate, sh_up, sh_down, ln_g, ln_b):
    router_t = router.T.astype(F32)
    bias_col = jnp.broadcast_to(bias.astype(F32)[:, None], (N_EXPERTS, LANES))
    xs, gs, a1, a1t, cnt = _router(h2d, router_t, bias_col)
    be, pstart, nbc, src = _dispatch_tables(cnt[:, 0, :].astype(jnp.int32))
    ys = _experts(xs, w_gate, w_up, w_down, layer, be, pstart, nbc, src)
    wgu_s = jnp.concatenate([sh_gate, sh_up], axis=-1).astype(BF16)
    vec = jnp.zeros((SUBLANES, D_MODEL), F32).at[0].set(ln_g).at[1].set(ln_b)
    return _combine(h2d, ys, gs, a1, a1t, wgu_s, sh_down.astype(BF16), vec)


def _rows(*vs):
    out = jnp.zeros((SUBLANES, vs[0].shape[-1]), F32)
    for i, v in enumerate(vs):
        out = out.at[i].set(v.astype(F32))
    return out


def kernel(x, meta, rel_bias, rwkv_mu, rwkv_w0, rwkv_w1, rwkv_w2, rwkv_a0, rwkv_a1, rwkv_a2, rwkv_g1, rwkv_g2, rwkv_k_k, rwkv_k_a, rwkv_r_k, rwkv_w_r, rwkv_w_k, rwkv_w_v, rwkv_w_o, rwkv_lnx_g, rwkv_lnx_b, attn_w_qkv, attn_b_qkv, attn_sinks, attn_w_o, attn_b_o, ln_mix_g, ln_mix_b, ln_ffn_g, ln_ffn_b, moe_router, moe_bias, moe_w_gate, moe_w_up, moe_w_down, shared_w_gate, shared_w_up, shared_w_down):
    B, S, D = x.shape
    assert D == D_MODEL and S % ATT_BLOCK == 0 and (FRONT + N_META) % WKV_CHUNK == 0
    lp = FRONT + N_META + S
    T = B * lp
    assert T % TM == 0 and lp >= TM
    h = jnp.concatenate([jnp.zeros((B, FRONT, D), x.dtype),
                         jnp.broadcast_to(meta[None].astype(x.dtype), (B, N_META, D)), x], axis=1)
    h = h.reshape(T, D)
    bf = lambda w: w.astype(BF16)

    H, N = RWKV_HEADS, RWKV_HEAD
    head_of = jnp.arange(D) // N
    gsum = (head_of[:, None] == jnp.arange(LANES)[None, :]).astype(BF16)
    gexp = gsum.T
    r, w, k, v, kk, b, g = _rwkv_proj(
        h, lp, _rows(*rwkv_mu[0]), _rows(rwkv_w0[0], rwkv_a0[0], rwkv_k_k[0], rwkv_k_a[0]),
        bf(rwkv_w_r[0]), bf(rwkv_w_k[0]), bf(rwkv_w_v[0]), bf(rwkv_w1[0]), bf(rwkv_w2[0]),
        bf(rwkv_a1[0]), bf(rwkv_a2[0]), bf(rwkv_g1[0]), bf(rwkv_g2[0]), gsum, gexp)
    to3 = lambda t: t.reshape(B, lp, D)
    prm = _rows(rwkv_r_k[0].reshape(D), rwkv_lnx_g[0], rwkv_lnx_b[0])
    o = _wkv(to3(r), to3(w), to3(k), to3(v), to3(kk), to3(b), to3(g), prm)
    h = _proj_ln(o.reshape(T, D), bf(rwkv_w_o[0]), _rows(jnp.zeros((D,), F32), ln_mix_g[0], ln_mix_b[0]), h)
    h = _moe_layer(h, 0, moe_router[0], moe_bias[0], moe_w_gate, moe_w_up, moe_w_down,
                   shared_w_gate[0], shared_w_up[0], shared_w_down[0], ln_ffn_g[0], ln_ffn_b[0])

    HD, KV = ATT_HEAD_DIM, ATT_KV_HEADS
    qw = ATT_HEADS * HD
    wqkv, bqkv = attn_w_qkv[0], attn_b_qkv[0]
    dup = lambda t: jnp.concatenate([t.reshape(-1, KV, 1, HD)] * 2, axis=2).reshape(t.shape[0], 2 * KV * HD)
    wq, wkd, wvd = wqkv[:, :qw], dup(wqkv[:, qw:qw + KV * HD]), dup(wqkv[:, qw + KV * HD:])
    bq, bkd, bvd = bqkv[None, :qw], dup(bqkv[None, qw:qw + KV * HD]), dup(bqkv[None, qw + KV * HD:])
    q, kd, vd = _qkv(h, bf(wq), bf(wkd), bf(wvd), bq.astype(F32), bkd.astype(F32), bvd.astype(F32))
    bband, bmeta, bmm = _bias_tables(rel_bias, S // ATT_BLOCK)
    o = _attention(q.reshape(B, lp, qw), kd.reshape(B, lp, -1), vd.reshape(B, lp, -1),
                   attn_sinks[0].astype(F32), bband, bmeta, bmm)
    h = _proj_ln(o.reshape(T, D), bf(attn_w_o[0]), _rows(attn_b_o[0], ln_mix_g[1], ln_mix_b[1]), h)
    h = _moe_layer(h, 1, moe_router[1], moe_bias[1], moe_w_gate, moe_w_up, moe_w_down,
                   shared_w_gate[1], shared_w_up[1], shared_w_down[1], ln_ffn_g[1], ln_ffn_b[1])
    return h.reshape(B, lp, D)[:, FRONT + N_META:]
```

```python
import functools
import math

import jax
import jax.numpy as jnp
from jax import lax
from jax.experimental import pallas as pl
from jax.experimental.pallas import tpu as pltpu

F32 = jnp.float32
BF16 = jnp.bfloat16

D_MODEL = 1024
DEPTH = 2
N_META = 16
RWKV_HEAD = 64
RWKV_HEADS = D_MODEL // RWKV_HEAD
GN_EPS = 64e-5
ATT_HEADS = 16
ATT_KV_HEADS = 4
ATT_HEAD_DIM = D_MODEL // ATT_HEADS
ATT_GROUP = ATT_HEADS // ATT_KV_HEADS
WINDOW = 128
ATT_BLOCK = 128
N_BUCKETS = 32
MAX_DISTANCE = 128
N_EXPERTS = 64
TOP_K = 8
N_GROUPS = 8
TOPK_GROUPS = 4
EXPERT_FF = 256
ROUTED_SCALE = 2.5
DEEPNORM_ALPHA = (2 * DEPTH) ** 0.25
LN_EPS = 1e-5

LANES = 128
SUBLANES = 8
BF16_ROWS = 16
VMEM_LIMIT = 56 * 1024 * 1024

FRONT = 48
TM = 256
WKV_CHUNK = 64
WKV_PAIRS = 8
WKV_BATCH = 2
WKV_SUB = 16
ATT_STEP_KV = 2
ATT_STEP_PAIRS = ATT_STEP_KV * ATT_GROUP // 2
ROW_ALIGN = FRONT + N_META
MOE_CH = BF16_ROWS
MOE_SLOTS = TM * TOP_K + N_EXPERTS * MOE_CH
MOE_CPT = MOE_SLOTS // MOE_CH
MOE_NL = D_MODEL // LANES
MOE_CB = 128
MOE_SEG = 32
MOE_SLOT_SPLITS = 3
NEG = -1e30

_NN = (((1,), (0,)), ((), ()))
_NT = (((1,), (1,)), ((), ()))


def _dot(a, b, dn=_NN):
    return lax.dot_general(a.astype(BF16), b.astype(BF16), dn, preferred_element_type=F32)


def _split2(x):
    hi = x.astype(BF16)
    lo = (x - hi.astype(F32)).astype(BF16)
    return hi, lo


def _split3(x):
    h1 = x.astype(BF16)
    r1 = x - h1.astype(F32)
    h2 = r1.astype(BF16)
    h3 = (r1 - h2.astype(F32)).astype(BF16)
    return h1, h2, h3


def _dot_exact_lhs(a01, b, dn=_NN):
    a = a01.astype(BF16)
    return sum(lax.dot_general(a, p, dn, preferred_element_type=F32) for p in _split3(b))


def _dot_hilo_lhs(a01, b, dn=_NN):
    a = a01.astype(BF16)
    return sum(lax.dot_general(a, p, dn, preferred_element_type=F32) for p in _split2(b))


def _dot_hilo_rhs(a, b01, dn=_NN):
    b = b01.astype(BF16)
    return sum(lax.dot_general(p, b, dn, preferred_element_type=F32) for p in _split2(a))


def _dot3(a, b, dn=_NN):
    ah, al = _split2(a)
    bh, bl = _split2(b)
    d = lambda x, y: lax.dot_general(x, y, dn, preferred_element_type=F32)
    return d(ah, bh) + (d(ah, bl) + d(al, bh))


def _sigmoid(x):
    return 1.0 / (1.0 + jnp.exp(-x))


def _silu(x):
    return x * _sigmoid(x)


def _layer_norm(x, g, b):
    mu = jnp.mean(x, axis=-1, keepdims=True)
    xc = x - mu
    var = jnp.mean(xc * xc, axis=-1, keepdims=True)
    return xc * lax.rsqrt(var + LN_EPS) * g + b


def _full(shape):
    nd = len(shape)
    return pl.BlockSpec(shape, lambda *_: (0,) * nd)


def _params(*sem):
    return pltpu.CompilerParams(dimension_semantics=sem, vmem_limit_bytes=VMEM_LIMIT)


def _rwkv_proj_kernel(x_ref, xp_ref, mu_ref, vec_ref, wr_ref, wk_ref, wv_ref, w1_ref, w2_ref,
                      a1_ref, a2_ref, g1_ref, g2_ref, gsum_ref, gexp_ref,
                      r_out, w_out, k_out, v_out, kk_out, b_out, g_out, *, lp):
    i = pl.program_id(0)
    x = x_ref[...]
    rows = x.shape[0]
    row = lax.broadcasted_iota(jnp.int32, (rows, 1), 0)
    pos = lax.rem(i * rows, lp) + row
    pos = jnp.where(pos >= lp, pos - lp, pos)
    prev = jnp.where(row == 0, xp_ref[SUBLANES - 1:SUBLANES, :], pltpu.roll(x, 1, axis=0))
    prev = jnp.where(pos == FRONT, 0.0, prev)
    xx = prev - x
    valid = pos >= FRONT

    def mix(j):
        return (x + xx * mu_ref[j:j + 1, :]).astype(BF16)

    w0, a0, k_k, k_a = (vec_ref[j:j + 1, :] for j in range(4))
    r = _dot(mix(0), wr_ref[...])
    z = w0 + _dot(jnp.tanh(_dot(mix(1), w1_ref[...])), w2_ref[...])
    w = -math.exp(-0.5) * _sigmoid(z)
    k = _dot(mix(2), wk_ref[...])
    v = _dot(mix(3), wv_ref[...])
    a = _sigmoid(a0 + _dot(_dot(mix(4), a1_ref[...]), a2_ref[...]))
    g = _dot(_sigmoid(_dot(mix(5), g1_ref[...])), g2_ref[...])
    kk = k * k_k
    ssq = _dot_hilo_rhs(kk * kk, gsum_ref[...])
    kk = kk * lax.rsqrt(jnp.maximum(_dot_hilo_rhs(ssq, gexp_ref[...]), 1e-24))
    k = k * (1.0 + (a - 1.0) * k_a)
    r_out[...] = r
    w_out[...] = jnp.where(valid, w, 0.0)
    k_out[...] = jnp.where(valid, k, 0.0)
    v_out[...] = jnp.where(valid, v, 0.0)
    kk_out[...] = jnp.where(valid, kk, 0.0)
    b_out[...] = jnp.where(valid, kk * a, 0.0)
    g_out[...] = g


def _rwkv_proj(h2d, lp, mu, vec, wr, wk, wv, w1, w2, a1, a2, g1, g2, gsum, gexp):
    T, D = h2d.shape
    tp = _proj_rows(T)
    assert tp <= lp
    row_spec = pl.BlockSpec((tp, D), lambda i: (i, 0))
    prev_spec = pl.BlockSpec((SUBLANES, D), lambda i: (jnp.maximum(i * (tp // SUBLANES) - 1, 0), 0))
    ws = [mu, vec, wr, wk, wv, w1, w2, a1, a2, g1, g2, gsum, gexp]
    once = lambda w: pl.BlockSpec(w.shape, lambda i: (0,) * w.ndim, pipeline_mode=pl.Buffered(1))
    return pl.pallas_call(
        functools.partial(_rwkv_proj_kernel, lp=lp),
        grid=(T // tp,),
        in_specs=[row_spec, prev_spec] + [once(w) for w in ws],
        out_specs=[row_spec] * 7,
        out_shape=[jax.ShapeDtypeStruct((T, D), F32)] * 7,
        compiler_params=_params("parallel"),
        name="rwkv_proj",
    )(h2d, h2d, *ws)


def _wkv_chunk(r, w, cum, k, v, kk, b, S, c):
    C = WKV_CHUNK
    P = range(len(r))
    bf = lambda xs: [x.astype(BF16) for x in xs]
    each = lambda f, *ls: [f(*a) for a in zip(*ls)]

    def stack(x):
        return jnp.concatenate([jnp.where(c["m0"], x, 0.0), jnp.where(c["m0"], 0.0, x)], axis=0)

    f32 = lambda xs: [x.astype(F32) for x in xs]
    cat0 = lambda x, y: jnp.concatenate([x, y], axis=0)
    cat1 = lambda x, y: jnp.concatenate([x, y], axis=1)
    nt = lambda x, y: _dot(x, y, _NT)
    C2 = 2 * C

    tot = [cum[p][C - 1:C, :] for p in P]
    inv = [jnp.exp(-cum[p]) for p in P]
    dend = [jnp.exp(tot[p] - cum[p]) for p in P]
    kr_s = bf([cat0(stack(kk[p] * jnp.exp(cum[p] - w[p])), stack(r[p] * jnp.exp(cum[p]))) for p in P])
    bk_s = bf([cat0(stack(b[p] * inv[p]), stack(k[p] * inv[p])) for p in P])
    bh_s = bf([stack(b[p] * dend[p]) for p in P])
    kh_s = bf([stack(k[p] * dend[p]) for p in P])
    vs_t = bf([stack(v[p]).T for p in P])

    sc = each(nt, bk_s, kr_s)
    lab_t = [jnp.where(c["upper"], x[:C2, :C2], 0.0) for x in sc]
    arb_t = bf([jnp.where(c["upinc"], x[:C2, C2:], 0.0) for x in sc])
    lak_t = bf([jnp.where(c["upper"], x[C2:, :C2], 0.0) for x in sc])
    ark_t = bf([jnp.where(c["upinc"], x[C2:, C2:], 0.0) for x in sc])

    eye = c["eye"]
    ud = [jnp.where(c["blk"], x, 0.0) for x in lab_t]
    nu = bf([x - y for x, y in zip(lab_t, ud)])
    udb = bf(ud)
    u2b = bf(each(_dot, udb, udb))
    u2 = f32(u2b)
    t1 = each(_dot, bf([eye + x for x in u2]), [cat1(x, (eye - y).astype(BF16)) for x, y in zip(u2b, ud)])
    u4b = bf([x[:, :C2] - y for x, y in zip(t1, u2)])
    u4 = f32(u4b)
    t2 = each(_dot, bf([eye + x for x in u4]), [cat1(x, y[:, C2:].astype(BF16)) for x, y in zip(u4b, t1)])
    u8 = [x[:, :C2] - y for x, y in zip(t2, u4)]
    dinv = each(_dot, bf([eye + x for x in u8]), bf([x[:, C2:] for x in t2]))
    dinvb = bf(dinv)
    wzb = bf(each(_dot, dinvb, nu))
    t3 = each(_dot, wzb, [cat1(x, y) for x, y in zip(wzb, dinvb)])
    tt_t = bf(each(_dot, bf([eye + x[:, :C2] for x in t3]), bf([y - x[:, C2:] for x, y in zip(t3, dinv)])))

    sb = bf(S)
    p2 = each(_dot, vs_t, [cat1(x, y) for x, y in zip(lak_t, ark_t)])
    p3 = each(_dot, vs_t, kh_s)
    p1 = each(nt, sb, kr_s)
    u_t = bf(each(_dot, bf([-(x[:, :C2] + y[:, :C2]) for x, y in zip(p1, p2)]), tt_t))
    p4 = each(_dot, u_t, [cat1(x, y) for x, y in zip(arb_t, bh_s)])
    s_new = [S[p] * jnp.exp(tot[p]) + p4[p][:, C2:] + p3[p] for p in P]
    y_t = [p1[p][:, C2:] + p4[p][:, :C2] + p2[p][:, C2:] for p in P]
    return y_t, s_new


def _wkv_group_norm(y_t, hblk):
    C = WKV_CHUNK
    mean = jnp.sum(y_t, axis=0, keepdims=True) * (1.0 / RWKV_HEAD)
    yc = jnp.where(hblk, y_t - mean, 0.0)
    var = jnp.sum(yc * yc, axis=0, keepdims=True) * (1.0 / RWKV_HEAD)
    ys = (yc * lax.rsqrt(var + GN_EPS)).T
    return ys[:C] + ys[C:]


def _wkv_kernel(r_ref, w_ref, k_ref, v_ref, kk_ref, b_ref, g_ref, prm_ref, o_ref, yt_ref, *s_refs, lb):
    C = WKV_CHUNK

    @pl.when(pl.program_id(2) == 0)
    def _():
        for s_ref in s_refs:
            s_ref[...] = jnp.zeros_like(s_ref)

    ri = lax.broadcasted_iota(jnp.int32, (2 * C, 2 * C), 0)
    ci = lax.broadcasted_iota(jnp.int32, (2 * C, 2 * C), 1)
    lane = lax.broadcasted_iota(jnp.int32, (1, LANES), 1)
    ti = lax.broadcasted_iota(jnp.int32, (C, C), 0)
    tj = lax.broadcasted_iota(jnp.int32, (C, C), 1)
    consts = dict(
        m0=lane < RWKV_HEAD,
        upper=ri < ci,
        upinc=ri <= ci,
        blk=(ri // WKV_SUB) == (ci // WKV_SUB),
        hblk=(ri // RWKV_HEAD) == (ci // C),
        eye=(ri == ci).astype(F32),
    )
    m0 = consts["m0"]
    tril = (ti >= tj).astype(BF16)

    def head_sum(x):
        s0 = jnp.sum(jnp.where(m0, x, 0.0), axis=-1, keepdims=True)
        s1 = jnp.sum(jnp.where(m0, 0.0, x), axis=-1, keepdims=True)
        return jnp.where(m0, s0, s1)

    nbat = r_ref.shape[0]
    items = [(bi, slice(p * LANES, (p + 1) * LANES)) for bi in range(nbat) for p in range(WKV_PAIRS)]

    def recurrence(ci_):
        rows = pl.ds(pl.multiple_of(ci_ * C, C), C)
        ld = lambda ref: [ref[bi, rows, ln] for bi, ln in items]
        r, w, k, v, kk, b = ld(r_ref), ld(w_ref), ld(k_ref), ld(v_ref), ld(kk_ref), ld(b_ref)
        cum_all = [_dot_hilo_lhs(tril, w_ref[bi, rows, :]) for bi in range(nbat)]
        cum = [cum_all[bi][:, ln] for bi, ln in items]
        ys, s_new = _wkv_chunk(r, w, cum, k, v, kk, b, [s[...] for s in s_refs], consts)
        for i, (bi, ln) in enumerate(items):
            s_refs[i][...] = s_new[i]
            yt_ref[bi, :, ln] = ys[i]

    def finish(ci_):
        rows = pl.ds(pl.multiple_of(ci_ * C, C), C)
        for bi, ln in items:
            r_k, lg, lb_ = prm_ref[0:1, ln], prm_ref[1:2, ln], prm_ref[2:3, ln]
            yn = _wkv_group_norm(yt_ref[bi, :, ln], consts["hblk"])
            bonus = head_sum(r_ref[bi, rows, ln] * k_ref[bi, rows, ln] * r_k) * v_ref[bi, rows, ln]
            o_ref[bi, rows, ln] = ((yn * lg + lb_ + bonus) * g_ref[bi, rows, ln]).astype(o_ref.dtype)

    def chunk(ci_, carry):
        finish(ci_ - 1)
        recurrence(ci_)
        return carry

    recurrence(0)
    lax.fori_loop(1, lb // C, chunk, 0)
    finish(lb // C - 1)


def _wkv_row_block(lp, nbat):
    nch = lp // WKV_CHUNK
    for d in ((11, 8, 6, 4, 3, 2, 1) if nbat == 1 else (4, 3, 2, 1)):
        if nch % d == 0:
            return d * WKV_CHUNK
    return WKV_CHUNK


def _wkv(r, w, k, v, kk, b, g, prm):
    B, lp, D = r.shape
    nbat = WKV_BATCH if B % WKV_BATCH == 0 else 1
    lb = _wkv_row_block(lp, nbat)
    wl = WKV_PAIRS * LANES
    spec = pl.BlockSpec((nbat, lb, wl), lambda bi, pi, li: (bi, li, pi))
    return pl.pallas_call(
        functools.partial(_wkv_kernel, lb=lb),
        grid=(B // nbat, D // wl, lp // lb),
        in_specs=[spec] * 7 + [pl.BlockSpec((SUBLANES, wl), lambda bi, pi, li: (0, pi))],
        out_specs=spec,
        out_shape=jax.ShapeDtypeStruct((B, lp, D), BF16),
        scratch_shapes=[pltpu.VMEM((nbat, LANES, wl), F32)]
        + [pltpu.VMEM((LANES, LANES), F32)] * (WKV_PAIRS * nbat),
        compiler_params=_params("parallel", "parallel", "arbitrary"),
        name="wkv7",
    )(r, w, k, v, kk, b, g, prm)


def _proj_ln_kernel(a_ref, w_ref, vec_ref, h_ref, o_ref):
    mix = _dot(a_ref[...], w_ref[...]) + vec_ref[0:1, :]
    o_ref[...] = _layer_norm(DEEPNORM_ALPHA * h_ref[...] + mix, vec_ref[1:2, :], vec_ref[2:3, :])


def _proj_rows(T):
    return 2 * TM if T % (2 * TM) == 0 else TM


def _proj_ln(a2d, w, vec, h2d):
    T, D = h2d.shape
    K = a2d.shape[1]
    tp = _proj_rows(T)
    return pl.pallas_call(
        _proj_ln_kernel,
        grid=(T // tp,),
        in_specs=[pl.BlockSpec((tp, K), lambda i: (i, 0)), _full(w.shape), _full(vec.shape),
                  pl.BlockSpec((tp, D), lambda i: (i, 0))],
        out_specs=pl.BlockSpec((tp, D), lambda i: (i, 0)),
        out_shape=jax.ShapeDtypeStruct((T, D), F32),
        compiler_params=_params("parallel"),
        name="proj_ln",
    )(a2d, w, vec, h2d)


def _qkv_kernel(x_ref, wq_ref, wk_ref, wv_ref, bq_ref, bk_ref, bv_ref, q_out, k_out, v_out):
    x = x_ref[...].astype(BF16)
    q_out[...] = ((_dot(x, wq_ref[...]) + bq_ref[...]) * ATT_HEAD_DIM ** -0.5).astype(BF16)
    k_out[...] = (_dot(x, wk_ref[...]) + bk_ref[...]).astype(BF16)
    v_out[...] = (_dot(x, wv_ref[...]) + bv_ref[...]).astype(BF16)


def _qkv(h2d, wq, wkd, wvd, bq, bkd, bvd):
    T, D = h2d.shape
    nq, nk = wq.shape[1], wkd.shape[1]
    ws = [wq, wkd, wvd, bq, bkd, bvd]
    tp = _proj_rows(T)
    return pl.pallas_call(
        _qkv_kernel,
        grid=(T // tp,),
        in_specs=[pl.BlockSpec((tp, D), lambda i: (i, 0))] + [_full(w.shape) for w in ws],
        out_specs=[pl.BlockSpec((tp, nq), lambda i: (i, 0)), pl.BlockSpec((tp, nk), lambda i: (i, 0)),
                   pl.BlockSpec((tp, nk), lambda i: (i, 0))],
        out_shape=[jax.ShapeDtypeStruct((T, nq), BF16), jax.ShapeDtypeStruct((T, nk), BF16),
                   jax.ShapeDtypeStruct((T, nk), BF16)],
        compiler_params=_params("parallel"),
        name="qkv_proj",
    )(h2d, *ws)


def _attn_kernel(sink_ref, q_ref, k_ref, v_ref, bband_ref, bmeta_ref, bmm_ref, o_ref, *, nb):
    c = pl.program_id(0)
    BLK = ATT_BLOCK
    M0 = FRONT
    lane = lax.broadcasted_iota(jnp.int32, (1, LANES), 1)
    m0 = lane < ATT_HEAD_DIM
    qi = lax.broadcasted_iota(jnp.int32, (2 * BLK, BLK), 0) % BLK
    sj = lax.broadcasted_iota(jnp.int32, (2 * BLK, BLK), 1)
    cur_vis = sj <= qi
    prev_vis = sj > qi
    half = lax.broadcasted_iota(jnp.int32, (2 * BLK, 1), 0) < BLK

    def stack(x):
        z = jnp.zeros_like(x)
        return jnp.concatenate([jnp.where(m0, x, z), jnp.where(m0, z, x)], axis=0)

    HP = range(ATT_STEP_PAIRS)
    ppk = ATT_GROUP // 2
    kl = [slice((hp // ppk) * LANES, (hp // ppk + 1) * LANES) for hp in HP]
    ql = [slice(hp * LANES, (hp + 1) * LANES) for hp in HP]

    def rowmax(lgs, floor):
        by_width = {}
        for lg in lgs:
            by_width.setdefault(lg.shape[1], []).append(lg)
        tops = [jnp.max(functools.reduce(jnp.maximum, g), axis=-1, keepdims=True) for g in by_width.values()]
        return functools.reduce(jnp.maximum, tops, floor)

    def attend(segs, sinks, n):
        it = range(len(segs))
        mx = [rowmax([lg for lg, _ in segs[i]], sinks[i]) for i in it]
        ps = [[jnp.exp(lg - mx[i]).astype(BF16) for lg, _ in segs[i]] for i in it]
        one = jnp.ones((1, LANES), BF16)
        va = [[jnp.where(m0, vals, one) for _, vals in segs[i]] for i in it]
        vb = [[jnp.where(m0, one, vals) for _, vals in segs[i]] for i in it]
        oa = [sum(_dot(p[:n], v) for p, v in zip(ps[i], va[i])) for i in it]
        ob = [sum(_dot(p[n:], v) for p, v in zip(ps[i], vb[i])) for i in it]
        out = []
        for i in it:
            st = jnp.exp(sinks[i] - mx[i])
            den = pltpu.roll(jnp.where(m0, ob[i], oa[i]), ATT_HEAD_DIM, axis=1)
            out.append(jnp.where(m0, oa[i], ob[i]) / (den + jnp.where(m0, st[:n], st[n:])))
        return out

    o_ref[0:M0, :] = jnp.zeros((M0, o_ref.shape[1]), o_ref.dtype)
    s0 = [sink_ref[(c * ATT_STEP_PAIRS + hp) * 2] for hp in HP]
    s1 = [sink_ref[(c * ATT_STEP_PAIRS + hp) * 2 + 1] for hp in HP]
    k_meta = [k_ref[M0:M0 + N_META, kl[hp]] for hp in HP]
    v_meta = [v_ref[M0:M0 + N_META, kl[hp]] for hp in HP]

    first = lax.broadcasted_iota(jnp.int32, (2 * N_META, 1), 0) < N_META
    mi = lax.broadcasted_iota(jnp.int32, (2 * N_META, N_META), 0) % N_META
    mj = lax.broadcasted_iota(jnp.int32, (2 * N_META, N_META), 1)
    qm = [stack(q_ref[M0:M0 + N_META, ql[hp]]) for hp in HP]
    lg = [jnp.where(mj <= mi, _dot(qm[hp], k_meta[hp], _NT) + bmm_ref[hp], NEG) for hp in HP]
    om = attend([[(lg[hp], v_meta[hp])] for hp in HP], [jnp.where(first, s0[hp], s1[hp]) for hp in HP], N_META)
    for hp in HP:
        o_ref[M0:M0 + N_META, ql[hp]] = om[hp].astype(o_ref.dtype)

    sink_q = [jnp.where(half, s0[hp], s1[hp]) for hp in HP]

    unroll = 2 if nb % 2 == 0 else 1

    def blocks(jo, carry):
        items = [(jo * unroll + u, hp) for u in range(unroll) for hp in HP]
        start = [pl.multiple_of(ROW_ALIGN + j * BLK, ROW_ALIGN) for j, _ in items]
        pstart = [pl.multiple_of(jnp.maximum(s - BLK, 0), ROW_ALIGN) for s in start]
        it = range(len(items))
        qs = [stack(q_ref[pl.ds(start[i], BLK), ql[items[i][1]]]) for i in it]
        k_prev = [k_ref[pl.ds(pstart[i], BLK), kl[items[i][1]]] for i in it]
        k_cur = [k_ref[pl.ds(start[i], BLK), kl[items[i][1]]] for i in it]
        lg_meta = [_dot(qs[i], k_meta[items[i][1]], _NT) + bmeta_ref[items[i][1], items[i][0]] for i in it]
        lg_prev = [jnp.where(prev_vis & (items[i][0] > 0),
                             _dot(qs[i], k_prev[i], _NT) + bband_ref[items[i][1], :, 0:BLK], NEG) for i in it]
        lg_cur = [jnp.where(cur_vis, _dot(qs[i], k_cur[i], _NT) + bband_ref[items[i][1], :, BLK:2 * BLK], NEG)
                  for i in it]
        out = attend([[(lg_meta[i], v_meta[items[i][1]]),
                       (lg_prev[i], v_ref[pl.ds(pstart[i], BLK), kl[items[i][1]]]),
                       (lg_cur[i], v_ref[pl.ds(start[i], BLK), kl[items[i][1]]])] for i in it],
                     [sink_q[hp] for _, hp in items], BLK)
        for i in it:
            o_ref[pl.ds(start[i], BLK), ql[items[i][1]]] = out[i].astype(o_ref.dtype)
        return carry

    lax.fori_loop(0, nb // unroll, blocks, 0)


def _attention(q, kd, vd, sinks, bband, bmeta, bmm):
    B, lp, _ = q.shape
    nb = (lp - FRONT - N_META) // ATT_BLOCK
    assert WINDOW == ATT_BLOCK and ROW_ALIGN % BF16_ROWS == 0 and ATT_KV_HEADS % ATT_STEP_KV == 0
    np_, qw, kw = ATT_STEP_PAIRS, ATT_STEP_PAIRS * LANES, ATT_STEP_KV * LANES
    return pl.pallas_call(
        functools.partial(_attn_kernel, nb=nb),
        grid=(ATT_KV_HEADS // ATT_STEP_KV, B),
        in_specs=[pl.BlockSpec(memory_space=pltpu.SMEM),
                  pl.BlockSpec((None, lp, qw), lambda c, b: (b, 0, c)),
                  pl.BlockSpec((None, lp, kw), lambda c, b: (b, 0, c)),
                  pl.BlockSpec((None, lp, kw), lambda c, b: (b, 0, c)),
                  pl.BlockSpec((np_,) + bband.shape[1:], lambda c, b: (c, 0, 0)),
                  pl.BlockSpec((np_,) + bmeta.shape[1:], lambda c, b: (c, 0, 0, 0)),
                  pl.BlockSpec((np_,) + bmm.shape[1:], lambda c, b: (c, 0, 0))],
        out_specs=pl.BlockSpec((None, lp, qw), lambda c, b: (b, 0, c)),
        out_shape=jax.ShapeDtypeStruct((B, lp, D_MODEL), BF16),
        compiler_params=_params("parallel", "parallel"),
        name="swa_attention",
    )(sinks, q, kd, vd, bband, bmeta, bmm)


def _t5_bucket(dist):
    exact = N_BUCKETS // 2
    d = jnp.maximum(dist, 0)
    ratio = jnp.log(jnp.maximum(d, 1).astype(F32) / exact) / math.log(MAX_DISTANCE / exact)
    large = jnp.minimum(exact + (ratio * (N_BUCKETS - exact)).astype(jnp.int32), N_BUCKETS - 1)
    return jnp.where(d < exact, d, large)


def _bias_tables(rel_bias, nb):
    H, BLK = ATT_HEADS, ATT_BLOCK

    def lookup(dist):
        onehot = (_t5_bucket(dist)[..., None] == jnp.arange(N_BUCKETS)).astype(F32)
        return jnp.dot(onehot, rel_bias.astype(F32), precision=lax.Precision.HIGHEST)

    qi = jnp.arange(BLK)[:, None]
    band = lookup(qi + BLK - jnp.arange(2 * BLK)[None, :])
    band = jnp.moveaxis(band, -1, 0).reshape(H // 2, 2 * BLK, 2 * BLK)
    pos = jnp.arange(nb * BLK)[:, None]
    meta = lookup(N_META + pos - jnp.arange(N_META)[None, :])
    meta = meta.reshape(nb, BLK, N_META, H // 2, 2).transpose(3, 0, 4, 1, 2)
    meta = meta.reshape(H // 2, nb, 2 * BLK, N_META)
    pm = jnp.arange(N_META)
    mm = lookup(pm[:, None] - pm[None, :])
    mm = jnp.moveaxis(mm, -1, 0).reshape(H // 2, 2 * N_META, N_META)
    return band.astype(F32), meta.astype(F32), mm.astype(F32)


def _dispatch_geometry(a1):
    E = N_EXPERTS
    routed = (a1 > 0.0).astype(BF16)
    n_col = jnp.sum(routed.astype(F32), axis=1, keepdims=True)
    ch_col = jnp.floor((n_col + (MOE_CH - 1)) * (1.0 / MOE_CH))
    ei = lax.broadcasted_iota(jnp.int32, (E, E), 0)
    ej = lax.broadcasted_iota(jnp.int32, (E, E), 1)
    off_col = _dot((ei > ej).astype(BF16), jnp.broadcast_to(ch_col, (E, LANES)))[:, 0:1]
    n_row = _dot(jnp.ones((SUBLANES, TM), BF16), routed, _NT)
    ch_row = jnp.floor((n_row + (MOE_CH - 1)) * (1.0 / MOE_CH))
    off_row = _dot(ch_row, (ei < ej).astype(BF16))
    return ch_col, off_col, ch_row, off_row


def _select_experts(choice):
    E, G, EPG = N_EXPERTS, N_GROUPS, N_EXPERTS // N_GROUPS
    grp = choice.reshape(G, EPG, TM)
    sub = lax.broadcasted_iota(jnp.int32, (G, EPG, TM), 1)
    top1 = jnp.max(grp, axis=1, keepdims=True)
    first = jnp.min(jnp.where(grp == top1, sub, EPG), axis=1, keepdims=True)
    top2 = jnp.max(jnp.where(sub == first, -jnp.inf, grp), axis=1, keepdims=True)
    gscore = jnp.broadcast_to(top1 + top2, (G, EPG, TM))

    gi = lax.broadcasted_iota(jnp.int32, (G, EPG, TM), 0)
    keep = jnp.zeros((G, EPG, TM), jnp.bool_)
    for _ in range(TOPK_GROUPS):
        m = jnp.max(gscore, axis=0, keepdims=True)
        sel = gi == jnp.min(jnp.where(gscore == m, gi, G), axis=0, keepdims=True)
        keep = keep | sel
        gscore = jnp.where(sel, -jnp.inf, gscore)
    keep_e = keep.reshape(E, TM)

    cand = jnp.where(keep_e, choice, -jnp.inf)
    ei = lax.broadcasted_iota(jnp.int32, (E, TM), 0)
    routed = jnp.zeros((E, TM), jnp.bool_)
    for _ in range(TOP_K):
        m = jnp.max(cand, axis=0, keepdims=True)
        sel = ei == jnp.min(jnp.where(cand == m, ei, E), axis=0, keepdims=True)
        routed = routed | sel
        cand = jnp.where(sel, -jnp.inf, cand)
    return routed


def _route_kernel(x_ref, rt_ref, bias_ref, a1_out, a1t_out, gate_out, cnt_out, off_out, *, tiles):
    U = range(tiles)
    xs = [x_ref[u * TM:(u + 1) * TM, :] for u in U]
    scores = [_sigmoid(_dot3(rt_ref[...], x, _NT)) for x in xs]
    routed = [_select_experts(s + bias_ref[:, 0:1]) for s in scores]
    gate = [jnp.where(r, s, 0.0) for r, s in zip(routed, scores)]
    gate = [g / jnp.sum(g, axis=0, keepdims=True) * ROUTED_SCALE for g in gate]
    ti = lax.broadcasted_iota(jnp.int32, (TM, TM), 0)
    tj = lax.broadcasted_iota(jnp.int32, (TM, TM), 1)
    eye = (ti == tj).astype(BF16)
    routed_b = [r.astype(BF16) for r in routed]
    rank = [_dot(rb, (ti < tj).astype(BF16)) for rb in routed_b]
    rank_t = [_dot((tj < ti).astype(BF16), rb, _NT) for rb in routed_b]
    routed_t = [_dot(eye, rb, _NT) for rb in routed_b]
    gate_t = [_dot_exact_lhs(eye, g, _NT) for g in gate]
    for u in U:
        a1 = jnp.where(routed[u], rank[u] + 1.0, 0.0)
        _, _, ch_row, off_row = _dispatch_geometry(a1)
        a1_out[u] = a1
        a1t_out[u] = routed_t[u] * (rank_t[u] + 1.0)
        gate_out[u] = jnp.concatenate(_split2(gate_t[u]), axis=1)
        cnt_out[u] = ch_row
        off_out[u] = off_row


def _dispatch_kernel(x_ref, a1_ref, gate_ref, cnt_ref, off_ref, xs_out, gs_out):
    E = N_EXPERTS
    lo = jnp.concatenate([off_ref[0:1, :]] * 2, axis=1) * MOE_CH
    hi = lo + jnp.concatenate([cnt_ref[0:1, :]] * 2, axis=1) * MOE_CH
    first = lax.broadcasted_iota(jnp.int32, (1, 2 * E), 1) < E
    lo1 = jnp.where(first, lo, 0.0)
    xg = jnp.concatenate([x_ref[...].astype(BF16), gate_ref[...]], axis=1)
    a1b = a1_ref[...].astype(BF16)
    SB = MOE_SLOTS // MOE_SLOT_SPLITS
    eye = (lax.broadcasted_iota(jnp.int32, (LANES, LANES), 0)
           == lax.broadcasted_iota(jnp.int32, (LANES, LANES), 1))
    for s0 in range(0, MOE_SLOTS, SB):
        si = (lax.broadcasted_iota(jnp.int32, (SB, 2 * E), 0) + s0).astype(F32)
        member = (si >= lo) & (si < hi)
        seg0 = jnp.sum(jnp.where(member, lo1, 0.0), axis=1, keepdims=True)
        rs1 = (lax.broadcasted_iota(jnp.int32, (SB, 1), 0) + (s0 + 1)).astype(F32) - seg0
        perm = _dot(member[:, :E], a1b) == rs1
        disp = _dot(perm, xg)
        for j in range(MOE_NL):
            piece = disp[:, j * LANES:(j + 1) * LANES].astype(BF16)
            xs_out[s0 // MOE_CH:(s0 + SB) // MOE_CH, j] = piece.reshape(SB // MOE_CH, MOE_CH, LANES)
        gcol = jnp.sum(jnp.where(member, disp[:, D_MODEL:], 0.0), axis=1, keepdims=True)
        for r0 in range(0, SB, LANES):
            grow = jnp.sum(jnp.where(eye, gcol[r0:r0 + LANES], 0.0), axis=0, keepdims=True)
            gs_out[:, s0 + r0:s0 + r0 + LANES] = jnp.broadcast_to(grow, (SUBLANES, LANES))


def _router(h2d, router_t, bias_col):
    T, D = h2d.shape
    nt = T // TM
    E = N_EXPERTS
    tiles = next(t for t in (4, 3, 2, 1) if nt % t == 0)
    a1, a1t, gate, cnt, off = pl.pallas_call(
        functools.partial(_route_kernel, tiles=tiles),
        grid=(nt // tiles,),
        in_specs=[pl.BlockSpec((tiles * TM, D), lambda i: (i, 0)), _full(router_t.shape), _full(bias_col.shape)],
        out_specs=[pl.BlockSpec((tiles, E, TM), lambda i: (i, 0, 0)),
                   pl.BlockSpec((tiles, TM, E), lambda i: (i, 0, 0)),
                   pl.BlockSpec((tiles, TM, 2 * E), lambda i: (i, 0, 0)),
                   pl.BlockSpec((tiles, SUBLANES, E), lambda i: (i, 0, 0)),
                   pl.BlockSpec((tiles, SUBLANES, E), lambda i: (i, 0, 0))],
        out_shape=[jax.ShapeDtypeStruct((nt, E, TM), F32),
                   jax.ShapeDtypeStruct((nt, TM, E), F32),
                   jax.ShapeDtypeStruct((nt, TM, 2 * E), BF16),
                   jax.ShapeDtypeStruct((nt, SUBLANES, E), F32),
                   jax.ShapeDtypeStruct((nt, SUBLANES, E), F32)],
        compiler_params=_params("parallel"),
        name="moe_route",
    )(h2d, router_t, bias_col)
    xs, gs = pl.pallas_call(
        _dispatch_kernel,
        grid=(nt,),
        in_specs=[pl.BlockSpec((TM, D), lambda i: (i, 0)),
                  pl.BlockSpec((None, E, TM), lambda i: (i, 0, 0)),
                  pl.BlockSpec((None, TM, 2 * E), lambda i: (i, 0, 0)),
                  pl.BlockSpec((None, SUBLANES, E), lambda i: (i, 0, 0)),
                  pl.BlockSpec((None, SUBLANES, E), lambda i: (i, 0, 0))],
        out_specs=[pl.BlockSpec((None, MOE_CPT, MOE_NL, MOE_CH, LANES), lambda i: (i, 0, 0, 0, 0)),
                   pl.BlockSpec((None, SUBLANES, MOE_SLOTS), lambda i: (i, 0, 0))],
        out_shape=[jax.ShapeDtypeStruct((nt, MOE_CPT, MOE_NL, MOE_CH, LANES), BF16),
                   jax.ShapeDtypeStruct((nt, SUBLANES, MOE_SLOTS), F32)],
        compiler_params=_params("parallel"),
        name="moe_dispatch",
    )(h2d, a1, gate, cnt, off)
    return xs, gs, a1, a1t, cnt


def _expert_kernel(be_ref, ps_ref, nb_ref, src_ref, xs_hbm, wg_ref, wu_ref, wd_ref, ys_hbm,
                   xbuf, ybuf, wgu_bf, wd_bf, in_sem, out_sem, *, nblk):
    b = pl.program_id(0)
    slot = lax.rem(b, 2)

    def in_copy(src_chunk, sl, c):
        return pltpu.make_async_copy(xs_hbm.at[src_chunk], xbuf.at[sl, c], in_sem.at[sl])

    def out_copy(dst_chunk, sl, c):
        return pltpu.make_async_copy(ybuf.at[sl, c], ys_hbm.at[dst_chunk], out_sem.at[sl])

    def dma_thread(c):
        return c % 2 if isinstance(c, int) else 0

    def for_chunks(blk, fn):
        p0 = ps_ref[blk]
        n = nb_ref[blk]
        for c0 in range(0, MOE_CB, MOE_SEG):

            @pl.when(n >= c0 + MOE_SEG)
            def _():
                for c in range(c0, c0 + MOE_SEG):
                    fn(src_ref[p0 + c], c)

            @pl.when((n > c0) & (n < c0 + MOE_SEG))
            def _():
                def body(c, carry):
                    fn(src_ref[p0 + c], c)
                    return carry

                lax.fori_loop(c0, n, body, 0)

    @pl.when(b == 0)
    def _():
        xbuf[...] = jnp.zeros_like(xbuf)
        for_chunks(0, lambda s, c: in_copy(s, 0, c).start(priority=dma_thread(c)))

    @pl.when(b + 1 < nblk)
    def _():
        for_chunks(b + 1, lambda s, c: in_copy(s, 1 - slot, c).start(priority=dma_thread(c)))

    for_chunks(b, lambda s, c: in_copy(s, slot, c).wait())

    @pl.when(b >= 2)
    def _():
        for_chunks(b - 2, lambda s, c: out_copy(s, slot, c).wait())

    @pl.when((b == 0) | (be_ref[b] != be_ref[jnp.maximum(b - 1, 0)]))
    def _():
        wgu_bf[:, :EXPERT_FF] = wg_ref[...].astype(BF16)
        wgu_bf[:, EXPERT_FF:] = wu_ref[...].astype(BF16)
        wd_bf[...] = wd_ref[...].astype(BF16)

    def swiglu(c0):
        x = jnp.concatenate([xbuf[slot, c0:c0 + MOE_SEG, j].reshape(MOE_SEG * MOE_CH, LANES)
                             for j in range(MOE_NL)], axis=1)
        hid = _dot(x, wgu_bf[...])
        act = _silu(hid[:, :EXPERT_FF]) * hid[:, EXPERT_FF:]
        y = _dot(act, wd_bf[...]).astype(BF16)
        for j in range(MOE_NL):
            ybuf[slot, c0:c0 + MOE_SEG, j] = y[:, j * LANES:(j + 1) * LANES].reshape(MOE_SEG, MOE_CH, LANES)

    @pl.when(nb_ref[b] == MOE_CB)
    def _():
        for c0 in range(0, MOE_CB, MOE_SEG):
            swiglu(c0)

    @pl.when(nb_ref[b] < MOE_CB)
    def _():
        for c0 in range(0, MOE_CB, MOE_SEG):

            @pl.when(nb_ref[b] > c0)
            def _():
                swiglu(c0)

    for_chunks(b, lambda s, c: out_copy(s, slot, c).start(priority=dma_thread(c)))

    @pl.when(b == nblk - 1)
    def _():
        for_chunks(b, lambda s, c: out_copy(s, slot, c).wait())

        @pl.when(b >= 1)
        def _():
            for_chunks(b - 1, lambda s, c: out_copy(s, 1 - slot, c).wait())


def _experts(xs, wg, wu, wd, layer, be, pstart, nbc, src):
    nt = xs.shape[0]
    nblk = be.shape[0]
    xs_c = xs.reshape(nt * MOE_CPT, MOE_NL, MOE_CH, LANES)
    by_expert = lambda b, be, ps, nb, src: (layer, be[b], 0, 0)
    grid_spec = pltpu.PrefetchScalarGridSpec(
        num_scalar_prefetch=4,
        grid=(nblk,),
        in_specs=[pl.BlockSpec(memory_space=pl.ANY),
                  pl.BlockSpec((None, None, D_MODEL, EXPERT_FF), by_expert),
                  pl.BlockSpec((None, None, D_MODEL, EXPERT_FF), by_expert),
                  pl.BlockSpec((None, None, EXPERT_FF, D_MODEL), by_expert)],
        out_specs=pl.BlockSpec(memory_space=pl.ANY),
        scratch_shapes=[pltpu.VMEM((2, MOE_CB, MOE_NL, MOE_CH, LANES), BF16),
                        pltpu.VMEM((2, MOE_CB, MOE_NL, MOE_CH, LANES), BF16),
                        pltpu.VMEM((D_MODEL, 2 * EXPERT_FF), BF16),
                        pltpu.VMEM((EXPERT_FF, D_MODEL), BF16),
                        pltpu.SemaphoreType.DMA((2,)), pltpu.SemaphoreType.DMA((2,))],
    )
    ys = pl.pallas_call(
        functools.partial(_expert_kernel, nblk=nblk),
        grid_spec=grid_spec,
        out_shape=jax.ShapeDtypeStruct(xs_c.shape, BF16),
        input_output_aliases={4: 0},
        compiler_params=_params("arbitrary"),
        name="moe_experts",
    )(be, pstart, nbc, src, xs_c, wg, wu, wd)
    return ys.reshape(xs.shape)


def _combine_kernel(h_ref, ys_ref, gs_ref, a1_ref, a1t_ref, wgu_ref, wd_ref, vec_ref, o_ref):
    for u in range(ys_ref.shape[0]):
        rows = slice(u * TM, (u + 1) * TM)
        o_ref[rows, :] = _combine_tile(h_ref[rows, :], ys_ref.at[u], gs_ref.at[u], a1_ref[u], a1t_ref[u],
                                       wgu_ref, wd_ref, vec_ref)


def _combine_tile(h, ys_ref, gs_ref, a1, a1t, wgu_ref, wd_ref, vec_ref):
    E = N_EXPERTS
    ch_col, off_col, _, _ = _dispatch_geometry(a1)
    si = lax.broadcasted_iota(jnp.int32, (E, MOE_SLOTS), 1).astype(F32)
    lo = off_col * MOE_CH
    member_t = (si >= lo) & (si < lo + ch_col * MOE_CH)
    seg0 = jnp.sum(jnp.where(member_t, lo, 0.0), axis=0, keepdims=True)
    rs1 = lax.broadcasted_iota(jnp.int32, (1, MOE_SLOTS), 1).astype(F32) - seg0 + 1.0
    perm_t = _dot(a1t, member_t) == rs1
    ys = jnp.concatenate([ys_ref[:, j].reshape(MOE_SLOTS, LANES) for j in range(MOE_NL)], axis=1)
    eye = (lax.broadcasted_iota(jnp.int32, (LANES, LANES), 0)
           == lax.broadcasted_iota(jnp.int32, (LANES, LANES), 1))
    gs = jnp.concatenate([jnp.sum(jnp.where(eye, gs_ref[0:1, s0:s0 + LANES], 0.0), axis=1, keepdims=True)
                          for s0 in range(0, MOE_SLOTS, LANES)], axis=0)
    routed = _dot(perm_t.astype(BF16), ys.astype(F32) * gs)
    xb = h.astype(BF16)
    hid = _dot(xb, wgu_ref[...])
    shared = _dot(_silu(hid[:, :EXPERT_FF]) * hid[:, EXPERT_FF:], wd_ref[...])
    return _layer_norm(DEEPNORM_ALPHA * h + (shared + routed), vec_ref[0:1, :], vec_ref[1:2, :])


def _combine(h2d, ys, gs, a1, a1t, wgu_s, wd_s, vec):
    T, D = h2d.shape
    nt = T // TM
    tiles = 2 if nt % 2 == 0 else 1
    return pl.pallas_call(
        _combine_kernel,
        grid=(nt // tiles,),
        in_specs=[pl.BlockSpec((tiles * TM, D), lambda i: (i, 0)),
                  pl.BlockSpec((tiles, MOE_CPT, MOE_NL, MOE_CH, LANES), lambda i: (i, 0, 0, 0, 0)),
                  pl.BlockSpec((tiles, SUBLANES, MOE_SLOTS), lambda i: (i, 0, 0)),
                  pl.BlockSpec((tiles, N_EXPERTS, TM), lambda i: (i, 0, 0)),
                  pl.BlockSpec((tiles, TM, N_EXPERTS), lambda i: (i, 0, 0)),
                  _full(wgu_s.shape), _full(wd_s.shape), _full(vec.shape)],
        out_specs=pl.BlockSpec((tiles * TM, D), lambda i: (i, 0)),
        out_shape=jax.ShapeDtypeStruct((T, D), F32),
        compiler_params=_params("parallel"),
        name="moe_combine",
    )(h2d, ys, gs, a1, a1t, wgu_s, wd_s, vec)


def _dispatch_tables(chunks):
    nt, E = chunks.shape
    maxc = nt * MOE_CPT
    nblk = -(-maxc // MOE_CB) + E
    hp = lax.Precision.HIGHEST
    ch = chunks.astype(F32)
    off = jnp.cumsum(ch, axis=1) - ch
    wend = jnp.cumsum(ch, axis=0)
    cnt_e = wend[-1]
    cend = jnp.cumsum(cnt_e)
    base_e = cend - cnt_e
    srcbase = jnp.arange(nt, dtype=F32)[:, None] * MOE_CPT + off - (wend - ch)
    p = jnp.arange(maxc, dtype=F32)
    e_p = jnp.minimum(jnp.sum((cend[None, :] <= p[:, None]).astype(jnp.int32), axis=1), E - 1)
    oh_e = (e_p[:, None] == jnp.arange(E, dtype=jnp.int32)[None, :]).astype(F32)
    look = jnp.dot(oh_e, jnp.concatenate([wend.T, srcbase.T, base_e[:, None]], axis=1), precision=hp)
    q = p - look[:, 2 * nt]
    i_p = jnp.sum((look[:, :nt] <= q[:, None]).astype(jnp.int32), axis=1)
    oh_i = i_p[:, None] == jnp.arange(nt, dtype=jnp.int32)[None, :]
    src = jnp.sum(jnp.where(oh_i, look[:, nt:2 * nt], 0.0), axis=1) + q
    src = jnp.clip(src, 0, maxc - 1).astype(jnp.int32)
    nblk_e = jnp.floor((cnt_e + (MOE_CB - 1)) * (1.0 / MOE_CB))
    bend = jnp.cumsum(nblk_e)
    bidx = jnp.arange(nblk, dtype=F32)
    be = jnp.minimum(jnp.sum((bend[None, :] <= bidx[:, None]).astype(jnp.int32), axis=1), E - 1)
    oh_b = (be[:, None] == jnp.arange(E, dtype=jnp.int32)[None, :]).astype(F32)
    lookb = jnp.dot(oh_b, jnp.stack([bend - nblk_e, cnt_e, base_e], axis=1), precision=hp)
    q0 = (bidx - lookb[:, 0]) * MOE_CB
    nbc = jnp.clip(lookb[:, 1] - q0, 0, MOE_CB).astype(jnp.int32)
    pstart = jnp.clip(lookb[:, 2] + q0, 0, maxc - MOE_CB).astype(jnp.int32)
    return be.astype(jnp.int32), pstart, nbc, src


def _moe_layer(h2d, layer, router, bias, w_gate, w_up, w_down, sh_gate, sh_up, sh_down, ln_g, ln_b):
    router_t = router.T.astype(F32)
    bias_col = jnp.broadcast_to(bias.astype(F32)[:, None], (N_EXPERTS, LANES))
    xs, gs, a1, a1t, cnt = _router(h2d, router_t, bias_col)
    be, pstart, nbc, src = _dispatch_tables(cnt[:, 0, :].astype(jnp.int32))
    ys = _experts(xs, w_gate, w_up, w_down, layer, be, pstart, nbc, src)
    wgu_s = jnp.concatenate([sh_gate, sh_up], axis=-1).astype(BF16)
    vec = jnp.zeros((SUBLANES, D_MODEL), F32).at[0].set(ln_g).at[1].set(ln_b)
    return _combine(h2d, ys, gs, a1, a1t, wgu_s, sh_down.astype(BF16), vec)


def _rows(*vs):
    out = jnp.zeros((SUBLANES, vs[0].shape[-1]), F32)
    for i, v in enumerate(vs):
        out = out.at[i].set(v.astype(F32))
    return out


def kernel(x, meta, rel_bias, rwkv_mu, rwkv_w0, rwkv_w1, rwkv_w2, rwkv_a0, rwkv_a1, rwkv_a2, rwkv_g1, rwkv_g2, rwkv_k_k, rwkv_k_a, rwkv_r_k, rwkv_w_r, rwkv_w_k, rwkv_w_v, rwkv_w_o, rwkv_lnx_g, rwkv_lnx_b, attn_w_qkv, attn_b_qkv, attn_sinks, attn_w_o, attn_b_o, ln_mix_g, ln_mix_b, ln_ffn_g, ln_ffn_b, moe_router, moe_bias, moe_w_gate, moe_w_up, moe_w_down, shared_w_gate, shared_w_up, shared_w_down):
    B, S, D = x.shape
    assert D == D_MODEL and S % ATT_BLOCK == 0 and (FRONT + N_META) % WKV_CHUNK == 0
    lp = FRONT + N_META + S
    T = B * lp
    assert T % TM == 0 and lp >= TM
    h = jnp.concatenate([jnp.zeros((B, FRONT, D), x.dtype),
                         jnp.broadcast_to(meta[None].astype(x.dtype), (B, N_META, D)), x], axis=1)
    h = h.reshape(T, D)
    bf = lambda w: w.astype(BF16)

    H, N = RWKV_HEADS, RWKV_HEAD
    head_of = jnp.arange(D) // N
    gsum = (head_of[:, None] == jnp.arange(LANES)[None, :]).astype(BF16)
    gexp = gsum.T
    r, w, k, v, kk, b, g = _rwkv_proj(
        h, lp, _rows(*rwkv_mu[0]), _rows(rwkv_w0[0], rwkv_a0[0], rwkv_k_k[0], rwkv_k_a[0]),
        bf(rwkv_w_r[0]), bf(rwkv_w_k[0]), bf(rwkv_w_v[0]), bf(rwkv_w1[0]), bf(rwkv_w2[0]),
        bf(rwkv_a1[0]), bf(rwkv_a2[0]), bf(rwkv_g1[0]), bf(rwkv_g2[0]), gsum, gexp)
    to3 = lambda t: t.reshape(B, lp, D)
    prm = _rows(rwkv_r_k[0].reshape(D), rwkv_lnx_g[0], rwkv_lnx_b[0])
    o = _wkv(to3(r), to3(w), to3(k), to3(v), to3(kk), to3(b), to3(g), prm)
    h = _proj_ln(o.reshape(T, D), bf(rwkv_w_o[0]), _rows(jnp.zeros((D,), F32), ln_mix_g[0], ln_mix_b[0]), h)
    h = _moe_layer(h, 0, moe_router[0], moe_bias[0], moe_w_gate, moe_w_up, moe_w_down,
                   shared_w_gate[0], shared_w_up[0], shared_w_down[0], ln_ffn_g[0], ln_ffn_b[0])

    HD, KV = ATT_HEAD_DIM, ATT_KV_HEADS
    qw = ATT_HEADS * HD
    wqkv, bqkv = attn_w_qkv[0], attn_b_qkv[0]
    dup = lambda t: jnp.concatenate([t.reshape(-1, KV, 1, HD)] * 2, axis=2).reshape(t.shape[0], 2 * KV * HD)
    wq, wkd, wvd = wqkv[:, :qw], dup(wqkv[:, qw:qw + KV * HD]), dup(wqkv[:, qw + KV * HD:])
    bq, bkd, bvd = bqkv[None, :qw], dup(bqkv[None, qw:qw + KV * HD]), dup(bqkv[None, qw + KV * HD:])
    q, kd, vd = _qkv(h, bf(wq), bf(wkd), bf(wvd), bq.astype(F32), bkd.astype(F32), bvd.astype(F32))
    bband, bmeta, bmm = _bias_tables(rel_bias, S // ATT_BLOCK)
    o = _attention(q.reshape(B, lp, qw), kd.reshape(B, lp, -1), vd.reshape(B, lp, -1),
                   attn_sinks[0].astype(F32), bband, bmeta, bmm)
    h = _proj_ln(o.reshape(T, D), bf(attn_w_o[0]), _rows(attn_b_o[0], ln_mix_g[1], ln_mix_b[1]), h)
    h = _moe_layer(h, 1, moe_router[1], moe_bias[1], moe_w_gate, moe_w_up, moe_w_down,
                   shared_w_gate[1], shared_w_up[1], shared_w_down[1], ln_ffn_g[1], ln_ffn_b[1])
    return h.reshape(B, lp, D)[:, FRONT + N_META:]
```
